```python
import math
import jax, jax.numpy as jnp
from jax import lax
import numpy as np

D_MODEL = 1024
BATCH = 16
SEQ = 256
DEPTH = 4
DEC_BATCH = 2
DEC_SEQ = 1024
PAST_LEN = 256

GRID_W = 64
EPS = 1e-6
ROPE_BASE = 10000.0
QBLOCK = 128
H_A = 4
DH_A = 64
DV_A = 2 * DH_A
W_A = H_A * DV_A
C_B = 512
SHORT_K = 3
H_EMB = 33
H_FFN = 64
H_FAST_DECAY = 0.3
H_SLOW_DECAY = 1.5
H_TARGET = 1e-2
H_C = 4
DK_C = 64
DV_C = 128
W_C = H_C * DV_C
CHUNK = 128
N_EXPERTS = 16
N_GROUPS = 4
EXP_PER_GROUP = N_EXPERTS // N_GROUPS
TOP_K = 2
D_FF = 512
SPLIT_SIZES = (H_A * 2 * DH_A, H_A * 2 * DH_A, W_A, 3 * C_B, H_C * DK_C, H_C * DK_C, W_C, W_C)
D_IN = 2 * H_A * 2 * DH_A + W_A + 3 * C_B + 2 * H_C * DK_C + 2 * W_C

kernel_name = 'hybrid_diffattn_hyena_retnet_groupmoe_prefix'


def split_points():
    pts, acc = [], 0
    for s in SPLIT_SIZES[:-1]:
        acc += s
        pts.append(acc)
    return pts


def rms_norm(x, g=None):
    xf = x.astype(jnp.float32)
    y = (xf * lax.rsqrt(jnp.mean(xf * xf, axis=-1, keepdims=True) + EPS)).astype(x.dtype)
    return y if g is None else y * g


def axial_rope(n, dim):
    rows = n // GRID_W
    row = jnp.repeat(jnp.arange(rows, dtype=jnp.float32), GRID_W)
    col = jnp.tile(jnp.arange(GRID_W, dtype=jnp.float32), rows)
    quarter = dim // 4
    inv = ROPE_BASE ** (-jnp.arange(quarter, dtype=jnp.float32) / quarter)
    ar = row[:, None] * inv
    ac = col[:, None] * inv
    ang = jnp.concatenate([ar, ar, ac, ac], axis=-1)
    return jnp.cos(ang), jnp.sin(ang)


def apply_rope(x, cos, sin):
    shape = (1, x.shape[1]) + (1,) * (x.ndim - 3) + (x.shape[-1],)
    c = cos.reshape(shape).astype(x.dtype)
    s = sin.reshape(shape).astype(x.dtype)
    a, b, cc, d = jnp.split(x, 4, axis=-1)
    rot = jnp.concatenate([-b, a, -d, cc], axis=-1)
    return x * c + rot * s


def diff_attention(q, k, v, lam, lam_init, subln_g):
    bsz, lq = q.shape[:2]
    nb = lq // QBLOCK
    qb = q.reshape(bsz, nb, QBLOCK, H_A, 2, DH_A).transpose(1, 0, 2, 3, 4, 5)
    scale = DH_A ** -0.5

    def block(qi):
        s = jnp.einsum('bqhcd,bkhcd->bchqk', qi, k).astype(jnp.float32) * scale
        p = jax.nn.softmax(s, axis=-1)
        a = (p[:, 0] - lam * p[:, 1]).astype(v.dtype)
        return jnp.einsum('bhqk,bkhe->bqhe', a, v)

    o = lax.map(block, qb)
    o = o.transpose(1, 0, 2, 3, 4).reshape(bsz, lq, H_A, DV_A)
    o = rms_norm(o, subln_g) * (1.0 - lam_init)
    return o.reshape(bsz, lq, W_A)


def short_conv(u, w, bias):
    pad = SHORT_K // 2
    L = u.shape[1]
    up = jnp.pad(u, ((0, 0), (pad, pad), (0, 0)))
    out = bias
    for j in range(SHORT_K):
        out = out + up[:, j:j + L] * w[j]
    return out


def hyena_filter(L, f_w1, f_b1, f_w2, f_b2, f_w3):
    f32 = jnp.float32
    t = jnp.linspace(0.0, 1.0, L, dtype=f32)[:, None]
    bands = (H_EMB - 1) // 2
    w = 2.0 * math.pi * jnp.arange(L, dtype=f32)[:, None] / L
    fr = jnp.linspace(1e-4, bands - 1, bands, dtype=f32)
    z = jnp.concatenate([t, jnp.cos(w * fr), -jnp.sin(w * fr)], axis=-1)
    hid = jnp.sin(z @ f_w1.astype(f32) + f_b1.astype(f32))
    hid = jnp.sin(hid @ f_w2.astype(f32) + f_b2.astype(f32))
    h = (hid @ f_w3.astype(f32)).reshape(L, 2, C_B)
    max_decay = math.log(H_TARGET) / H_FAST_DECAY
    min_decay = math.log(H_TARGET) / H_SLOW_DECAY
    deltas = jnp.linspace(min_decay, max_decay, C_B, dtype=f32)
    h = h * jnp.exp(-t * jnp.abs(deltas))[:, None, :]
    h = h / jnp.sum(jnp.abs(h), axis=(0, 1), keepdims=True)
    return jnp.concatenate([h[:, 0], jnp.zeros((1, C_B), f32), jnp.flip(h[1:, 1], axis=0)], axis=0)


def long_conv(u, filt, skip):
    L = u.shape[1]
    uf = jnp.fft.rfft(u.astype(jnp.float32), n=2 * L, axis=1)
    ff = jnp.fft.rfft(filt, n=2 * L, axis=0)
    y = jnp.fft.irfft(uf * ff[None], n=2 * L, axis=1)[:, :L]
    return (y + u.astype(jnp.float32) * skip.astype(jnp.float32)).astype(u.dtype)


def hyena_mixer(u, conv_w, conv_b, f_w1, f_b1, f_w2, f_b2, f_w3, skip):
    L = u.shape[1]
    uc = short_conv(u, conv_w, conv_b)
    x0, x1, v = jnp.split(uc, 3, axis=-1)
    filt = hyena_filter(L, f_w1, f_b1, f_w2, f_b2, f_w3)
    return x0 * long_conv(v * x1, filt, skip)


def retention_scan(q, k, v, log_gamma, s0):
    dt = q.dtype
    f32 = jnp.float32
    bsz, L = q.shape[:2]
    nc = L // CHUNK
    pos = jnp.arange(CHUNK, dtype=f32)
    lg = log_gamma.astype(f32)[:, None]
    diff = pos[:, None] - pos[None, :]
    dmask = jnp.where(diff >= 0, jnp.exp(lg[:, :, None] * jnp.maximum(diff, 0.0)), 0.0)
    xi = jnp.exp(lg * (pos + 1.0)).T[:, :, None]
    zeta = jnp.exp(lg * (CHUNK - 1.0 - pos)).T[:, :, None]
    g_blk = jnp.exp(lg[:, 0] * CHUNK)[:, None, None]

    def chunks(a):
        return a.astype(f32).reshape(bsz, nc, CHUNK, a.shape[2], a.shape[3]).swapaxes(0, 1)

    def step(S, xs):
        qi, ki, vi = xs
        att = jnp.einsum('bnhd,bmhd->bhnm', qi, ki) * dmask
        inner = jnp.einsum('bhnm,bmhe->bnhe', att, vi)
        cross = jnp.einsum('bnhd,bhde->bnhe', qi, S) * xi
        S = g_blk * S + jnp.einsum('bmhd,bmhe->bhde', ki * zeta, vi)
        return S, inner + cross

    S, o = lax.scan(step, s0.astype(f32), (chunks(q), chunks(k), chunks(v)))
    o = o.swapaxes(0, 1).reshape(bsz, L, H_C, DV_C)
    return o.astype(dt), S.astype(dt)


def bidir_retention(q, k, v, lg_f, lg_b, s0_f, s0_b):
    of, sf = retention_scan(q, k, v, lg_f, s0_f)
    ob, sb = retention_scan(jnp.flip(q, 1), jnp.flip(k, 1), jnp.flip(v, 1), lg_b, s0_b)
    return of + jnp.flip(ob, 1), sf, sb


def moe(h, w_router, b_router, w1, w3, w2):
    scores = jax.nn.softmax((h @ w_router).astype(jnp.float32), axis=-1)
    sel = scores + b_router.astype(jnp.float32)
    grp = sel.reshape(sel.shape[:-1] + (N_GROUPS, EXP_PER_GROUP))
    gscore = lax.top_k(grp, TOP_K)[0].sum(-1)
    gidx = jnp.argmax(gscore, axis=-1)
    gmask = gidx[..., None] == jnp.arange(N_GROUPS)
    masked = jnp.where(gmask[..., None], grp, -jnp.inf).reshape(sel.shape)
    _, eidx = lax.top_k(masked, TOP_K)
    wsel = jnp.take_along_axis(scores, eidx, axis=-1)
    wsel = wsel / jnp.sum(wsel, axis=-1, keepdims=True)
    gate = jnp.sum((eidx[..., None] == jnp.arange(N_EXPERTS)).astype(jnp.float32) * wsel[..., None], axis=-2)
    a = jnp.einsum('bld,edf->blef', h, w1)
    b = jnp.einsum('bld,edf->blef', h, w3)
    act = jax.nn.silu(a) * b * gate[..., None].astype(h.dtype)
    return jnp.einsum('blef,efd->bld', act, w2)


def trunk_layer(x, cond, p, layer, w_router, b_router, rope, ctx):
    bsz, L, _ = x.shape
    mod = jax.nn.silu(cond) @ p['w_ada'] + p['b_ada']
    sh1, sc1, g1, sh2, sc2, g2 = jnp.split(mod, 6, axis=-1)
    h = rms_norm(x, p['norm1']) * (1 + sc1) + sh1
    qa, ka, va, hy, qr, kr, vr, gr = jnp.split(h @ p['w_in'], split_points(), axis=-1)
    qa = qa.reshape(bsz, L, H_A, 2, DH_A)
    ka = ka.reshape(bsz, L, H_A, 2, DH_A)
    va = va.reshape(bsz, L, H_A, DV_A)
    qr = qr.reshape(bsz, L, H_C, DK_C)
    kr = kr.reshape(bsz, L, H_C, DK_C) * (DK_C ** -0.5)
    vr = vr.reshape(bsz, L, H_C, DV_C)
    if ctx is None:
        k_all, v_all = ka, va
        s0_f = jnp.zeros((bsz, H_C, DK_C, DV_C), jnp.float32)
        s0_b = s0_f
    else:
        k_ctx, v_ctx, s0_f, s0_b = ctx
        (cos_a, sin_a), (cos_c, sin_c) = rope
        qa = apply_rope(qa, cos_a, sin_a)
        ka = apply_rope(ka, cos_a, sin_a)
        qr = apply_rope(qr, cos_c, sin_c)
        kr = apply_rope(kr, cos_c, sin_c)
        k_c = k_ctx.reshape(bsz, k_ctx.shape[1], H_A, 2, DH_A).astype(ka.dtype)
        k_all = jnp.concatenate([k_c, ka], axis=1)
        v_all = jnp.concatenate([v_ctx.astype(va.dtype), va], axis=1)
    al = p['attn_lambda'].astype(jnp.float32)
    lam_init = 0.8 - 0.6 * math.exp(-0.3 * layer)
    lam = jnp.exp(jnp.sum(al[0] * al[1])) - jnp.exp(jnp.sum(al[2] * al[3])) + lam_init
    ya = diff_attention(qa, k_all, v_all, lam, lam_init, p['attn_subln_g'])
    yb = hyena_mixer(hy, p['hy_conv_w'], p['hy_conv_b'], p['hy_f_w1'], p['hy_f_b1'],
                     p['hy_f_w2'], p['hy_f_b2'], p['hy_f_w3'], p['hy_skip'])
    lg = jnp.log1p(-jnp.exp2(-p['ret_decay_exp'].astype(jnp.float32)))
    o_r, sf, sb = bidir_retention(qr, kr, vr, lg[0], lg[1], s0_f, s0_b)
    yc = rms_norm(o_r).reshape(bsz, L, W_C) * jax.nn.silu(gr)
    ga, gb, gc = jnp.split(jax.nn.sigmoid(h @ p['w_gate'] + p['b_gate']), 3, axis=-1)
    merged = ga * (ya @ p['w_branch_a']) + gb * (yb @ p['w_branch_b']) + gc * (yc @ p['w_branch_c'])
    x = x + g1 * (merged @ p['w_out'])
    h2 = rms_norm(x, p['norm2']) * (1 + sc2) + sh2
    x = x + g2 * moe(h2, w_router, b_router, p['moe_w1'], p['moe_w3'], p['moe_w2'])
    return x, (ka.reshape(bsz, L, H_A, 2 * DH_A), va, sf, sb)


def setup_inputs(seed: int = 0) -> dict:
    key = jax.random.key(seed)
    ks = iter(jax.random.split(key, 48))
    f32 = jnp.float32

    def nrm(shape, scale):
        return jax.random.normal(next(ks), shape, f32) * scale

    def gain(shape):
        return 1.0 + nrm(shape, 0.02)

    return {
        'x_prompt': nrm((BATCH, SEQ, D_MODEL), 1.0),
        'x_sample': nrm((DEC_BATCH, DEC_SEQ, D_MODEL), 1.0),
        'cache_attn_k': nrm((DEC_BATCH, DEPTH, PAST_LEN, H_A, 2 * DH_A), 1.0),
        'cache_attn_v': nrm((DEC_BATCH, DEPTH, PAST_LEN, H_A, DV_A), 1.0),
        'state_retention': nrm((DEC_BATCH, DEPTH, 2, H_C, DK_C, DV_C), 1.0),
        'c': nrm((DEC_BATCH, D_MODEL), 1.0),
        'c_ctx': nrm((D_MODEL,), 1.0),
        'w_ada': nrm((DEPTH, D_MODEL, 6 * D_MODEL), 0.5 * D_MODEL ** -0.5),
        'b_ada': nrm((DEPTH, 6 * D_MODEL), 0.02),
        'norm1_g': gain((DEPTH, D_MODEL)),
        'norm2_g': gain((DEPTH, D_MODEL)),
        'final_g': gain((D_MODEL,)),
        'w_in': nrm((DEPTH, D_MODEL, D_IN), D_MODEL ** -0.5),
        'attn_lambda': nrm((DEPTH, 4, DH_A), 0.1),
        'attn_subln_g': gain((DEPTH, DV_A)),
        'hy_conv_w': nrm((DEPTH, SHORT_K, 3 * C_B), SHORT_K ** -0.5),
        'hy_conv_b': nrm((DEPTH, 3 * C_B), 0.02),
        'hy_f_w1': nrm((DEPTH, H_EMB, H_FFN), H_EMB ** -0.5),
        'hy_f_b1': nrm((DEPTH, H_FFN), 0.02),
        'hy_f_w2': nrm((DEPTH, H_FFN, H_FFN), H_FFN ** -0.5),
        'hy_f_b2': nrm((DEPTH, H_FFN), 0.02),
        'hy_f_w3': nrm((DEPTH, H_FFN, 2 * C_B), H_FFN ** -0.5),
        'hy_skip': nrm((DEPTH, C_B), 0.1),
        'ret_decay_exp': 5.0 + jnp.arange(H_C, dtype=f32) + jax.random.uniform(next(ks), (DEPTH, 2, H_C), f32, 0.0, 0.5),
        'w_branch_a': nrm((DEPTH, W_A, D_MODEL), W_A ** -0.5),
        'w_branch_b': nrm((DEPTH, C_B, D_MODEL), C_B ** -0.5),
        'w_branch_c': nrm((DEPTH, W_C, D_MODEL), W_C ** -0.5),
        'w_gate': nrm((DEPTH, D_MODEL, 3 * D_MODEL), D_MODEL ** -0.5),
        'b_gate': nrm((DEPTH, 3 * D_MODEL), 0.02),
        'w_out': nrm((DEPTH, D_MODEL, D_MODEL), D_MODEL ** -0.5),
        'w_router': nrm((D_MODEL, N_EXPERTS), D_MODEL ** -0.5),
        'b_router': nrm((N_EXPERTS,), 0.01),
        'moe_w1': nrm((DEPTH, N_EXPERTS, D_MODEL, D_FF), D_MODEL ** -0.5),
        'moe_w3': nrm((DEPTH, N_EXPERTS, D_MODEL, D_FF), D_MODEL ** -0.5),
        'moe_w2': nrm((DEPTH, N_EXPERTS, D_FF, D_MODEL), D_FF ** -0.5),
    }


def reference(x_prompt, x_sample, cache_attn_k, cache_attn_v, state_retention, c, c_ctx,
              w_ada, b_ada, norm1_g, norm2_g, final_g, w_in, attn_lambda, attn_subln_g,
              hy_conv_w, hy_conv_b, hy_f_w1, hy_f_b1, hy_f_w2, hy_f_b2, hy_f_w3, hy_skip,
              ret_decay_exp, w_branch_a, w_branch_b, w_branch_c, w_gate, b_gate, w_out,
              w_router, b_router, moe_w1, moe_w3, moe_w2):
    def layer_params(l):
        return {
            'w_ada': w_ada[l], 'b_ada': b_ada[l], 'norm1': norm1_g[l], 'norm2': norm2_g[l],
            'w_in': w_in[l], 'attn_lambda': attn_lambda[l], 'attn_subln_g': attn_subln_g[l],
            'hy_conv_w': hy_conv_w[l], 'hy_conv_b': hy_conv_b[l], 'hy_f_w1': hy_f_w1[l],
            'hy_f_b1': hy_f_b1[l], 'hy_f_w2': hy_f_w2[l], 'hy_f_b2': hy_f_b2[l],
            'hy_f_w3': hy_f_w3[l], 'hy_skip': hy_skip[l], 'ret_decay_exp': ret_decay_exp[l],
            'w_branch_a': w_branch_a[l], 'w_branch_b': w_branch_b[l], 'w_branch_c': w_branch_c[l],
            'w_gate': w_gate[l], 'b_gate': b_gate[l], 'w_out': w_out[l],
            'moe_w1': moe_w1[l], 'moe_w3': moe_w3[l], 'moe_w2': moe_w2[l],
        }

    x = x_prompt
    ks, vs, ss = [], [], []
    for l in range(DEPTH):
        x, (k_l, v_l, sf, sb) = trunk_layer(x, c_ctx, layer_params(l), l, w_router, b_router, None, None)
        ks.append(k_l)
        vs.append(v_l)
        ss.append(jnp.stack([sf, sb], axis=1))
    y_prompt = rms_norm(x, final_g)
    new_attn_k = jnp.stack(ks, axis=1)
    new_attn_v = jnp.stack(vs, axis=1)
    new_retention_state = jnp.stack(ss, axis=1)

    n = x_sample.shape[1]
    rope = (axial_rope(n, DH_A), axial_rope(n, DK_C))
    cond = c[:, None, :]
    x = x_sample
    for l in range(DEPTH):
        ctx = (cache_attn_k[:, l], cache_attn_v[:, l], state_retention[:, l, 0], state_retention[:, l, 1])
        x, _ = trunk_layer(x, cond, layer_params(l), l, w_router, b_router, rope, ctx)
    y_sample = rms_norm(x, final_g)
    return (y_prompt, y_sample, new_attn_k, new_attn_v, new_retention_state)
```

```python
import functools
import math

import jax
import jax.numpy as jnp
from jax import lax
from jax.experimental import pallas as pl
from jax.experimental.pallas import tpu as pltpu

F32 = jnp.float32
BF16 = jnp.bfloat16

D_MODEL = 1024
BATCH = 16
SEQ = 256
DEPTH = 4
DEC_BATCH = 2
DEC_SEQ = 1024
PAST_LEN = 256
GRID_W = 64
EPS = 1e-6
ROPE_BASE = 10000.0
H_A = 4
DH_A = 64
DV_A = 128
W_A = 512
C_B = 512
SHORT_K = 3
H_EMB = 33
H_FFN = 64
H_FAST_DECAY = 0.3
H_SLOW_DECAY = 1.5
H_TARGET = 1e-2
H_C = 4
DK_C = 64
DV_C = 128
W_C = 512
CHUNK = 128
N_EXPERTS = 16
N_GROUPS = 4
EXP_PER_GROUP = 4
D_FF = 512
D_IN = 4608

T_CTX = BATCH * SEQ
T_LAT = DEC_BATCH * DEC_SEQ
T_ALL = T_CTX + T_LAT
TM = 1024
N_TILES = T_ALL // TM
CTX_TILES = T_CTX // TM
LANES = 128
VMEM_LIMIT = 56 * 1024 * 1024


def _params(n_axes):
    return pltpu.CompilerParams(
        dimension_semantics=("arbitrary",) * n_axes, vmem_limit_bytes=VMEM_LIMIT)


def _mod_row(i):
    return jnp.maximum(i - (CTX_TILES - 1), 0)


def _rms(x):
    return x * lax.rsqrt(jnp.mean(x * x, axis=-1, keepdims=True) + EPS)


def _silu(x):
    return x * jax.nn.sigmoid(x)


def _dot(a, b):
    return jnp.dot(a, b, preferred_element_type=F32)


def _dot_nt(a, b):
    return lax.dot_general(a, b, (((1,), (1,)), ((), ())), preferred_element_type=F32)


def _dot_tn(a, b):
    return lax.dot_general(a, b, (((0,), (0,)), ((), ())), preferred_element_type=F32)


def _split3(x):
    hi = x.astype(BF16)
    lo = (x - hi.astype(F32)).astype(BF16)
    return hi, lo


def _dot3(a, b):
    ah, al = _split3(a)
    bh, bl = _split3(b)
    return _dot(ah, bh) + (_dot(ah, bl) + _dot(al, bh))


def _dot3_nt(a, b):
    ah, al = _split3(a)
    bh, bl = _split3(b)
    return _dot_nt(ah, bh) + (_dot_nt(ah, bl) + _dot_nt(al, bh))


def _ada_kernel(c_ref, w_ref, b_ref, o_ref):
    s = _silu(c_ref[...])
    o_ref[0] = _dot(s.astype(BF16), w_ref[0].astype(BF16)) + b_ref[0]


def ada_modulation(cond8, w_ada, b_ada):
    tn = 1536
    n = 6 * D_MODEL
    return pl.pallas_call(
        _ada_kernel,
        grid=(DEPTH, n // tn),
        in_specs=[
            pl.BlockSpec((8, D_MODEL), lambda l, j: (0, 0)),
            pl.BlockSpec((1, D_MODEL, tn), lambda l, j: (l, 0, j)),
            pl.BlockSpec((1, 1, tn), lambda l, j: (l, 0, j)),
        ],
        out_specs=pl.BlockSpec((1, 8, tn), lambda l, j: (l, 0, j)),
        out_shape=jax.ShapeDtypeStruct((DEPTH, 8, n), F32),
        compiler_params=_params(2),
        name="ada_modulation",
    )(cond8, w_ada, b_ada.reshape(DEPTH, 1, n))


def _normmod_kernel(x_ref, g_ref, m_ref, o_ref, *, layer, which):
    y = _rms(x_ref[...]) * g_ref[layer:layer + 1, :]
    sh = m_ref[0, 0, 3 * which:3 * which + 1, :]
    sc = m_ref[0, 0, 3 * which + 1:3 * which + 2, :]
    o_ref[...] = (y * (1.0 + sc) + sh).astype(o_ref.dtype)


def norm_modulate(x, gain, mods, layer, which):
    return pl.pallas_call(
        functools.partial(_normmod_kernel, layer=layer, which=which),
        grid=(N_TILES,),
        in_specs=[
            pl.BlockSpec((TM, D_MODEL), lambda i: (i, 0)),
            pl.BlockSpec((DEPTH, D_MODEL), lambda i: (0, 0)),
            pl.BlockSpec((1, 1, 6, D_MODEL), lambda i: (layer, _mod_row(i), 0, 0)),
        ],
        out_specs=pl.BlockSpec((TM, D_MODEL), lambda i: (i, 0)),
        out_shape=jax.ShapeDtypeStruct((T_ALL, D_MODEL), BF16),
        compiler_params=_params(1),
        name="norm_modulate",
    )(x, gain, mods)


def _mm_kernel(a_ref, w_ref, b_ref, o_ref, wbf_ref, *, act):
    @pl.when(pl.program_id(1) == 0)
    def _():
        wbf_ref[...] = w_ref[0].astype(BF16)

    acc = _dot(a_ref[...], wbf_ref[...]) + b_ref[0]
    if act == "sigmoid":
        acc = jax.nn.sigmoid(acc)
    o_ref[...] = acc.astype(o_ref.dtype)


def token_matmul(a, w, bias, layer, col0, ncols, out_dtype, act=None, tn=512):
    k = w.shape[1]
    ntot = w.shape[2]
    c0 = col0 // tn
    return pl.pallas_call(
        functools.partial(_mm_kernel, act=act),
        grid=(ncols // tn, N_TILES),
        in_specs=[
            pl.BlockSpec((TM, k), lambda j, i: (i, 0)),
            pl.BlockSpec((1, k, tn), lambda j, i: (layer, 0, c0 + j)),
            pl.BlockSpec((1, 1, tn), lambda j, i: (layer, 0, c0 + j)),
        ],
        out_specs=pl.BlockSpec((TM, tn), lambda j, i: (i, j)),
        out_shape=jax.ShapeDtypeStruct((T_ALL, ncols), out_dtype),
        scratch_shapes=[pltpu.VMEM((k, tn), BF16)],
        compiler_params=_params(2),
        name="token_matmul",
    )(a, w, bias.reshape(DEPTH, 1, ntot))


def _rope_tables():
    n = DEC_SEQ
    rows = n // GRID_W
    row = jnp.repeat(jnp.arange(rows, dtype=F32), GRID_W)
    col = jnp.tile(jnp.arange(GRID_W, dtype=F32), rows)
    quarter = DH_A // 4
    inv = ROPE_BASE ** (-jnp.arange(quarter, dtype=F32) / quarter)
    ar = row[:, None] * inv
    ac = col[:, None] * inv
    ang = jnp.concatenate([ar, ar, ac, ac], axis=-1)
    cos = jnp.tile(jnp.cos(ang), (1, 512 // DH_A))
    sgn = jnp.tile(jnp.concatenate([-jnp.ones((quarter,), F32), jnp.ones((quarter,), F32)]), 512 // (2 * quarter))
    sin = jnp.tile(jnp.sin(ang), (1, 512 // DH_A)) * sgn
    return cos, sin


def _rope(x, cos, sin):
    w = x.shape[-1]
    lane = lax.broadcasted_iota(jnp.int32, x.shape, 1)
    first = (lane % 32) < 16
    rot = jnp.where(first, pltpu.roll(x, w - 16, 1), pltpu.roll(x, 16, 1))
    return x * cos + rot * sin


def _lambda(al_ref, layer):
    al = al_ref[layer]
    a = jnp.sum(al[0:1, :] * al[1:2, :], axis=-1, keepdims=True)
    b = jnp.sum(al[2:3, :] * al[3:4, :], axis=-1, keepdims=True)
    lam_init = 0.8 - 0.6 * math.exp(-0.3 * layer)
    return jnp.exp(a) - jnp.exp(b) + lam_init, lam_init


def _softmax(s):
    m = jnp.max(s, axis=-1, keepdims=True)
    e = jnp.exp(s - m)
    return e / jnp.sum(e, axis=-1, keepdims=True)


def _diff_attention(q, k, v, lam, lam_init, subln, o_ref):
    lane = lax.broadcasted_iota(jnp.int32, q.shape, 1)
    scale = DH_A ** -0.5
    q = q * scale
    q0 = jnp.where((lane % LANES) < DH_A, q, 0.0).astype(BF16)
    q1 = jnp.where((lane % LANES) >= DH_A, q, 0.0).astype(BF16)
    for h in range(H_A):
        sl = slice(h * LANES, (h + 1) * LANES)
        kh = k[:, sl]
        p0 = _softmax(_dot_nt(q0[:, sl], kh))
        p1 = _softmax(_dot_nt(q1[:, sl], kh))
        a = (p0 - lam * p1).astype(BF16)
        o = _dot(a, v[:, sl])
        o = _rms(o) * subln * (1.0 - lam_init)
        o_ref[:, sl] = o.astype(o_ref.dtype)


def _attn_ctx_kernel(q_ref, k_ref, v_ref, al_ref, g_ref, o_ref, *, layer):
    lam, lam_init = _lambda(al_ref, layer)
    subln = g_ref[layer:layer + 1, :]
    _diff_attention(q_ref[...].astype(F32), k_ref[...].astype(BF16), v_ref[...].astype(BF16),
                    lam, lam_init, subln, o_ref)


def attention_ctx(q, kv, attn_lambda, subln_g, layer):
    return pl.pallas_call(
        functools.partial(_attn_ctx_kernel, layer=layer),
        grid=(BATCH,),
        in_specs=[
            pl.BlockSpec((SEQ, W_A), lambda b: (b, 0)),
            pl.BlockSpec((SEQ, W_A), lambda b: (b, 0)),
            pl.BlockSpec((SEQ, W_A), lambda b: (b, 1)),
            pl.BlockSpec((DEPTH, 4, DH_A), lambda b: (0, 0, 0)),
            pl.BlockSpec((DEPTH, DV_A), lambda b: (0, 0)),
        ],
        out_specs=pl.BlockSpec((SEQ, W_A), lambda b: (b, 0)),
        out_shape=jax.ShapeDtypeStruct((T_ALL, W_A), BF16),
        compiler_params=_params(1),
        name="attention_ctx",
    )(q, kv, kv, attn_lambda, subln_g)


QB = 256


def _attn_lat_kernel(ya_ref, q_ref, k_ref, v_ref, ck_ref, cv_ref, cosq_ref, sinq_ref, cos_ref, sin_ref,
                     al_ref, g_ref, o_ref, kall_ref, vall_ref, *, layer):
    del ya_ref

    @pl.when(pl.program_id(1) == 0)
    def _():
        kall_ref[0:PAST_LEN, :] = ck_ref[0, 0].astype(BF16)
        vall_ref[0:PAST_LEN, :] = cv_ref[0, 0].astype(BF16)
        kall_ref[PAST_LEN:, :] = _rope(k_ref[...], cos_ref[...], sin_ref[...]).astype(BF16)
        vall_ref[PAST_LEN:, :] = v_ref[...].astype(BF16)

    lam, lam_init = _lambda(al_ref, layer)
    subln = g_ref[layer:layer + 1, :]
    q = _rope(q_ref[...].astype(F32), cosq_ref[...], sinq_ref[...])
    _diff_attention(q, kall_ref[...], vall_ref[...], lam, lam_init, subln, o_ref)


def attention_lat(ya, q, kv, cache_k, cache_v, cos, sin, attn_lambda, subln_g, layer):
    nqb = DEC_SEQ // QB
    row0 = T_CTX // QB
    seq0 = T_CTX // DEC_SEQ
    return pl.pallas_call(
        functools.partial(_attn_lat_kernel, layer=layer),
        grid=(DEC_BATCH, nqb),
        in_specs=[
            pl.BlockSpec(memory_space=pl.ANY),
            pl.BlockSpec((QB, W_A), lambda b, i: (row0 + b * nqb + i, 0)),
            pl.BlockSpec((DEC_SEQ, W_A), lambda b, i: (seq0 + b, 0)),
            pl.BlockSpec((DEC_SEQ, W_A), lambda b, i: (seq0 + b, 1)),
            pl.BlockSpec((1, 1, PAST_LEN, W_A), lambda b, i: (b, layer, 0, 0)),
            pl.BlockSpec((1, 1, PAST_LEN, W_A), lambda b, i: (b, layer, 0, 0)),
            pl.BlockSpec((QB, W_A), lambda b, i: (i, 0)),
            pl.BlockSpec((QB, W_A), lambda b, i: (i, 0)),
            pl.BlockSpec((DEC_SEQ, W_A), lambda b, i: (0, 0)),
            pl.BlockSpec((DEC_SEQ, W_A), lambda b, i: (0, 0)),
            pl.BlockSpec((DEPTH, 4, DH_A), lambda b, i: (0, 0, 0)),
            pl.BlockSpec((DEPTH, DV_A), lambda b, i: (0, 0)),
        ],
        out_specs=pl.BlockSpec((QB, W_A), lambda b, i: (row0 + b * nqb + i, 0)),
        out_shape=jax.ShapeDtypeStruct((T_ALL, W_A), BF16),
        scratch_shapes=[pltpu.VMEM((PAST_LEN + DEC_SEQ, W_A), BF16),
                        pltpu.VMEM((PAST_LEN + DEC_SEQ, W_A), BF16)],
        input_output_aliases={0: 0},
        compiler_params=_params(2),
        name="attention_lat",
    )(ya, q, kv, kv, cache_k, cache_v, cos, sin, cos, sin, attn_lambda, subln_g)


def _dft_tables(n):
    k = jnp.arange(n, dtype=jnp.int32)
    prod = (2 * k[:, None] + 1) * k[None, :]
    ang = (prod % (4 * n)).astype(F32) * (math.pi / (2 * n))
    c = jnp.cos(ang)
    s = jnp.sin(ang)
    return c.astype(BF16), s.astype(BF16), c.T.astype(BF16), s.T.astype(BF16)


def _filter_features(n):
    t = jnp.linspace(0.0, 1.0, n, dtype=F32)[:, None]
    bands = (H_EMB - 1) // 2
    w = 2.0 * math.pi * jnp.arange(n, dtype=F32)[:, None] / n
    fr = jnp.linspace(1e-4, bands - 1, bands, dtype=F32)
    z = jnp.concatenate([t, jnp.cos(w * fr), -jnp.sin(w * fr)], axis=-1)
    return jnp.pad(z, ((0, 0), (0, LANES - H_EMB)))


def _filter_decay_rates():
    max_decay = math.log(H_TARGET) / H_FAST_DECAY
    min_decay = math.log(H_TARGET) / H_SLOW_DECAY
    return jnp.abs(jnp.linspace(min_decay, max_decay, C_B, dtype=F32))[None, :]


def _hyena_filter_kernel(z_ref, w1_ref, b1_ref, w2_ref, b2_ref, w3_ref, dr_ref, c_ref, s_ref,
                         gre_ref, gim_ref):
    z = z_ref[...]
    hid = jnp.sin(_dot3(z, w1_ref[0]) + b1_ref[0])
    hid = jnp.sin(_dot3(hid, w2_ref[0]) + b2_ref[0])
    h = _dot3(hid, w3_ref[0])
    window = jnp.exp(-z[:, 0:1] * dr_ref[...])
    hf = h[:, :C_B] * window
    hb = h[:, C_B:] * window
    total = jnp.sum(jnp.abs(hf) + jnp.abs(hb), axis=0, keepdims=True)
    hf = hf / total
    hb = hb / total
    row = lax.broadcasted_iota(jnp.int32, hb.shape, 0)
    hb = jnp.where(row == 0, 0.0, hb)
    ah, al = _split3(hf + hb)
    dh, dl = _split3(hb - hf)
    gre_ref[0] = _dot(c_ref[...], ah) + _dot(c_ref[...], al)
    gim_ref[0] = _dot(s_ref[...], dh) + _dot(s_ref[...], dl)


def hyena_filters(n, z, decay_rates, cf, sf, f_w1, f_b1, f_w2, f_b2, f_w3):
    w1 = jnp.pad(f_w1, ((0, 0), (0, LANES - H_EMB), (0, 0)))
    full = lambda shape: pl.BlockSpec(shape, lambda l: (0,) * len(shape))
    per_layer = lambda shape: pl.BlockSpec((1,) + shape, lambda l: (l,) + (0,) * len(shape))
    return pl.pallas_call(
        _hyena_filter_kernel,
        grid=(DEPTH,),
        in_specs=[
            full((n, LANES)),
            per_layer((LANES, H_FFN)), per_layer((1, H_FFN)),
            per_layer((H_FFN, H_FFN)), per_layer((1, H_FFN)),
            per_layer((H_FFN, 2 * C_B)),
            full((1, C_B)), full((n, n)), full((n, n)),
        ],
        out_specs=[per_layer((n, C_B)), per_layer((n, C_B))],
        out_shape=[jax.ShapeDtypeStruct((DEPTH, n, C_B), F32)] * 2,
        compiler_params=_params(1),
        name="hyena_filters",
    )(z, w1, f_b1.reshape(DEPTH, 1, H_FFN), f_w2, f_b2.reshape(DEPTH, 1, H_FFN), f_w3,
      decay_rates, cf, sf)


def _hyena_kernel(*refs, layer, n, aliased):
    if aliased:
        refs = refs[1:]
    u_ref, cw_ref, cb_ref, skip_ref, gre_ref, gim_ref, c_ref, s_ref, ct_ref, st_ref, o_ref = refs
    u = u_ref[...].astype(F32)
    row = lax.broadcasted_iota(jnp.int32, u.shape, 0)
    prev = jnp.where(row == 0, 0.0, pltpu.roll(u, 1, 0))
    nxt = jnp.where(row == n - 1, 0.0, pltpu.roll(u, n - 1, 0))
    cw = cw_ref[layer]
    uc = cb_ref[layer:layer + 1, :] + prev * cw[0:1, :] + u * cw[1:2, :] + nxt * cw[2:3, :]
    x0 = uc[:, :C_B]
    x1 = uc[:, C_B:2 * C_B]
    v = uc[:, 2 * C_B:]
    w = v * x1
    wb = w.astype(BF16)
    ure = _dot(c_ref[...], wb)
    uim = -_dot(s_ref[...], wb)
    gre = gre_ref[0]
    gim = gim_ref[0]
    yre = (ure * gre - uim * gim).astype(BF16)
    yim = (ure * gim + uim * gre).astype(BF16)
    y = (_dot(ct_ref[...], yre) - _dot(st_ref[...], yim)) * (1.0 / n)
    y = y + w * skip_ref[layer:layer + 1, :]
    o_ref[...] = (x0 * y).astype(o_ref.dtype)


def hyena(yb, r, conv_w, conv_b, skip, gre, gim, tables, layer, n, seq0, nseq):
    cf, sf, cft, sft = tables
    aliased = yb is not None
    full = lambda shape: pl.BlockSpec(shape, lambda b: (0,) * len(shape))
    in_specs = [
        pl.BlockSpec((n, 3 * C_B), lambda b: (seq0 + b, 0)),
        full((DEPTH, SHORT_K, 3 * C_B)), full((DEPTH, 3 * C_B)), full((DEPTH, C_B)),
        pl.BlockSpec((1, n, C_B), lambda b: (layer, 0, 0)),
        pl.BlockSpec((1, n, C_B), lambda b: (layer, 0, 0)),
        full((n, n)), full((n, n)), full((n, n)), full((n, n)),
    ]
    args = [r, conv_w, conv_b, skip, gre, gim, cf, sf, cft, sft]
    if aliased:
        in_specs = [pl.BlockSpec(memory_space=pl.ANY)] + in_specs
        args = [yb] + args
    return pl.pallas_call(
        functools.partial(_hyena_kernel, layer=layer, n=n, aliased=aliased),
        grid=(nseq,),
        in_specs=in_specs,
        out_specs=pl.BlockSpec((n, C_B), lambda b: (seq0 + b, 0)),
        out_shape=jax.ShapeDtypeStruct((T_ALL, C_B), BF16),
        input_output_aliases={0: 0} if aliased else {},
        compiler_params=_params(1),
        name="hyena",
    )(*args)


def _retention_kernel(*refs, layer, n, latent):
    if latent:
        (_, q_ref, k_ref, v_ref, g_ref, de_ref, cos_ref, sin_ref, s0_ref,
         o_ref, dm_ref, vec_ref, of_ref) = refs
        st_ref = None
    else:
        q_ref, k_ref, v_ref, g_ref, de_ref, o_ref, st_ref, dm_ref, vec_ref, of_ref = refs
    nc = n // CHUNK

    @pl.when(pl.program_id(0) == 0)
    def _():
        de = de_ref[layer]
        lg = jnp.log1p(-jnp.exp2(-de))
        r = lax.broadcasted_iota(jnp.int32, (CHUNK, CHUNK), 0)
        c = lax.broadcasted_iota(jnp.int32, (CHUNK, CHUNK), 1)
        pos = lax.broadcasted_iota(jnp.int32, (CHUNK, 1), 0).astype(F32)
        for d in range(2):
            diff = (r - c) if d == 0 else (c - r)
            dpos = jnp.maximum(diff, 0).astype(F32)
            for h in range(H_C):
                g = lg[d:d + 1, h:h + 1]
                dm_ref[d, h] = jnp.where(diff >= 0, jnp.exp(g * dpos), 0.0)
                fwd_pos = pos if d == 0 else (CHUNK - 1.0) - pos
                xi = jnp.exp(g * (fwd_pos + 1.0))
                zeta = jnp.exp(g * (CHUNK - 1.0 - fwd_pos))
                gblk = jnp.exp(g * float(CHUNK)) + jnp.zeros((CHUNK, 1), F32)
                vec_ref[d, h] = jnp.concatenate(
                    [xi, zeta, gblk, jnp.zeros((CHUNK, LANES - 3), F32)], axis=1)

    q = q_ref[...].astype(F32)
    k = k_ref[...].astype(F32) * (DK_C ** -0.5)
    if latent:
        q = _rope(q, cos_ref[...], sin_ref[...])
        k = _rope(k, cos_ref[...], sin_ref[...])
    v = v_ref[...].astype(BF16)
    lane = lax.broadcasted_iota(jnp.int32, (CHUNK, LANES), 1)
    zeros64 = jnp.zeros((DK_C, DV_C), F32)

    for h in range(H_C):
        pair = slice((h // 2) * LANES, (h // 2 + 1) * LANES)
        lo = (h % 2) * DK_C
        own = (lane >= lo) & (lane < lo + DK_C)
        vs = slice(h * DV_C, (h + 1) * DV_C)
        for d in range(2):
            xi = vec_ref[d, h][:, 0:1]
            zeta = vec_ref[d, h][:, 1:2]
            gblk = vec_ref[d, h][0:1, 2:3]
            dmask = dm_ref[d, h]
            if latent:
                s0 = s0_ref[0, 0, d, h]
                state = jnp.concatenate([s0, zeros64] if lo == 0 else [zeros64, s0], axis=0)
            else:
                state = jnp.zeros((LANES, DV_C), F32)
            order = range(nc) if d == 0 else range(nc - 1, -1, -1)
            for ci in order:
                rows = slice(ci * CHUNK, (ci + 1) * CHUNK)
                qi = jnp.where(own, q[rows, pair], 0.0)
                ki = jnp.where(own, k[rows, pair], 0.0)
                vi = v[rows, vs]
                qb = qi.astype(BF16)
                att = _dot_nt(qb, ki.astype(BF16)) * dmask
                out = _dot(att.astype(BF16), vi) + _dot(qb, state.astype(BF16)) * xi
                state = gblk * state + _dot_tn((ki * zeta).astype(BF16), vi)
                if d == 0:
                    of_ref[rows, vs] = out
                else:
                    of_ref[rows, vs] = of_ref[rows, vs] + out
            if st_ref is not None:
                st_ref[0, d, h] = state[lo:lo + DK_C, :]

    gate = g_ref[...].astype(F32)
    for h in range(H_C):
        vs = slice(h * DV_C, (h + 1) * DV_C)
        o_ref[:, vs] = (_rms(of_ref[:, vs]) * _silu(gate[:, vs])).astype(o_ref.dtype)


def retention(yc, r, decay_exp, cos, sin, state0, layer, n, seq0, nseq, latent):
    full = lambda shape: pl.BlockSpec(shape, lambda b: (0,) * len(shape))
    in_specs = [
        pl.BlockSpec((n, H_C * DK_C), lambda b: (seq0 + b, 6)),
        pl.BlockSpec((n, H_C * DK_C), lambda b: (seq0 + b, 7)),
        pl.BlockSpec((n, W_C), lambda b: (seq0 + b, 4)),
        pl.BlockSpec((n, W_C), lambda b: (seq0 + b, 5)),
        full((DEPTH, 2, H_C)),
    ]
    args = [r, r, r, r, decay_exp]
    out_specs = [pl.BlockSpec((n, W_C), lambda b: (seq0 + b, 0))]
    out_shape = [jax.ShapeDtypeStruct((T_ALL, W_C), BF16)]
    aliases = {}
    if latent:
        in_specs = [pl.BlockSpec(memory_space=pl.ANY)] + in_specs + [
            full((n, H_C * DK_C)), full((n, H_C * DK_C)),
            pl.BlockSpec((1, 1, 2, H_C, DK_C, DV_C), lambda b: (b, layer, 0, 0, 0, 0)),
        ]
        args = [yc] + args + [cos, sin, state0]
        aliases = {0: 0}
    else:
        out_specs.append(pl.BlockSpec((1, 2, H_C, DK_C, DV_C), lambda b: (b, 0, 0, 0, 0)))
        out_shape.append(jax.ShapeDtypeStruct((nseq, 2, H_C, DK_C, DV_C), F32))
    return pl.pallas_call(
        functools.partial(_retention_kernel, layer=layer, n=n, latent=latent),
        grid=(nseq,),
        in_specs=in_specs,
        out_specs=out_specs,
        out_shape=out_shape,
        scratch_shapes=[pltpu.VMEM((2, H_C, CHUNK, CHUNK), F32),
                        pltpu.VMEM((2, H_C, CHUNK, LANES), F32),
                        pltpu.VMEM((n, W_C), F32)],
        input_output_aliases=aliases,
        compiler_params=_params(1),
        name="retention",
    )(*args)


def _merge_kernel(x_ref, ya_ref, yb_ref, yc_ref, g_ref, m_ref, wa_ref, wb_ref, wc_ref, wo_ref,
                  o_ref, wbf_ref, wobf_ref):
    @pl.when(pl.program_id(0) == 0)
    def _():
        wbf_ref[0] = wa_ref[0].astype(BF16)
        wbf_ref[1] = wb_ref[0].astype(BF16)
        wbf_ref[2] = wc_ref[0].astype(BF16)
        wobf_ref[...] = wo_ref[0].astype(BF16)

    g = g_ref[...]
    merged = (g[:, :D_MODEL] * _dot(ya_ref[...], wbf_ref[0])
              + g[:, D_MODEL:2 * D_MODEL] * _dot(yb_ref[...], wbf_ref[1])
              + g[:, 2 * D_MODEL:] * _dot(yc_ref[...], wbf_ref[2]))
    g1 = m_ref[0, 0, 2:3, :]
    o_ref[...] = x_ref[...] + g1 * _dot(merged.astype(BF16), wobf_ref[...])


def merge_branches(x, ya, yb, yc, gates, mods, w_a, w_b, w_c, w_out, layer):
    tile = lambda w: pl.BlockSpec((TM, w), lambda i: (i, 0))
    wspec = lambda k: pl.BlockSpec((1, k, D_MODEL), lambda i: (layer, 0, 0))
    return pl.pallas_call(
        _merge_kernel,
        grid=(N_TILES,),
        in_specs=[
            tile(D_MODEL), tile(W_A), tile(C_B), tile(W_C), tile(3 * D_MODEL),
            pl.BlockSpec((1, 1, 6, D_MODEL), lambda i: (layer, _mod_row(i), 0, 0)),
            wspec(W_A), wspec(C_B), wspec(W_C), wspec(D_MODEL),
        ],
        out_specs=tile(D_MODEL),
        out_shape=jax.ShapeDtypeStruct((T_ALL, D_MODEL), F32),
        scratch_shapes=[pltpu.VMEM((3, W_A, D_MODEL), BF16), pltpu.VMEM((D_MODEL, D_MODEL), BF16)],
        compiler_params=_params(1),
        name="merge_branches",
    )(x, ya, yb, yc, gates, mods, w_a, w_b, w_c, w_out)


def _route(h2, wr_t, b_r):
    logits = _dot3_nt(wr_t, h2)
    m = jnp.max(logits, axis=0, keepdims=True)
    e = jnp.exp(logits - m)
    scores = e / jnp.sum(e, axis=0, keepdims=True)
    sel = scores + b_r
    rows = [sel[i:i + 1, :] for i in range(N_EXPERTS)]
    in_group = []
    gscore = []
    for g in range(N_GROUPS):
        members = range(g * EXP_PER_GROUP, (g + 1) * EXP_PER_GROUP)
        total = None
        for i in members:
            rank = None
            for j in members:
                if j == i:
                    continue
                ahead = (rows[j] >= rows[i]) if j < i else (rows[j] > rows[i])
                ahead = ahead.astype(F32)
                rank = ahead if rank is None else rank + ahead
            chosen = rank < 2.0
            in_group.append(chosen)
            part = jnp.where(chosen, rows[i], 0.0)
            total = part if total is None else total + part
        gscore.append(total)
    gates = []
    for g in range(N_GROUPS):
        best = None
        for g2 in range(N_GROUPS):
            if g2 == g:
                continue
            wins = gscore[g] > gscore[g2] if g2 < g else gscore[g] >= gscore[g2]
            best = wins if best is None else best & wins
        for i in range(g * EXP_PER_GROUP, (g + 1) * EXP_PER_GROUP):
            gates.append(jnp.where(best & in_group[i], scores[i:i + 1, :], 0.0))
    gates = jnp.concatenate(gates, axis=0)
    return gates / jnp.sum(gates, axis=0, keepdims=True)


def _moe_dense_kernel(x_ref, g_ref, m_ref, wr_ref, br_ref, w1_ref, w3_ref, w2_ref, o_ref,
                      h_ref, gate_ref, acc_ref, *, layer):
    e = pl.program_id(1)

    @pl.when(e == 0)
    def _():
        y = _rms(x_ref[...]) * g_ref[layer:layer + 1, :]
        h2 = y * (1.0 + m_ref[0, 0, 4:5, :]) + m_ref[0, 0, 3:4, :]
        h_ref[...] = h2.astype(BF16)
        gates = _route(h2, wr_ref[...], br_ref[...])
        padded = jnp.concatenate([gates, jnp.zeros((LANES - N_EXPERTS, TM), F32)], axis=0)
        gate_ref[...] = padded.T
        acc_ref[...] = jnp.zeros_like(acc_ref)

    h = h_ref[...]
    a = _dot(h, w1_ref[0, 0].astype(BF16))
    b = _dot(h, w3_ref[0, 0].astype(BF16))
    lane = lax.broadcasted_iota(jnp.int32, (TM, LANES), 1)
    gate = jnp.sum(jnp.where(lane == e, gate_ref[...], 0.0), axis=-1, keepdims=True)
    act = (_silu(a) * b * gate).astype(BF16)
    acc_ref[...] += _dot(act, w2_ref[0, 0].astype(BF16))

    @pl.when(e == N_EXPERTS - 1)
    def _():
        o_ref[...] = x_ref[...] + m_ref[0, 0, 5:6, :] * acc_ref[...]


def moe_dense(x, gain, mods, wr_t, b_r, w1, w3, w2, layer):
    return pl.pallas_call(
        functools.partial(_moe_dense_kernel, layer=layer),
        grid=(N_TILES, N_EXPERTS),
        in_specs=[
            pl.BlockSpec((TM, D_MODEL), lambda i, e: (i, 0)),
            pl.BlockSpec((DEPTH, D_MODEL), lambda i, e: (0, 0)),
            pl.BlockSpec((1, 1, 6, D_MODEL), lambda i, e: (layer, _mod_row(i), 0, 0)),
            pl.BlockSpec((N_EXPERTS, D_MODEL), lambda i, e: (0, 0)),
            pl.BlockSpec((N_EXPERTS, 1), lambda i, e: (0, 0)),
            pl.BlockSpec((1, 1, D_MODEL, D_FF), lambda i, e: (layer, e, 0, 0)),
            pl.BlockSpec((1, 1, D_MODEL, D_FF), lambda i, e: (layer, e, 0, 0)),
            pl.BlockSpec((1, 1, D_FF, D_MODEL), lambda i, e: (layer, e, 0, 0)),
        ],
        out_specs=pl.BlockSpec((TM, D_MODEL), lambda i, e: (i, 0)),
        out_shape=jax.ShapeDtypeStruct((T_ALL, D_MODEL), F32),
        scratch_shapes=[pltpu.VMEM((TM, D_MODEL), BF16), pltpu.VMEM((TM, LANES), F32),
                        pltpu.VMEM((TM, D_MODEL), F32)],
        compiler_params=_params(2),
        name="moe_dense",
    )(x, gain, mods, wr_t, b_r, w1, w3, w2)


def _final_norm_kernel(x_ref, g_ref, o_ref):
    o_ref[...] = _rms(x_ref[...]) * g_ref[...]


def final_norm(x, gain, tile0, ntiles):
    return pl.pallas_call(
        _final_norm_kernel,
        grid=(ntiles,),
        in_specs=[pl.BlockSpec((TM, D_MODEL), lambda i: (tile0 + i, 0)),
                  pl.BlockSpec((1, D_MODEL), lambda i: (0, 0))],
        out_specs=pl.BlockSpec((TM, D_MODEL), lambda i: (i, 0)),
        out_shape=jax.ShapeDtypeStruct((ntiles * TM, D_MODEL), F32),
        compiler_params=_params(1),
        name="final_norm",
    )(x, gain.reshape(1, D_MODEL))


def kernel(x_prompt, x_sample, cache_attn_k, cache_attn_v, state_retention, c, c_ctx, w_ada, b_ada, norm1_g, norm2_g, final_g, w_in, attn_lambda, attn_subln_g, hy_conv_w, hy_conv_b, hy_f_w1, hy_f_b1, hy_f_w2, hy_f_b2, hy_f_w3, hy_skip, ret_decay_exp, w_branch_a, w_branch_b, w_branch_c, w_gate, b_gate, w_out, w_router, b_router, moe_w1, moe_w3, moe_w2):
    x = jnp.concatenate([x_prompt.reshape(T_CTX, D_MODEL), x_sample.reshape(T_LAT, D_MODEL)], axis=0)
    cond8 = jnp.concatenate([c_ctx[None, :], c, jnp.zeros((8 - 1 - DEC_BATCH, D_MODEL), F32)], axis=0)
    mods = ada_modulation(cond8, w_ada, b_ada).reshape(DEPTH, 8, 6, D_MODEL)

    cos, sin = _rope_tables()
    cos_c, sin_c = cos[:, :H_C * DK_C], sin[:, :H_C * DK_C]
    cache_k = cache_attn_k.reshape(DEC_BATCH, DEPTH, PAST_LEN, W_A)
    cache_v = cache_attn_v.reshape(DEC_BATCH, DEPTH, PAST_LEN, W_A)
    decay_rates = _filter_decay_rates()
    tables_ctx = _dft_tables(SEQ)
    tables_lat = _dft_tables(DEC_SEQ)
    filt_ctx = hyena_filters(SEQ, _filter_features(SEQ), decay_rates, tables_ctx[0], tables_ctx[1],
                             hy_f_w1, hy_f_b1, hy_f_w2, hy_f_b2, hy_f_w3)
    filt_lat = hyena_filters(DEC_SEQ, _filter_features(DEC_SEQ), decay_rates, tables_lat[0], tables_lat[1],
                             hy_f_w1, hy_f_b1, hy_f_w2, hy_f_b2, hy_f_w3)
    wr_t = w_router.T
    b_r = b_router.reshape(N_EXPERTS, 1)
    zero_bias = jnp.zeros((DEPTH, D_IN), F32)

    ks, vs, ss = [], [], []
    for l in range(DEPTH):
        h = norm_modulate(x, norm1_g, mods, l, 0)
        q = token_matmul(h, w_in, zero_bias, l, 0, W_A, BF16)
        kv = token_matmul(h, w_in, zero_bias, l, W_A, 2 * W_A, F32)
        r = token_matmul(h, w_in, zero_bias, l, 3 * W_A, D_IN - 3 * W_A, BF16)
        gates = token_matmul(h, w_gate, b_gate, l, 0, 3 * D_MODEL, BF16, act="sigmoid")

        ya = attention_ctx(q, kv, attn_lambda, attn_subln_g, l)
        ya = attention_lat(ya, q, kv, cache_k, cache_v, cos, sin, attn_lambda, attn_subln_g, l)
        yb = hyena(None, r, hy_conv_w, hy_conv_b, hy_skip, filt_ctx[0], filt_ctx[1], tables_ctx,
                   l, SEQ, 0, BATCH)
        yb = hyena(yb, r, hy_conv_w, hy_conv_b, hy_skip, filt_lat[0], filt_lat[1], tables_lat,
                   l, DEC_SEQ, T_CTX // DEC_SEQ, DEC_BATCH)
        yc, st = retention(None, r, ret_decay_exp, None, None, None, l, SEQ, 0, BATCH, False)
        yc = retention(yc, r, ret_decay_exp, cos_c, sin_c, state_retention, l, DEC_SEQ,
                       T_CTX // DEC_SEQ, DEC_BATCH, True)[0]

        x = merge_branches(x, ya, yb, yc, gates, mods, w_branch_a, w_branch_b, w_branch_c, w_out, l)
        x = moe_dense(x, norm2_g, mods, wr_t, b_r, moe_w1, moe_w3, moe_w2, l)

        ks.append(kv[:T_CTX, :W_A].reshape(BATCH, SEQ, H_A, 2 * DH_A))
        vs.append(kv[:T_CTX, W_A:].reshape(BATCH, SEQ, H_A, DV_A))
        ss.append(st)

    y_prompt = final_norm(x, final_g, 0, CTX_TILES).reshape(BATCH, SEQ, D_MODEL)
    y_sample = final_norm(x, final_g, CTX_TILES, N_TILES - CTX_TILES).reshape(DEC_BATCH, DEC_SEQ, D_MODEL)
    return (y_prompt, y_sample, jnp.stack(ks, axis=1), jnp.stack(vs, axis=1), jnp.stack(ss, axis=1))
```

```python
import functools
import math

import jax
import jax.numpy as jnp
import numpy as np
from jax import lax
from jax.experimental import pallas as pl
from jax.experimental.pallas import tpu as pltpu

F32 = jnp.float32
BF16 = jnp.bfloat16

D_MODEL = 1024
BATCH = 16
SEQ = 256
DEPTH = 4
DEC_BATCH = 2
DEC_SEQ = 1024
PAST_LEN = 256
GRID_W = 64
EPS = 1e-6
ROPE_BASE = 10000.0
H_A = 4
DH_A = 64
DV_A = 128
W_A = 512
C_B = 512
SHORT_K = 3
H_EMB = 33
H_FFN = 64
H_FAST_DECAY = 0.3
H_SLOW_DECAY = 1.5
H_TARGET = 1e-2
H_C = 4
DK_C = 64
DV_C = 128
W_C = 512
CHUNK = 128
N_EXPERTS = 16
N_GROUPS = 4
EXP_PER_GROUP = 4
D_FF = 512
D_IN = 4608

T_CTX = BATCH * SEQ
T_LAT = DEC_BATCH * DEC_SEQ
T_ALL = T_CTX + T_LAT
TM = 1024
N_TILES = T_ALL // TM
CTX_TILES = T_CTX // TM
LANES = 128
SUBLANES = 8
MOE_BLOCK = 128
MOE_BLOCKS = T_ALL // MOE_BLOCK + N_GROUPS
T_PAD = MOE_BLOCKS * MOE_BLOCK
VMEM_LIMIT = 56 * 1024 * 1024


def _params(n_axes):
    return pltpu.CompilerParams(
        dimension_semantics=("arbitrary",) * n_axes, vmem_limit_bytes=VMEM_LIMIT)


def _mod_row(i):
    return jnp.maximum(i - (CTX_TILES - 1), 0)


def _rms(x):
    return x * lax.rsqrt(jnp.mean(x * x, axis=-1, keepdims=True) + EPS)


def _silu(x):
    return x * jax.nn.sigmoid(x)


def _dot(a, b):
    return jnp.dot(a, b, preferred_element_type=F32)


def _dot_nt(a, b):
    return lax.dot_general(a, b, (((1,), (1,)), ((), ())), preferred_element_type=F32)


def _dot_tn(a, b):
    return lax.dot_general(a, b, (((0,), (0,)), ((), ())), preferred_element_type=F32)


def _split3(x):
    hi = x.astype(BF16)
    lo = (x - hi.astype(F32)).astype(BF16)
    return hi, lo


def _dot3(a, b):
    ah, al = _split3(a)
    bh, bl = _split3(b)
    return _dot(ah, bh) + (_dot(ah, bl) + _dot(al, bh))


def _split6(x):
    hi = x.astype(BF16)
    rest = x - hi.astype(F32)
    mid = rest.astype(BF16)
    lo = (rest - mid.astype(F32)).astype(BF16)
    return hi, mid, lo


def _dot6_nt(a, b):
    a1, a2, a3 = _split6(a)
    b1, b2, b3 = _split6(b)
    return (_dot_nt(a1, b1) + (_dot_nt(a1, b2) + _dot_nt(a2, b1))
            + (_dot_nt(a1, b3) + _dot_nt(a2, b2) + _dot_nt(a3, b1)))


def _ada_kernel(c_ref, w_ref, b_ref, o_ref):
    s = _silu(c_ref[...])
    o_ref[0] = _dot(s.astype(BF16), w_ref[0].astype(BF16)) + b_ref[0]


def ada_modulation(cond8, w_ada, b_ada):
    tn = 1536
    n = 6 * D_MODEL
    return pl.pallas_call(
        _ada_kernel,
        grid=(DEPTH, n // tn),
        in_specs=[
            pl.BlockSpec((8, D_MODEL), lambda l, j: (0, 0)),
            pl.BlockSpec((1, D_MODEL, tn), lambda l, j: (l, 0, j)),
            pl.BlockSpec((1, 1, tn), lambda l, j: (l, 0, j)),
        ],
        out_specs=pl.BlockSpec((1, 8, tn), lambda l, j: (l, 0, j)),
        out_shape=jax.ShapeDtypeStruct((DEPTH, 8, n), F32),
        compiler_params=_params(2),
        name="ada_modulation",
    )(cond8, w_ada, b_ada.reshape(DEPTH, 1, n))


def _normmod_kernel(x_ref, g_ref, m_ref, o_ref, *, layer, which):
    y = _rms(x_ref[...]) * g_ref[layer:layer + 1, :]
    sh = m_ref[0, 0, 3 * which:3 * which + 1, :]
    sc = m_ref[0, 0, 3 * which + 1:3 * which + 2, :]
    o_ref[...] = (y * (1.0 + sc) + sh).astype(o_ref.dtype)


def norm_modulate(x, gain, mods, layer, which):
    return pl.pallas_call(
        functools.partial(_normmod_kernel, layer=layer, which=which),
        grid=(N_TILES,),
        in_specs=[
            pl.BlockSpec((TM, D_MODEL), lambda i: (i, 0)),
            pl.BlockSpec((DEPTH, D_MODEL), lambda i: (0, 0)),
            pl.BlockSpec((1, 1, 6, D_MODEL), lambda i: (layer, _mod_row(i), 0, 0)),
        ],
        out_specs=pl.BlockSpec((TM, D_MODEL), lambda i: (i, 0)),
        out_shape=jax.ShapeDtypeStruct((T_ALL, D_MODEL), BF16),
        compiler_params=_params(1),
        name="norm_modulate",
    )(x, gain, mods)


def _mm_kernel(a_ref, w_ref, b_ref, o_ref, wbf_ref, *, act):
    @pl.when(pl.program_id(1) == 0)
    def _():
        wbf_ref[...] = w_ref[0].astype(BF16)

    acc = _dot(a_ref[...], wbf_ref[...]) + b_ref[0]
    if act == "sigmoid":
        acc = jax.nn.sigmoid(acc)
    o_ref[...] = acc.astype(o_ref.dtype)


def token_matmul(a, w, bias, layer, col0, ncols, out_dtype, act=None, tn=512):
    k = w.shape[1]
    ntot = w.shape[2]
    c0 = col0 // tn
    return pl.pallas_call(
        functools.partial(_mm_kernel, act=act),
        grid=(ncols // tn, N_TILES),
        in_specs=[
            pl.BlockSpec((TM, k), lambda j, i: (i, 0)),
            pl.BlockSpec((1, k, tn), lambda j, i: (layer, 0, c0 + j)),
            pl.BlockSpec((1, 1, tn), lambda j, i: (layer, 0, c0 + j)),
        ],
        out_specs=pl.BlockSpec((TM, tn), lambda j, i: (i, j)),
        out_shape=jax.ShapeDtypeStruct((T_ALL, ncols), out_dtype),
        scratch_shapes=[pltpu.VMEM((k, tn), BF16)],
        compiler_params=_params(2),
        name="token_matmul",
    )(a, w, bias.reshape(DEPTH, 1, ntot))


def _rope_tables():
    n = DEC_SEQ
    rows = n // GRID_W
    row = jnp.repeat(jnp.arange(rows, dtype=F32), GRID_W)
    col = jnp.tile(jnp.arange(GRID_W, dtype=F32), rows)
    quarter = DH_A // 4
    inv = ROPE_BASE ** (-jnp.arange(quarter, dtype=F32) / quarter)
    ar = row[:, None] * inv
    ac = col[:, None] * inv
    ang = jnp.concatenate([ar, ar, ac, ac], axis=-1)
    cos = jnp.tile(jnp.cos(ang), (1, 512 // DH_A))
    sgn = jnp.tile(jnp.concatenate([-jnp.ones((quarter,), F32), jnp.ones((quarter,), F32)]), 512 // (2 * quarter))
    sin = jnp.tile(jnp.sin(ang), (1, 512 // DH_A)) * sgn
    return cos, sin


def _rope(x, cos, sin):
    w = x.shape[-1]
    lane = lax.broadcasted_iota(jnp.int32, x.shape, 1)
    first = (lane % 32) < 16
    rot = jnp.where(first, pltpu.roll(x, w - 16, 1), pltpu.roll(x, 16, 1))
    return x * cos + rot * sin


def _lambda(al_ref, layer):
    al = al_ref[layer]
    a = jnp.sum(al[0:1, :] * al[1:2, :], axis=-1, keepdims=True)
    b = jnp.sum(al[2:3, :] * al[3:4, :], axis=-1, keepdims=True)
    lam_init = 0.8 - 0.6 * math.exp(-0.3 * layer)
    return jnp.exp(a) - jnp.exp(b) + lam_init, lam_init


def _softmax(s):
    m = jnp.max(s, axis=-1, keepdims=True)
    e = jnp.exp(s - m)
    return e / jnp.sum(e, axis=-1, keepdims=True)


def _diff_attention(q, k, v, lam, lam_init, subln, o_ref):
    lane = lax.broadcasted_iota(jnp.int32, q.shape, 1)
    scale = DH_A ** -0.5
    q = q * scale
    q0 = jnp.where((lane % LANES) < DH_A, q, 0.0).astype(BF16)
    q1 = jnp.where((lane % LANES) >= DH_A, q, 0.0).astype(BF16)
    for h in range(H_A):
        sl = slice(h * LANES, (h + 1) * LANES)
        kh = k[:, sl]
        p0 = _softmax(_dot_nt(q0[:, sl], kh))
        p1 = _softmax(_dot_nt(q1[:, sl], kh))
        a = (p0 - lam * p1).astype(BF16)
        o = _dot(a, v[:, sl])
        o = _rms(o) * subln * (1.0 - lam_init)
        o_ref[:, sl] = o.astype(o_ref.dtype)


def _attn_ctx_kernel(*refs, layer, n_caches):
    q_ref, k_ref, v_ref, al_ref, g_ref, o_ref, knew_ref, vnew_ref = refs[n_caches:]
    lam, lam_init = _lambda(al_ref, layer)
    subln = g_ref[layer:layer + 1, :]
    k = k_ref[...]
    v = v_ref[...]
    knew_ref[0, 0] = k
    vnew_ref[0, 0] = v
    _diff_attention(q_ref[...].astype(F32), k.astype(BF16), v.astype(BF16), lam, lam_init, subln, o_ref)


def attention_ctx(q, kv, attn_lambda, subln_g, layer, caches):
    cache_shape = jax.ShapeDtypeStruct((BATCH, DEPTH, SEQ, W_A), F32)
    cache_spec = pl.BlockSpec((1, 1, SEQ, W_A), lambda b: (b, layer, 0, 0))
    n_caches = len(caches)
    return pl.pallas_call(
        functools.partial(_attn_ctx_kernel, layer=layer, n_caches=n_caches),
        grid=(BATCH,),
        in_specs=[pl.BlockSpec(memory_space=pl.ANY)] * n_caches + [
            pl.BlockSpec((SEQ, W_A), lambda b: (b, 0)),
            pl.BlockSpec((SEQ, W_A), lambda b: (b, 0)),
            pl.BlockSpec((SEQ, W_A), lambda b: (b, 1)),
            pl.BlockSpec((DEPTH, 4, DH_A), lambda b: (0, 0, 0)),
            pl.BlockSpec((DEPTH, DV_A), lambda b: (0, 0)),
        ],
        out_specs=[pl.BlockSpec((SEQ, W_A), lambda b: (b, 0)), cache_spec, cache_spec],
        out_shape=[jax.ShapeDtypeStruct((T_ALL, W_A), BF16), cache_shape, cache_shape],
        input_output_aliases={i: i + 1 for i in range(n_caches)},
        compiler_params=_params(1),
        name="attention_ctx",
    )(*caches, q, kv, kv, attn_lambda, subln_g)


QB = 256


def _attn_lat_kernel(ya_ref, q_ref, k_ref, v_ref, ck_ref, cv_ref, cosq_ref, sinq_ref, cos_ref, sin_ref,
                     al_ref, g_ref, o_ref, kall_ref, vall_ref, *, layer):
    del ya_ref

    @pl.when(pl.program_id(1) == 0)
    def _():
        kall_ref[0:PAST_LEN, :] = ck_ref[0, 0].astype(BF16)
        vall_ref[0:PAST_LEN, :] = cv_ref[0, 0].astype(BF16)
        kall_ref[PAST_LEN:, :] = _rope(k_ref[...], cos_ref[...], sin_ref[...]).astype(BF16)
        vall_ref[PAST_LEN:, :] = v_ref[...].astype(BF16)

    lam, lam_init = _lambda(al_ref, layer)
    subln = g_ref[layer:layer + 1, :]
    q = _rope(q_ref[...].astype(F32), cosq_ref[...], sinq_ref[...])
    _diff_attention(q, kall_ref[...], vall_ref[...], lam, lam_init, subln, o_ref)


def attention_lat(ya, q, kv, cache_k, cache_v, cos, sin, attn_lambda, subln_g, layer):
    nqb = DEC_SEQ // QB
    row0 = T_CTX // QB
    seq0 = T_CTX // DEC_SEQ
    return pl.pallas_call(
        functools.partial(_attn_lat_kernel, layer=layer),
        grid=(DEC_BATCH, nqb),
        in_specs=[
            pl.BlockSpec(memory_space=pl.ANY),
            pl.BlockSpec((QB, W_A), lambda b, i: (row0 + b * nqb + i, 0)),
            pl.BlockSpec((DEC_SEQ, W_A), lambda b, i: (seq0 + b, 0)),
            pl.BlockSpec((DEC_SEQ, W_A), lambda b, i: (seq0 + b, 1)),
            pl.BlockSpec((1, 1, PAST_LEN, W_A), lambda b, i: (b, layer, 0, 0)),
            pl.BlockSpec((1, 1, PAST_LEN, W_A), lambda b, i: (b, layer, 0, 0)),
            pl.BlockSpec((QB, W_A), lambda b, i: (i, 0)),
            pl.BlockSpec((QB, W_A), lambda b, i: (i, 0)),
            pl.BlockSpec((DEC_SEQ, W_A), lambda b, i: (0, 0)),
            pl.BlockSpec((DEC_SEQ, W_A), lambda b, i: (0, 0)),
            pl.BlockSpec((DEPTH, 4, DH_A), lambda b, i: (0, 0, 0)),
            pl.BlockSpec((DEPTH, DV_A), lambda b, i: (0, 0)),
        ],
        out_specs=pl.BlockSpec((QB, W_A), lambda b, i: (row0 + b * nqb + i, 0)),
        out_shape=jax.ShapeDtypeStruct((T_ALL, W_A), BF16),
        scratch_shapes=[pltpu.VMEM((PAST_LEN + DEC_SEQ, W_A), BF16),
                        pltpu.VMEM((PAST_LEN + DEC_SEQ, W_A), BF16)],
        input_output_aliases={0: 0},
        compiler_params=_params(2),
        name="attention_lat",
    )(ya, q, kv, kv, cache_k, cache_v, cos, sin, cos, sin, attn_lambda, subln_g)


def _dft_tables(n):
    k = np.arange(n, dtype=np.int64)
    prod = (2 * k[:, None] + 1) * k[None, :]
    ang = (prod % (4 * n)).astype(np.float64) * (math.pi / (2 * n))
    c = np.cos(ang).astype(np.float32)
    s = np.sin(ang).astype(np.float32)
    return tuple(jnp.asarray(t).astype(BF16) for t in (c, s, c.T, s.T))


def _filter_features(n):
    t = jnp.linspace(0.0, 1.0, n, dtype=F32)[:, None]
    bands = (H_EMB - 1) // 2
    w = 2.0 * math.pi * jnp.arange(n, dtype=F32)[:, None] / n
    fr = jnp.linspace(1e-4, bands - 1, bands, dtype=F32)
    z = jnp.concatenate([t, jnp.cos(w * fr), -jnp.sin(w * fr)], axis=-1)
    return jnp.pad(z, ((0, 0), (0, LANES - H_EMB)))


def _filter_decay_rates():
    max_decay = math.log(H_TARGET) / H_FAST_DECAY
    min_decay = math.log(H_TARGET) / H_SLOW_DECAY
    return jnp.abs(jnp.linspace(min_decay, max_decay, C_B, dtype=F32))[None, :]


def _hyena_filter_kernel(z_ref, w1_ref, b1_ref, w2_ref, b2_ref, w3_ref, dr_ref, c_ref, s_ref,
                         gre_ref, gim_ref):
    z = z_ref[...]
    hid = jnp.sin(_dot3(z, w1_ref[0]) + b1_ref[0])
    hid = jnp.sin(_dot3(hid, w2_ref[0]) + b2_ref[0])
    h = _dot3(hid, w3_ref[0])
    window = jnp.exp(-z[:, 0:1] * dr_ref[...])
    hf = h[:, :C_B] * window
    hb = h[:, C_B:] * window
    total = jnp.sum(jnp.abs(hf) + jnp.abs(hb), axis=0, keepdims=True)
    hf = hf / total
    hb = hb / total
    row = lax.broadcasted_iota(jnp.int32, hb.shape, 0)
    hb = jnp.where(row == 0, 0.0, hb)
    ah, al = _split3(hf + hb)
    dh, dl = _split3(hb - hf)
    gre_ref[0] = _dot(c_ref[...], ah) + _dot(c_ref[...], al)
    gim_ref[0] = _dot(s_ref[...], dh) + _dot(s_ref[...], dl)


def hyena_filters(n, z, decay_rates, cf, sf, f_w1, f_b1, f_w2, f_b2, f_w3):
    w1 = jnp.pad(f_w1, ((0, 0), (0, LANES - H_EMB), (0, 0)))
    full = lambda shape: pl.BlockSpec(shape, lambda l: (0,) * len(shape))
    per_layer = lambda shape: pl.BlockSpec((1,) + shape, lambda l: (l,) + (0,) * len(shape))
    return pl.pallas_call(
        _hyena_filter_kernel,
        grid=(DEPTH,),
        in_specs=[
            full((n, LANES)),
            per_layer((LANES, H_FFN)), per_layer((1, H_FFN)),
            per_layer((H_FFN, H_FFN)), per_layer((1, H_FFN)),
            per_layer((H_FFN, 2 * C_B)),
            full((1, C_B)), full((n, n)), full((n, n)),
        ],
        out_specs=[per_layer((n, C_B)), per_layer((n, C_B))],
        out_shape=[jax.ShapeDtypeStruct((DEPTH, n, C_B), F32)] * 2,
        compiler_params=_params(1),
        name="hyena_filters",
    )(z, w1, f_b1.reshape(DEPTH, 1, H_FFN), f_w2, f_b2.reshape(DEPTH, 1, H_FFN), f_w3,
      decay_rates, cf, sf)


def _hyena_kernel(*refs, layer, n, aliased):
    if aliased:
        refs = refs[1:]
    u_ref, cw_ref, cb_ref, skip_ref, gre_ref, gim_ref, c_ref, s_ref, ct_ref, st_ref, o_ref = refs
    u = u_ref[...].astype(F32)
    row = lax.broadcasted_iota(jnp.int32, u.shape, 0)
    prev = jnp.where(row == 0, 0.0, pltpu.roll(u, 1, 0))
    nxt = jnp.where(row == n - 1, 0.0, pltpu.roll(u, n - 1, 0))
    cw = cw_ref[layer]
    uc = cb_ref[layer:layer + 1, :] + prev * cw[0:1, :] + u * cw[1:2, :] + nxt * cw[2:3, :]
    x0 = uc[:, :C_B]
    x1 = uc[:, C_B:2 * C_B]
    v = uc[:, 2 * C_B:]
    w = v * x1
    wb = w.astype(BF16)
    ure = _dot(c_ref[...], wb)
    uim = -_dot(s_ref[...], wb)
    gre = gre_ref[0]
    gim = gim_ref[0]
    yre = (ure * gre - uim * gim).astype(BF16)
    yim = (ure * gim + uim * gre).astype(BF16)
    y = (_dot(ct_ref[...], yre) - _dot(st_ref[...], yim)) * (1.0 / n)
    y = y + w * skip_ref[layer:layer + 1, :]
    o_ref[...] = (x0 * y).astype(o_ref.dtype)


def hyena(yb, r, conv_w, conv_b, skip, gre, gim, tables, layer, n, seq0, nseq):
    cf, sf, cft, sft = tables
    aliased = yb is not None
    full = lambda shape: pl.BlockSpec(shape, lambda b: (0,) * len(shape))
    in_specs = [
        pl.BlockSpec((n, 3 * C_B), lambda b: (seq0 + b, 0)),
        full((DEPTH, SHORT_K, 3 * C_B)), full((DEPTH, 3 * C_B)), full((DEPTH, C_B)),
        pl.BlockSpec((1, n, C_B), lambda b: (layer, 0, 0)),
        pl.BlockSpec((1, n, C_B), lambda b: (layer, 0, 0)),
        full((n, n)), full((n, n)), full((n, n)), full((n, n)),
    ]
    args = [r, conv_w, conv_b, skip, gre, gim, cf, sf, cft, sft]
    if aliased:
        in_specs = [pl.BlockSpec(memory_space=pl.ANY)] + in_specs
        args = [yb] + args
    return pl.pallas_call(
        functools.partial(_hyena_kernel, layer=layer, n=n, aliased=aliased),
        grid=(nseq,),
        in_specs=in_specs,
        out_specs=pl.BlockSpec((n, C_B), lambda b: (seq0 + b, 0)),
        out_shape=jax.ShapeDtypeStruct((T_ALL, C_B), BF16),
        input_output_aliases={0: 0} if aliased else {},
        compiler_params=_params(1),
        name="hyena",
    )(*args)


def _retention_kernel(*refs, layer, n, latent):
    if latent:
        (_, q_ref, k_ref, v_ref, g_ref, de_ref, cos_ref, sin_ref, s0_ref,
         o_ref, dm_ref, vec_ref, of_ref) = refs
        st_ref = None
    else:
        q_ref, k_ref, v_ref, g_ref, de_ref, o_ref, st_ref, dm_ref, vec_ref, of_ref = refs[-10:]
    nc = n // CHUNK

    @pl.when(pl.program_id(0) == 0)
    def _():
        de = de_ref[layer]
        lg = jnp.log1p(-jnp.exp2(-de))
        r = lax.broadcasted_iota(jnp.int32, (CHUNK, CHUNK), 0)
        c = lax.broadcasted_iota(jnp.int32, (CHUNK, CHUNK), 1)
        pos = lax.broadcasted_iota(jnp.int32, (CHUNK, 1), 0).astype(F32)
        for d in range(2):
            diff = (r - c) if d == 0 else (c - r)
            dpos = jnp.maximum(diff, 0).astype(F32)
            for h in range(H_C):
                g = lg[d:d + 1, h:h + 1]
                dm_ref[d, h] = jnp.where(diff >= 0, jnp.exp(g * dpos), 0.0)
                fwd_pos = pos if d == 0 else (CHUNK - 1.0) - pos
                xi = jnp.exp(g * (fwd_pos + 1.0))
                zeta = jnp.exp(g * (CHUNK - 1.0 - fwd_pos))
                gblk = jnp.exp(g * float(CHUNK)) + jnp.zeros((CHUNK, 1), F32)
                vec_ref[d, h] = jnp.concatenate(
                    [xi, zeta, gblk, jnp.zeros((CHUNK, LANES - 3), F32)], axis=1)

    q = q_ref[...].astype(F32)
    k = k_ref[...].astype(F32) * (DK_C ** -0.5)
    if latent:
        q = _rope(q, cos_ref[...], sin_ref[...])
        k = _rope(k, cos_ref[...], sin_ref[...])
    v = v_ref[...].astype(BF16)
    lane = lax.broadcasted_iota(jnp.int32, (CHUNK, LANES), 1)
    zeros64 = jnp.zeros((DK_C, DV_C), F32)

    for h in range(H_C):
        pair = slice((h // 2) * LANES, (h // 2 + 1) * LANES)
        lo = (h % 2) * DK_C
        own = (lane >= lo) & (lane < lo + DK_C)
        vs = slice(h * DV_C, (h + 1) * DV_C)
        for d in range(2):
            xi = vec_ref[d, h][:, 0:1]
            zeta = vec_ref[d, h][:, 1:2]
            gblk = vec_ref[d, h][0:1, 2:3]
            dmask = dm_ref[d, h]
            if latent:
                s0 = s0_ref[0, 0, d, h]
                state = jnp.concatenate([s0, zeros64] if lo == 0 else [zeros64, s0], axis=0)
            else:
                state = jnp.zeros((LANES, DV_C), F32)
            order = range(nc) if d == 0 else range(nc - 1, -1, -1)
            for ci in order:
                rows = slice(ci * CHUNK, (ci + 1) * CHUNK)
                qi = jnp.where(own, q[rows, pair], 0.0)
                ki = jnp.where(own, k[rows, pair], 0.0)
                vi = v[rows, vs]
                qb = qi.astype(BF16)
                att = _dot_nt(qb, ki.astype(BF16)) * dmask
                out = _dot(att.astype(BF16), vi) + _dot(qb, state.astype(BF16)) * xi
                state = gblk * state + _dot_tn((ki * zeta).astype(BF16), vi)
                if d == 0:
                    of_ref[rows, vs] = out
                else:
                    of_ref[rows, vs] = of_ref[rows, vs] + out
            if st_ref is not None:
                st_ref[0, 0, d, h] = state[lo:lo + DK_C, :]

    gate = g_ref[...].astype(F32)
    for h in range(H_C):
        vs = slice(h * DV_C, (h + 1) * DV_C)
        o_ref[:, vs] = (_rms(of_ref[:, vs]) * _silu(gate[:, vs])).astype(o_ref.dtype)


def retention(yc, r, decay_exp, cos, sin, state0, layer, n, seq0, nseq, latent):
    full = lambda shape: pl.BlockSpec(shape, lambda b: (0,) * len(shape))
    in_specs = [
        pl.BlockSpec((n, H_C * DK_C), lambda b: (seq0 + b, 6)),
        pl.BlockSpec((n, H_C * DK_C), lambda b: (seq0 + b, 7)),
        pl.BlockSpec((n, W_C), lambda b: (seq0 + b, 4)),
        pl.BlockSpec((n, W_C), lambda b: (seq0 + b, 5)),
        full((DEPTH, 2, H_C)),
    ]
    args = [r, r, r, r, decay_exp]
    out_specs = [pl.BlockSpec((n, W_C), lambda b: (seq0 + b, 0))]
    out_shape = [jax.ShapeDtypeStruct((T_ALL, W_C), BF16)]
    aliases = {}
    if latent:
        in_specs = [pl.BlockSpec(memory_space=pl.ANY)] + in_specs + [
            full((n, H_C * DK_C)), full((n, H_C * DK_C)),
            pl.BlockSpec((1, 1, 2, H_C, DK_C, DV_C), lambda b: (b, layer, 0, 0, 0, 0)),
        ]
        args = [yc] + args + [cos, sin, state0]
        aliases = {0: 0}
    else:
        out_specs.append(pl.BlockSpec((1, 1, 2, H_C, DK_C, DV_C), lambda b: (b, layer, 0, 0, 0, 0)))
        out_shape.append(jax.ShapeDtypeStruct((nseq, DEPTH, 2, H_C, DK_C, DV_C), F32))
        if state0 is not None:
            in_specs = [pl.BlockSpec(memory_space=pl.ANY)] + in_specs
            args = [state0] + args
            aliases = {0: 1}
    return pl.pallas_call(
        functools.partial(_retention_kernel, layer=layer, n=n, latent=latent),
        grid=(nseq,),
        in_specs=in_specs,
        out_specs=out_specs,
        out_shape=out_shape,
        scratch_shapes=[pltpu.VMEM((2, H_C, CHUNK, CHUNK), F32),
                        pltpu.VMEM((2, H_C, CHUNK, LANES), F32),
                        pltpu.VMEM((n, W_C), F32)],
        input_output_aliases=aliases,
        compiler_params=_params(1),
        name="retention",
    )(*args)


def _merge_kernel(x_ref, ya_ref, yb_ref, yc_ref, g_ref, m_ref, wa_ref, wb_ref, wc_ref, wo_ref,
                  o_ref, wbf_ref, wobf_ref):
    @pl.when(pl.program_id(0) == 0)
    def _():
        wbf_ref[0] = wa_ref[0].astype(BF16)
        wbf_ref[1] = wb_ref[0].astype(BF16)
        wbf_ref[2] = wc_ref[0].astype(BF16)
        wobf_ref[...] = wo_ref[0].astype(BF16)

    g = g_ref[...]
    merged = (g[:, :D_MODEL] * _dot(ya_ref[...], wbf_ref[0])
              + g[:, D_MODEL:2 * D_MODEL] * _dot(yb_ref[...], wbf_ref[1])
              + g[:, 2 * D_MODEL:] * _dot(yc_ref[...], wbf_ref[2]))
    g1 = m_ref[0, 0, 2:3, :]
    o_ref[...] = x_ref[...] + g1 * _dot(merged.astype(BF16), wobf_ref[...])


def merge_branches(x, ya, yb, yc, gates, mods, w_a, w_b, w_c, w_out, layer):
    tile = lambda w: pl.BlockSpec((TM, w), lambda i: (i, 0))
    wspec = lambda k: pl.BlockSpec((1, k, D_MODEL), lambda i: (layer, 0, 0))
    return pl.pallas_call(
        _merge_kernel,
        grid=(N_TILES,),
        in_specs=[
            tile(D_MODEL), tile(W_A), tile(C_B), tile(W_C), tile(3 * D_MODEL),
            pl.BlockSpec((1, 1, 6, D_MODEL), lambda i: (layer, _mod_row(i), 0, 0)),
            wspec(W_A), wspec(C_B), wspec(W_C), wspec(D_MODEL),
        ],
        out_specs=tile(D_MODEL),
        out_shape=jax.ShapeDtypeStruct((T_ALL, D_MODEL), F32),
        scratch_shapes=[pltpu.VMEM((3, W_A, D_MODEL), BF16), pltpu.VMEM((D_MODEL, D_MODEL), BF16)],
        compiler_params=_params(1),
        name="merge_branches",
    )(x, ya, yb, yc, gates, mods, w_a, w_b, w_c, w_out)


def _route(h2, wr_t, b_r):
    logits = _dot6_nt(wr_t, h2)
    m = jnp.max(logits, axis=0, keepdims=True)
    e = jnp.exp(logits - m)
    scores = e / jnp.sum(e, axis=0, keepdims=True)
    sel = scores + b_r
    rows = [sel[i:i + 1, :] for i in range(N_EXPERTS)]
    in_group = []
    gscore = []
    for g in range(N_GROUPS):
        members = range(g * EXP_PER_GROUP, (g + 1) * EXP_PER_GROUP)
        total = None
        for i in members:
            rank = None
            for j in members:
                if j == i:
                    continue
                ahead = (rows[j] >= rows[i]) if j < i else (rows[j] > rows[i])
                ahead = ahead.astype(F32)
                rank = ahead if rank is None else rank + ahead
            chosen = rank < 2.0
            in_group.append(chosen)
            part = jnp.where(chosen, rows[i], 0.0)
            total = part if total is None else total + part
        gscore.append(total)
    gates = []
    group_hot = []
    for g in range(N_GROUPS):
        best = None
        for g2 in range(N_GROUPS):
            if g2 == g:
                continue
            wins = gscore[g] > gscore[g2] if g2 < g else gscore[g] >= gscore[g2]
            best = wins if best is None else best & wins
        group_hot.append(best.astype(F32))
        for i in range(g * EXP_PER_GROUP, (g + 1) * EXP_PER_GROUP):
            gates.append(jnp.where(best & in_group[i], scores[i:i + 1, :], 0.0))
    gates = jnp.concatenate(gates, axis=0)
    return gates / jnp.sum(gates, axis=0, keepdims=True), group_hot


def _moe_route_kernel(x_ref, g_ref, m_ref, wr_ref, br_ref, tri_ref, tiles_ref, gate_ref, meta_ref, count_ref,
                      *, layer):
    y = _rms(x_ref[...]) * g_ref[layer:layer + 1, :]
    h2 = y * (1.0 + m_ref[0, 0, 4:5, :]) + m_ref[0, 0, 3:4, :]
    gates, group_hot = _route(h2, wr_ref[...], br_ref[...])
    for s in range(SUBLANES):
        tiles_ref[pl.ds(s, TM, stride=SUBLANES), :] = h2[:, s * LANES:(s + 1) * LANES]
    padded = jnp.concatenate([gates, jnp.zeros((LANES - N_EXPERTS, TM), F32)], axis=0)
    gate_ref[...] = padded.T
    row =lax.broadcasted_iota(jnp.int32, (SUBLANES, TM), 0)
    hot = jnp.zeros((SUBLANES, TM), F32)
    for g in range(N_GROUPS):
        hot = jnp.where(row == g, group_hot[g], hot)
    before = _dot(hot.astype(BF16), tri_ref[...])
    rank = jnp.sum(hot * before, axis=0, keepdims=True)
    gid = group_hot[1] + 2.0 * group_hot[2] + 3.0 * group_hot[3]
    meta = jnp.where(row == 0, gid, jnp.where(row == 1, rank, 0.0))
    meta_ref[0] = meta.astype(jnp.int32)
    counts = jnp.sum(hot, axis=1, keepdims=True) + jnp.zeros((SUBLANES, LANES), F32)
    count_ref[0] = counts.astype(jnp.int32)


def moe_route(x, gain, mods, wr_t, b_r, tri, layer):
    return pl.pallas_call(
        functools.partial(_moe_route_kernel, layer=layer),
        grid=(N_TILES,),
        in_specs=[
            pl.BlockSpec((TM, D_MODEL), lambda i: (i, 0)),
            pl.BlockSpec((DEPTH, D_MODEL), lambda i: (0, 0)),
            pl.BlockSpec((1, 1, 6, D_MODEL), lambda i: (layer, _mod_row(i), 0, 0)),
            pl.BlockSpec((N_EXPERTS, D_MODEL), lambda i: (0, 0)),
            pl.BlockSpec((N_EXPERTS, 1), lambda i: (0, 0)),
            pl.BlockSpec((TM, TM), lambda i: (0, 0)),
        ],
        out_specs=[pl.BlockSpec((TM * SUBLANES, LANES), lambda i: (i, 0)),
                   pl.BlockSpec((TM, LANES), lambda i: (i, 0)),
                   pl.BlockSpec((1, SUBLANES, TM), lambda i: (i, 0, 0)),
                   pl.BlockSpec((1, SUBLANES, LANES), lambda i: (i, 0, 0))],
        out_shape=[jax.ShapeDtypeStruct((T_ALL * SUBLANES, LANES), F32),
                   jax.ShapeDtypeStruct((T_ALL, LANES), F32),
                   jax.ShapeDtypeStruct((N_TILES, SUBLANES, TM), jnp.int32),
                   jax.ShapeDtypeStruct((N_TILES, SUBLANES, LANES), jnp.int32)],
        compiler_params=_params(1),
        name="moe_route",
    )(x, gain, mods, wr_t, b_r, tri)


def _moe_positions(meta, counts):
    gid = meta[:, 0, :]
    rank = meta[:, 1, :]
    cnt = counts[:, :N_GROUPS, 0]
    total = jnp.sum(cnt, axis=0)
    padded = (total + (MOE_BLOCK - 1)) // MOE_BLOCK * MOE_BLOCK
    group_start = jnp.cumsum(padded) - padded
    base = group_start[None, :] + jnp.cumsum(cnt, axis=0) - cnt
    pos = rank
    for g in range(N_GROUPS):
        pos = pos + jnp.where(gid == g, base[:, g:g + 1], 0)
    first_block = group_start // MOE_BLOCK
    blk = jnp.arange(MOE_BLOCKS, dtype=jnp.int32)
    block_gid = sum((blk >= first_block[g]).astype(jnp.int32) for g in range(1, N_GROUPS))
    return pos.reshape(T_ALL).astype(jnp.int32), block_gid


def _moe_permute_kernel(pos_ref, tiles_ref, gate_ref, sorted_ref, gsorted_ref):
    i = pl.program_id(0)

    @pl.when(i == 0)
    def _():
        def zero(b, carry):
            start = pl.multiple_of(b * TM, TM)
            sorted_ref[pl.ds(start, TM), :] = jnp.zeros((TM, LANES), F32)
            return carry
        lax.fori_loop(0, T_PAD * SUBLANES // TM, zero, 0)
        gsorted_ref[...] = jnp.zeros((T_PAD, LANES), F32)

    def move(t, carry):
        p = pos_ref[i * TM + t]
        dst = pl.multiple_of(p * SUBLANES, SUBLANES)
        src = pl.multiple_of(t * SUBLANES, SUBLANES)
        sorted_ref[pl.ds(dst, SUBLANES), :] = tiles_ref[pl.ds(src, SUBLANES), :]
        gsorted_ref[pl.ds(p, 1), :] = gate_ref[pl.ds(t, 1), :]
        return carry
    lax.fori_loop(0, TM, move, 0, unroll=8)


def moe_permute(pos, tiles, gates):
    return pl.pallas_call(
        _moe_permute_kernel,
        grid_spec=pltpu.PrefetchScalarGridSpec(
            num_scalar_prefetch=1,
            grid=(N_TILES,),
            in_specs=[pl.BlockSpec((TM * SUBLANES, LANES), lambda i, pos: (i, 0)),
                      pl.BlockSpec((TM, LANES), lambda i, pos: (i, 0))],
            out_specs=[pl.BlockSpec(memory_space=pltpu.VMEM), pl.BlockSpec(memory_space=pltpu.VMEM)],
        ),
        out_shape=[jax.ShapeDtypeStruct((T_PAD * SUBLANES, LANES), F32),
                   jax.ShapeDtypeStruct((T_PAD, LANES), F32)],
        compiler_params=_params(1),
        name="moe_permute",
    )(pos, tiles, gates)


def _group_changed(gid_ref, b):
    return (b == 0) | (gid_ref[b] != gid_ref[jnp.maximum(b - 1, 0)])


def _moe_up_kernel(gid_ref, s_ref, gate_ref, w1_ref, w3_ref, act_ref, w1b_ref, w3b_ref):
    j = pl.program_id(0)
    b = pl.program_id(1)

    @pl.when(_group_changed(gid_ref, b))
    def _():
        w1b_ref[...] = w1_ref[0, 0].astype(BF16)
        w3b_ref[...] = w3_ref[0, 0].astype(BF16)

    lhs = jnp.concatenate([s_ref[pl.ds(s, MOE_BLOCK, stride=SUBLANES), :].astype(BF16)
                           for s in range(SUBLANES)], axis=1)
    lane = lax.broadcasted_iota(jnp.int32, (MOE_BLOCK, LANES), 1)
    expert = gid_ref[b] * EXP_PER_GROUP + j
    gate = jnp.sum(jnp.where(lane == expert, gate_ref[...], 0.0), axis=1, keepdims=True)
    a = _dot(lhs, w1b_ref[...])
    g = _dot(lhs, w3b_ref[...])
    act_ref[...] = (_silu(a) * g * gate).astype(BF16)


def moe_up(block_gid, sorted_rows, sorted_gates, w1, w3, layer):
    wspec = pl.BlockSpec((1, 1, D_MODEL, D_FF),
                         lambda j, b, gid: (layer, gid[b] * EXP_PER_GROUP + j, 0, 0))
    return pl.pallas_call(
        _moe_up_kernel,
        grid_spec=pltpu.PrefetchScalarGridSpec(
            num_scalar_prefetch=1,
            grid=(EXP_PER_GROUP, MOE_BLOCKS),
            in_specs=[pl.BlockSpec((MOE_BLOCK * SUBLANES, LANES), lambda j, b, gid: (b, 0)),
                      pl.BlockSpec((MOE_BLOCK, LANES), lambda j, b, gid: (b, 0)), wspec, wspec],
            out_specs=pl.BlockSpec((MOE_BLOCK, D_FF), lambda j, b, gid: (b, j)),
            scratch_shapes=[pltpu.VMEM((D_MODEL, D_FF), BF16), pltpu.VMEM((D_MODEL, D_FF), BF16)],
        ),
        out_shape=jax.ShapeDtypeStruct((T_PAD, EXP_PER_GROUP * D_FF), BF16),
        compiler_params=_params(2),
        name="moe_up",
    )(block_gid, sorted_rows, sorted_gates, w1, w3)


def _moe_down_kernel(gid_ref, act_ref, w2_ref, y_ref, w2b_ref):
    b = pl.program_id(0)

    @pl.when(_group_changed(gid_ref, b))
    def _():
        w2b_ref[...] = w2_ref[0, 0].astype(BF16)

    y = _dot(act_ref[...], w2b_ref[...])
    for s in range(SUBLANES):
        y_ref[pl.ds(s, MOE_BLOCK, stride=SUBLANES), :] = y[:, s * LANES:(s + 1) * LANES]


def moe_down(block_gid, act, w2, layer):
    w2g = w2.reshape(DEPTH, N_GROUPS, EXP_PER_GROUP * D_FF, D_MODEL)
    return pl.pallas_call(
        _moe_down_kernel,
        grid_spec=pltpu.PrefetchScalarGridSpec(
            num_scalar_prefetch=1,
            grid=(MOE_BLOCKS,),
            in_specs=[
                pl.BlockSpec((MOE_BLOCK, EXP_PER_GROUP * D_FF), lambda b, gid: (b, 0)),
                pl.BlockSpec((1, 1, EXP_PER_GROUP * D_FF, D_MODEL), lambda b, gid: (layer, gid[b], 0, 0)),
            ],
            out_specs=pl.BlockSpec((MOE_BLOCK * SUBLANES, LANES), lambda b, gid: (b, 0)),
            scratch_shapes=[pltpu.VMEM((EXP_PER_GROUP * D_FF, D_MODEL), BF16)],
        ),
        out_shape=jax.ShapeDtypeStruct((T_PAD * SUBLANES, LANES), F32),
        compiler_params=_params(1),
        name="moe_down",
    )(block_gid, act, w2g)


def _moe_combine_kernel(pos_ref, ys_ref, x_ref, m_ref, o_ref, buf_ref):
    i = pl.program_id(0)

    def move(t, carry):
        src = pl.multiple_of(pos_ref[i * TM + t] * SUBLANES, SUBLANES)
        dst = pl.multiple_of(t * SUBLANES, SUBLANES)
        buf_ref[pl.ds(dst, SUBLANES), :] = ys_ref[pl.ds(src, SUBLANES), :]
        return carry
    lax.fori_loop(0, TM, move, 0, unroll=8)

    for s in range(SUBLANES):
        cols = slice(s * LANES, (s + 1) * LANES)
        y = buf_ref[pl.ds(s, TM, stride=SUBLANES), :]
        o_ref[:, cols] = x_ref[:, cols] + m_ref[0, 0, 5:6, cols] * y


def moe_combine(pos, y_sorted, x, mods, layer):
    return pl.pallas_call(
        _moe_combine_kernel,
        grid_spec=pltpu.PrefetchScalarGridSpec(
            num_scalar_prefetch=1,
            grid=(N_TILES,),
            in_specs=[
                pl.BlockSpec(memory_space=pltpu.VMEM),
                pl.BlockSpec((TM, D_MODEL), lambda i, pos: (i, 0)),
                pl.BlockSpec((1, 1, 6, D_MODEL), lambda i, pos: (layer, _mod_row(i), 0, 0)),
            ],
            out_specs=pl.BlockSpec((TM, D_MODEL), lambda i, pos: (i, 0)),
            scratch_shapes=[pltpu.VMEM((TM * SUBLANES, LANES), F32)],
        ),
        out_shape=jax.ShapeDtypeStruct((T_ALL, D_MODEL), F32),
        compiler_params=_params(1),
        name="moe_combine",
    )(pos, y_sorted, x, mods)


def moe(x, gain, mods, wr_t, b_r, tri, w1, w3, w2, layer):
    tiles, gates, meta, counts = moe_route(x, gain, mods, wr_t, b_r, tri, layer)
    pos, block_gid = _moe_positions(meta, counts)
    sorted_rows, sorted_gates = moe_permute(pos, tiles, gates)
    act = moe_up(block_gid, sorted_rows, sorted_gates, w1, w3, layer)
    y_sorted = moe_down(block_gid, act, w2, layer)
    return moe_combine(pos, y_sorted, x, mods, layer)


def _final_norm_kernel(x_ref, g_ref, o_ref):
    o_ref[...] = _rms(x_ref[...]) * g_ref[...]


def final_norm(x, gain, tile0, ntiles):
    return pl.pallas_call(
        _final_norm_kernel,
        grid=(ntiles,),
        in_specs=[pl.BlockSpec((TM, D_MODEL), lambda i: (tile0 + i, 0)),
                  pl.BlockSpec((1, D_MODEL), lambda i: (0, 0))],
        out_specs=pl.BlockSpec((TM, D_MODEL), lambda i: (i, 0)),
        out_shape=jax.ShapeDtypeStruct((ntiles * TM, D_MODEL), F32),
        compiler_params=_params(1),
        name="final_norm",
    )(x, gain.reshape(1, D_MODEL))


def kernel(x_prompt, x_sample, cache_attn_k, cache_attn_v, state_retention, c, c_ctx, w_ada, b_ada, norm1_g, norm2_g, final_g, w_in, attn_lambda, attn_subln_g, hy_conv_w, hy_conv_b, hy_f_w1, hy_f_b1, hy_f_w2, hy_f_b2, hy_f_w3, hy_skip, ret_decay_exp, w_branch_a, w_branch_b, w_branch_c, w_gate, b_gate, w_out, w_router, b_router, moe_w1, moe_w3, moe_w2):
    x = jnp.concatenate([x_prompt.reshape(T_CTX, D_MODEL), x_sample.reshape(T_LAT, D_MODEL)], axis=0)
    cond8 = jnp.concatenate([c_ctx[None, :], c, jnp.zeros((8 - 1 - DEC_BATCH, D_MODEL), F32)], axis=0)
    mods = ada_modulation(cond8, w_ada, b_ada).reshape(DEPTH, 8, 6, D_MODEL)

    cos, sin = _rope_tables()
    cos_c, sin_c = cos[:, :H_C * DK_C], sin[:, :H_C * DK_C]
    cache_k = cache_attn_k.reshape(DEC_BATCH, DEPTH, PAST_LEN, W_A)
    cache_v = cache_attn_v.reshape(DEC_BATCH, DEPTH, PAST_LEN, W_A)
    decay_rates = _filter_decay_rates()
    tables_ctx = _dft_tables(SEQ)
    tables_lat = _dft_tables(DEC_SEQ)
    filt_ctx = hyena_filters(SEQ, _filter_features(SEQ), decay_rates, tables_ctx[0], tables_ctx[1],
                             hy_f_w1, hy_f_b1, hy_f_w2, hy_f_b2, hy_f_w3)
    filt_lat = hyena_filters(DEC_SEQ, _filter_features(DEC_SEQ), decay_rates, tables_lat[0], tables_lat[1],
                             hy_f_w1, hy_f_b1, hy_f_w2, hy_f_b2, hy_f_w3)
    wr_t = w_router.T
    b_r = b_router.reshape(N_EXPERTS, 1)
    tri = jnp.asarray(np.triu(np.ones((TM, TM), np.float32), 1), dtype=BF16)
    zero_bias = jnp.zeros((DEPTH, D_IN), F32)

    caches = ()
    states = None
    for l in range(DEPTH):
        h = norm_modulate(x, norm1_g, mods, l, 0)
        q = token_matmul(h, w_in, zero_bias, l, 0, W_A, BF16)
        kv = token_matmul(h, w_in, zero_bias, l, W_A, 2 * W_A, F32)
        r = token_matmul(h, w_in, zero_bias, l, 3 * W_A, D_IN - 3 * W_A, BF16)
        gates = token_matmul(h, w_gate, b_gate, l, 0, 3 * D_MODEL, BF16, act="sigmoid")

        ya, *caches = attention_ctx(q, kv, attn_lambda, attn_subln_g, l, caches)
        ya = attention_lat(ya, q, kv, cache_k, cache_v, cos, sin, attn_lambda, attn_subln_g, l)
        yb = hyena(None, r, hy_conv_w, hy_conv_b, hy_skip, filt_ctx[0], filt_ctx[1], tables_ctx,
                   l, SEQ, 0, BATCH)
        yb = hyena(yb, r, hy_conv_w, hy_conv_b, hy_skip, filt_lat[0], filt_lat[1], tables_lat,
                   l, DEC_SEQ, T_CTX // DEC_SEQ, DEC_BATCH)
        yc, states = retention(None, r, ret_decay_exp, None, None, states, l, SEQ, 0, BATCH, False)
        yc = retention(yc, r, ret_decay_exp, cos_c, sin_c, state_retention, l, DEC_SEQ,
                       T_CTX // DEC_SEQ, DEC_BATCH, True)[0]

        x = merge_branches(x, ya, yb, yc, gates, mods, w_branch_a, w_branch_b, w_branch_c, w_out, l)
        x = moe(x, norm2_g, mods, wr_t, b_r, tri, moe_w1, moe_w3, moe_w2, l)

    new_k = caches[0].reshape(BATCH, DEPTH, SEQ, H_A, 2 * DH_A)
    new_v = caches[1].reshape(BATCH, DEPTH, SEQ, H_A, DV_A)
    y_prompt = final_norm(x, final_g, 0, CTX_TILES).reshape(BATCH, SEQ, D_MODEL)
    y_sample = final_norm(x, final_g, CTX_TILES, N_TILES - CTX_TILES).reshape(DEC_BATCH, DEC_SEQ, D_MODEL)
    return (y_prompt, y_sample, new_k, new_v, states)
```

```python
import functools
import math

import jax
import jax.numpy as jnp
import numpy as np
from jax import lax
from jax.experimental import pallas as pl
from jax.experimental.pallas import tpu as pltpu

F32 = jnp.float32
BF16 = jnp.bfloat16

D_MODEL = 1024
BATCH = 16
SEQ = 256
DEPTH = 4
DEC_BATCH = 2
DEC_SEQ = 1024
PAST_LEN = 256
GRID_W = 64
EPS = 1e-6
ROPE_BASE = 10000.0
H_A = 4
DH_A = 64
DV_A = 128
W_A = 512
C_B = 512
SHORT_K = 3
H_EMB = 33
H_FFN = 64
H_FAST_DECAY = 0.3
H_SLOW_DECAY = 1.5
H_TARGET = 1e-2
H_C = 4
DK_C = 64
DV_C = 128
W_C = 512
CHUNK = 128
N_EXPERTS = 16
N_GROUPS = 4
EXP_PER_GROUP = 4
D_FF = 512
D_IN = 4608

T_CTX = BATCH * SEQ
T_LAT = DEC_BATCH * DEC_SEQ
T_ALL = T_CTX + T_LAT
TM = 1024
N_TILES = T_ALL // TM
CTX_TILES = T_CTX // TM
LANES = 128
SUBLANES = 8
MOE_BLOCK = 256
MOE_BLOCKS = T_ALL // MOE_BLOCK + N_GROUPS
T_PAD = MOE_BLOCKS * MOE_BLOCK
VMEM_LIMIT = 56 * 1024 * 1024


def _params(n_axes):
    return pltpu.CompilerParams(
        dimension_semantics=("arbitrary",) * n_axes, vmem_limit_bytes=VMEM_LIMIT)


def _mod_row(i):
    return jnp.maximum(i - (CTX_TILES - 1), 0)


def _rms(x):
    return x * lax.rsqrt(jnp.mean(x * x, axis=-1, keepdims=True) + EPS)


def _silu(x):
    return x * jax.nn.sigmoid(x)


def _dot(a, b):
    return jnp.dot(a, b, preferred_element_type=F32)


def _dot_nt(a, b):
    return lax.dot_general(a, b, (((1,), (1,)), ((), ())), preferred_element_type=F32)


def _dot_tn(a, b):
    return lax.dot_general(a, b, (((0,), (0,)), ((), ())), preferred_element_type=F32)


def _split3(x):
    hi = x.astype(BF16)
    lo = (x - hi.astype(F32)).astype(BF16)
    return hi, lo


def _dot3(a, b):
    ah, al = _split3(a)
    bh, bl = _split3(b)
    return _dot(ah, bh) + (_dot(ah, bl) + _dot(al, bh))


def _split6(x):
    hi = x.astype(BF16)
    rest = x - hi.astype(F32)
    mid = rest.astype(BF16)
    lo = (rest - mid.astype(F32)).astype(BF16)
    return hi, mid, lo


def _dot6_nt(a, b):
    a1, a2, a3 = _split6(a)
    b1, b2, b3 = _split6(b)
    return (_dot_nt(a1, b1) + (_dot_nt(a1, b2) + _dot_nt(a2, b1))
            + (_dot_nt(a1, b3) + _dot_nt(a2, b2) + _dot_nt(a3, b1)))


def _ada_kernel(c_ref, w_ref, b_ref, o_ref):
    s = _silu(c_ref[...])
    o_ref[0] = _dot(s.astype(BF16), w_ref[0].astype(BF16)) + b_ref[0]


def ada_modulation(cond8, w_ada, b_ada):
    tn = 1536
    n = 6 * D_MODEL
    return pl.pallas_call(
        _ada_kernel,
        grid=(DEPTH, n // tn),
        in_specs=[
            pl.BlockSpec((8, D_MODEL), lambda l, j: (0, 0)),
            pl.BlockSpec((1, D_MODEL, tn), lambda l, j: (l, 0, j)),
            pl.BlockSpec((1, 1, tn), lambda l, j: (l, 0, j)),
        ],
        out_specs=pl.BlockSpec((1, 8, tn), lambda l, j: (l, 0, j)),
        out_shape=jax.ShapeDtypeStruct((DEPTH, 8, n), F32),
        compiler_params=_params(2),
        name="ada_modulation",
    )(cond8, w_ada, b_ada.reshape(DEPTH, 1, n))


PROJ_TM = 3072
PROJ_TN = 512
GATE_BLOCKS = 3 * D_MODEL // PROJ_TN
IN_BLOCKS = D_IN // PROJ_TN
P_WIDTH = 3 * D_MODEL + D_IN
KV_FIRST = GATE_BLOCKS + 1
P_Q = 3 * D_MODEL // W_A
P_HYENA = (3 * D_MODEL + 3 * W_A) // (3 * C_B)
P_RET_Q = (3 * D_MODEL + 3 * W_A + 3 * C_B) // (H_C * DK_C)
P_RET_V = (3 * D_MODEL + 3 * W_A + 3 * C_B + 2 * H_C * DK_C) // W_C


def _in_proj_kernel(x_ref, g_ref, m_ref, win_ref, wg_ref, bg_ref, p_ref, kv_ref, h_ref, *, layer):
    i = pl.program_id(0)
    j = pl.program_id(1)

    @pl.when(j == 0)
    def _():
        for s in range(PROJ_TM // TM):
            rows = slice(s * TM, (s + 1) * TM)
            mod = m_ref[layer, _mod_row(i * (PROJ_TM // TM) + s)]
            y = _rms(x_ref[rows, :]) * g_ref[layer:layer + 1, :]
            h_ref[rows, :] = (y * (1.0 + mod[1:2, :]) + mod[0:1, :]).astype(BF16)

    @pl.when(j < GATE_BLOCKS)
    def _():
        acc = _dot(h_ref[...], wg_ref[0].astype(BF16)) + bg_ref[0]
        p_ref[...] = jax.nn.sigmoid(acc).astype(BF16)

    @pl.when(j >= GATE_BLOCKS)
    def _():
        acc = _dot(h_ref[...], win_ref[0].astype(BF16))
        p_ref[...] = acc.astype(BF16)

        @pl.when((j == KV_FIRST) | (j == KV_FIRST + 1))
        def _():
            kv_ref[...] = acc


def in_projection(x, gain, mods, w_in, w_gate, b_gate, layer):
    return pl.pallas_call(
        functools.partial(_in_proj_kernel, layer=layer),
        grid=(T_ALL // PROJ_TM, GATE_BLOCKS + IN_BLOCKS),
        in_specs=[
            pl.BlockSpec((PROJ_TM, D_MODEL), lambda i, j: (i, 0), pipeline_mode=pl.Buffered(1)),
            pl.BlockSpec((DEPTH, D_MODEL), lambda i, j: (0, 0)),
            pl.BlockSpec((DEPTH, 8, 6, D_MODEL), lambda i, j: (0, 0, 0, 0)),
            pl.BlockSpec((1, D_MODEL, PROJ_TN), lambda i, j: (layer, 0, jnp.maximum(j - GATE_BLOCKS, 0))),
            pl.BlockSpec((1, D_MODEL, PROJ_TN), lambda i, j: (layer, 0, jnp.minimum(j, GATE_BLOCKS - 1))),
            pl.BlockSpec((1, 1, PROJ_TN), lambda i, j: (layer, 0, jnp.minimum(j, GATE_BLOCKS - 1))),
        ],
        out_specs=[
            pl.BlockSpec((PROJ_TM, PROJ_TN), lambda i, j: (i, j)),
            pl.BlockSpec((PROJ_TM, PROJ_TN), lambda i, j: (i, jnp.clip(j - KV_FIRST, 0, 1))),
        ],
        out_shape=[jax.ShapeDtypeStruct((T_ALL, P_WIDTH), BF16),
                   jax.ShapeDtypeStruct((T_ALL, 2 * W_A), F32)],
        scratch_shapes=[pltpu.VMEM((PROJ_TM, D_MODEL), BF16)],
        compiler_params=_params(2),
        name="in_projection",
    )(x, gain, mods, w_in, w_gate, b_gate.reshape(DEPTH, 1, 3 * D_MODEL))


def _rope_tables():
    n = DEC_SEQ
    rows = n // GRID_W
    row = jnp.repeat(jnp.arange(rows, dtype=F32), GRID_W)
    col = jnp.tile(jnp.arange(GRID_W, dtype=F32), rows)
    quarter = DH_A // 4
    inv = ROPE_BASE ** (-jnp.arange(quarter, dtype=F32) / quarter)
    ar = row[:, None] * inv
    ac = col[:, None] * inv
    ang = jnp.concatenate([ar, ar, ac, ac], axis=-1)
    cos = jnp.tile(jnp.cos(ang), (1, 512 // DH_A))
    sgn = jnp.tile(jnp.concatenate([-jnp.ones((quarter,), F32), jnp.ones((quarter,), F32)]), 512 // (2 * quarter))
    sin = jnp.tile(jnp.sin(ang), (1, 512 // DH_A)) * sgn
    return cos, sin


def _rope(x, cos, sin):
    w = x.shape[-1]
    lane = lax.broadcasted_iota(jnp.int32, x.shape, 1)
    first = (lane % 32) < 16
    rot = jnp.where(first, pltpu.roll(x, w - 16, 1), pltpu.roll(x, 16, 1))
    return x * cos + rot * sin


def _lambda(al_ref, layer):
    al = al_ref[layer]
    a = jnp.sum(al[0:1, :] * al[1:2, :], axis=-1, keepdims=True)
    b = jnp.sum(al[2:3, :] * al[3:4, :], axis=-1, keepdims=True)
    lam_init = 0.8 - 0.6 * math.exp(-0.3 * layer)
    return jnp.exp(a) - jnp.exp(b) + lam_init, lam_init


def _softmax(s):
    m = jnp.max(s, axis=-1, keepdims=True)
    e = jnp.exp(s - m)
    return e / jnp.sum(e, axis=-1, keepdims=True)


def _diff_attention(q, k, v, lam, lam_init, subln, o_ref):
    lane = lax.broadcasted_iota(jnp.int32, q.shape, 1)
    scale = DH_A ** -0.5
    q = q * scale
    q0 = jnp.where((lane % LANES) < DH_A, q, 0.0).astype(BF16)
    q1 = jnp.where((lane % LANES) >= DH_A, q, 0.0).astype(BF16)
    for h in range(H_A):
        sl = slice(h * LANES, (h + 1) * LANES)
        kh = k[:, sl]
        p0 = _softmax(_dot_nt(q0[:, sl], kh))
        p1 = _softmax(_dot_nt(q1[:, sl], kh))
        a = (p0 - lam * p1).astype(BF16)
        o = _dot(a, v[:, sl])
        o = _rms(o) * subln * (1.0 - lam_init)
        o_ref[:, sl] = o.astype(o_ref.dtype)


def _attn_ctx_kernel(*refs, layer, n_caches):
    q_ref, k_ref, v_ref, al_ref, g_ref, o_ref, knew_ref, vnew_ref = refs[n_caches:]
    lam, lam_init = _lambda(al_ref, layer)
    subln = g_ref[layer:layer + 1, :]
    k = k_ref[...]
    v = v_ref[...]
    knew_ref[0, 0] = k
    vnew_ref[0, 0] = v
    _diff_attention(q_ref[...].astype(F32), k.astype(BF16), v.astype(BF16), lam, lam_init, subln, o_ref)


def attention_ctx(q, kv, attn_lambda, subln_g, layer, caches):
    cache_shape = jax.ShapeDtypeStruct((BATCH, DEPTH, SEQ, W_A), F32)
    cache_spec = pl.BlockSpec((1, 1, SEQ, W_A), lambda b: (b, layer, 0, 0))
    n_caches = len(caches)
    return pl.pallas_call(
        functools.partial(_attn_ctx_kernel, layer=layer, n_caches=n_caches),
        grid=(BATCH,),
        in_specs=[pl.BlockSpec(memory_space=pl.ANY)] * n_caches + [
            pl.BlockSpec((SEQ, W_A), lambda b: (b, P_Q)),
            pl.BlockSpec((SEQ, W_A), lambda b: (b, 0)),
            pl.BlockSpec((SEQ, W_A), lambda b: (b, 1)),
            pl.BlockSpec((DEPTH, 4, DH_A), lambda b: (0, 0, 0)),
            pl.BlockSpec((DEPTH, DV_A), lambda b: (0, 0)),
        ],
        out_specs=[pl.BlockSpec((SEQ, W_A), lambda b: (b, 0)), cache_spec, cache_spec],
        out_shape=[jax.ShapeDtypeStruct((T_ALL, W_A), BF16), cache_shape, cache_shape],
        input_output_aliases={i: i + 1 for i in range(n_caches)},
        compiler_params=_params(1),
        name="attention_ctx",
    )(*caches, q, kv, kv, attn_lambda, subln_g)


QB = 256


def _attn_lat_kernel(ya_ref, q_ref, k_ref, v_ref, ck_ref, cv_ref, cosq_ref, sinq_ref, cos_ref, sin_ref,
                     al_ref, g_ref, o_ref, kall_ref, vall_ref, *, layer):
    del ya_ref

    @pl.when(pl.program_id(1) == 0)
    def _():
        kall_ref[0:PAST_LEN, :] = ck_ref[0, 0].astype(BF16)
        vall_ref[0:PAST_LEN, :] = cv_ref[0, 0].astype(BF16)
        kall_ref[PAST_LEN:, :] = _rope(k_ref[...], cos_ref[...], sin_ref[...]).astype(BF16)
        vall_ref[PAST_LEN:, :] = v_ref[...].astype(BF16)

    lam, lam_init = _lambda(al_ref, layer)
    subln = g_ref[layer:layer + 1, :]
    q = _rope(q_ref[...].astype(F32), cosq_ref[...], sinq_ref[...])
    _diff_attention(q, kall_ref[...], vall_ref[...], lam, lam_init, subln, o_ref)


def attention_lat(ya, q, kv, cache_k, cache_v, cos, sin, attn_lambda, subln_g, layer):
    nqb = DEC_SEQ // QB
    row0 = T_CTX // QB
    seq0 = T_CTX // DEC_SEQ
    return pl.pallas_call(
        functools.partial(_attn_lat_kernel, layer=layer),
        grid=(DEC_BATCH, nqb),
        in_specs=[
            pl.BlockSpec(memory_space=pl.ANY),
            pl.BlockSpec((QB, W_A), lambda b, i: (row0 + b * nqb + i, P_Q)),
            pl.BlockSpec((DEC_SEQ, W_A), lambda b, i: (seq0 + b, 0)),
            pl.BlockSpec((DEC_SEQ, W_A), lambda b, i: (seq0 + b, 1)),
            pl.BlockSpec((1, 1, PAST_LEN, W_A), lambda b, i: (b, layer, 0, 0)),
            pl.BlockSpec((1, 1, PAST_LEN, W_A), lambda b, i: (b, layer, 0, 0)),
            pl.BlockSpec((QB, W_A), lambda b, i: (i, 0)),
            pl.BlockSpec((QB, W_A), lambda b, i: (i, 0)),
            pl.BlockSpec((DEC_SEQ, W_A), lambda b, i: (0, 0)),
            pl.BlockSpec((DEC_SEQ, W_A), lambda b, i: (0, 0)),
            pl.BlockSpec((DEPTH, 4, DH_A), lambda b, i: (0, 0, 0)),
            pl.BlockSpec((DEPTH, DV_A), lambda b, i: (0, 0)),
        ],
        out_specs=pl.BlockSpec((QB, W_A), lambda b, i: (row0 + b * nqb + i, 0)),
        out_shape=jax.ShapeDtypeStruct((T_ALL, W_A), BF16),
        scratch_shapes=[pltpu.VMEM((PAST_LEN + DEC_SEQ, W_A), BF16),
                        pltpu.VMEM((PAST_LEN + DEC_SEQ, W_A), BF16)],
        input_output_aliases={0: 0},
        compiler_params=_params(2),
        name="attention_lat",
    )(ya, q, kv, kv, cache_k, cache_v, cos, sin, cos, sin, attn_lambda, subln_g)


def _dft_tables(n):
    k = np.arange(n, dtype=np.int64)
    prod = (2 * k[:, None] + 1) * k[None, :]
    ang = (prod % (4 * n)).astype(np.float64) * (math.pi / (2 * n))
    c = np.cos(ang).astype(np.float32)
    s = np.sin(ang).astype(np.float32)
    return tuple(jnp.asarray(t).astype(BF16) for t in (c, s, c.T, s.T))


def _filter_features(n):
    t = jnp.linspace(0.0, 1.0, n, dtype=F32)[:, None]
    bands = (H_EMB - 1) // 2
    w = 2.0 * math.pi * jnp.arange(n, dtype=F32)[:, None] / n
    fr = jnp.linspace(1e-4, bands - 1, bands, dtype=F32)
    z = jnp.concatenate([t, jnp.cos(w * fr), -jnp.sin(w * fr)], axis=-1)
    return jnp.pad(z, ((0, 0), (0, LANES - H_EMB)))


def _filter_decay_rates():
    max_decay = math.log(H_TARGET) / H_FAST_DECAY
    min_decay = math.log(H_TARGET) / H_SLOW_DECAY
    return jnp.abs(jnp.linspace(min_decay, max_decay, C_B, dtype=F32))[None, :]


def _hyena_filter_kernel(z_ref, w1_ref, b1_ref, w2_ref, b2_ref, w3_ref, dr_ref, c_ref, s_ref,
                         gre_ref, gim_ref):
    z = z_ref[...]
    hid = jnp.sin(_dot3(z, w1_ref[0]) + b1_ref[0])
    hid = jnp.sin(_dot3(hid, w2_ref[0]) + b2_ref[0])
    h = _dot3(hid, w3_ref[0])
    window = jnp.exp(-z[:, 0:1] * dr_ref[...])
    hf = h[:, :C_B] * window
    hb = h[:, C_B:] * window
    total = jnp.sum(jnp.abs(hf) + jnp.abs(hb), axis=0, keepdims=True)
    hf = hf / total
    hb = hb / total
    row = lax.broadcasted_iota(jnp.int32, hb.shape, 0)
    hb = jnp.where(row == 0, 0.0, hb)
    ah, al = _split3(hf + hb)
    dh, dl = _split3(hb - hf)
    gre_ref[0] = _dot(c_ref[...], ah) + _dot(c_ref[...], al)
    gim_ref[0] = _dot(s_ref[...], dh) + _dot(s_ref[...], dl)


def hyena_filters(n, z, decay_rates, cf, sf, f_w1, f_b1, f_w2, f_b2, f_w3):
    w1 = jnp.pad(f_w1, ((0, 0), (0, LANES - H_EMB), (0, 0)))
    full = lambda shape: pl.BlockSpec(shape, lambda l: (0,) * len(shape))
    per_layer = lambda shape: pl.BlockSpec((1,) + shape, lambda l: (l,) + (0,) * len(shape))
    return pl.pallas_call(
        _hyena_filter_kernel,
        grid=(DEPTH,),
        in_specs=[
            full((n, LANES)),
            per_layer((LANES, H_FFN)), per_layer((1, H_FFN)),
            per_layer((H_FFN, H_FFN)), per_layer((1, H_FFN)),
            per_layer((H_FFN, 2 * C_B)),
            full((1, C_B)), full((n, n)), full((n, n)),
        ],
        out_specs=[per_layer((n, C_B)), per_layer((n, C_B))],
        out_shape=[jax.ShapeDtypeStruct((DEPTH, n, C_B), F32)] * 2,
        compiler_params=_params(1),
        name="hyena_filters",
    )(z, w1, f_b1.reshape(DEPTH, 1, H_FFN), f_w2, f_b2.reshape(DEPTH, 1, H_FFN), f_w3,
      decay_rates, cf, sf)


def _hyena_kernel(*refs, layer, n, aliased):
    if aliased:
        refs = refs[1:]
    u_ref, cw_ref, cb_ref, skip_ref, gre_ref, gim_ref, c_ref, s_ref, ct_ref, st_ref, o_ref = refs
    u = u_ref[...].astype(F32)
    row = lax.broadcasted_iota(jnp.int32, u.shape, 0)
    prev = jnp.where(row == 0, 0.0, pltpu.roll(u, 1, 0))
    nxt = jnp.where(row == n - 1, 0.0, pltpu.roll(u, n - 1, 0))
    cw = cw_ref[layer]
    uc = cb_ref[layer:layer + 1, :] + prev * cw[0:1, :] + u * cw[1:2, :] + nxt * cw[2:3, :]
    x0 = uc[:, :C_B]
    x1 = uc[:, C_B:2 * C_B]
    v = uc[:, 2 * C_B:]
    w = v * x1
    wb = w.astype(BF16)
    ure = _dot(c_ref[...], wb)
    uim = -_dot(s_ref[...], wb)
    gre = gre_ref[0]
    gim = gim_ref[0]
    yre = (ure * gre - uim * gim).astype(BF16)
    yim = (ure * gim + uim * gre).astype(BF16)
    y = (_dot(ct_ref[...], yre) - _dot(st_ref[...], yim)) * (1.0 / n)
    y = y + w * skip_ref[layer:layer + 1, :]
    o_ref[...] = (x0 * y).astype(o_ref.dtype)


def hyena(yb, r, conv_w, conv_b, skip, gre, gim, tables, layer, n, seq0, nseq):
    cf, sf, cft, sft = tables
    aliased = yb is not None
    full = lambda shape: pl.BlockSpec(shape, lambda b: (0,) * len(shape))
    in_specs = [
        pl.BlockSpec((n, 3 * C_B), lambda b: (seq0 + b, P_HYENA)),
        full((DEPTH, SHORT_K, 3 * C_B)), full((DEPTH, 3 * C_B)), full((DEPTH, C_B)),
        pl.BlockSpec((1, n, C_B), lambda b: (layer, 0, 0)),
        pl.BlockSpec((1, n, C_B), lambda b: (layer, 0, 0)),
        full((n, n)), full((n, n)), full((n, n)), full((n, n)),
    ]
    args = [r, conv_w, conv_b, skip, gre, gim, cf, sf, cft, sft]
    if aliased:
        in_specs = [pl.BlockSpec(memory_space=pl.ANY)] + in_specs
        args = [yb] + args
    return pl.pallas_call(
        functools.partial(_hyena_kernel, layer=layer, n=n, aliased=aliased),
        grid=(nseq,),
        in_specs=in_specs,
        out_specs=pl.BlockSpec((n, C_B), lambda b: (seq0 + b, 0)),
        out_shape=jax.ShapeDtypeStruct((T_ALL, C_B), BF16),
        input_output_aliases={0: 0} if aliased else {},
        compiler_params=_params(1),
        name="hyena",
    )(*args)


def _retention_kernel(*refs, layer, n, latent):
    if latent:
        (_, q_ref, k_ref, v_ref, g_ref, de_ref, cos_ref, sin_ref, s0_ref,
         o_ref, dm_ref, vec_ref, of_ref) = refs
        st_ref = None
    else:
        q_ref, k_ref, v_ref, g_ref, de_ref, o_ref, st_ref, dm_ref, vec_ref, of_ref = refs[-10:]
    nc = n // CHUNK

    @pl.when(pl.program_id(0) == 0)
    def _():
        de = de_ref[layer]
        lg = jnp.log1p(-jnp.exp2(-de))
        r = lax.broadcasted_iota(jnp.int32, (CHUNK, CHUNK), 0)
        c = lax.broadcasted_iota(jnp.int32, (CHUNK, CHUNK), 1)
        pos = lax.broadcasted_iota(jnp.int32, (CHUNK, 1), 0).astype(F32)
        for d in range(2):
            diff = (r - c) if d == 0 else (c - r)
            dpos = jnp.maximum(diff, 0).astype(F32)
            for h in range(H_C):
                g = lg[d:d + 1, h:h + 1]
                dm_ref[d, h] = jnp.where(diff >= 0, jnp.exp(g * dpos), 0.0)
                fwd_pos = pos if d == 0 else (CHUNK - 1.0) - pos
                xi = jnp.exp(g * (fwd_pos + 1.0))
                zeta = jnp.exp(g * (CHUNK - 1.0 - fwd_pos))
                gblk = jnp.exp(g * float(CHUNK)) + jnp.zeros((CHUNK, 1), F32)
                vec_ref[d, h] = jnp.concatenate(
                    [xi, zeta, gblk, jnp.zeros((CHUNK, LANES - 3), F32)], axis=1)

    q = q_ref[...].astype(F32)
    k = k_ref[...].astype(F32) * (DK_C ** -0.5)
    if latent:
        q = _rope(q, cos_ref[...], sin_ref[...])
        k = _rope(k, cos_ref[...], sin_ref[...])
    v = v_ref[...].astype(BF16)
    lane = lax.broadcasted_iota(jnp.int32, (CHUNK, LANES), 1)
    zeros64 = jnp.zeros((DK_C, DV_C), F32)

    for h in range(H_C):
        pair = slice((h // 2) * LANES, (h // 2 + 1) * LANES)
        lo = (h % 2) * DK_C
        own = (lane >= lo) & (lane < lo + DK_C)
        vs = slice(h * DV_C, (h + 1) * DV_C)
        for d in range(2):
            xi = vec_ref[d, h][:, 0:1]
            zeta = vec_ref[d, h][:, 1:2]
            gblk = vec_ref[d, h][0:1, 2:3]
            dmask = dm_ref[d, h]
            if latent:
                s0 = s0_ref[0, 0, d, h]
                state = jnp.concatenate([s0, zeros64] if lo == 0 else [zeros64, s0], axis=0)
            else:
                state = jnp.zeros((LANES, DV_C), F32)
            order = range(nc) if d == 0 else range(nc - 1, -1, -1)
            for ci in order:
                rows = slice(ci * CHUNK, (ci + 1) * CHUNK)
                qi = jnp.where(own, q[rows, pair], 0.0)
                ki = jnp.where(own, k[rows, pair], 0.0)
                vi = v[rows, vs]
                qb = qi.astype(BF16)
                att = _dot_nt(qb, ki.astype(BF16)) * dmask
                out = _dot(att.astype(BF16), vi) + _dot(qb, state.astype(BF16)) * xi
                state = gblk * state + _dot_tn((ki * zeta).astype(BF16), vi)
                if d == 0:
                    of_ref[rows, vs] = out
                else:
                    of_ref[rows, vs] = of_ref[rows, vs] + out
            if st_ref is not None:
                st_ref[0, 0, d, h] = state[lo:lo + DK_C, :]

    gate = g_ref[...].astype(F32)
    for h in range(H_C):
        vs = slice(h * DV_C, (h + 1) * DV_C)
        o_ref[:, vs] = (_rms(of_ref[:, vs]) * _silu(gate[:, vs])).astype(o_ref.dtype)


def retention(yc, r, decay_exp, cos, sin, state0, layer, n, seq0, nseq, latent):
    full = lambda shape: pl.BlockSpec(shape, lambda b: (0,) * len(shape))
    in_specs = [
        pl.BlockSpec((n, H_C * DK_C), lambda b: (seq0 + b, P_RET_Q)),
        pl.BlockSpec((n, H_C * DK_C), lambda b: (seq0 + b, P_RET_Q + 1)),
        pl.BlockSpec((n, W_C), lambda b: (seq0 + b, P_RET_V)),
        pl.BlockSpec((n, W_C), lambda b: (seq0 + b, P_RET_V + 1)),
        full((DEPTH, 2, H_C)),
    ]
    args = [r, r, r, r, decay_exp]
    out_specs = [pl.BlockSpec((n, W_C), lambda b: (seq0 + b, 0))]
    out_shape = [jax.ShapeDtypeStruct((T_ALL, W_C), BF16)]
    aliases = {}
    if latent:
        in_specs = [pl.BlockSpec(memory_space=pl.ANY)] + in_specs + [
            full((n, H_C * DK_C)), full((n, H_C * DK_C)),
            pl.BlockSpec((1, 1, 2, H_C, DK_C, DV_C), lambda b: (b, layer, 0, 0, 0, 0)),
        ]
        args = [yc] + args + [cos, sin, state0]
        aliases = {0: 0}
    else:
        out_specs.append(pl.BlockSpec((1, 1, 2, H_C, DK_C, DV_C), lambda b: (b, layer, 0, 0, 0, 0)))
        out_shape.append(jax.ShapeDtypeStruct((nseq, DEPTH, 2, H_C, DK_C, DV_C), F32))
        if state0 is not None:
            in_specs = [pl.BlockSpec(memory_space=pl.ANY)] + in_specs
            args = [state0] + args
            aliases = {0: 1}
    return pl.pallas_call(
        functools.partial(_retention_kernel, layer=layer, n=n, latent=latent),
        grid=(nseq,),
        in_specs=in_specs,
        out_specs=out_specs,
        out_shape=out_shape,
        scratch_shapes=[pltpu.VMEM((2, H_C, CHUNK, CHUNK), F32),
                        pltpu.VMEM((2, H_C, CHUNK, LANES), F32),
                        pltpu.VMEM((n, W_C), F32)],
        input_output_aliases=aliases,
        compiler_params=_params(1),
        name="retention",
    )(*args)


def _merge_kernel(x_ref, ya_ref, yb_ref, yc_ref, g_ref, m_ref, wa_ref, wb_ref, wc_ref, wo_ref,
                  o_ref, wbf_ref, wobf_ref):
    @pl.when(pl.program_id(0) == 0)
    def _():
        wbf_ref[0] = wa_ref[0].astype(BF16)
        wbf_ref[1] = wb_ref[0].astype(BF16)
        wbf_ref[2] = wc_ref[0].astype(BF16)
        wobf_ref[...] = wo_ref[0].astype(BF16)

    g = g_ref[...]
    merged = (g[:, :D_MODEL] * _dot(ya_ref[...], wbf_ref[0])
              + g[:, D_MODEL:2 * D_MODEL] * _dot(yb_ref[...], wbf_ref[1])
              + g[:, 2 * D_MODEL:] * _dot(yc_ref[...], wbf_ref[2]))
    g1 = m_ref[0, 0, 2:3, :]
    o_ref[...] = x_ref[...] + g1 * _dot(merged.astype(BF16), wobf_ref[...])


def merge_branches(x, ya, yb, yc, gates, mods, w_a, w_b, w_c, w_out, layer):
    tile = lambda w: pl.BlockSpec((TM, w), lambda i: (i, 0))
    wspec = lambda k: pl.BlockSpec((1, k, D_MODEL), lambda i: (layer, 0, 0))
    return pl.pallas_call(
        _merge_kernel,
        grid=(N_TILES,),
        in_specs=[
            tile(D_MODEL), tile(W_A), tile(C_B), tile(W_C), tile(3 * D_MODEL),
            pl.BlockSpec((1, 1, 6, D_MODEL), lambda i: (layer, _mod_row(i), 0, 0)),
            wspec(W_A), wspec(C_B), wspec(W_C), wspec(D_MODEL),
        ],
        out_specs=tile(D_MODEL),
        out_shape=jax.ShapeDtypeStruct((T_ALL, D_MODEL), F32),
        scratch_shapes=[pltpu.VMEM((3, W_A, D_MODEL), BF16), pltpu.VMEM((D_MODEL, D_MODEL), BF16)],
        compiler_params=_params(1),
        name="merge_branches",
    )(x, ya, yb, yc, gates, mods, w_a, w_b, w_c, w_out)


def _route(h2, wr_t, b_r):
    logits = _dot6_nt(wr_t, h2)
    m = jnp.max(logits, axis=0, keepdims=True)
    e = jnp.exp(logits - m)
    scores = e / jnp.sum(e, axis=0, keepdims=True)
    sel = scores + b_r
    rows = [sel[i:i + 1, :] for i in range(N_EXPERTS)]
    in_group = []
    gscore = []
    for g in range(N_GROUPS):
        members = range(g * EXP_PER_GROUP, (g + 1) * EXP_PER_GROUP)
        total = None
        for i in members:
            rank = None
            for j in members:
                if j == i:
                    continue
                ahead = (rows[j] >= rows[i]) if j < i else (rows[j] > rows[i])
                ahead = ahead.astype(F32)
                rank = ahead if rank is None else rank + ahead
            chosen = rank < 2.0
            in_group.append(chosen)
            part = jnp.where(chosen, rows[i], 0.0)
            total = part if total is None else total + part
        gscore.append(total)
    gates = []
    group_hot = []
    for g in range(N_GROUPS):
        best = None
        for g2 in range(N_GROUPS):
            if g2 == g:
                continue
            wins = gscore[g] > gscore[g2] if g2 < g else gscore[g] >= gscore[g2]
            best = wins if best is None else best & wins
        group_hot.append(best.astype(F32))
        for i in range(g * EXP_PER_GROUP, (g + 1) * EXP_PER_GROUP):
            gates.append(jnp.where(best & in_group[i], scores[i:i + 1, :], 0.0))
    gates = jnp.concatenate(gates, axis=0)
    return gates / jnp.sum(gates, axis=0, keepdims=True), group_hot


def _moe_route_kernel(x_ref, g_ref, m_ref, wr_ref, br_ref, tri_ref, tiles_ref, gate_ref, meta_ref, count_ref,
                      *, layer):
    y = _rms(x_ref[...]) * g_ref[layer:layer + 1, :]
    h2 = y * (1.0 + m_ref[0, 0, 4:5, :]) + m_ref[0, 0, 3:4, :]
    gates, group_hot = _route(h2, wr_ref[...], br_ref[...])
    for s in range(SUBLANES):
        tiles_ref[pl.ds(s, TM, stride=SUBLANES), :] = h2[:, s * LANES:(s + 1) * LANES]
    padded = jnp.concatenate([gates, jnp.zeros((LANES - N_EXPERTS, TM), F32)], axis=0)
    gate_ref[...] = padded.T
    row =lax.broadcasted_iota(jnp.int32, (SUBLANES, TM), 0)
    hot = jnp.zeros((SUBLANES, TM), F32)
    for g in range(N_GROUPS):
        hot = jnp.where(row == g, group_hot[g], hot)
    before = _dot(hot.astype(BF16), tri_ref[...])
    rank = jnp.sum(hot * before, axis=0, keepdims=True)
    gid = group_hot[1] + 2.0 * group_hot[2] + 3.0 * group_hot[3]
    meta = jnp.where(row == 0, gid, jnp.where(row == 1, rank, 0.0))
    meta_ref[0] = meta.astype(jnp.int32)
    counts = jnp.sum(hot, axis=1, keepdims=True) + jnp.zeros((SUBLANES, LANES), F32)
    count_ref[0] = counts.astype(jnp.int32)


def moe_route(x, gain, mods, wr_t, b_r, tri, layer):
    return pl.pallas_call(
        functools.partial(_moe_route_kernel, layer=layer),
        grid=(N_TILES,),
        in_specs=[
            pl.BlockSpec((TM, D_MODEL), lambda i: (i, 0)),
            pl.BlockSpec((DEPTH, D_MODEL), lambda i: (0, 0)),
            pl.BlockSpec((1, 1, 6, D_MODEL), lambda i: (layer, _mod_row(i), 0, 0)),
            pl.BlockSpec((N_EXPERTS, D_MODEL), lambda i: (0, 0)),
            pl.BlockSpec((N_EXPERTS, 1), lambda i: (0, 0)),
            pl.BlockSpec((TM, TM), lambda i: (0, 0)),
        ],
        out_specs=[pl.BlockSpec((TM * SUBLANES, LANES), lambda i: (i, 0)),
                   pl.BlockSpec((TM, LANES), lambda i: (i, 0)),
                   pl.BlockSpec((1, SUBLANES, TM), lambda i: (i, 0, 0)),
                   pl.BlockSpec((1, SUBLANES, LANES), lambda i: (i, 0, 0))],
        out_shape=[jax.ShapeDtypeStruct((T_ALL * SUBLANES, LANES), F32),
                   jax.ShapeDtypeStruct((T_ALL, LANES), F32),
                   jax.ShapeDtypeStruct((N_TILES, SUBLANES, TM), jnp.int32),
                   jax.ShapeDtypeStruct((N_TILES, SUBLANES, LANES), jnp.int32)],
        compiler_params=_params(1),
        name="moe_route",
    )(x, gain, mods, wr_t, b_r, tri)


def _moe_positions(meta, counts):
    gid = meta[:, 0, :]
    rank = meta[:, 1, :]
    cnt = counts[:, :N_GROUPS, 0]
    total = jnp.sum(cnt, axis=0)
    padded = (total + (MOE_BLOCK - 1)) // MOE_BLOCK * MOE_BLOCK
    group_start = jnp.cumsum(padded) - padded
    base = group_start[None, :] + jnp.cumsum(cnt, axis=0) - cnt
    pos = rank
    for g in range(N_GROUPS):
        pos = pos + jnp.where(gid == g, base[:, g:g + 1], 0)
    first_block = group_start // MOE_BLOCK
    blk = jnp.arange(MOE_BLOCKS, dtype=jnp.int32)
    block_gid = sum((blk >= first_block[g]).astype(jnp.int32) for g in range(1, N_GROUPS))
    return pos.reshape(T_ALL).astype(jnp.int32), block_gid


def _moe_permute_kernel(pos_ref, tiles_ref, gate_ref, sorted_ref, gsorted_ref):
    i = pl.program_id(0)

    @pl.when(i == 0)
    def _():
        def zero(b, carry):
            start = pl.multiple_of(b * TM, TM)
            sorted_ref[pl.ds(start, TM), :] = jnp.zeros((TM, LANES), F32)
            return carry
        lax.fori_loop(0, T_PAD * SUBLANES // TM, zero, 0)
        gsorted_ref[...] = jnp.zeros((T_PAD, LANES), F32)

    def move(t, carry):
        p = pos_ref[i * TM + t]
        dst = pl.multiple_of(p * SUBLANES, SUBLANES)
        src = pl.multiple_of(t * SUBLANES, SUBLANES)
        sorted_ref[pl.ds(dst, SUBLANES), :] = tiles_ref[pl.ds(src, SUBLANES), :]
        gsorted_ref[pl.ds(p, 1), :] = gate_ref[pl.ds(t, 1), :]
        return carry
    lax.fori_loop(0, TM, move, 0, unroll=8)


def moe_permute(pos, tiles, gates):
    return pl.pallas_call(
        _moe_permute_kernel,
        grid_spec=pltpu.PrefetchScalarGridSpec(
            num_scalar_prefetch=1,
            grid=(N_TILES,),
            in_specs=[pl.BlockSpec((TM * SUBLANES, LANES), lambda i, pos: (i, 0)),
                      pl.BlockSpec((TM, LANES), lambda i, pos: (i, 0))],
            out_specs=[pl.BlockSpec(memory_space=pltpu.VMEM), pl.BlockSpec(memory_space=pltpu.VMEM)],
        ),
        out_shape=[jax.ShapeDtypeStruct((T_PAD * SUBLANES, LANES), F32),
                   jax.ShapeDtypeStruct((T_PAD, LANES), F32)],
        compiler_params=_params(1),
        name="moe_permute",
    )(pos, tiles, gates)


def _group_changed(gid_ref, b):
    return (b == 0) | (gid_ref[b] != gid_ref[jnp.maximum(b - 1, 0)])


def _moe_expert_kernel(gid_ref, s_ref, gate_ref, w1_ref, w3_ref, w2_ref, y_ref, w1b_ref, w3b_ref, w2b_ref):
    b = pl.program_id(0)
    group_ff = EXP_PER_GROUP * D_FF

    @pl.when(_group_changed(gid_ref, b))
    def _():
        for j in range(EXP_PER_GROUP):
            cols = slice(j * D_FF, (j + 1) * D_FF)
            w1b_ref[:, cols] = w1_ref[0, 0, j].astype(BF16)
            w3b_ref[:, cols] = w3_ref[0, 0, j].astype(BF16)
        w2b_ref[...] = w2_ref[0, 0].astype(BF16)

    lhs = jnp.concatenate([s_ref[pl.ds(s, MOE_BLOCK, stride=SUBLANES), :].astype(BF16)
                           for s in range(SUBLANES)], axis=1)
    a = _dot(lhs, w1b_ref[...])
    g = _dot(lhs, w3b_ref[...])
    gates = gate_ref[...]
    lane = lax.broadcasted_iota(jnp.int32, (MOE_BLOCK, LANES), 1)
    first = gid_ref[b] * EXP_PER_GROUP
    parts = []
    for j in range(EXP_PER_GROUP):
        cols = slice(j * D_FF, (j + 1) * D_FF)
        gate = jnp.sum(jnp.where(lane == first + j, gates, 0.0), axis=1, keepdims=True)
        parts.append((_silu(a[:, cols]) * g[:, cols] * gate).astype(BF16))
    act = jnp.concatenate(parts, axis=1)
    y = _dot(act, w2b_ref[...])
    for s in range(SUBLANES):
        y_ref[pl.ds(s, MOE_BLOCK, stride=SUBLANES), :] = y[:, s * LANES:(s + 1) * LANES]


def moe_experts(block_gid, sorted_rows, sorted_gates, w1, w3, w2, layer):
    group_ff = EXP_PER_GROUP * D_FF
    w1g = w1.reshape(DEPTH, N_GROUPS, EXP_PER_GROUP, D_MODEL, D_FF)
    w3g = w3.reshape(DEPTH, N_GROUPS, EXP_PER_GROUP, D_MODEL, D_FF)
    w2g = w2.reshape(DEPTH, N_GROUPS, group_ff, D_MODEL)
    up_spec = pl.BlockSpec((1, 1, EXP_PER_GROUP, D_MODEL, D_FF), lambda b, gid: (layer, gid[b], 0, 0, 0),
                           pipeline_mode=pl.Buffered(1))
    return pl.pallas_call(
        _moe_expert_kernel,
        grid_spec=pltpu.PrefetchScalarGridSpec(
            num_scalar_prefetch=1,
            grid=(MOE_BLOCKS,),
            in_specs=[pl.BlockSpec((MOE_BLOCK * SUBLANES, LANES), lambda b, gid: (b, 0)),
                      pl.BlockSpec((MOE_BLOCK, LANES), lambda b, gid: (b, 0)),
                      up_spec, up_spec,
                      pl.BlockSpec((1, 1, group_ff, D_MODEL), lambda b, gid: (layer, gid[b], 0, 0))],
            out_specs=pl.BlockSpec((MOE_BLOCK * SUBLANES, LANES), lambda b, gid: (b, 0)),
            scratch_shapes=[pltpu.VMEM((D_MODEL, group_ff), BF16), pltpu.VMEM((D_MODEL, group_ff), BF16),
                            pltpu.VMEM((group_ff, D_MODEL), BF16)],
        ),
        out_shape=jax.ShapeDtypeStruct((T_PAD * SUBLANES, LANES), F32),
        compiler_params=_params(1),
        name="moe_experts",
    )(block_gid, sorted_rows, sorted_gates, w1g, w3g, w2g)


def _moe_combine_kernel(pos_ref, ys_ref, x_ref, m_ref, o_ref, buf_ref):
    i = pl.program_id(0)

    def move(t, carry):
        src = pl.multiple_of(pos_ref[i * TM + t] * SUBLANES, SUBLANES)
        dst = pl.multiple_of(t * SUBLANES, SUBLANES)
        buf_ref[pl.ds(dst, SUBLANES), :] = ys_ref[pl.ds(src, SUBLANES), :]
        return carry
    lax.fori_loop(0, TM, move, 0, unroll=8)

    for s in range(SUBLANES):
        cols = slice(s * LANES, (s + 1) * LANES)
        y = buf_ref[pl.ds(s, TM, stride=SUBLANES), :]
        o_ref[:, cols] = x_ref[:, cols] + m_ref[0, 0, 5:6, cols] * y


def moe_combine(pos, y_sorted, x, mods, layer):
    return pl.pallas_call(
        _moe_combine_kernel,
        grid_spec=pltpu.PrefetchScalarGridSpec(
            num_scalar_prefetch=1,
            grid=(N_TILES,),
            in_specs=[
                pl.BlockSpec(memory_space=pltpu.VMEM),
                pl.BlockSpec((TM, D_MODEL), lambda i, pos: (i, 0)),
                pl.BlockSpec((1, 1, 6, D_MODEL), lambda i, pos: (layer, _mod_row(i), 0, 0)),
            ],
            out_specs=pl.BlockSpec((TM, D_MODEL), lambda i, pos: (i, 0)),
            scratch_shapes=[pltpu.VMEM((TM * SUBLANES, LANES), F32)],
        ),
        out_shape=jax.ShapeDtypeStruct((T_ALL, D_MODEL), F32),
        compiler_params=_params(1),
        name="moe_combine",
    )(pos, y_sorted, x, mods)


def moe(x, gain, mods, wr_t, b_r, tri, w1, w3, w2, layer):
    tiles, gates, meta, counts = moe_route(x, gain, mods, wr_t, b_r, tri, layer)
    pos, block_gid = _moe_positions(meta, counts)
    sorted_rows, sorted_gates = moe_permute(pos, tiles, gates)
    y_sorted = moe_experts(block_gid, sorted_rows, sorted_gates, w1, w3, w2, layer)
    return moe_combine(pos, y_sorted, x, mods, layer)


def _final_norm_kernel(x_ref, g_ref, o_ref):
    o_ref[...] = _rms(x_ref[...]) * g_ref[...]


def final_norm(x, gain, tile0, ntiles):
    return pl.pallas_call(
        _final_norm_kernel,
        grid=(ntiles,),
        in_specs=[pl.BlockSpec((TM, D_MODEL), lambda i: (tile0 + i, 0)),
                  pl.BlockSpec((1, D_MODEL), lambda i: (0, 0))],
        out_specs=pl.BlockSpec((TM, D_MODEL), lambda i: (i, 0)),
        out_shape=jax.ShapeDtypeStruct((ntiles * TM, D_MODEL), F32),
        compiler_params=_params(1),
        name="final_norm",
    )(x, gain.reshape(1, D_MODEL))


def kernel(x_prompt, x_sample, cache_attn_k, cache_attn_v, state_retention, c, c_ctx, w_ada, b_ada, norm1_g, norm2_g, final_g, w_in, attn_lambda, attn_subln_g, hy_conv_w, hy_conv_b, hy_f_w1, hy_f_b1, hy_f_w2, hy_f_b2, hy_f_w3, hy_skip, ret_decay_exp, w_branch_a, w_branch_b, w_branch_c, w_gate, b_gate, w_out, w_router, b_router, moe_w1, moe_w3, moe_w2):
    x = jnp.concatenate([x_prompt.reshape(T_CTX, D_MODEL), x_sample.reshape(T_LAT, D_MODEL)], axis=0)
    cond8 = jnp.concatenate([c_ctx[None, :], c, jnp.zeros((8 - 1 - DEC_BATCH, D_MODEL), F32)], axis=0)
    mods = ada_modulation(cond8, w_ada, b_ada).reshape(DEPTH, 8, 6, D_MODEL)

    cos, sin = _rope_tables()
    cos_c, sin_c = cos[:, :H_C * DK_C], sin[:, :H_C * DK_C]
    cache_k = cache_attn_k.reshape(DEC_BATCH, DEPTH, PAST_LEN, W_A)
    cache_v = cache_attn_v.reshape(DEC_BATCH, DEPTH, PAST_LEN, W_A)
    decay_rates = _filter_decay_rates()
    tables_ctx = _dft_tables(SEQ)
    tables_lat = _dft_tables(DEC_SEQ)
    filt_ctx = hyena_filters(SEQ, _filter_features(SEQ), decay_rates, tables_ctx[0], tables_ctx[1],
                             hy_f_w1, hy_f_b1, hy_f_w2, hy_f_b2, hy_f_w3)
    filt_lat = hyena_filters(DEC_SEQ, _filter_features(DEC_SEQ), decay_rates, tables_lat[0], tables_lat[1],
                             hy_f_w1, hy_f_b1, hy_f_w2, hy_f_b2, hy_f_w3)
    wr_t = w_router.T
    b_r = b_router.reshape(N_EXPERTS, 1)
    tri = jnp.asarray(np.triu(np.ones((TM, TM), np.float32), 1), dtype=BF16)

    caches = ()
    states = None
    for l in range(DEPTH):
        p, kv = in_projection(x, norm1_g, mods, w_in, w_gate, b_gate, l)

        ya, *caches = attention_ctx(p, kv, attn_lambda, attn_subln_g, l, caches)
        ya = attention_lat(ya, p, kv, cache_k, cache_v, cos, sin, attn_lambda, attn_subln_g, l)
        yb = hyena(None, p, hy_conv_w, hy_conv_b, hy_skip, filt_ctx[0], filt_ctx[1], tables_ctx,
                   l, SEQ, 0, BATCH)
        yb = hyena(yb, p, hy_conv_w, hy_conv_b, hy_skip, filt_lat[0], filt_lat[1], tables_lat,
                   l, DEC_SEQ, T_CTX // DEC_SEQ, DEC_BATCH)
        yc, states = retention(None, p, ret_decay_exp, None, None, states, l, SEQ, 0, BATCH, False)
        yc = retention(yc, p, ret_decay_exp, cos_c, sin_c, state_retention, l, DEC_SEQ,
                       T_CTX // DEC_SEQ, DEC_BATCH, True)[0]

        x = merge_branches(x, ya, yb, yc, p, mods, w_branch_a, w_branch_b, w_branch_c, w_out, l)
        x = moe(x, norm2_g, mods, wr_t, b_r, tri, moe_w1, moe_w3, moe_w2, l)

    new_k = caches[0].reshape(BATCH, DEPTH, SEQ, H_A, 2 * DH_A)
    new_v = caches[1].reshape(BATCH, DEPTH, SEQ, H_A, DV_A)
    y_prompt = final_norm(x, final_g, 0, CTX_TILES).reshape(BATCH, SEQ, D_MODEL)
    y_sample = final_norm(x, final_g, CTX_TILES, N_TILES - CTX_TILES).reshape(DEC_BATCH, DEC_SEQ, D_MODEL)
    return (y_prompt, y_sample, new_k, new_v, states)
```

```python
import functools
import math

import jax
import jax.numpy as jnp
import numpy as np
from jax import lax
from jax.experimental import pallas as pl
from jax.experimental.pallas import tpu as pltpu

F32 = jnp.float32
BF16 = jnp.bfloat16

D_MODEL = 1024
BATCH = 16
SEQ = 256
DEPTH = 4
DEC_BATCH = 2
DEC_SEQ = 1024
PAST_LEN = 256
GRID_W = 64
EPS = 1e-6
ROPE_BASE = 10000.0
H_A = 4
DH_A = 64
DV_A = 128
W_A = 512
C_B = 512
SHORT_K = 3
H_EMB = 33
H_FFN = 64
H_FAST_DECAY = 0.3
H_SLOW_DECAY = 1.5
H_TARGET = 1e-2
H_C = 4
DK_C = 64
DV_C = 128
W_C = 512
CHUNK = 128
N_EXPERTS = 16
N_GROUPS = 4
EXP_PER_GROUP = 4
D_FF = 512
D_IN = 4608

T_CTX = BATCH * SEQ
T_LAT = DEC_BATCH * DEC_SEQ
T_ALL = T_CTX + T_LAT
TM = 1024
N_TILES = T_ALL // TM
CTX_TILES = T_CTX // TM
LANES = 128
SUBLANES = 8
MOE_BLOCK = 256
MOE_BLOCKS = T_ALL // MOE_BLOCK + N_GROUPS
T_PAD = MOE_BLOCKS * MOE_BLOCK
VMEM_LIMIT = 56 * 1024 * 1024


def _params(n_axes):
    return pltpu.CompilerParams(
        dimension_semantics=("arbitrary",) * n_axes, vmem_limit_bytes=VMEM_LIMIT)


def _mod_row(i):
    return jnp.maximum(i - (CTX_TILES - 1), 0)


def _rms(x):
    return x * lax.rsqrt(jnp.mean(x * x, axis=-1, keepdims=True) + EPS)


def _silu(x):
    return x * jax.nn.sigmoid(x)


def _dot(a, b):
    return jnp.dot(a, b, preferred_element_type=F32)


def _dot_nt(a, b):
    return lax.dot_general(a, b, (((1,), (1,)), ((), ())), preferred_element_type=F32)


def _dot_tn(a, b):
    return lax.dot_general(a, b, (((0,), (0,)), ((), ())), preferred_element_type=F32)


def _split3(x):
    hi = x.astype(BF16)
    lo = (x - hi.astype(F32)).astype(BF16)
    return hi, lo


def _dot3(a, b):
    ah, al = _split3(a)
    bh, bl = _split3(b)
    return _dot(ah, bh) + (_dot(ah, bl) + _dot(al, bh))


def _dot3_nt(a, b):
    ah, al = _split3(a)
    bh, bl = _split3(b)
    return _dot_nt(ah, bh) + (_dot_nt(ah, bl) + _dot_nt(al, bh))


def _ada_kernel(c_ref, w_ref, b_ref, o_ref):
    s = _silu(c_ref[...])
    o_ref[0] = _dot(s.astype(BF16), w_ref[0].astype(BF16)) + b_ref[0]


def ada_modulation(cond8, w_ada, b_ada):
    tn = 1536
    n = 6 * D_MODEL
    return pl.pallas_call(
        _ada_kernel,
        grid=(DEPTH, n // tn),
        in_specs=[
            pl.BlockSpec((8, D_MODEL), lambda l, j: (0, 0)),
            pl.BlockSpec((1, D_MODEL, tn), lambda l, j: (l, 0, j)),
            pl.BlockSpec((1, 1, tn), lambda l, j: (l, 0, j)),
        ],
        out_specs=pl.BlockSpec((1, 8, tn), lambda l, j: (l, 0, j)),
        out_shape=jax.ShapeDtypeStruct((DEPTH, 8, n), F32),
        compiler_params=_params(2),
        name="ada_modulation",
    )(cond8, w_ada, b_ada.reshape(DEPTH, 1, n))


PROJ_TM = 3072
PROJ_TN = 512
GATE_BLOCKS = 3 * D_MODEL // PROJ_TN
IN_BLOCKS = D_IN // PROJ_TN
P_WIDTH = 3 * D_MODEL + D_IN
KV_FIRST = GATE_BLOCKS + 1
P_Q = 3 * D_MODEL // W_A
P_HYENA = (3 * D_MODEL + 3 * W_A) // (3 * C_B)
P_RET_Q = (3 * D_MODEL + 3 * W_A + 3 * C_B) // (H_C * DK_C)
P_RET_V = (3 * D_MODEL + 3 * W_A + 3 * C_B + 2 * H_C * DK_C) // W_C


def _in_proj_kernel(x_ref, g_ref, m_ref, win_ref, wg_ref, bg_ref, p_ref, kv_ref, h_ref, *, layer):
    i = pl.program_id(0)
    j = pl.program_id(1)

    @pl.when(j == 0)
    def _():
        for s in range(PROJ_TM // TM):
            rows = slice(s * TM, (s + 1) * TM)
            mod = m_ref[layer, _mod_row(i * (PROJ_TM // TM) + s)]
            y = _rms(x_ref[rows, :]) * g_ref[layer:layer + 1, :]
            h_ref[rows, :] = (y * (1.0 + mod[1:2, :]) + mod[0:1, :]).astype(BF16)

    @pl.when(j < GATE_BLOCKS)
    def _():
        acc = _dot(h_ref[...], wg_ref[0].astype(BF16)) + bg_ref[0]
        p_ref[...] = jax.nn.sigmoid(acc).astype(BF16)

    @pl.when(j >= GATE_BLOCKS)
    def _():
        acc = _dot(h_ref[...], win_ref[0].astype(BF16))
        p_ref[...] = acc.astype(BF16)

        @pl.when((j == KV_FIRST) | (j == KV_FIRST + 1))
        def _():
            kv_ref[...] = acc


def in_projection(x, gain, mods, w_in, w_gate, b_gate, layer):
    return pl.pallas_call(
        functools.partial(_in_proj_kernel, layer=layer),
        grid=(T_ALL // PROJ_TM, GATE_BLOCKS + IN_BLOCKS),
        in_specs=[
            pl.BlockSpec((PROJ_TM, D_MODEL), lambda i, j: (i, 0), pipeline_mode=pl.Buffered(1)),
            pl.BlockSpec((DEPTH, D_MODEL), lambda i, j: (0, 0)),
            pl.BlockSpec((DEPTH, 8, 6, D_MODEL), lambda i, j: (0, 0, 0, 0)),
            pl.BlockSpec((1, D_MODEL, PROJ_TN), lambda i, j: (layer, 0, jnp.maximum(j - GATE_BLOCKS, 0))),
            pl.BlockSpec((1, D_MODEL, PROJ_TN), lambda i, j: (layer, 0, jnp.minimum(j, GATE_BLOCKS - 1))),
            pl.BlockSpec((1, 1, PROJ_TN), lambda i, j: (layer, 0, jnp.minimum(j, GATE_BLOCKS - 1))),
        ],
        out_specs=[
            pl.BlockSpec((PROJ_TM, PROJ_TN), lambda i, j: (i, j)),
            pl.BlockSpec((PROJ_TM, PROJ_TN), lambda i, j: (i, jnp.clip(j - KV_FIRST, 0, 1))),
        ],
        out_shape=[jax.ShapeDtypeStruct((T_ALL, P_WIDTH), BF16),
                   jax.ShapeDtypeStruct((T_ALL, 2 * W_A), F32)],
        scratch_shapes=[pltpu.VMEM((PROJ_TM, D_MODEL), BF16)],
        compiler_params=_params(2),
        name="in_projection",
    )(x, gain, mods, w_in, w_gate, b_gate.reshape(DEPTH, 1, 3 * D_MODEL))


def _rope_tables():
    n = DEC_SEQ
    rows = n // GRID_W
    row = jnp.repeat(jnp.arange(rows, dtype=F32), GRID_W)
    col = jnp.tile(jnp.arange(GRID_W, dtype=F32), rows)
    quarter = DH_A // 4
    inv = ROPE_BASE ** (-jnp.arange(quarter, dtype=F32) / quarter)
    ar = row[:, None] * inv
    ac = col[:, None] * inv
    ang = jnp.concatenate([ar, ar, ac, ac], axis=-1)
    cos = jnp.tile(jnp.cos(ang), (1, 512 // DH_A))
    sgn = jnp.tile(jnp.concatenate([-jnp.ones((quarter,), F32), jnp.ones((quarter,), F32)]), 512 // (2 * quarter))
    sin = jnp.tile(jnp.sin(ang), (1, 512 // DH_A)) * sgn
    return cos, sin


def _rope(x, cos, sin):
    w = x.shape[-1]
    lane = lax.broadcasted_iota(jnp.int32, x.shape, 1)
    first = (lane % 32) < 16
    rot = jnp.where(first, pltpu.roll(x, w - 16, 1), pltpu.roll(x, 16, 1))
    return x * cos + rot * sin


def _lambda(al_ref, layer):
    al = al_ref[layer]
    a = jnp.sum(al[0:1, :] * al[1:2, :], axis=-1, keepdims=True)
    b = jnp.sum(al[2:3, :] * al[3:4, :], axis=-1, keepdims=True)
    lam_init = 0.8 - 0.6 * math.exp(-0.3 * layer)
    return jnp.exp(a) - jnp.exp(b) + lam_init, lam_init


def _softmax_parts(s):
    m = jnp.max(s, axis=-1, keepdims=True)
    e = jnp.exp(s - m)
    return e.astype(BF16), 1.0 / jnp.sum(e, axis=-1, keepdims=True)


def _diff_attention(q, k, v, lam, lam_init, subln, o_ref):
    lane = lax.broadcasted_iota(jnp.int32, q.shape, 1)
    scale = DH_A ** -0.5
    q = q * scale
    q0 = jnp.where((lane % LANES) < DH_A, q, 0.0).astype(BF16)
    q1 = jnp.where((lane % LANES) >= DH_A, q, 0.0).astype(BF16)
    for h in range(H_A):
        sl = slice(h * LANES, (h + 1) * LANES)
        kh = k[:, sl]
        vh = v[:, sl]
        e0, r0 = _softmax_parts(_dot_nt(q0[:, sl], kh))
        e1, r1 = _softmax_parts(_dot_nt(q1[:, sl], kh))
        o = _dot(e0, vh) * r0 - _dot(e1, vh) * (lam * r1)
        o = _rms(o) * subln * (1.0 - lam_init)
        o_ref[:, sl] = o.astype(o_ref.dtype)


def _attn_ctx_kernel(*refs, layer, n_caches):
    q_ref, k_ref, v_ref, al_ref, g_ref, o_ref, knew_ref, vnew_ref = refs[n_caches:]
    lam, lam_init = _lambda(al_ref, layer)
    subln = g_ref[layer:layer + 1, :]
    k = k_ref[...]
    v = v_ref[...]
    for h in range(H_A):
        knew_ref[0, 0, :, h, :] = k[:, h * LANES:(h + 1) * LANES]
        vnew_ref[0, 0, :, h, :] = v[:, h * LANES:(h + 1) * LANES]
    _diff_attention(q_ref[...].astype(F32), k.astype(BF16), v.astype(BF16), lam, lam_init, subln, o_ref)


def attention_ctx(q, kv, attn_lambda, subln_g, layer, caches):
    cache_shape = jax.ShapeDtypeStruct((BATCH, DEPTH, SEQ, H_A, DV_A), F32)
    cache_spec = pl.BlockSpec((1, 1, SEQ, H_A, DV_A), lambda b: (b, layer, 0, 0, 0))
    n_caches = len(caches)
    return pl.pallas_call(
        functools.partial(_attn_ctx_kernel, layer=layer, n_caches=n_caches),
        grid=(BATCH,),
        in_specs=[pl.BlockSpec(memory_space=pl.ANY)] * n_caches + [
            pl.BlockSpec((SEQ, W_A), lambda b: (b, P_Q)),
            pl.BlockSpec((SEQ, W_A), lambda b: (b, 0)),
            pl.BlockSpec((SEQ, W_A), lambda b: (b, 1)),
            pl.BlockSpec((DEPTH, 4, DH_A), lambda b: (0, 0, 0)),
            pl.BlockSpec((DEPTH, DV_A), lambda b: (0, 0)),
        ],
        out_specs=[pl.BlockSpec((SEQ, W_A), lambda b: (b, 0)), cache_spec, cache_spec],
        out_shape=[jax.ShapeDtypeStruct((T_ALL, W_A), BF16), cache_shape, cache_shape],
        input_output_aliases={i: i + 1 for i in range(n_caches)},
        compiler_params=_params(1),
        name="attention_ctx",
    )(*caches, q, kv, kv, attn_lambda, subln_g)


QB = 256


def _attn_lat_kernel(ya_ref, q_ref, k_ref, v_ref, ck_ref, cv_ref, cosq_ref, sinq_ref, cos_ref, sin_ref,
                     al_ref, g_ref, o_ref, kall_ref, vall_ref, *, layer):
    del ya_ref

    @pl.when(pl.program_id(1) == 0)
    def _():
        kall_ref[0:PAST_LEN, :] = ck_ref[0, 0].astype(BF16)
        vall_ref[0:PAST_LEN, :] = cv_ref[0, 0].astype(BF16)
        kall_ref[PAST_LEN:, :] = _rope(k_ref[...], cos_ref[...], sin_ref[...]).astype(BF16)
        vall_ref[PAST_LEN:, :] = v_ref[...].astype(BF16)

    lam, lam_init = _lambda(al_ref, layer)
    subln = g_ref[layer:layer + 1, :]
    q = _rope(q_ref[...].astype(F32), cosq_ref[...], sinq_ref[...])
    _diff_attention(q, kall_ref[...], vall_ref[...], lam, lam_init, subln, o_ref)


def attention_lat(ya, q, kv, cache_k, cache_v, cos, sin, attn_lambda, subln_g, layer):
    nqb = DEC_SEQ // QB
    row0 = T_CTX // QB
    seq0 = T_CTX // DEC_SEQ
    return pl.pallas_call(
        functools.partial(_attn_lat_kernel, layer=layer),
        grid=(DEC_BATCH, nqb),
        in_specs=[
            pl.BlockSpec(memory_space=pl.ANY),
            pl.BlockSpec((QB, W_A), lambda b, i: (row0 + b * nqb + i, P_Q)),
            pl.BlockSpec((DEC_SEQ, W_A), lambda b, i: (seq0 + b, 0)),
            pl.BlockSpec((DEC_SEQ, W_A), lambda b, i: (seq0 + b, 1)),
            pl.BlockSpec((1, 1, PAST_LEN, W_A), lambda b, i: (b, layer, 0, 0)),
            pl.BlockSpec((1, 1, PAST_LEN, W_A), lambda b, i: (b, layer, 0, 0)),
            pl.BlockSpec((QB, W_A), lambda b, i: (i, 0)),
            pl.BlockSpec((QB, W_A), lambda b, i: (i, 0)),
            pl.BlockSpec((DEC_SEQ, W_A), lambda b, i: (0, 0)),
            pl.BlockSpec((DEC_SEQ, W_A), lambda b, i: (0, 0)),
            pl.BlockSpec((DEPTH, 4, DH_A), lambda b, i: (0, 0, 0)),
            pl.BlockSpec((DEPTH, DV_A), lambda b, i: (0, 0)),
        ],
        out_specs=pl.BlockSpec((QB, W_A), lambda b, i: (row0 + b * nqb + i, 0)),
        out_shape=jax.ShapeDtypeStruct((T_ALL, W_A), BF16),
        scratch_shapes=[pltpu.VMEM((PAST_LEN + DEC_SEQ, W_A), BF16),
                        pltpu.VMEM((PAST_LEN + DEC_SEQ, W_A), BF16)],
        input_output_aliases={0: 0},
        compiler_params=_params(2),
        name="attention_lat",
    )(ya, q, kv, kv, cache_k, cache_v, cos, sin, cos, sin, attn_lambda, subln_g)


def _dft_tables(n):
    k = np.arange(n, dtype=np.int64)
    prod = (2 * k[:, None] + 1) * k[None, :]
    ang = (prod % (4 * n)).astype(np.float64) * (math.pi / (2 * n))
    c = np.cos(ang).astype(np.float32)
    s = np.sin(ang).astype(np.float32)
    return tuple(jnp.asarray(t).astype(BF16) for t in (c, s, c.T, s.T))


def _filter_features(n):
    t = jnp.linspace(0.0, 1.0, n, dtype=F32)[:, None]
    bands = (H_EMB - 1) // 2
    w = 2.0 * math.pi * jnp.arange(n, dtype=F32)[:, None] / n
    fr = jnp.linspace(1e-4, bands - 1, bands, dtype=F32)
    z = jnp.concatenate([t, jnp.cos(w * fr), -jnp.sin(w * fr)], axis=-1)
    return jnp.pad(z, ((0, 0), (0, LANES - H_EMB)))


def _filter_decay_rates():
    max_decay = math.log(H_TARGET) / H_FAST_DECAY
    min_decay = math.log(H_TARGET) / H_SLOW_DECAY
    return jnp.abs(jnp.linspace(min_decay, max_decay, C_B, dtype=F32))[None, :]


def _hyena_filter_kernel(z_ref, w1_ref, b1_ref, w2_ref, b2_ref, w3_ref, dr_ref, c_ref, s_ref,
                         gre_ref, gim_ref):
    z = z_ref[...]
    hid = jnp.sin(_dot3(z, w1_ref[0]) + b1_ref[0])
    hid = jnp.sin(_dot3(hid, w2_ref[0]) + b2_ref[0])
    h = _dot3(hid, w3_ref[0])
    window = jnp.exp(-z[:, 0:1] * dr_ref[...])
    hf = h[:, :C_B] * window
    hb = h[:, C_B:] * window
    total = jnp.sum(jnp.abs(hf) + jnp.abs(hb), axis=0, keepdims=True)
    hf = hf / total
    hb = hb / total
    row = lax.broadcasted_iota(jnp.int32, hb.shape, 0)
    hb = jnp.where(row == 0, 0.0, hb)
    ah, al = _split3(hf + hb)
    dh, dl = _split3(hb - hf)
    gre_ref[0] = _dot(c_ref[...], ah) + _dot(c_ref[...], al)
    gim_ref[0] = _dot(s_ref[...], dh) + _dot(s_ref[...], dl)


def hyena_filters(n, z, decay_rates, cf, sf, f_w1, f_b1, f_w2, f_b2, f_w3):
    w1 = jnp.pad(f_w1, ((0, 0), (0, LANES - H_EMB), (0, 0)))
    full = lambda shape: pl.BlockSpec(shape, lambda l: (0,) * len(shape))
    per_layer = lambda shape: pl.BlockSpec((1,) + shape, lambda l: (l,) + (0,) * len(shape))
    return pl.pallas_call(
        _hyena_filter_kernel,
        grid=(DEPTH,),
        in_specs=[
            full((n, LANES)),
            per_layer((LANES, H_FFN)), per_layer((1, H_FFN)),
            per_layer((H_FFN, H_FFN)), per_layer((1, H_FFN)),
            per_layer((H_FFN, 2 * C_B)),
            full((1, C_B)), full((n, n)), full((n, n)),
        ],
        out_specs=[per_layer((n, C_B)), per_layer((n, C_B))],
        out_shape=[jax.ShapeDtypeStruct((DEPTH, n, C_B), F32)] * 2,
        compiler_params=_params(1),
        name="hyena_filters",
    )(z, w1, f_b1.reshape(DEPTH, 1, H_FFN), f_w2, f_b2.reshape(DEPTH, 1, H_FFN), f_w3,
      decay_rates, cf, sf)


def _hyena_kernel(*refs, layer, n, aliased):
    if aliased:
        refs = refs[1:]
    u_ref, cw_ref, cb_ref, skip_ref, gre_ref, gim_ref, c_ref, s_ref, ct_ref, st_ref, o_ref = refs
    u = u_ref[...].astype(F32)
    row = lax.broadcasted_iota(jnp.int32, u.shape, 0)
    prev = jnp.where(row == 0, 0.0, pltpu.roll(u, 1, 0))
    nxt = jnp.where(row == n - 1, 0.0, pltpu.roll(u, n - 1, 0))
    cw = cw_ref[layer]
    uc = cb_ref[layer:layer + 1, :] + prev * cw[0:1, :] + u * cw[1:2, :] + nxt * cw[2:3, :]
    x0 = uc[:, :C_B]
    x1 = uc[:, C_B:2 * C_B]
    v = uc[:, 2 * C_B:]
    w = v * x1
    wb = w.astype(BF16)
    ure = _dot(c_ref[...], wb)
    uim = -_dot(s_ref[...], wb)
    gre = gre_ref[0]
    gim = gim_ref[0]
    yre = (ure * gre - uim * gim).astype(BF16)
    yim = (ure * gim + uim * gre).astype(BF16)
    y = (_dot(ct_ref[...], yre) - _dot(st_ref[...], yim)) * (1.0 / n)
    y = y + w * skip_ref[layer:layer + 1, :]
    o_ref[...] = (x0 * y).astype(o_ref.dtype)


def hyena(yb, r, conv_w, conv_b, skip, gre, gim, tables, layer, n, seq0, nseq):
    cf, sf, cft, sft = tables
    aliased = yb is not None
    full = lambda shape: pl.BlockSpec(shape, lambda b: (0,) * len(shape))
    in_specs = [
        pl.BlockSpec((n, 3 * C_B), lambda b: (seq0 + b, P_HYENA)),
        full((DEPTH, SHORT_K, 3 * C_B)), full((DEPTH, 3 * C_B)), full((DEPTH, C_B)),
        pl.BlockSpec((1, n, C_B), lambda b: (layer, 0, 0)),
        pl.BlockSpec((1, n, C_B), lambda b: (layer, 0, 0)),
        full((n, n)), full((n, n)), full((n, n)), full((n, n)),
    ]
    args = [r, conv_w, conv_b, skip, gre, gim, cf, sf, cft, sft]
    if aliased:
        in_specs = [pl.BlockSpec(memory_space=pl.ANY)] + in_specs
        args = [yb] + args
    return pl.pallas_call(
        functools.partial(_hyena_kernel, layer=layer, n=n, aliased=aliased),
        grid=(nseq,),
        in_specs=in_specs,
        out_specs=pl.BlockSpec((n, C_B), lambda b: (seq0 + b, 0)),
        out_shape=jax.ShapeDtypeStruct((T_ALL, C_B), BF16),
        input_output_aliases={0: 0} if aliased else {},
        compiler_params=_params(1),
        name="hyena",
    )(*args)


def _retention_kernel(*refs, layer, n, latent):
    if latent:
        (_, q_ref, k_ref, v_ref, g_ref, de_ref, cos_ref, sin_ref, s0_ref,
         o_ref, dm_ref, vec_ref, of_ref) = refs
        st_ref = None
    else:
        q_ref, k_ref, v_ref, g_ref, de_ref, o_ref, st_ref, dm_ref, vec_ref, of_ref = refs[-10:]
    nc = n // CHUNK

    @pl.when(pl.program_id(0) == 0)
    def _():
        de = de_ref[layer]
        lg = jnp.log1p(-jnp.exp2(-de))
        r = lax.broadcasted_iota(jnp.int32, (CHUNK, CHUNK), 0)
        c = lax.broadcasted_iota(jnp.int32, (CHUNK, CHUNK), 1)
        pos = lax.broadcasted_iota(jnp.int32, (CHUNK, 1), 0).astype(F32)
        for d in range(2):
            diff = (r - c) if d == 0 else (c - r)
            dpos = jnp.maximum(diff, 0).astype(F32)
            for h in range(H_C):
                g = lg[d:d + 1, h:h + 1]
                dm_ref[d, h] = jnp.where(diff >= 0, jnp.exp(g * dpos), 0.0)
                fwd_pos = pos if d == 0 else (CHUNK - 1.0) - pos
                xi = jnp.exp(g * (fwd_pos + 1.0))
                zeta = jnp.exp(g * (CHUNK - 1.0 - fwd_pos))
                gblk = jnp.exp(g * float(CHUNK)) + jnp.zeros((CHUNK, 1), F32)
                vec_ref[d, h] = jnp.concatenate(
                    [xi, zeta, gblk, jnp.zeros((CHUNK, LANES - 3), F32)], axis=1)

    q = q_ref[...].astype(F32)
    k = k_ref[...].astype(F32) * (DK_C ** -0.5)
    if latent:
        q = _rope(q, cos_ref[...], sin_ref[...])
        k = _rope(k, cos_ref[...], sin_ref[...])
    v = v_ref[...].astype(BF16)
    lane = lax.broadcasted_iota(jnp.int32, (CHUNK, LANES), 1)
    zeros64 = jnp.zeros((DK_C, DV_C), F32)

    for h in range(H_C):
        pair = slice((h // 2) * LANES, (h // 2 + 1) * LANES)
        lo = (h % 2) * DK_C
        own = (lane >= lo) & (lane < lo + DK_C)
        vs = slice(h * DV_C, (h + 1) * DV_C)
        for d in range(2):
            xi = vec_ref[d, h][:, 0:1]
            zeta = vec_ref[d, h][:, 1:2]
            gblk = vec_ref[d, h][0:1, 2:3]
            dmask = dm_ref[d, h]
            if latent:
                s0 = s0_ref[0, 0, d, h]
                state = jnp.concatenate([s0, zeros64] if lo == 0 else [zeros64, s0], axis=0)
            else:
                state = jnp.zeros((LANES, DV_C), F32)
            order = range(nc) if d == 0 else range(nc - 1, -1, -1)
            for ci in order:
                rows = slice(ci * CHUNK, (ci + 1) * CHUNK)
                qi = jnp.where(own, q[rows, pair], 0.0)
                ki = jnp.where(own, k[rows, pair], 0.0)
                vi = v[rows, vs]
                qb = qi.astype(BF16)
                att = _dot_nt(qb, ki.astype(BF16)) * dmask
                out = _dot(att.astype(BF16), vi) + _dot(qb, state.astype(BF16)) * xi
                state = gblk * state + _dot_tn((ki * zeta).astype(BF16), vi)
                if d == 0:
                    of_ref[rows, vs] = out
                else:
                    of_ref[rows, vs] = of_ref[rows, vs] + out
            if st_ref is not None:
                st_ref[0, 0, d, h] = state[lo:lo + DK_C, :]

    gate = g_ref[...].astype(F32)
    for h in range(H_C):
        vs = slice(h * DV_C, (h + 1) * DV_C)
        o_ref[:, vs] = (_rms(of_ref[:, vs]) * _silu(gate[:, vs])).astype(o_ref.dtype)


def retention(yc, r, decay_exp, cos, sin, state0, layer, n, seq0, nseq, latent):
    full = lambda shape: pl.BlockSpec(shape, lambda b: (0,) * len(shape))
    in_specs = [
        pl.BlockSpec((n, H_C * DK_C), lambda b: (seq0 + b, P_RET_Q)),
        pl.BlockSpec((n, H_C * DK_C), lambda b: (seq0 + b, P_RET_Q + 1)),
        pl.BlockSpec((n, W_C), lambda b: (seq0 + b, P_RET_V)),
        pl.BlockSpec((n, W_C), lambda b: (seq0 + b, P_RET_V + 1)),
        full((DEPTH, 2, H_C)),
    ]
    args = [r, r, r, r, decay_exp]
    out_specs = [pl.BlockSpec((n, W_C), lambda b: (seq0 + b, 0))]
    out_shape = [jax.ShapeDtypeStruct((T_ALL, W_C), BF16)]
    aliases = {}
    if latent:
        in_specs = [pl.BlockSpec(memory_space=pl.ANY)] + in_specs + [
            full((n, H_C * DK_C)), full((n, H_C * DK_C)),
            pl.BlockSpec((1, 1, 2, H_C, DK_C, DV_C), lambda b: (b, layer, 0, 0, 0, 0)),
        ]
        args = [yc] + args + [cos, sin, state0]
        aliases = {0: 0}
    else:
        out_specs.append(pl.BlockSpec((1, 1, 2, H_C, DK_C, DV_C), lambda b: (b, layer, 0, 0, 0, 0)))
        out_shape.append(jax.ShapeDtypeStruct((nseq, DEPTH, 2, H_C, DK_C, DV_C), F32))
        if state0 is not None:
            in_specs = [pl.BlockSpec(memory_space=pl.ANY)] + in_specs
            args = [state0] + args
            aliases = {0: 1}
    return pl.pallas_call(
        functools.partial(_retention_kernel, layer=layer, n=n, latent=latent),
        grid=(nseq,),
        in_specs=in_specs,
        out_specs=out_specs,
        out_shape=out_shape,
        scratch_shapes=[pltpu.VMEM((2, H_C, CHUNK, CHUNK), F32),
                        pltpu.VMEM((2, H_C, CHUNK, LANES), F32),
                        pltpu.VMEM((n, W_C), F32)],
        input_output_aliases=aliases,
        compiler_params=_params(1),
        name="retention",
    )(*args)


def _merge_kernel(x_ref, ya_ref, yb_ref, yc_ref, g_ref, m_ref, wa_ref, wb_ref, wc_ref, wo_ref,
                  o_ref, wbf_ref, wobf_ref):
    @pl.when(pl.program_id(0) == 0)
    def _():
        wbf_ref[0] = wa_ref[0].astype(BF16)
        wbf_ref[1] = wb_ref[0].astype(BF16)
        wbf_ref[2] = wc_ref[0].astype(BF16)
        wobf_ref[...] = wo_ref[0].astype(BF16)

    g = g_ref[...]
    merged = (g[:, :D_MODEL] * _dot(ya_ref[...], wbf_ref[0])
              + g[:, D_MODEL:2 * D_MODEL] * _dot(yb_ref[...], wbf_ref[1])
              + g[:, 2 * D_MODEL:] * _dot(yc_ref[...], wbf_ref[2]))
    g1 = m_ref[0, 0, 2:3, :]
    o_ref[...] = x_ref[...] + g1 * _dot(merged.astype(BF16), wobf_ref[...])


def merge_branches(x, ya, yb, yc, gates, mods, w_a, w_b, w_c, w_out, layer):
    tile = lambda w: pl.BlockSpec((TM, w), lambda i: (i, 0))
    wspec = lambda k: pl.BlockSpec((1, k, D_MODEL), lambda i: (layer, 0, 0))
    return pl.pallas_call(
        _merge_kernel,
        grid=(N_TILES,),
        in_specs=[
            tile(D_MODEL), tile(W_A), tile(C_B), tile(W_C), tile(3 * D_MODEL),
            pl.BlockSpec((1, 1, 6, D_MODEL), lambda i: (layer, _mod_row(i), 0, 0)),
            wspec(W_A), wspec(C_B), wspec(W_C), wspec(D_MODEL),
        ],
        out_specs=tile(D_MODEL),
        out_shape=jax.ShapeDtypeStruct((T_ALL, D_MODEL), F32),
        scratch_shapes=[pltpu.VMEM((3, W_A, D_MODEL), BF16), pltpu.VMEM((D_MODEL, D_MODEL), BF16)],
        compiler_params=_params(1),
        name="merge_branches",
    )(x, ya, yb, yc, gates, mods, w_a, w_b, w_c, w_out)


def _route(h2, wr_t, b_r):
    logits = _dot3_nt(wr_t, h2)
    m = jnp.max(logits, axis=0, keepdims=True)
    e = jnp.exp(logits - m)
    scores = e / jnp.sum(e, axis=0, keepdims=True)
    sel = scores + b_r
    rows = [sel[i:i + 1, :] for i in range(N_EXPERTS)]
    in_group = []
    gscore = []
    for g in range(N_GROUPS):
        members = range(g * EXP_PER_GROUP, (g + 1) * EXP_PER_GROUP)
        total = None
        for i in members:
            rank = None
            for j in members:
                if j == i:
                    continue
                ahead = (rows[j] >= rows[i]) if j < i else (rows[j] > rows[i])
                ahead = ahead.astype(F32)
                rank = ahead if rank is None else rank + ahead
            chosen = rank < 2.0
            in_group.append(chosen)
            part = jnp.where(chosen, rows[i], 0.0)
            total = part if total is None else total + part
        gscore.append(total)
    gates = []
    group_hot = []
    for g in range(N_GROUPS):
        best = None
        for g2 in range(N_GROUPS):
            if g2 == g:
                continue
            wins = gscore[g] > gscore[g2] if g2 < g else gscore[g] >= gscore[g2]
            best = wins if best is None else best & wins
        group_hot.append(best.astype(F32))
        for i in range(g * EXP_PER_GROUP, (g + 1) * EXP_PER_GROUP):
            gates.append(jnp.where(best & in_group[i], scores[i:i + 1, :], 0.0))
    gates = jnp.concatenate(gates, axis=0)
    return gates / jnp.sum(gates, axis=0, keepdims=True), group_hot


def _moe_route_kernel(x_ref, g_ref, m_ref, wr_ref, br_ref, tri_ref, tiles_ref, gate_ref, meta_ref, count_ref,
                      *, layer):
    y = _rms(x_ref[...]) * g_ref[layer:layer + 1, :]
    h2 = y * (1.0 + m_ref[0, 0, 4:5, :]) + m_ref[0, 0, 3:4, :]
    gates, group_hot = _route(h2, wr_ref[...], br_ref[...])
    for s in range(SUBLANES):
        tiles_ref[pl.ds(s, TM, stride=SUBLANES), :] = h2[:, s * LANES:(s + 1) * LANES]
    padded = jnp.concatenate([gates, jnp.zeros((LANES - N_EXPERTS, TM), F32)], axis=0)
    gate_ref[...] = padded.T
    row =lax.broadcasted_iota(jnp.int32, (SUBLANES, TM), 0)
    hot = jnp.zeros((SUBLANES, TM), F32)
    for g in range(N_GROUPS):
        hot = jnp.where(row == g, group_hot[g], hot)
    before = _dot(hot.astype(BF16), tri_ref[...])
    rank = jnp.sum(hot * before, axis=0, keepdims=True)
    gid = group_hot[1] + 2.0 * group_hot[2] + 3.0 * group_hot[3]
    meta = jnp.where(row == 0, gid, jnp.where(row == 1, rank, 0.0))
    meta_ref[0] = meta.astype(jnp.int32)
    counts = jnp.sum(hot, axis=1, keepdims=True) + jnp.zeros((SUBLANES, LANES), F32)
    count_ref[0] = counts.astype(jnp.int32)


def moe_route(x, gain, mods, wr_t, b_r, tri, layer):
    return pl.pallas_call(
        functools.partial(_moe_route_kernel, layer=layer),
        grid=(N_TILES,),
        in_specs=[
            pl.BlockSpec((TM, D_MODEL), lambda i: (i, 0)),
            pl.BlockSpec((DEPTH, D_MODEL), lambda i: (0, 0)),
            pl.BlockSpec((1, 1, 6, D_MODEL), lambda i: (layer, _mod_row(i), 0, 0)),
            pl.BlockSpec((N_EXPERTS, D_MODEL), lambda i: (0, 0)),
            pl.BlockSpec((N_EXPERTS, 1), lambda i: (0, 0)),
            pl.BlockSpec((TM, TM), lambda i: (0, 0)),
        ],
        out_specs=[pl.BlockSpec((TM * SUBLANES, LANES), lambda i: (i, 0)),
                   pl.BlockSpec((TM, LANES), lambda i: (i, 0)),
                   pl.BlockSpec((1, SUBLANES, TM), lambda i: (i, 0, 0)),
                   pl.BlockSpec((1, SUBLANES, LANES), lambda i: (i, 0, 0))],
        out_shape=[jax.ShapeDtypeStruct((T_ALL * SUBLANES, LANES), F32),
                   jax.ShapeDtypeStruct((T_ALL, LANES), F32),
                   jax.ShapeDtypeStruct((N_TILES, SUBLANES, TM), jnp.int32),
                   jax.ShapeDtypeStruct((N_TILES, SUBLANES, LANES), jnp.int32)],
        compiler_params=_params(1),
        name="moe_route",
    )(x, gain, mods, wr_t, b_r, tri)


def _moe_positions(meta, counts):
    gid = meta[:, 0, :]
    rank = meta[:, 1, :]
    cnt = counts[:, :N_GROUPS, 0]
    total = jnp.sum(cnt, axis=0)
    padded = (total + (MOE_BLOCK - 1)) // MOE_BLOCK * MOE_BLOCK
    group_start = jnp.cumsum(padded) - padded
    base = group_start[None, :] + jnp.cumsum(cnt, axis=0) - cnt
    pos = rank
    for g in range(N_GROUPS):
        pos = pos + jnp.where(gid == g, base[:, g:g + 1], 0)
    first_block = group_start // MOE_BLOCK
    blk = jnp.arange(MOE_BLOCKS, dtype=jnp.int32)
    block_gid = sum((blk >= first_block[g]).astype(jnp.int32) for g in range(1, N_GROUPS))
    return pos.reshape(T_ALL).astype(jnp.int32), block_gid


def _moe_permute_kernel(pos_ref, tiles_ref, gate_ref, sorted_ref, gsorted_ref):
    i = pl.program_id(0)

    @pl.when(i == 0)
    def _():
        def zero(b, carry):
            start = pl.multiple_of(b * TM, TM)
            sorted_ref[pl.ds(start, TM), :] = jnp.zeros((TM, LANES), F32)
            return carry
        lax.fori_loop(0, T_PAD * SUBLANES // TM, zero, 0)
        gsorted_ref[...] = jnp.zeros((T_PAD, LANES), F32)

    def move(t, carry):
        p = pos_ref[i * TM + t]
        dst = pl.multiple_of(p * SUBLANES, SUBLANES)
        src = pl.multiple_of(t * SUBLANES, SUBLANES)
        sorted_ref[pl.ds(dst, SUBLANES), :] = tiles_ref[pl.ds(src, SUBLANES), :]
        gsorted_ref[pl.ds(p, 1), :] = gate_ref[pl.ds(t, 1), :]
        return carry
    lax.fori_loop(0, TM, move, 0, unroll=8)


def moe_permute(pos, tiles, gates):
    return pl.pallas_call(
        _moe_permute_kernel,
        grid_spec=pltpu.PrefetchScalarGridSpec(
            num_scalar_prefetch=1,
            grid=(N_TILES,),
            in_specs=[pl.BlockSpec((TM * SUBLANES, LANES), lambda i, pos: (i, 0)),
                      pl.BlockSpec((TM, LANES), lambda i, pos: (i, 0))],
            out_specs=[pl.BlockSpec(memory_space=pltpu.VMEM), pl.BlockSpec(memory_space=pltpu.VMEM)],
        ),
        out_shape=[jax.ShapeDtypeStruct((T_PAD * SUBLANES, LANES), F32),
                   jax.ShapeDtypeStruct((T_PAD, LANES), F32)],
        compiler_params=_params(1),
        name="moe_permute",
    )(pos, tiles, gates)


def _group_changed(gid_ref, b):
    return (b == 0) | (gid_ref[b] != gid_ref[jnp.maximum(b - 1, 0)])


def _moe_expert_kernel(gid_ref, s_ref, gate_ref, w1_ref, w3_ref, w2_ref, y_ref, w1b_ref, w3b_ref, w2b_ref):
    b = pl.program_id(0)
    group_ff = EXP_PER_GROUP * D_FF

    @pl.when(_group_changed(gid_ref, b))
    def _():
        for j in range(EXP_PER_GROUP):
            cols = slice(j * D_FF, (j + 1) * D_FF)
            w1b_ref[:, cols] = w1_ref[0, 0, j].astype(BF16)
            w3b_ref[:, cols] = w3_ref[0, 0, j].astype(BF16)
        w2b_ref[...] = w2_ref[0, 0].astype(BF16)

    lhs = jnp.concatenate([s_ref[pl.ds(s, MOE_BLOCK, stride=SUBLANES), :].astype(BF16)
                           for s in range(SUBLANES)], axis=1)
    a = _dot(lhs, w1b_ref[...])
    g = _dot(lhs, w3b_ref[...])
    gates = gate_ref[...]
    lane = lax.broadcasted_iota(jnp.int32, (MOE_BLOCK, LANES), 1)
    first = gid_ref[b] * EXP_PER_GROUP
    parts = []
    for j in range(EXP_PER_GROUP):
        cols = slice(j * D_FF, (j + 1) * D_FF)
        gate = jnp.sum(jnp.where(lane == first + j, gates, 0.0), axis=1, keepdims=True)
        parts.append((_silu(a[:, cols]) * g[:, cols] * gate).astype(BF16))
    act = jnp.concatenate(parts, axis=1)
    y = _dot(act, w2b_ref[...])
    for s in range(SUBLANES):
        y_ref[pl.ds(s, MOE_BLOCK, stride=SUBLANES), :] = y[:, s * LANES:(s + 1) * LANES]


def moe_experts(block_gid, sorted_rows, sorted_gates, w1, w3, w2, layer):
    group_ff = EXP_PER_GROUP * D_FF
    w1g = w1.reshape(DEPTH, N_GROUPS, EXP_PER_GROUP, D_MODEL, D_FF)
    w3g = w3.reshape(DEPTH, N_GROUPS, EXP_PER_GROUP, D_MODEL, D_FF)
    w2g = w2.reshape(DEPTH, N_GROUPS, group_ff, D_MODEL)
    up_spec = pl.BlockSpec((1, 1, EXP_PER_GROUP, D_MODEL, D_FF), lambda b, gid: (layer, gid[b], 0, 0, 0),
                           pipeline_mode=pl.Buffered(1))
    return pl.pallas_call(
        _moe_expert_kernel,
        grid_spec=pltpu.PrefetchScalarGridSpec(
            num_scalar_prefetch=1,
            grid=(MOE_BLOCKS,),
            in_specs=[pl.BlockSpec((MOE_BLOCK * SUBLANES, LANES), lambda b, gid: (b, 0)),
                      pl.BlockSpec((MOE_BLOCK, LANES), lambda b, gid: (b, 0)),
                      up_spec, up_spec,
                      pl.BlockSpec((1, 1, group_ff, D_MODEL), lambda b, gid: (layer, gid[b], 0, 0))],
            out_specs=pl.BlockSpec((MOE_BLOCK * SUBLANES, LANES), lambda b, gid: (b, 0)),
            scratch_shapes=[pltpu.VMEM((D_MODEL, group_ff), BF16), pltpu.VMEM((D_MODEL, group_ff), BF16),
                            pltpu.VMEM((group_ff, D_MODEL), BF16)],
        ),
        out_shape=jax.ShapeDtypeStruct((T_PAD * SUBLANES, LANES), F32),
        compiler_params=_params(1),
        name="moe_experts",
    )(block_gid, sorted_rows, sorted_gates, w1g, w3g, w2g)


def _moe_combine_kernel(pos_ref, ys_ref, x_ref, m_ref, o_ref, buf_ref):
    i = pl.program_id(0)

    def move(t, carry):
        src = pl.multiple_of(pos_ref[i * TM + t] * SUBLANES, SUBLANES)
        dst = pl.multiple_of(t * SUBLANES, SUBLANES)
        buf_ref[pl.ds(dst, SUBLANES), :] = ys_ref[pl.ds(src, SUBLANES), :]
        return carry
    lax.fori_loop(0, TM, move, 0, unroll=8)

    for s in range(SUBLANES):
        cols = slice(s * LANES, (s + 1) * LANES)
        y = buf_ref[pl.ds(s, TM, stride=SUBLANES), :]
        o_ref[:, cols] = x_ref[:, cols] + m_ref[0, 0, 5:6, cols] * y


def moe_combine(pos, y_sorted, x, mods, layer):
    return pl.pallas_call(
        _moe_combine_kernel,
        grid_spec=pltpu.PrefetchScalarGridSpec(
            num_scalar_prefetch=1,
            grid=(N_TILES,),
            in_specs=[
                pl.BlockSpec(memory_space=pltpu.VMEM),
                pl.BlockSpec((TM, D_MODEL), lambda i, pos: (i, 0)),
                pl.BlockSpec((1, 1, 6, D_MODEL), lambda i, pos: (layer, _mod_row(i), 0, 0)),
            ],
            out_specs=pl.BlockSpec((TM, D_MODEL), lambda i, pos: (i, 0)),
            scratch_shapes=[pltpu.VMEM((TM * SUBLANES, LANES), F32)],
        ),
        out_shape=jax.ShapeDtypeStruct((T_ALL, D_MODEL), F32),
        compiler_params=_params(1),
        name="moe_combine",
    )(pos, y_sorted, x, mods)


def moe(x, gain, mods, wr_t, b_r, tri, w1, w3, w2, layer):
    tiles, gates, meta, counts = moe_route(x, gain, mods, wr_t, b_r, tri, layer)
    pos, block_gid = _moe_positions(meta, counts)
    sorted_rows, sorted_gates = moe_permute(pos, tiles, gates)
    y_sorted = moe_experts(block_gid, sorted_rows, sorted_gates, w1, w3, w2, layer)
    return moe_combine(pos, y_sorted, x, mods, layer)


def _final_norm_kernel(x_ref, g_ref, o_ref):
    o_ref[...] = _rms(x_ref[...]) * g_ref[...]


def final_norm(x, gain, tile0, ntiles):
    return pl.pallas_call(
        _final_norm_kernel,
        grid=(ntiles,),
        in_specs=[pl.BlockSpec((TM, D_MODEL), lambda i: (tile0 + i, 0)),
                  pl.BlockSpec((1, D_MODEL), lambda i: (0, 0))],
        out_specs=pl.BlockSpec((TM, D_MODEL), lambda i: (i, 0)),
        out_shape=jax.ShapeDtypeStruct((ntiles * TM, D_MODEL), F32),
        compiler_params=_params(1),
        name="final_norm",
    )(x, gain.reshape(1, D_MODEL))


def kernel(x_prompt, x_sample, cache_attn_k, cache_attn_v, state_retention, c, c_ctx, w_ada, b_ada, norm1_g, norm2_g, final_g, w_in, attn_lambda, attn_subln_g, hy_conv_w, hy_conv_b, hy_f_w1, hy_f_b1, hy_f_w2, hy_f_b2, hy_f_w3, hy_skip, ret_decay_exp, w_branch_a, w_branch_b, w_branch_c, w_gate, b_gate, w_out, w_router, b_router, moe_w1, moe_w3, moe_w2):
    x = jnp.concatenate([x_prompt.reshape(T_CTX, D_MODEL), x_sample.reshape(T_LAT, D_MODEL)], axis=0)
    cond8 = jnp.concatenate([c_ctx[None, :], c, jnp.zeros((8 - 1 - DEC_BATCH, D_MODEL), F32)], axis=0)
    mods = ada_modulation(cond8, w_ada, b_ada).reshape(DEPTH, 8, 6, D_MODEL)

    cos, sin = _rope_tables()
    cos_c, sin_c = cos[:, :H_C * DK_C], sin[:, :H_C * DK_C]
    cache_k = cache_attn_k.reshape(DEC_BATCH, DEPTH, PAST_LEN, W_A)
    cache_v = cache_attn_v.reshape(DEC_BATCH, DEPTH, PAST_LEN, W_A)
    decay_rates = _filter_decay_rates()
    tables_ctx = _dft_tables(SEQ)
    tables_lat = _dft_tables(DEC_SEQ)
    filt_ctx = hyena_filters(SEQ, _filter_features(SEQ), decay_rates, tables_ctx[0], tables_ctx[1],
                             hy_f_w1, hy_f_b1, hy_f_w2, hy_f_b2, hy_f_w3)
    filt_lat = hyena_filters(DEC_SEQ, _filter_features(DEC_SEQ), decay_rates, tables_lat[0], tables_lat[1],
                             hy_f_w1, hy_f_b1, hy_f_w2, hy_f_b2, hy_f_w3)
    wr_t = w_router.T
    b_r = b_router.reshape(N_EXPERTS, 1)
    tri = jnp.asarray(np.triu(np.ones((TM, TM), np.float32), 1), dtype=BF16)

    caches = ()
    states = None
    for l in range(DEPTH):
        p, kv = in_projection(x, norm1_g, mods, w_in, w_gate, b_gate, l)

        ya, *caches = attention_ctx(p, kv, attn_lambda, attn_subln_g, l, caches)
        ya = attention_lat(ya, p, kv, cache_k, cache_v, cos, sin, attn_lambda, attn_subln_g, l)
        yb = hyena(None, p, hy_conv_w, hy_conv_b, hy_skip, filt_ctx[0], filt_ctx[1], tables_ctx,
                   l, SEQ, 0, BATCH)
        yb = hyena(yb, p, hy_conv_w, hy_conv_b, hy_skip, filt_lat[0], filt_lat[1], tables_lat,
                   l, DEC_SEQ, T_CTX // DEC_SEQ, DEC_BATCH)
        yc, states = retention(None, p, ret_decay_exp, None, None, states, l, SEQ, 0, BATCH, False)
        yc = retention(yc, p, ret_decay_exp, cos_c, sin_c, state_retention, l, DEC_SEQ,
                       T_CTX // DEC_SEQ, DEC_BATCH, True)[0]

        x = merge_branches(x, ya, yb, yc, p, mods, w_branch_a, w_branch_b, w_branch_c, w_out, l)
        x = moe(x, norm2_g, mods, wr_t, b_r, tri, moe_w1, moe_w3, moe_w2, l)

    new_k, new_v = caches
    y_prompt = final_norm(x, final_g, 0, CTX_TILES).reshape(BATCH, SEQ, D_MODEL)
    y_sample = final_norm(x, final_g, CTX_TILES, N_TILES - CTX_TILES).reshape(DEC_BATCH, DEC_SEQ, D_MODEL)
    return (y_prompt, y_sample, new_k, new_v, states)
```

```python
import functools
import math

import jax
import jax.numpy as jnp
import numpy as np
from jax import lax
from jax.experimental import pallas as pl
from jax.experimental.pallas import tpu as pltpu

F32 = jnp.float32
BF16 = jnp.bfloat16

D_MODEL = 1024
BATCH = 16
SEQ = 256
DEPTH = 4
DEC_BATCH = 2
DEC_SEQ = 1024
PAST_LEN = 256
GRID_W = 64
EPS = 1e-6
ROPE_BASE = 10000.0
H_A = 4
DH_A = 64
DV_A = 128
W_A = 512
C_B = 512
SHORT_K = 3
H_EMB = 33
H_FFN = 64
H_FAST_DECAY = 0.3
H_SLOW_DECAY = 1.5
H_TARGET = 1e-2
H_C = 4
DK_C = 64
DV_C = 128
W_C = 512
CHUNK = 128
N_EXPERTS = 16
N_GROUPS = 4
EXP_PER_GROUP = 4
D_FF = 512
D_IN = 4608

T_CTX = BATCH * SEQ
T_LAT = DEC_BATCH * DEC_SEQ
T_ALL = T_CTX + T_LAT
TM = 1024
N_TILES = T_ALL // TM
CTX_TILES = T_CTX // TM
LANES = 128
SUBLANES = 8
MOE_BLOCK = 256
MOE_BLOCKS = T_ALL // MOE_BLOCK + N_GROUPS
T_PAD = MOE_BLOCKS * MOE_BLOCK
EXPERT_PAIRS = ((0, 1), (0, 2), (0, 3), (1, 2), (1, 3), (2, 3))
N_PAIRS = len(EXPERT_PAIRS)
N_BUCKETS = N_GROUPS * N_PAIRS
BUCKET_ROWS = 32
VMEM_LIMIT = 56 * 1024 * 1024


def _params(n_axes):
    return pltpu.CompilerParams(
        dimension_semantics=("arbitrary",) * n_axes, vmem_limit_bytes=VMEM_LIMIT)


def _mod_row(i):
    return jnp.maximum(i - (CTX_TILES - 1), 0)


def _rms(x):
    return x * lax.rsqrt(jnp.mean(x * x, axis=-1, keepdims=True) + EPS)


def _silu(x):
    return x * jax.nn.sigmoid(x)


def _dot(a, b):
    return jnp.dot(a, b, preferred_element_type=F32)


def _dot_nt(a, b):
    return lax.dot_general(a, b, (((1,), (1,)), ((), ())), preferred_element_type=F32)


def _dot_tn(a, b):
    return lax.dot_general(a, b, (((0,), (0,)), ((), ())), preferred_element_type=F32)


def _split3(x):
    hi = x.astype(BF16)
    lo = (x - hi.astype(F32)).astype(BF16)
    return hi, lo


def _dot3(a, b):
    ah, al = _split3(a)
    bh, bl = _split3(b)
    return _dot(ah, bh) + (_dot(ah, bl) + _dot(al, bh))


def _dot3_nt(a, b):
    ah, al = _split3(a)
    bh, bl = _split3(b)
    return _dot_nt(ah, bh) + (_dot_nt(ah, bl) + _dot_nt(al, bh))


def _ada_kernel(c_ref, w_ref, b_ref, o_ref):
    s = _silu(c_ref[...])
    o_ref[0] = _dot(s.astype(BF16), w_ref[0].astype(BF16)) + b_ref[0]


def ada_modulation(cond8, w_ada, b_ada):
    tn = 1536
    n = 6 * D_MODEL
    return pl.pallas_call(
        _ada_kernel,
        grid=(DEPTH, n // tn),
        in_specs=[
            pl.BlockSpec((8, D_MODEL), lambda l, j: (0, 0)),
            pl.BlockSpec((1, D_MODEL, tn), lambda l, j: (l, 0, j)),
            pl.BlockSpec((1, 1, tn), lambda l, j: (l, 0, j)),
        ],
        out_specs=pl.BlockSpec((1, 8, tn), lambda l, j: (l, 0, j)),
        out_shape=jax.ShapeDtypeStruct((DEPTH, 8, n), F32),
        compiler_params=_params(2),
        name="ada_modulation",
    )(cond8, w_ada, b_ada.reshape(DEPTH, 1, n))


PROJ_TM = 3072
PROJ_TN = 512
GATE_BLOCKS = 3 * D_MODEL // PROJ_TN
IN_BLOCKS = D_IN // PROJ_TN
P_WIDTH = 3 * D_MODEL + D_IN
KV_FIRST = GATE_BLOCKS + 1
P_Q = 3 * D_MODEL // W_A
P_HYENA = (3 * D_MODEL + 3 * W_A) // (3 * C_B)
P_RET_Q = (3 * D_MODEL + 3 * W_A + 3 * C_B) // (H_C * DK_C)
P_RET_V = (3 * D_MODEL + 3 * W_A + 3 * C_B + 2 * H_C * DK_C) // W_C


def _in_proj_kernel(x_ref, g_ref, m_ref, win_ref, wg_ref, bg_ref, p_ref, kv_ref, h_ref, *, layer):
    i = pl.program_id(0)
    j = pl.program_id(1)

    @pl.when(j == 0)
    def _():
        for s in range(PROJ_TM // TM):
            rows = slice(s * TM, (s + 1) * TM)
            mod = m_ref[layer, _mod_row(i * (PROJ_TM // TM) + s)]
            y = _rms(x_ref[rows, :]) * g_ref[layer:layer + 1, :]
            h_ref[rows, :] = (y * (1.0 + mod[1:2, :]) + mod[0:1, :]).astype(BF16)

    @pl.when(j < GATE_BLOCKS)
    def _():
        acc = _dot(h_ref[...], wg_ref[0].astype(BF16)) + bg_ref[0]
        p_ref[...] = jax.nn.sigmoid(acc).astype(BF16)

    @pl.when(j >= GATE_BLOCKS)
    def _():
        acc = _dot(h_ref[...], win_ref[0].astype(BF16))
        p_ref[...] = acc.astype(BF16)

        @pl.when((j == KV_FIRST) | (j == KV_FIRST + 1))
        def _():
            kv_ref[...] = acc


def in_projection(x, gain, mods, w_in, w_gate, b_gate, layer):
    return pl.pallas_call(
        functools.partial(_in_proj_kernel, layer=layer),
        grid=(T_ALL // PROJ_TM, GATE_BLOCKS + IN_BLOCKS),
        in_specs=[
            pl.BlockSpec((PROJ_TM, D_MODEL), lambda i, j: (i, 0), pipeline_mode=pl.Buffered(1)),
            pl.BlockSpec((DEPTH, D_MODEL), lambda i, j: (0, 0)),
            pl.BlockSpec((DEPTH, 8, 6, D_MODEL), lambda i, j: (0, 0, 0, 0)),
            pl.BlockSpec((1, D_MODEL, PROJ_TN), lambda i, j: (layer, 0, jnp.maximum(j - GATE_BLOCKS, 0))),
            pl.BlockSpec((1, D_MODEL, PROJ_TN), lambda i, j: (layer, 0, jnp.minimum(j, GATE_BLOCKS - 1))),
            pl.BlockSpec((1, 1, PROJ_TN), lambda i, j: (layer, 0, jnp.minimum(j, GATE_BLOCKS - 1))),
        ],
        out_specs=[
            pl.BlockSpec((PROJ_TM, PROJ_TN), lambda i, j: (i, j)),
            pl.BlockSpec((PROJ_TM, PROJ_TN), lambda i, j: (i, jnp.clip(j - KV_FIRST, 0, 1))),
        ],
        out_shape=[jax.ShapeDtypeStruct((T_ALL, P_WIDTH), BF16),
                   jax.ShapeDtypeStruct((T_ALL, 2 * W_A), F32)],
        scratch_shapes=[pltpu.VMEM((PROJ_TM, D_MODEL), BF16)],
        compiler_params=_params(2),
        name="in_projection",
    )(x, gain, mods, w_in, w_gate, b_gate.reshape(DEPTH, 1, 3 * D_MODEL))


def _rope_tables():
    n = DEC_SEQ
    rows = n // GRID_W
    row = jnp.repeat(jnp.arange(rows, dtype=F32), GRID_W)
    col = jnp.tile(jnp.arange(GRID_W, dtype=F32), rows)
    quarter = DH_A // 4
    inv = ROPE_BASE ** (-jnp.arange(quarter, dtype=F32) / quarter)
    ar = row[:, None] * inv
    ac = col[:, None] * inv
    ang = jnp.concatenate([ar, ar, ac, ac], axis=-1)
    cos = jnp.tile(jnp.cos(ang), (1, 512 // DH_A))
    sgn = jnp.tile(jnp.concatenate([-jnp.ones((quarter,), F32), jnp.ones((quarter,), F32)]), 512 // (2 * quarter))
    sin = jnp.tile(jnp.sin(ang), (1, 512 // DH_A)) * sgn
    return cos, sin


def _rope(x, cos, sin):
    w = x.shape[-1]
    lane = lax.broadcasted_iota(jnp.int32, x.shape, 1)
    first = (lane % 32) < 16
    rot = jnp.where(first, pltpu.roll(x, w - 16, 1), pltpu.roll(x, 16, 1))
    return x * cos + rot * sin


def _lambda(al_ref, layer):
    al = al_ref[layer]
    a = jnp.sum(al[0:1, :] * al[1:2, :], axis=-1, keepdims=True)
    b = jnp.sum(al[2:3, :] * al[3:4, :], axis=-1, keepdims=True)
    lam_init = 0.8 - 0.6 * math.exp(-0.3 * layer)
    return jnp.exp(a) - jnp.exp(b) + lam_init, lam_init


def _softmax_parts(s):
    m = jnp.max(s, axis=-1, keepdims=True)
    e = jnp.exp2(s - m)
    return e.astype(BF16), 1.0 / jnp.sum(e, axis=-1, keepdims=True)


def _diff_attention(q, k, v, lam, lam_init, subln, o_ref):
    lane = lax.broadcasted_iota(jnp.int32, q.shape, 1)
    q = q * (DH_A ** -0.5 * math.log2(math.e))
    q0 = jnp.where((lane % LANES) < DH_A, q, 0.0).astype(BF16)
    q1 = jnp.where((lane % LANES) >= DH_A, q, 0.0).astype(BF16)
    for h in range(H_A):
        sl = slice(h * LANES, (h + 1) * LANES)
        kh = k[:, sl]
        vh = v[:, sl]
        e0, r0 = _softmax_parts(_dot_nt(q0[:, sl], kh))
        e1, r1 = _softmax_parts(_dot_nt(q1[:, sl], kh))
        o = _dot(e0, vh) * r0 - _dot(e1, vh) * (lam * r1)
        o = _rms(o) * subln * (1.0 - lam_init)
        o_ref[:, sl] = o.astype(o_ref.dtype)


def _attn_ctx_kernel(*refs, layer, n_caches):
    q_ref, k_ref, v_ref, al_ref, g_ref, o_ref, knew_ref, vnew_ref = refs[n_caches:]
    lam, lam_init = _lambda(al_ref, layer)
    subln = g_ref[layer:layer + 1, :]
    k = k_ref[...]
    v = v_ref[...]
    for h in range(H_A):
        knew_ref[0, 0, :, h, :] = k[:, h * LANES:(h + 1) * LANES]
        vnew_ref[0, 0, :, h, :] = v[:, h * LANES:(h + 1) * LANES]
    _diff_attention(q_ref[...].astype(F32), k.astype(BF16), v.astype(BF16), lam, lam_init, subln, o_ref)


def attention_ctx(q, kv, attn_lambda, subln_g, layer, caches):
    cache_shape = jax.ShapeDtypeStruct((BATCH, DEPTH, SEQ, H_A, DV_A), F32)
    cache_spec = pl.BlockSpec((1, 1, SEQ, H_A, DV_A), lambda b: (b, layer, 0, 0, 0))
    n_caches = len(caches)
    return pl.pallas_call(
        functools.partial(_attn_ctx_kernel, layer=layer, n_caches=n_caches),
        grid=(BATCH,),
        in_specs=[pl.BlockSpec(memory_space=pl.ANY)] * n_caches + [
            pl.BlockSpec((SEQ, W_A), lambda b: (b, P_Q)),
            pl.BlockSpec((SEQ, W_A), lambda b: (b, 0)),
            pl.BlockSpec((SEQ, W_A), lambda b: (b, 1)),
            pl.BlockSpec((DEPTH, 4, DH_A), lambda b: (0, 0, 0)),
            pl.BlockSpec((DEPTH, DV_A), lambda b: (0, 0)),
        ],
        out_specs=[pl.BlockSpec((SEQ, W_A), lambda b: (b, 0)), cache_spec, cache_spec],
        out_shape=[jax.ShapeDtypeStruct((T_ALL, W_A), BF16), cache_shape, cache_shape],
        input_output_aliases={i: i + 1 for i in range(n_caches)},
        compiler_params=_params(1),
        name="attention_ctx",
    )(*caches, q, kv, kv, attn_lambda, subln_g)


QB = 256


def _attn_lat_kernel(ya_ref, q_ref, k_ref, v_ref, ck_ref, cv_ref, cosq_ref, sinq_ref, cos_ref, sin_ref,
                     al_ref, g_ref, o_ref, kall_ref, vall_ref, *, layer):
    del ya_ref

    @pl.when(pl.program_id(1) == 0)
    def _():
        kall_ref[0:PAST_LEN, :] = ck_ref[0, 0].astype(BF16)
        vall_ref[0:PAST_LEN, :] = cv_ref[0, 0].astype(BF16)
        kall_ref[PAST_LEN:, :] = _rope(k_ref[...], cos_ref[...], sin_ref[...]).astype(BF16)
        vall_ref[PAST_LEN:, :] = v_ref[...].astype(BF16)

    lam, lam_init = _lambda(al_ref, layer)
    subln = g_ref[layer:layer + 1, :]
    q = _rope(q_ref[...].astype(F32), cosq_ref[...], sinq_ref[...])
    _diff_attention(q, kall_ref[...], vall_ref[...], lam, lam_init, subln, o_ref)


def attention_lat(ya, q, kv, cache_k, cache_v, cos, sin, attn_lambda, subln_g, layer):
    nqb = DEC_SEQ // QB
    row0 = T_CTX // QB
    seq0 = T_CTX // DEC_SEQ
    return pl.pallas_call(
        functools.partial(_attn_lat_kernel, layer=layer),
        grid=(DEC_BATCH, nqb),
        in_specs=[
            pl.BlockSpec(memory_space=pl.ANY),
            pl.BlockSpec((QB, W_A), lambda b, i: (row0 + b * nqb + i, P_Q)),
            pl.BlockSpec((DEC_SEQ, W_A), lambda b, i: (seq0 + b, 0)),
            pl.BlockSpec((DEC_SEQ, W_A), lambda b, i: (seq0 + b, 1)),
            pl.BlockSpec((1, 1, PAST_LEN, W_A), lambda b, i: (b, layer, 0, 0)),
            pl.BlockSpec((1, 1, PAST_LEN, W_A), lambda b, i: (b, layer, 0, 0)),
            pl.BlockSpec((QB, W_A), lambda b, i: (i, 0)),
            pl.BlockSpec((QB, W_A), lambda b, i: (i, 0)),
            pl.BlockSpec((DEC_SEQ, W_A), lambda b, i: (0, 0)),
            pl.BlockSpec((DEC_SEQ, W_A), lambda b, i: (0, 0)),
            pl.BlockSpec((DEPTH, 4, DH_A), lambda b, i: (0, 0, 0)),
            pl.BlockSpec((DEPTH, DV_A), lambda b, i: (0, 0)),
        ],
        out_specs=pl.BlockSpec((QB, W_A), lambda b, i: (row0 + b * nqb + i, 0)),
        out_shape=jax.ShapeDtypeStruct((T_ALL, W_A), BF16),
        scratch_shapes=[pltpu.VMEM((PAST_LEN + DEC_SEQ, W_A), BF16),
                        pltpu.VMEM((PAST_LEN + DEC_SEQ, W_A), BF16)],
        input_output_aliases={0: 0},
        compiler_params=_params(2),
        name="attention_lat",
    )(ya, q, kv, kv, cache_k, cache_v, cos, sin, cos, sin, attn_lambda, subln_g)


def _dft_tables(n):
    k = np.arange(n, dtype=np.int64)
    prod = (2 * k[:, None] + 1) * k[None, :]
    ang = (prod % (4 * n)).astype(np.float64) * (math.pi / (2 * n))
    c = np.cos(ang).astype(np.float32)
    s = np.sin(ang).astype(np.float32)
    return tuple(jnp.asarray(t).astype(BF16) for t in (c, s, c.T, s.T))


def _filter_features(n):
    t = jnp.linspace(0.0, 1.0, n, dtype=F32)[:, None]
    bands = (H_EMB - 1) // 2
    w = 2.0 * math.pi * jnp.arange(n, dtype=F32)[:, None] / n
    fr = jnp.linspace(1e-4, bands - 1, bands, dtype=F32)
    z = jnp.concatenate([t, jnp.cos(w * fr), -jnp.sin(w * fr)], axis=-1)
    return jnp.pad(z, ((0, 0), (0, LANES - H_EMB)))


def _filter_decay_rates():
    max_decay = math.log(H_TARGET) / H_FAST_DECAY
    min_decay = math.log(H_TARGET) / H_SLOW_DECAY
    return jnp.abs(jnp.linspace(min_decay, max_decay, C_B, dtype=F32))[None, :]


def _hyena_filter_kernel(z_ref, w1_ref, b1_ref, w2_ref, b2_ref, w3_ref, dr_ref, c_ref, s_ref,
                         gre_ref, gim_ref):
    z = z_ref[...]
    hid = jnp.sin(_dot3(z, w1_ref[0]) + b1_ref[0])
    hid = jnp.sin(_dot3(hid, w2_ref[0]) + b2_ref[0])
    h = _dot3(hid, w3_ref[0])
    window = jnp.exp(-z[:, 0:1] * dr_ref[...])
    hf = h[:, :C_B] * window
    hb = h[:, C_B:] * window
    total = jnp.sum(jnp.abs(hf) + jnp.abs(hb), axis=0, keepdims=True)
    hf = hf / total
    hb = hb / total
    row = lax.broadcasted_iota(jnp.int32, hb.shape, 0)
    hb = jnp.where(row == 0, 0.0, hb)
    ah, al = _split3(hf + hb)
    dh, dl = _split3(hb - hf)
    gre_ref[0] = _dot(c_ref[...], ah) + _dot(c_ref[...], al)
    gim_ref[0] = _dot(s_ref[...], dh) + _dot(s_ref[...], dl)


def hyena_filters(n, z, decay_rates, cf, sf, f_w1, f_b1, f_w2, f_b2, f_w3):
    w1 = jnp.pad(f_w1, ((0, 0), (0, LANES - H_EMB), (0, 0)))
    full = lambda shape: pl.BlockSpec(shape, lambda l: (0,) * len(shape))
    per_layer = lambda shape: pl.BlockSpec((1,) + shape, lambda l: (l,) + (0,) * len(shape))
    return pl.pallas_call(
        _hyena_filter_kernel,
        grid=(DEPTH,),
        in_specs=[
            full((n, LANES)),
            per_layer((LANES, H_FFN)), per_layer((1, H_FFN)),
            per_layer((H_FFN, H_FFN)), per_layer((1, H_FFN)),
            per_layer((H_FFN, 2 * C_B)),
            full((1, C_B)), full((n, n)), full((n, n)),
        ],
        out_specs=[per_layer((n, C_B)), per_layer((n, C_B))],
        out_shape=[jax.ShapeDtypeStruct((DEPTH, n, C_B), F32)] * 2,
        compiler_params=_params(1),
        name="hyena_filters",
    )(z, w1, f_b1.reshape(DEPTH, 1, H_FFN), f_w2, f_b2.reshape(DEPTH, 1, H_FFN), f_w3,
      decay_rates, cf, sf)


def _hyena_kernel(*refs, layer, n, aliased):
    if aliased:
        refs = refs[1:]
    u_ref, cw_ref, cb_ref, skip_ref, gre_ref, gim_ref, c_ref, s_ref, ct_ref, st_ref, o_ref = refs
    u = u_ref[...].astype(F32)
    row = lax.broadcasted_iota(jnp.int32, u.shape, 0)
    prev = jnp.where(row == 0, 0.0, pltpu.roll(u, 1, 0))
    nxt = jnp.where(row == n - 1, 0.0, pltpu.roll(u, n - 1, 0))
    cw = cw_ref[layer]
    uc = cb_ref[layer:layer + 1, :] + prev * cw[0:1, :] + u * cw[1:2, :] + nxt * cw[2:3, :]
    x0 = uc[:, :C_B]
    x1 = uc[:, C_B:2 * C_B]
    v = uc[:, 2 * C_B:]
    w = v * x1
    wb = w.astype(BF16)
    ure = _dot(c_ref[...], wb)
    uim = -_dot(s_ref[...], wb)
    gre = gre_ref[0]
    gim = gim_ref[0]
    yre = (ure * gre - uim * gim).astype(BF16)
    yim = (ure * gim + uim * gre).astype(BF16)
    y = (_dot(ct_ref[...], yre) - _dot(st_ref[...], yim)) * (1.0 / n)
    y = y + w * skip_ref[layer:layer + 1, :]
    o_ref[...] = (x0 * y).astype(o_ref.dtype)


def hyena(yb, r, conv_w, conv_b, skip, gre, gim, tables, layer, n, seq0, nseq):
    cf, sf, cft, sft = tables
    aliased = yb is not None
    full = lambda shape: pl.BlockSpec(shape, lambda b: (0,) * len(shape))
    in_specs = [
        pl.BlockSpec((n, 3 * C_B), lambda b: (seq0 + b, P_HYENA)),
        full((DEPTH, SHORT_K, 3 * C_B)), full((DEPTH, 3 * C_B)), full((DEPTH, C_B)),
        pl.BlockSpec((1, n, C_B), lambda b: (layer, 0, 0)),
        pl.BlockSpec((1, n, C_B), lambda b: (layer, 0, 0)),
        full((n, n)), full((n, n)), full((n, n)), full((n, n)),
    ]
    args = [r, conv_w, conv_b, skip, gre, gim, cf, sf, cft, sft]
    if aliased:
        in_specs = [pl.BlockSpec(memory_space=pl.ANY)] + in_specs
        args = [yb] + args
    return pl.pallas_call(
        functools.partial(_hyena_kernel, layer=layer, n=n, aliased=aliased),
        grid=(nseq,),
        in_specs=in_specs,
        out_specs=pl.BlockSpec((n, C_B), lambda b: (seq0 + b, 0)),
        out_shape=jax.ShapeDtypeStruct((T_ALL, C_B), BF16),
        input_output_aliases={0: 0} if aliased else {},
        compiler_params=_params(1),
        name="hyena",
    )(*args)


def _retention_kernel(*refs, layer, n, latent):
    if latent:
        (_, q_ref, k_ref, v_ref, g_ref, de_ref, cos_ref, sin_ref, s0_ref,
         o_ref, dm_ref, vec_ref, of_ref) = refs
        st_ref = None
    else:
        q_ref, k_ref, v_ref, g_ref, de_ref, o_ref, st_ref, dm_ref, vec_ref, of_ref = refs[-10:]
    nc = n // CHUNK

    @pl.when(pl.program_id(0) == 0)
    def _():
        de = de_ref[layer]
        lg = jnp.log1p(-jnp.exp2(-de))
        r = lax.broadcasted_iota(jnp.int32, (CHUNK, CHUNK), 0)
        c = lax.broadcasted_iota(jnp.int32, (CHUNK, CHUNK), 1)
        pos = lax.broadcasted_iota(jnp.int32, (CHUNK, 1), 0).astype(F32)
        for d in range(2):
            diff = (r - c) if d == 0 else (c - r)
            dpos = jnp.maximum(diff, 0).astype(F32)
            for h in range(H_C):
                g = lg[d:d + 1, h:h + 1]
                dm_ref[d, h] = jnp.where(diff >= 0, jnp.exp(g * dpos), 0.0)
                fwd_pos = pos if d == 0 else (CHUNK - 1.0) - pos
                xi = jnp.exp(g * (fwd_pos + 1.0))
                zeta = jnp.exp(g * (CHUNK - 1.0 - fwd_pos))
                gblk = jnp.exp(g * float(CHUNK)) + jnp.zeros((CHUNK, 1), F32)
                vec_ref[d, h] = jnp.concatenate(
                    [xi, zeta, gblk, jnp.zeros((CHUNK, LANES - 3), F32)], axis=1)

    q = q_ref[...].astype(F32)
    k = k_ref[...].astype(F32) * (DK_C ** -0.5)
    if latent:
        q = _rope(q, cos_ref[...], sin_ref[...])
        k = _rope(k, cos_ref[...], sin_ref[...])
    v = v_ref[...].astype(BF16)
    lane = lax.broadcasted_iota(jnp.int32, (CHUNK, LANES), 1)
    zeros64 = jnp.zeros((DK_C, DV_C), F32)

    for h in range(H_C):
        pair = slice((h // 2) * LANES, (h // 2 + 1) * LANES)
        lo = (h % 2) * DK_C
        own = (lane >= lo) & (lane < lo + DK_C)
        vs = slice(h * DV_C, (h + 1) * DV_C)
        for d in range(2):
            xi = vec_ref[d, h][:, 0:1]
            zeta = vec_ref[d, h][:, 1:2]
            gblk = vec_ref[d, h][0:1, 2:3]
            dmask = dm_ref[d, h]
            if latent:
                s0 = s0_ref[0, 0, d, h]
                state = jnp.concatenate([s0, zeros64] if lo == 0 else [zeros64, s0], axis=0)
            else:
                state = jnp.zeros((LANES, DV_C), F32)
            order = range(nc) if d == 0 else range(nc - 1, -1, -1)
            for ci in order:
                rows = slice(ci * CHUNK, (ci + 1) * CHUNK)
                qi = jnp.where(own, q[rows, pair], 0.0)
                ki = jnp.where(own, k[rows, pair], 0.0)
                vi = v[rows, vs]
                qb = qi.astype(BF16)
                att = _dot_nt(qb, ki.astype(BF16)) * dmask
                out = _dot(att.astype(BF16), vi) + _dot(qb, state.astype(BF16)) * xi
                state = gblk * state + _dot_tn((ki * zeta).astype(BF16), vi)
                if d == 0:
                    of_ref[rows, vs] = out
                else:
                    of_ref[rows, vs] = of_ref[rows, vs] + out
            if st_ref is not None:
                st_ref[0, 0, d, h] = state[lo:lo + DK_C, :]

    gate = g_ref[...].astype(F32)
    for h in range(H_C):
        vs = slice(h * DV_C, (h + 1) * DV_C)
        o_ref[:, vs] = (_rms(of_ref[:, vs]) * _silu(gate[:, vs])).astype(o_ref.dtype)


def retention(yc, r, decay_exp, cos, sin, state0, layer, n, seq0, nseq, latent):
    full = lambda shape: pl.BlockSpec(shape, lambda b: (0,) * len(shape))
    in_specs = [
        pl.BlockSpec((n, H_C * DK_C), lambda b: (seq0 + b, P_RET_Q)),
        pl.BlockSpec((n, H_C * DK_C), lambda b: (seq0 + b, P_RET_Q + 1)),
        pl.BlockSpec((n, W_C), lambda b: (seq0 + b, P_RET_V)),
        pl.BlockSpec((n, W_C), lambda b: (seq0 + b, P_RET_V + 1)),
        full((DEPTH, 2, H_C)),
    ]
    args = [r, r, r, r, decay_exp]
    out_specs = [pl.BlockSpec((n, W_C), lambda b: (seq0 + b, 0))]
    out_shape = [jax.ShapeDtypeStruct((T_ALL, W_C), BF16)]
    aliases = {}
    if latent:
        in_specs = [pl.BlockSpec(memory_space=pl.ANY)] + in_specs + [
            full((n, H_C * DK_C)), full((n, H_C * DK_C)),
            pl.BlockSpec((1, 1, 2, H_C, DK_C, DV_C), lambda b: (b, layer, 0, 0, 0, 0)),
        ]
        args = [yc] + args + [cos, sin, state0]
        aliases = {0: 0}
    else:
        out_specs.append(pl.BlockSpec((1, 1, 2, H_C, DK_C, DV_C), lambda b: (b, layer, 0, 0, 0, 0)))
        out_shape.append(jax.ShapeDtypeStruct((nseq, DEPTH, 2, H_C, DK_C, DV_C), F32))
        if state0 is not None:
            in_specs = [pl.BlockSpec(memory_space=pl.ANY)] + in_specs
            args = [state0] + args
            aliases = {0: 1}
    return pl.pallas_call(
        functools.partial(_retention_kernel, layer=layer, n=n, latent=latent),
        grid=(nseq,),
        in_specs=in_specs,
        out_specs=out_specs,
        out_shape=out_shape,
        scratch_shapes=[pltpu.VMEM((2, H_C, CHUNK, CHUNK), F32),
                        pltpu.VMEM((2, H_C, CHUNK, LANES), F32),
                        pltpu.VMEM((n, W_C), F32)],
        input_output_aliases=aliases,
        compiler_params=_params(1),
        name="retention",
    )(*args)


def _merge_kernel(x_ref, ya_ref, yb_ref, yc_ref, g_ref, m_ref, wa_ref, wb_ref, wc_ref, wo_ref,
                  o_ref, wbf_ref, wobf_ref):
    @pl.when(pl.program_id(0) == 0)
    def _():
        wbf_ref[0] = wa_ref[0].astype(BF16)
        wbf_ref[1] = wb_ref[0].astype(BF16)
        wbf_ref[2] = wc_ref[0].astype(BF16)
        wobf_ref[...] = wo_ref[0].astype(BF16)

    g = g_ref[...]
    merged = (g[:, :D_MODEL] * _dot(ya_ref[...], wbf_ref[0])
              + g[:, D_MODEL:2 * D_MODEL] * _dot(yb_ref[...], wbf_ref[1])
              + g[:, 2 * D_MODEL:] * _dot(yc_ref[...], wbf_ref[2]))
    g1 = m_ref[0, 0, 2:3, :]
    o_ref[...] = x_ref[...] + g1 * _dot(merged.astype(BF16), wobf_ref[...])


def merge_branches(x, ya, yb, yc, gates, mods, w_a, w_b, w_c, w_out, layer):
    tile = lambda w: pl.BlockSpec((TM, w), lambda i: (i, 0))
    wspec = lambda k: pl.BlockSpec((1, k, D_MODEL), lambda i: (layer, 0, 0))
    return pl.pallas_call(
        _merge_kernel,
        grid=(N_TILES,),
        in_specs=[
            tile(D_MODEL), tile(W_A), tile(C_B), tile(W_C), tile(3 * D_MODEL),
            pl.BlockSpec((1, 1, 6, D_MODEL), lambda i: (layer, _mod_row(i), 0, 0)),
            wspec(W_A), wspec(C_B), wspec(W_C), wspec(D_MODEL),
        ],
        out_specs=tile(D_MODEL),
        out_shape=jax.ShapeDtypeStruct((T_ALL, D_MODEL), F32),
        scratch_shapes=[pltpu.VMEM((3, W_A, D_MODEL), BF16), pltpu.VMEM((D_MODEL, D_MODEL), BF16)],
        compiler_params=_params(1),
        name="merge_branches",
    )(x, ya, yb, yc, gates, mods, w_a, w_b, w_c, w_out)


def _route(h2, wr_t, b_r):
    logits = _dot3_nt(wr_t, h2)
    m = jnp.max(logits, axis=0, keepdims=True)
    e = jnp.exp(logits - m)
    scores = e / jnp.sum(e, axis=0, keepdims=True)
    sel = scores + b_r
    rows = [sel[i:i + 1, :] for i in range(N_EXPERTS)]
    in_group = []
    gscore = []
    for g in range(N_GROUPS):
        members = range(g * EXP_PER_GROUP, (g + 1) * EXP_PER_GROUP)
        total = None
        for i in members:
            rank = None
            for j in members:
                if j == i:
                    continue
                ahead = (rows[j] >= rows[i]) if j < i else (rows[j] > rows[i])
                ahead = ahead.astype(F32)
                rank = ahead if rank is None else rank + ahead
            chosen = rank < 2.0
            in_group.append(chosen)
            part = jnp.where(chosen, rows[i], 0.0)
            total = part if total is None else total + part
        gscore.append(total)
    gates = []
    bucket_hot = []
    for g in range(N_GROUPS):
        best = None
        for g2 in range(N_GROUPS):
            if g2 == g:
                continue
            wins = gscore[g] > gscore[g2] if g2 < g else gscore[g] >= gscore[g2]
            best = wins if best is None else best & wins
        first = g * EXP_PER_GROUP
        for i in range(first, first + EXP_PER_GROUP):
            gates.append(jnp.where(best & in_group[i], scores[i:i + 1, :], 0.0))
        for a, b in EXPERT_PAIRS:
            bucket_hot.append((best & in_group[first + a] & in_group[first + b]).astype(F32))
    gates = jnp.concatenate(gates, axis=0)
    return gates / jnp.sum(gates, axis=0, keepdims=True), bucket_hot


def _moe_route_kernel(x_ref, g_ref, m_ref, wr_ref, br_ref, tri_ref, tiles_ref, gate_ref, meta_ref, count_ref,
                      *, layer):
    y = _rms(x_ref[...]) * g_ref[layer:layer + 1, :]
    h2 = y * (1.0 + m_ref[0, 0, 4:5, :]) + m_ref[0, 0, 3:4, :]
    gates, bucket_hot = _route(h2, wr_ref[...], br_ref[...])
    for s in range(SUBLANES):
        tiles_ref[pl.ds(s, TM, stride=SUBLANES), :] = h2[:, s * LANES:(s + 1) * LANES]
    padded = jnp.concatenate([gates, jnp.zeros((LANES - N_EXPERTS, TM), F32)], axis=0)
    gate_ref[...] = padded.T
    row = lax.broadcasted_iota(jnp.int32, (BUCKET_ROWS, TM), 0)
    hot = jnp.zeros((BUCKET_ROWS, TM), F32)
    bucket = jnp.zeros((1, TM), F32)
    for k in range(N_BUCKETS):
        hot = jnp.where(row == k, bucket_hot[k], hot)
        bucket = bucket + float(k) * bucket_hot[k]
    before = _dot(hot.astype(BF16), tri_ref[...])
    rank = jnp.sum(hot * before, axis=0, keepdims=True)
    row8 = lax.broadcasted_iota(jnp.int32, (SUBLANES, TM), 0)
    meta = jnp.where(row8 == 0, bucket, jnp.where(row8 == 1, rank, 0.0))
    meta_ref[0] = meta.astype(jnp.int32)
    counts = jnp.sum(hot, axis=1, keepdims=True) + jnp.zeros((BUCKET_ROWS, LANES), F32)
    count_ref[0] = counts.astype(jnp.int32)


def moe_route(x, gain, mods, wr_t, b_r, tri, layer):
    return pl.pallas_call(
        functools.partial(_moe_route_kernel, layer=layer),
        grid=(N_TILES,),
        in_specs=[
            pl.BlockSpec((TM, D_MODEL), lambda i: (i, 0)),
            pl.BlockSpec((DEPTH, D_MODEL), lambda i: (0, 0)),
            pl.BlockSpec((1, 1, 6, D_MODEL), lambda i: (layer, _mod_row(i), 0, 0)),
            pl.BlockSpec((N_EXPERTS, D_MODEL), lambda i: (0, 0)),
            pl.BlockSpec((N_EXPERTS, 1), lambda i: (0, 0)),
            pl.BlockSpec((TM, TM), lambda i: (0, 0)),
        ],
        out_specs=[pl.BlockSpec((TM * SUBLANES, LANES), lambda i: (i, 0)),
                   pl.BlockSpec((TM, LANES), lambda i: (i, 0)),
                   pl.BlockSpec((1, SUBLANES, TM), lambda i: (i, 0, 0)),
                   pl.BlockSpec((1, BUCKET_ROWS, LANES), lambda i: (i, 0, 0))],
        out_shape=[jax.ShapeDtypeStruct((T_ALL * SUBLANES, LANES), F32),
                   jax.ShapeDtypeStruct((T_ALL, LANES), F32),
                   jax.ShapeDtypeStruct((N_TILES, SUBLANES, TM), jnp.int32),
                   jax.ShapeDtypeStruct((N_TILES, BUCKET_ROWS, LANES), jnp.int32)],
        compiler_params=_params(1),
        name="moe_route",
    )(x, gain, mods, wr_t, b_r, tri)


def _moe_positions(meta, counts):
    bucket = meta[:, 0, :]
    rank = meta[:, 1, :]
    cnt = counts[:, :N_BUCKETS, 0]
    total = jnp.sum(cnt, axis=0)
    per_group = total.reshape(N_GROUPS, N_PAIRS)
    group_total = jnp.sum(per_group, axis=1)
    padded = (group_total + (MOE_BLOCK - 1)) // MOE_BLOCK * MOE_BLOCK
    group_start = jnp.cumsum(padded) - padded
    within = jnp.cumsum(per_group, axis=1) - per_group
    bucket_start = (group_start[:, None] + within).reshape(N_BUCKETS)
    base = bucket_start[None, :] + jnp.cumsum(cnt, axis=0) - cnt
    pos = rank
    for k in range(N_BUCKETS):
        pos = pos + jnp.where(bucket == k, base[:, k:k + 1], 0)
    first_block = group_start // MOE_BLOCK
    blk = jnp.arange(MOE_BLOCKS, dtype=jnp.int32)
    block_gid = sum((blk >= first_block[g]).astype(jnp.int32) for g in range(1, N_GROUPS))
    lo = blk[:, None] * MOE_BLOCK
    present = (total[None, :] > 0) & (bucket_start[None, :] < lo + MOE_BLOCK) & (bucket_start + total > lo)
    block_mask = jnp.zeros((MOE_BLOCKS,), jnp.int32)
    for j in range(EXP_PER_GROUP):
        uses = np.array([j in EXPERT_PAIRS[k % N_PAIRS] for k in range(N_BUCKETS)])
        block_mask = block_mask + (jnp.any(present & uses[None, :], axis=1).astype(jnp.int32) << j)
    return pos.reshape(T_ALL).astype(jnp.int32), block_gid, block_mask


def _moe_permute_kernel(pos_ref, tiles_ref, gate_ref, sorted_ref, gsorted_ref):
    i = pl.program_id(0)

    @pl.when(i == 0)
    def _():
        def zero(b, carry):
            start = pl.multiple_of(b * TM, TM)
            sorted_ref[pl.ds(start, TM), :] = jnp.zeros((TM, LANES), F32)
            return carry
        lax.fori_loop(0, T_PAD * SUBLANES // TM, zero, 0)
        gsorted_ref[...] = jnp.zeros((T_PAD, LANES), F32)

    def move(t, carry):
        p = pos_ref[i * TM + t]
        dst = pl.multiple_of(p * SUBLANES, SUBLANES)
        src = pl.multiple_of(t * SUBLANES, SUBLANES)
        sorted_ref[pl.ds(dst, SUBLANES), :] = tiles_ref[pl.ds(src, SUBLANES), :]
        gsorted_ref[pl.ds(p, 1), :] = gate_ref[pl.ds(t, 1), :]
        return carry
    lax.fori_loop(0, TM, move, 0, unroll=8)


def moe_permute(pos, tiles, gates):
    return pl.pallas_call(
        _moe_permute_kernel,
        grid_spec=pltpu.PrefetchScalarGridSpec(
            num_scalar_prefetch=1,
            grid=(N_TILES,),
            in_specs=[pl.BlockSpec((TM * SUBLANES, LANES), lambda i, pos: (i, 0)),
                      pl.BlockSpec((TM, LANES), lambda i, pos: (i, 0))],
            out_specs=[pl.BlockSpec(memory_space=pltpu.VMEM), pl.BlockSpec(memory_space=pltpu.VMEM)],
        ),
        out_shape=[jax.ShapeDtypeStruct((T_PAD * SUBLANES, LANES), F32),
                   jax.ShapeDtypeStruct((T_PAD, LANES), F32)],
        compiler_params=_params(1),
        name="moe_permute",
    )(pos, tiles, gates)


def _group_changed(gid_ref, b):
    return (b == 0) | (gid_ref[b] != gid_ref[jnp.maximum(b - 1, 0)])


def _moe_expert_kernel(gid_ref, mask_ref, s_ref, gate_ref, w1_ref, w3_ref, w2_ref, y_ref,
                       w1b_ref, w3b_ref, w2b_ref, lhs_ref, acc_ref):
    b = pl.program_id(0)

    @pl.when(_group_changed(gid_ref, b))
    def _():
        for j in range(EXP_PER_GROUP):
            cols = slice(j * D_FF, (j + 1) * D_FF)
            w1b_ref[:, cols] = w1_ref[0, 0, j].astype(BF16)
            w3b_ref[:, cols] = w3_ref[0, 0, j].astype(BF16)
        w2b_ref[...] = w2_ref[0, 0].astype(BF16)

    for s in range(SUBLANES):
        lhs_ref[:, s * LANES:(s + 1) * LANES] = s_ref[pl.ds(s, MOE_BLOCK, stride=SUBLANES), :].astype(BF16)
    acc_ref[...] = jnp.zeros_like(acc_ref)
    lane = lax.broadcasted_iota(jnp.int32, (MOE_BLOCK, LANES), 1)
    first = gid_ref[b] * EXP_PER_GROUP
    mask = mask_ref[b]
    for j in range(EXP_PER_GROUP):
        @pl.when(((mask >> j) & 1) == 1)
        def _(j=j):
            cols = slice(j * D_FF, (j + 1) * D_FF)
            lhs = lhs_ref[...]
            a = _dot(lhs, w1b_ref[:, cols])
            g = _dot(lhs, w3b_ref[:, cols])
            gate = jnp.sum(jnp.where(lane == first + j, gate_ref[...], 0.0), axis=1, keepdims=True)
            act = (_silu(a) * g * gate).astype(BF16)
            acc_ref[...] += _dot(act, w2b_ref[cols, :])
    for s in range(SUBLANES):
        y_ref[pl.ds(s, MOE_BLOCK, stride=SUBLANES), :] = acc_ref[:, s * LANES:(s + 1) * LANES]


def moe_experts(block_gid, block_mask, sorted_rows, sorted_gates, w1, w3, w2, layer):
    group_ff = EXP_PER_GROUP * D_FF
    w1g = w1.reshape(DEPTH, N_GROUPS, EXP_PER_GROUP, D_MODEL, D_FF)
    w3g = w3.reshape(DEPTH, N_GROUPS, EXP_PER_GROUP, D_MODEL, D_FF)
    w2g = w2.reshape(DEPTH, N_GROUPS, group_ff, D_MODEL)
    up_spec = pl.BlockSpec((1, 1, EXP_PER_GROUP, D_MODEL, D_FF), lambda b, gid, mask: (layer, gid[b], 0, 0, 0),
                           pipeline_mode=pl.Buffered(1))
    return pl.pallas_call(
        _moe_expert_kernel,
        grid_spec=pltpu.PrefetchScalarGridSpec(
            num_scalar_prefetch=2,
            grid=(MOE_BLOCKS,),
            in_specs=[pl.BlockSpec((MOE_BLOCK * SUBLANES, LANES), lambda b, gid, mask: (b, 0)),
                      pl.BlockSpec((MOE_BLOCK, LANES), lambda b, gid, mask: (b, 0)),
                      up_spec, up_spec,
                      pl.BlockSpec((1, 1, group_ff, D_MODEL), lambda b, gid, mask: (layer, gid[b], 0, 0))],
            out_specs=pl.BlockSpec((MOE_BLOCK * SUBLANES, LANES), lambda b, gid, mask: (b, 0)),
            scratch_shapes=[pltpu.VMEM((D_MODEL, group_ff), BF16), pltpu.VMEM((D_MODEL, group_ff), BF16),
                            pltpu.VMEM((group_ff, D_MODEL), BF16),
                            pltpu.VMEM((MOE_BLOCK, D_MODEL), BF16), pltpu.VMEM((MOE_BLOCK, D_MODEL), F32)],
        ),
        out_shape=jax.ShapeDtypeStruct((T_PAD * SUBLANES, LANES), F32),
        compiler_params=_params(1),
        name="moe_experts",
    )(block_gid, block_mask, sorted_rows, sorted_gates, w1g, w3g, w2g)


def _moe_combine_kernel(pos_ref, ys_ref, x_ref, m_ref, o_ref, buf_ref):
    i = pl.program_id(0)

    def move(t, carry):
        src = pl.multiple_of(pos_ref[i * TM + t] * SUBLANES, SUBLANES)
        dst = pl.multiple_of(t * SUBLANES, SUBLANES)
        buf_ref[pl.ds(dst, SUBLANES), :] = ys_ref[pl.ds(src, SUBLANES), :]
        return carry
    lax.fori_loop(0, TM, move, 0, unroll=8)

    for s in range(SUBLANES):
        cols = slice(s * LANES, (s + 1) * LANES)
        y = buf_ref[pl.ds(s, TM, stride=SUBLANES), :]
        o_ref[:, cols] = x_ref[:, cols] + m_ref[0, 0, 5:6, cols] * y


def moe_combine(pos, y_sorted, x, mods, layer):
    return pl.pallas_call(
        _moe_combine_kernel,
        grid_spec=pltpu.PrefetchScalarGridSpec(
            num_scalar_prefetch=1,
            grid=(N_TILES,),
            in_specs=[
                pl.BlockSpec(memory_space=pltpu.VMEM),
                pl.BlockSpec((TM, D_MODEL), lambda i, pos: (i, 0)),
                pl.BlockSpec((1, 1, 6, D_MODEL), lambda i, pos: (layer, _mod_row(i), 0, 0)),
            ],
            out_specs=pl.BlockSpec((TM, D_MODEL), lambda i, pos: (i, 0)),
            scratch_shapes=[pltpu.VMEM((TM * SUBLANES, LANES), F32)],
        ),
        out_shape=jax.ShapeDtypeStruct((T_ALL, D_MODEL), F32),
        compiler_params=_params(1),
        name="moe_combine",
    )(pos, y_sorted, x, mods)


def moe(x, gain, mods, wr_t, b_r, tri, w1, w3, w2, layer):
    tiles, gates, meta, counts = moe_route(x, gain, mods, wr_t, b_r, tri, layer)
    pos, block_gid, block_mask = _moe_positions(meta, counts)
    sorted_rows, sorted_gates = moe_permute(pos, tiles, gates)
    y_sorted = moe_experts(block_gid, block_mask, sorted_rows, sorted_gates, w1, w3, w2, layer)
    return moe_combine(pos, y_sorted, x, mods, layer)


def _final_norm_kernel(x_ref, g_ref, o_ref):
    o_ref[...] = _rms(x_ref[...]) * g_ref[...]


def final_norm(x, gain, tile0, ntiles):
    return pl.pallas_call(
        _final_norm_kernel,
        grid=(ntiles,),
        in_specs=[pl.BlockSpec((TM, D_MODEL), lambda i: (tile0 + i, 0)),
                  pl.BlockSpec((1, D_MODEL), lambda i: (0, 0))],
        out_specs=pl.BlockSpec((TM, D_MODEL), lambda i: (i, 0)),
        out_shape=jax.ShapeDtypeStruct((ntiles * TM, D_MODEL), F32),
        compiler_params=_params(1),
        name="final_norm",
    )(x, gain.reshape(1, D_MODEL))


def kernel(x_prompt, x_sample, cache_attn_k, cache_attn_v, state_retention, c, c_ctx, w_ada, b_ada, norm1_g, norm2_g, final_g, w_in, attn_lambda, attn_subln_g, hy_conv_w, hy_conv_b, hy_f_w1, hy_f_b1, hy_f_w2, hy_f_b2, hy_f_w3, hy_skip, ret_decay_exp, w_branch_a, w_branch_b, w_branch_c, w_gate, b_gate, w_out, w_router, b_router, moe_w1, moe_w3, moe_w2):
    x = jnp.concatenate([x_prompt.reshape(T_CTX, D_MODEL), x_sample.reshape(T_LAT, D_MODEL)], axis=0)
    cond8 = jnp.concatenate([c_ctx[None, :], c, jnp.zeros((8 - 1 - DEC_BATCH, D_MODEL), F32)], axis=0)
    mods = ada_modulation(cond8, w_ada, b_ada).reshape(DEPTH, 8, 6, D_MODEL)

    cos, sin = _rope_tables()
    cos_c, sin_c = cos[:, :H_C * DK_C], sin[:, :H_C * DK_C]
    cache_k = cache_attn_k.reshape(DEC_BATCH, DEPTH, PAST_LEN, W_A)
    cache_v = cache_attn_v.reshape(DEC_BATCH, DEPTH, PAST_LEN, W_A)
    decay_rates = _filter_decay_rates()
    tables_ctx = _dft_tables(SEQ)
    tables_lat = _dft_tables(DEC_SEQ)
    filt_ctx = hyena_filters(SEQ, _filter_features(SEQ), decay_rates, tables_ctx[0], tables_ctx[1],
                             hy_f_w1, hy_f_b1, hy_f_w2, hy_f_b2, hy_f_w3)
    filt_lat = hyena_filters(DEC_SEQ, _filter_features(DEC_SEQ), decay_rates, tables_lat[0], tables_lat[1],
                             hy_f_w1, hy_f_b1, hy_f_w2, hy_f_b2, hy_f_w3)
    wr_t = w_router.T
    b_r = b_router.reshape(N_EXPERTS, 1)
    tri = jnp.asarray(np.triu(np.ones((TM, TM), np.float32), 1), dtype=BF16)

    caches = ()
    states = None
    for l in range(DEPTH):
        p, kv = in_projection(x, norm1_g, mods, w_in, w_gate, b_gate, l)

        ya, *caches = attention_ctx(p, kv, attn_lambda, attn_subln_g, l, caches)
        ya = attention_lat(ya, p, kv, cache_k, cache_v, cos, sin, attn_lambda, attn_subln_g, l)
        yb = hyena(None, p, hy_conv_w, hy_conv_b, hy_skip, filt_ctx[0], filt_ctx[1], tables_ctx,
                   l, SEQ, 0, BATCH)
        yb = hyena(yb, p, hy_conv_w, hy_conv_b, hy_skip, filt_lat[0], filt_lat[1], tables_lat,
                   l, DEC_SEQ, T_CTX // DEC_SEQ, DEC_BATCH)
        yc, states = retention(None, p, ret_decay_exp, None, None, states, l, SEQ, 0, BATCH, False)
        yc = retention(yc, p, ret_decay_exp, cos_c, sin_c, state_retention, l, DEC_SEQ,
                       T_CTX // DEC_SEQ, DEC_BATCH, True)[0]

        x = merge_branches(x, ya, yb, yc, p, mods, w_branch_a, w_branch_b, w_branch_c, w_out, l)
        x = moe(x, norm2_g, mods, wr_t, b_r, tri, moe_w1, moe_w3, moe_w2, l)

    new_k, new_v = caches
    y_prompt = final_norm(x, final_g, 0, CTX_TILES).reshape(BATCH, SEQ, D_MODEL)
    y_sample = final_norm(x, final_g, CTX_TILES, N_TILES - CTX_TILES).reshape(DEC_BATCH, DEC_SEQ, D_MODEL)
    return (y_prompt, y_sample, new_k, new_v, states)
```

```python
import functools
import math

import jax
import jax.numpy as jnp
import numpy as np
from jax import lax
from jax.experimental import pallas as pl
from jax.experimental.pallas import tpu as pltpu

F32 = jnp.float32
BF16 = jnp.bfloat16

D_MODEL = 1024
BATCH = 16
SEQ = 256
DEPTH = 4
DEC_BATCH = 2
DEC_SEQ = 1024
PAST_LEN = 256
GRID_W = 64
EPS = 1e-6
ROPE_BASE = 10000.0
H_A = 4
DH_A = 64
DV_A = 128
W_A = 512
C_B = 512
SHORT_K = 3
H_EMB = 33
H_FFN = 64
H_FAST_DECAY = 0.3
H_SLOW_DECAY = 1.5
H_TARGET = 1e-2
H_C = 4
DK_C = 64
DV_C = 128
W_C = 512
N_EXPERTS = 16
N_GROUPS = 4
EXP_PER_GROUP = 4
D_FF = 512
D_IN = 4608

T_CTX = BATCH * SEQ
T_LAT = DEC_BATCH * DEC_SEQ
T_ALL = T_CTX + T_LAT
TM = 1024
N_TILES = T_ALL // TM
CTX_TILES = T_CTX // TM
LANES = 128
SUBLANES = 8
MOE_BLOCK = 256
MOE_BLOCKS = T_ALL // MOE_BLOCK + N_GROUPS
T_PAD = MOE_BLOCKS * MOE_BLOCK
VMEM_LIMIT = 56 * 1024 * 1024


def _params(n_axes):
    return pltpu.CompilerParams(
        dimension_semantics=("arbitrary",) * n_axes, vmem_limit_bytes=VMEM_LIMIT)


def _mod_row(i):
    return jnp.maximum(i - (CTX_TILES - 1), 0)


def _rms(x):
    return x * lax.rsqrt(jnp.mean(x * x, axis=-1, keepdims=True) + EPS)


def _silu(x):
    return x * jax.nn.sigmoid(x)


def _dot(a, b):
    return jnp.dot(a, b, preferred_element_type=F32)


def _dot_nt(a, b):
    return lax.dot_general(a, b, (((1,), (1,)), ((), ())), preferred_element_type=F32)


def _dot_tn(a, b):
    return lax.dot_general(a, b, (((0,), (0,)), ((), ())), preferred_element_type=F32)


def _split3(x):
    hi = x.astype(BF16)
    lo = (x - hi.astype(F32)).astype(BF16)
    return hi, lo


def _dot3(a, b):
    ah, al = _split3(a)
    bh, bl = _split3(b)
    return _dot(ah, bh) + (_dot(ah, bl) + _dot(al, bh))


def _dot3_nt(a, b):
    ah, al = _split3(a)
    bh, bl = _split3(b)
    return _dot_nt(ah, bh) + (_dot_nt(ah, bl) + _dot_nt(al, bh))


def _ada_kernel(c_ref, w_ref, b_ref, o_ref):
    s = _silu(c_ref[...])
    o_ref[0] = _dot(s.astype(BF16), w_ref[0].astype(BF16)) + b_ref[0]


def ada_modulation(cond8, w_ada, b_ada):
    tn = 1536
    n = 6 * D_MODEL
    return pl.pallas_call(
        _ada_kernel,
        grid=(DEPTH, n // tn),
        in_specs=[
            pl.BlockSpec((8, D_MODEL), lambda l, j: (0, 0)),
            pl.BlockSpec((1, D_MODEL, tn), lambda l, j: (l, 0, j)),
            pl.BlockSpec((1, 1, tn), lambda l, j: (l, 0, j)),
        ],
        out_specs=pl.BlockSpec((1, 8, tn), lambda l, j: (l, 0, j)),
        out_shape=jax.ShapeDtypeStruct((DEPTH, 8, n), F32),
        compiler_params=_params(2),
        name="ada_modulation",
    )(cond8, w_ada, b_ada.reshape(DEPTH, 1, n))


PROJ_TM = 3072
PROJ_TN = 512
GATE_BLOCKS = 3 * D_MODEL // PROJ_TN
IN_BLOCKS = D_IN // PROJ_TN
P_WIDTH = 3 * D_MODEL + D_IN
KV_FIRST = GATE_BLOCKS + 1
P_Q = 3 * D_MODEL // W_A
P_HYENA = (3 * D_MODEL + 3 * W_A) // (3 * C_B)
P_RET_Q = (3 * D_MODEL + 3 * W_A + 3 * C_B) // (H_C * DK_C)
P_RET_V = (3 * D_MODEL + 3 * W_A + 3 * C_B + 2 * H_C * DK_C) // W_C


def _in_proj_kernel(x_ref, g_ref, m_ref, win_ref, wg_ref, bg_ref, p_ref, kv_ref, h_ref, *, layer):
    i = pl.program_id(0)
    j = pl.program_id(1)

    @pl.when(j == 0)
    def _():
        for s in range(PROJ_TM // TM):
            rows = slice(s * TM, (s + 1) * TM)
            mod = m_ref[layer, _mod_row(i * (PROJ_TM // TM) + s)]
            y = _rms(x_ref[rows, :]) * g_ref[layer:layer + 1, :]
            h_ref[rows, :] = (y * (1.0 + mod[1:2, :]) + mod[0:1, :]).astype(BF16)

    @pl.when(j < GATE_BLOCKS)
    def _():
        acc = _dot(h_ref[...], wg_ref[0].astype(BF16)) + bg_ref[0]
        p_ref[...] = jax.nn.sigmoid(acc).astype(BF16)

    @pl.when(j >= GATE_BLOCKS)
    def _():
        acc = _dot(h_ref[...], win_ref[0].astype(BF16))
        p_ref[...] = acc.astype(BF16)

        @pl.when((j == KV_FIRST) | (j == KV_FIRST + 1))
        def _():
            kv_ref[...] = acc


def in_projection(x, gain, mods, w_in, w_gate, b_gate, layer):
    return pl.pallas_call(
        functools.partial(_in_proj_kernel, layer=layer),
        grid=(T_ALL // PROJ_TM, GATE_BLOCKS + IN_BLOCKS),
        in_specs=[
            pl.BlockSpec((PROJ_TM, D_MODEL), lambda i, j: (i, 0), pipeline_mode=pl.Buffered(1)),
            pl.BlockSpec((DEPTH, D_MODEL), lambda i, j: (0, 0)),
            pl.BlockSpec((DEPTH, 8, 6, D_MODEL), lambda i, j: (0, 0, 0, 0)),
            pl.BlockSpec((1, D_MODEL, PROJ_TN), lambda i, j: (layer, 0, jnp.maximum(j - GATE_BLOCKS, 0))),
            pl.BlockSpec((1, D_MODEL, PROJ_TN), lambda i, j: (layer, 0, jnp.minimum(j, GATE_BLOCKS - 1))),
            pl.BlockSpec((1, 1, PROJ_TN), lambda i, j: (layer, 0, jnp.minimum(j, GATE_BLOCKS - 1))),
        ],
        out_specs=[
            pl.BlockSpec((PROJ_TM, PROJ_TN), lambda i, j: (i, j)),
            pl.BlockSpec((PROJ_TM, PROJ_TN), lambda i, j: (i, jnp.clip(j - KV_FIRST, 0, 1))),
        ],
        out_shape=[jax.ShapeDtypeStruct((T_ALL, P_WIDTH), BF16),
                   jax.ShapeDtypeStruct((T_ALL, 2 * W_A), F32)],
        scratch_shapes=[pltpu.VMEM((PROJ_TM, D_MODEL), BF16)],
        compiler_params=_params(2),
        name="in_projection",
    )(x, gain, mods, w_in, w_gate, b_gate.reshape(DEPTH, 1, 3 * D_MODEL))


def _rope_tables():
    n = DEC_SEQ
    rows = n // GRID_W
    row = jnp.repeat(jnp.arange(rows, dtype=F32), GRID_W)
    col = jnp.tile(jnp.arange(GRID_W, dtype=F32), rows)
    quarter = DH_A // 4
    inv = ROPE_BASE ** (-jnp.arange(quarter, dtype=F32) / quarter)
    ar = row[:, None] * inv
    ac = col[:, None] * inv
    ang = jnp.concatenate([ar, ar, ac, ac], axis=-1)
    cos = jnp.tile(jnp.cos(ang), (1, 512 // DH_A))
    sgn = jnp.tile(jnp.concatenate([-jnp.ones((quarter,), F32), jnp.ones((quarter,), F32)]), 512 // (2 * quarter))
    sin = jnp.tile(jnp.sin(ang), (1, 512 // DH_A)) * sgn
    return cos, sin


def _rope(x, cos, sin):
    w = x.shape[-1]
    lane = lax.broadcasted_iota(jnp.int32, x.shape, 1)
    first = (lane % 32) < 16
    rot = jnp.where(first, pltpu.roll(x, w - 16, 1), pltpu.roll(x, 16, 1))
    return x * cos + rot * sin


def _lambda(al_ref, layer):
    al = al_ref[layer]
    a = jnp.sum(al[0:1, :] * al[1:2, :], axis=-1, keepdims=True)
    b = jnp.sum(al[2:3, :] * al[3:4, :], axis=-1, keepdims=True)
    lam_init = 0.8 - 0.6 * math.exp(-0.3 * layer)
    return jnp.exp(a) - jnp.exp(b) + lam_init, lam_init


def _softmax_parts(s):
    m = jnp.max(s, axis=-1, keepdims=True)
    e = jnp.exp2(s - m)
    return e.astype(BF16), 1.0 / jnp.sum(e, axis=-1, keepdims=True)


def _diff_attention(q, k, v, lam, lam_init, subln, o_ref):
    lane = lax.broadcasted_iota(jnp.int32, q.shape, 1)
    q = q * (DH_A ** -0.5 * math.log2(math.e))
    q0 = jnp.where((lane % LANES) < DH_A, q, 0.0).astype(BF16)
    q1 = jnp.where((lane % LANES) >= DH_A, q, 0.0).astype(BF16)
    for h in range(H_A):
        sl = slice(h * LANES, (h + 1) * LANES)
        kh = k[:, sl]
        vh = v[:, sl]
        e0, r0 = _softmax_parts(_dot_nt(q0[:, sl], kh))
        e1, r1 = _softmax_parts(_dot_nt(q1[:, sl], kh))
        o = _dot(e0, vh) * r0 - _dot(e1, vh) * (lam * r1)
        o = _rms(o) * subln * (1.0 - lam_init)
        o_ref[:, sl] = o.astype(o_ref.dtype)


def _attn_ctx_kernel(*refs, layer, n_caches):
    q_ref, k_ref, v_ref, al_ref, g_ref, o_ref, knew_ref, vnew_ref = refs[n_caches:]
    lam, lam_init = _lambda(al_ref, layer)
    subln = g_ref[layer:layer + 1, :]
    k = k_ref[...]
    v = v_ref[...]
    for h in range(H_A):
        knew_ref[0, 0, :, h, :] = k[:, h * LANES:(h + 1) * LANES]
        vnew_ref[0, 0, :, h, :] = v[:, h * LANES:(h + 1) * LANES]
    _diff_attention(q_ref[...].astype(F32), k.astype(BF16), v.astype(BF16), lam, lam_init, subln, o_ref)


def attention_ctx(q, kv, attn_lambda, subln_g, layer, caches):
    cache_shape = jax.ShapeDtypeStruct((BATCH, DEPTH, SEQ, H_A, DV_A), F32)
    cache_spec = pl.BlockSpec((1, 1, SEQ, H_A, DV_A), lambda b: (b, layer, 0, 0, 0))
    n_caches = len(caches)
    return pl.pallas_call(
        functools.partial(_attn_ctx_kernel, layer=layer, n_caches=n_caches),
        grid=(BATCH,),
        in_specs=[pl.BlockSpec(memory_space=pl.ANY)] * n_caches + [
            pl.BlockSpec((SEQ, W_A), lambda b: (b, P_Q)),
            pl.BlockSpec((SEQ, W_A), lambda b: (b, 0)),
            pl.BlockSpec((SEQ, W_A), lambda b: (b, 1)),
            pl.BlockSpec((DEPTH, 4, DH_A), lambda b: (0, 0, 0)),
            pl.BlockSpec((DEPTH, DV_A), lambda b: (0, 0)),
        ],
        out_specs=[pl.BlockSpec((SEQ, W_A), lambda b: (b, 0)), cache_spec, cache_spec],
        out_shape=[jax.ShapeDtypeStruct((T_ALL, W_A), BF16), cache_shape, cache_shape],
        input_output_aliases={i: i + 1 for i in range(n_caches)},
        compiler_params=_params(1),
        name="attention_ctx",
    )(*caches, q, kv, kv, attn_lambda, subln_g)


QB = 256


def _attn_lat_kernel(ya_ref, q_ref, k_ref, v_ref, ck_ref, cv_ref, cosq_ref, sinq_ref, cos_ref, sin_ref,
                     al_ref, g_ref, o_ref, kall_ref, vall_ref, *, layer):
    del ya_ref

    @pl.when(pl.program_id(1) == 0)
    def _():
        kall_ref[0:PAST_LEN, :] = ck_ref[0, 0].astype(BF16)
        vall_ref[0:PAST_LEN, :] = cv_ref[0, 0].astype(BF16)
        kall_ref[PAST_LEN:, :] = _rope(k_ref[...], cos_ref[...], sin_ref[...]).astype(BF16)
        vall_ref[PAST_LEN:, :] = v_ref[...].astype(BF16)

    lam, lam_init = _lambda(al_ref, layer)
    subln = g_ref[layer:layer + 1, :]
    q = _rope(q_ref[...].astype(F32), cosq_ref[...], sinq_ref[...])
    _diff_attention(q, kall_ref[...], vall_ref[...], lam, lam_init, subln, o_ref)


def attention_lat(ya, q, kv, cache_k, cache_v, cos, sin, attn_lambda, subln_g, layer):
    nqb = DEC_SEQ // QB
    row0 = T_CTX // QB
    seq0 = T_CTX // DEC_SEQ
    return pl.pallas_call(
        functools.partial(_attn_lat_kernel, layer=layer),
        grid=(DEC_BATCH, nqb),
        in_specs=[
            pl.BlockSpec(memory_space=pl.ANY),
            pl.BlockSpec((QB, W_A), lambda b, i: (row0 + b * nqb + i, P_Q)),
            pl.BlockSpec((DEC_SEQ, W_A), lambda b, i: (seq0 + b, 0)),
            pl.BlockSpec((DEC_SEQ, W_A), lambda b, i: (seq0 + b, 1)),
            pl.BlockSpec((1, 1, PAST_LEN, W_A), lambda b, i: (b, layer, 0, 0)),
            pl.BlockSpec((1, 1, PAST_LEN, W_A), lambda b, i: (b, layer, 0, 0)),
            pl.BlockSpec((QB, W_A), lambda b, i: (i, 0)),
            pl.BlockSpec((QB, W_A), lambda b, i: (i, 0)),
            pl.BlockSpec((DEC_SEQ, W_A), lambda b, i: (0, 0)),
            pl.BlockSpec((DEC_SEQ, W_A), lambda b, i: (0, 0)),
            pl.BlockSpec((DEPTH, 4, DH_A), lambda b, i: (0, 0, 0)),
            pl.BlockSpec((DEPTH, DV_A), lambda b, i: (0, 0)),
        ],
        out_specs=pl.BlockSpec((QB, W_A), lambda b, i: (row0 + b * nqb + i, 0)),
        out_shape=jax.ShapeDtypeStruct((T_ALL, W_A), BF16),
        scratch_shapes=[pltpu.VMEM((PAST_LEN + DEC_SEQ, W_A), BF16),
                        pltpu.VMEM((PAST_LEN + DEC_SEQ, W_A), BF16)],
        input_output_aliases={0: 0},
        compiler_params=_params(2),
        name="attention_lat",
    )(ya, q, kv, kv, cache_k, cache_v, cos, sin, cos, sin, attn_lambda, subln_g)


def _dft_tables(n):
    k = np.arange(n, dtype=np.int64)
    prod = (2 * k[:, None] + 1) * k[None, :]
    ang = (prod % (4 * n)).astype(np.float64) * (math.pi / (2 * n))
    c = np.cos(ang).astype(np.float32)
    s = np.sin(ang).astype(np.float32)
    return tuple(jnp.asarray(t).astype(BF16) for t in (c, s, c.T, s.T))


def _filter_features(n):
    t = jnp.linspace(0.0, 1.0, n, dtype=F32)[:, None]
    bands = (H_EMB - 1) // 2
    w = 2.0 * math.pi * jnp.arange(n, dtype=F32)[:, None] / n
    fr = jnp.linspace(1e-4, bands - 1, bands, dtype=F32)
    z = jnp.concatenate([t, jnp.cos(w * fr), -jnp.sin(w * fr)], axis=-1)
    return jnp.pad(z, ((0, 0), (0, LANES - H_EMB)))


def _filter_decay_rates():
    max_decay = math.log(H_TARGET) / H_FAST_DECAY
    min_decay = math.log(H_TARGET) / H_SLOW_DECAY
    return jnp.abs(jnp.linspace(min_decay, max_decay, C_B, dtype=F32))[None, :]


def _hyena_filter_kernel(z_ref, w1_ref, b1_ref, w2_ref, b2_ref, w3_ref, dr_ref, c_ref, s_ref,
                         gre_ref, gim_ref):
    z = z_ref[...]
    hid = jnp.sin(_dot3(z, w1_ref[0]) + b1_ref[0])
    hid = jnp.sin(_dot3(hid, w2_ref[0]) + b2_ref[0])
    h = _dot3(hid, w3_ref[0])
    window = jnp.exp(-z[:, 0:1] * dr_ref[...])
    hf = h[:, :C_B] * window
    hb = h[:, C_B:] * window
    total = jnp.sum(jnp.abs(hf) + jnp.abs(hb), axis=0, keepdims=True)
    hf = hf / total
    hb = hb / total
    row = lax.broadcasted_iota(jnp.int32, hb.shape, 0)
    hb = jnp.where(row == 0, 0.0, hb)
    ah, al = _split3(hf + hb)
    dh, dl = _split3(hb - hf)
    gre_ref[0] = _dot(c_ref[...], ah) + _dot(c_ref[...], al)
    gim_ref[0] = _dot(s_ref[...], dh) + _dot(s_ref[...], dl)


def hyena_filters(n, z, decay_rates, cf, sf, f_w1, f_b1, f_w2, f_b2, f_w3):
    w1 = jnp.pad(f_w1, ((0, 0), (0, LANES - H_EMB), (0, 0)))
    full = lambda shape: pl.BlockSpec(shape, lambda l: (0,) * len(shape))
    per_layer = lambda shape: pl.BlockSpec((1,) + shape, lambda l: (l,) + (0,) * len(shape))
    return pl.pallas_call(
        _hyena_filter_kernel,
        grid=(DEPTH,),
        in_specs=[
            full((n, LANES)),
            per_layer((LANES, H_FFN)), per_layer((1, H_FFN)),
            per_layer((H_FFN, H_FFN)), per_layer((1, H_FFN)),
            per_layer((H_FFN, 2 * C_B)),
            full((1, C_B)), full((n, n)), full((n, n)),
        ],
        out_specs=[per_layer((n, C_B)), per_layer((n, C_B))],
        out_shape=[jax.ShapeDtypeStruct((DEPTH, n, C_B), F32)] * 2,
        compiler_params=_params(1),
        name="hyena_filters",
    )(z, w1, f_b1.reshape(DEPTH, 1, H_FFN), f_w2, f_b2.reshape(DEPTH, 1, H_FFN), f_w3,
      decay_rates, cf, sf)


def _hyena_kernel(*refs, layer, n, aliased):
    if aliased:
        refs = refs[1:]
    u_ref, cw_ref, cb_ref, skip_ref, gre_ref, gim_ref, c_ref, s_ref, ct_ref, st_ref, o_ref = refs
    u = u_ref[...].astype(F32)
    row = lax.broadcasted_iota(jnp.int32, u.shape, 0)
    prev = jnp.where(row == 0, 0.0, pltpu.roll(u, 1, 0))
    nxt = jnp.where(row == n - 1, 0.0, pltpu.roll(u, n - 1, 0))
    cw = cw_ref[layer]
    uc = cb_ref[layer:layer + 1, :] + prev * cw[0:1, :] + u * cw[1:2, :] + nxt * cw[2:3, :]
    x0 = uc[:, :C_B]
    x1 = uc[:, C_B:2 * C_B]
    v = uc[:, 2 * C_B:]
    w = v * x1
    wb = w.astype(BF16)
    ure = _dot(c_ref[...], wb)
    uim = -_dot(s_ref[...], wb)
    gre = gre_ref[0]
    gim = gim_ref[0]
    yre = (ure * gre - uim * gim).astype(BF16)
    yim = (ure * gim + uim * gre).astype(BF16)
    y = (_dot(ct_ref[...], yre) - _dot(st_ref[...], yim)) * (1.0 / n)
    y = y + w * skip_ref[layer:layer + 1, :]
    o_ref[...] = (x0 * y).astype(o_ref.dtype)


def hyena(yb, r, conv_w, conv_b, skip, gre, gim, tables, layer, n, seq0, nseq):
    cf, sf, cft, sft = tables
    aliased = yb is not None
    full = lambda shape: pl.BlockSpec(shape, lambda b: (0,) * len(shape))
    in_specs = [
        pl.BlockSpec((n, 3 * C_B), lambda b: (seq0 + b, P_HYENA)),
        full((DEPTH, SHORT_K, 3 * C_B)), full((DEPTH, 3 * C_B)), full((DEPTH, C_B)),
        pl.BlockSpec((1, n, C_B), lambda b: (layer, 0, 0)),
        pl.BlockSpec((1, n, C_B), lambda b: (layer, 0, 0)),
        full((n, n)), full((n, n)), full((n, n)), full((n, n)),
    ]
    args = [r, conv_w, conv_b, skip, gre, gim, cf, sf, cft, sft]
    if aliased:
        in_specs = [pl.BlockSpec(memory_space=pl.ANY)] + in_specs
        args = [yb] + args
    return pl.pallas_call(
        functools.partial(_hyena_kernel, layer=layer, n=n, aliased=aliased),
        grid=(nseq,),
        in_specs=in_specs,
        out_specs=pl.BlockSpec((n, C_B), lambda b: (seq0 + b, 0)),
        out_shape=jax.ShapeDtypeStruct((T_ALL, C_B), BF16),
        input_output_aliases={0: 0} if aliased else {},
        compiler_params=_params(1),
        name="hyena",
    )(*args)


RET_QB = 256


def _retention_kernel(*refs, layer, n, latent):
    if latent:
        _, q_ref, k_ref, v_ref, g_ref, de_ref, cos_ref, sin_ref, s0_ref, o_ref, w_ref = refs
        st_ref = None
    else:
        q_ref, k_ref, v_ref, g_ref, de_ref, o_ref, st_ref, w_ref = refs[-8:]
    log_gamma = jnp.log1p(-jnp.exp2(-de_ref[layer]))

    @pl.when(pl.program_id(0) == 0)
    def _():
        t = lax.broadcasted_iota(jnp.int32, (n, n), 0)
        s = lax.broadcasted_iota(jnp.int32, (n, n), 1)
        lag = (t - s).astype(F32)
        for h in range(H_C):
            rate = jnp.where(lag > 0, log_gamma[0:1, h:h + 1], log_gamma[1:2, h:h + 1])
            w_ref[h] = jnp.where(lag == 0, 2.0, jnp.exp(rate * jnp.abs(lag)))

    q = q_ref[...].astype(F32)
    k = k_ref[...].astype(F32) * (DK_C ** -0.5)
    if latent:
        q = _rope(q, cos_ref[...], sin_ref[...])
        k = _rope(k, cos_ref[...], sin_ref[...])
    v = v_ref[...].astype(BF16)
    gate = g_ref[...].astype(F32)
    lane = lax.broadcasted_iota(jnp.int32, (n, LANES), 1)
    pos = lax.broadcasted_iota(jnp.int32, (n, 1), 0).astype(F32)
    zeros64 = jnp.zeros((DK_C, DV_C), F32)

    for h in range(H_C):
        pair = slice((h // 2) * LANES, (h // 2 + 1) * LANES)
        lo = (h % 2) * DK_C
        own = (lane >= lo) & (lane < lo + DK_C)
        vs = slice(h * DV_C, (h + 1) * DV_C)
        gf = log_gamma[0:1, h:h + 1]
        gb = log_gamma[1:2, h:h + 1]
        qh = jnp.where(own, q[:, pair], 0.0).astype(BF16)
        kh = jnp.where(own, k[:, pair], 0.0)
        kb = kh.astype(BF16)
        vh = v[:, vs]
        if latent:
            s0 = [jnp.concatenate([s0_ref[0, 0, d, h], zeros64] if lo == 0 else [zeros64, s0_ref[0, 0, d, h]],
                                  axis=0).astype(BF16) for d in range(2)]
            carry = (_dot(qh, s0[0]) * jnp.exp(gf * (pos + 1.0))
                     + _dot(qh, s0[1]) * jnp.exp(gb * (float(n) - pos)))
        for r0 in range(0, n, RET_QB):
            rows = slice(r0, r0 + RET_QB)
            att = _dot_nt(qh[rows], kb) * w_ref[h, rows, :]
            out = _dot(att.astype(BF16), vh)
            if latent:
                out = out + carry[rows]
            o_ref[rows, vs] = (_rms(out) * _silu(gate[rows, vs])).astype(o_ref.dtype)
        if st_ref is not None:
            sf = _dot_tn((kh * jnp.exp(gf * (float(n - 1) - pos))).astype(BF16), vh)
            sb = _dot_tn((kh * jnp.exp(gb * pos)).astype(BF16), vh)
            st_ref[0, 0, 0, h] = sf[lo:lo + DK_C, :]
            st_ref[0, 0, 1, h] = sb[lo:lo + DK_C, :]


def retention(yc, r, decay_exp, cos, sin, state0, layer, n, seq0, nseq, latent):
    full = lambda shape: pl.BlockSpec(shape, lambda b: (0,) * len(shape))
    in_specs = [
        pl.BlockSpec((n, H_C * DK_C), lambda b: (seq0 + b, P_RET_Q)),
        pl.BlockSpec((n, H_C * DK_C), lambda b: (seq0 + b, P_RET_Q + 1)),
        pl.BlockSpec((n, W_C), lambda b: (seq0 + b, P_RET_V)),
        pl.BlockSpec((n, W_C), lambda b: (seq0 + b, P_RET_V + 1)),
        full((DEPTH, 2, H_C)),
    ]
    args = [r, r, r, r, decay_exp]
    out_specs = [pl.BlockSpec((n, W_C), lambda b: (seq0 + b, 0))]
    out_shape = [jax.ShapeDtypeStruct((T_ALL, W_C), BF16)]
    aliases = {}
    if latent:
        in_specs = [pl.BlockSpec(memory_space=pl.ANY)] + in_specs + [
            full((n, H_C * DK_C)), full((n, H_C * DK_C)),
            pl.BlockSpec((1, 1, 2, H_C, DK_C, DV_C), lambda b: (b, layer, 0, 0, 0, 0)),
        ]
        args = [yc] + args + [cos, sin, state0]
        aliases = {0: 0}
    else:
        out_specs.append(pl.BlockSpec((1, 1, 2, H_C, DK_C, DV_C), lambda b: (b, layer, 0, 0, 0, 0)))
        out_shape.append(jax.ShapeDtypeStruct((nseq, DEPTH, 2, H_C, DK_C, DV_C), F32))
        if state0 is not None:
            in_specs = [pl.BlockSpec(memory_space=pl.ANY)] + in_specs
            args = [state0] + args
            aliases = {0: 1}
    return pl.pallas_call(
        functools.partial(_retention_kernel, layer=layer, n=n, latent=latent),
        grid=(nseq,),
        in_specs=in_specs,
        out_specs=out_specs,
        out_shape=out_shape,
        scratch_shapes=[pltpu.VMEM((H_C, n, n), F32)],
        input_output_aliases=aliases,
        compiler_params=_params(1),
        name="retention",
    )(*args)


def _merge_kernel(x_ref, ya_ref, yb_ref, yc_ref, g_ref, m_ref, wa_ref, wb_ref, wc_ref, wo_ref,
                  o_ref, wbf_ref, wobf_ref):
    @pl.when(pl.program_id(0) == 0)
    def _():
        wbf_ref[0] = wa_ref[0].astype(BF16)
        wbf_ref[1] = wb_ref[0].astype(BF16)
        wbf_ref[2] = wc_ref[0].astype(BF16)
        wobf_ref[...] = wo_ref[0].astype(BF16)

    g = g_ref[...]
    merged = (g[:, :D_MODEL] * _dot(ya_ref[...], wbf_ref[0])
              + g[:, D_MODEL:2 * D_MODEL] * _dot(yb_ref[...], wbf_ref[1])
              + g[:, 2 * D_MODEL:] * _dot(yc_ref[...], wbf_ref[2]))
    g1 = m_ref[0, 0, 2:3, :]
    o_ref[...] = x_ref[...] + g1 * _dot(merged.astype(BF16), wobf_ref[...])


def merge_branches(x, ya, yb, yc, gates, mods, w_a, w_b, w_c, w_out, layer):
    tile = lambda w: pl.BlockSpec((TM, w), lambda i: (i, 0))
    wspec = lambda k: pl.BlockSpec((1, k, D_MODEL), lambda i: (layer, 0, 0))
    return pl.pallas_call(
        _merge_kernel,
        grid=(N_TILES,),
        in_specs=[
            tile(D_MODEL), tile(W_A), tile(C_B), tile(W_C), tile(3 * D_MODEL),
            pl.BlockSpec((1, 1, 6, D_MODEL), lambda i: (layer, _mod_row(i), 0, 0)),
            wspec(W_A), wspec(C_B), wspec(W_C), wspec(D_MODEL),
        ],
        out_specs=tile(D_MODEL),
        out_shape=jax.ShapeDtypeStruct((T_ALL, D_MODEL), F32),
        scratch_shapes=[pltpu.VMEM((3, W_A, D_MODEL), BF16), pltpu.VMEM((D_MODEL, D_MODEL), BF16)],
        compiler_params=_params(1),
        name="merge_branches",
    )(x, ya, yb, yc, gates, mods, w_a, w_b, w_c, w_out)


def _route(h2, wr_t, b_r):
    logits = _dot3_nt(wr_t, h2)
    m = jnp.max(logits, axis=0, keepdims=True)
    e = jnp.exp(logits - m)
    scores = e / jnp.sum(e, axis=0, keepdims=True)
    sel = scores + b_r
    rows = [sel[i:i + 1, :] for i in range(N_EXPERTS)]
    in_group = []
    gscore = []
    for g in range(N_GROUPS):
        members = range(g * EXP_PER_GROUP, (g + 1) * EXP_PER_GROUP)
        total = None
        for i in members:
            rank = None
            for j in members:
                if j == i:
                    continue
                ahead = (rows[j] >= rows[i]) if j < i else (rows[j] > rows[i])
                ahead = ahead.astype(F32)
                rank = ahead if rank is None else rank + ahead
            chosen = rank < 2.0
            in_group.append(chosen)
            part = jnp.where(chosen, rows[i], 0.0)
            total = part if total is None else total + part
        gscore.append(total)
    gates = []
    group_hot = []
    for g in range(N_GROUPS):
        best = None
        for g2 in range(N_GROUPS):
            if g2 == g:
                continue
            wins = gscore[g] > gscore[g2] if g2 < g else gscore[g] >= gscore[g2]
            best = wins if best is None else best & wins
        group_hot.append(best.astype(F32))
        for i in range(g * EXP_PER_GROUP, (g + 1) * EXP_PER_GROUP):
            gates.append(jnp.where(best & in_group[i], scores[i:i + 1, :], 0.0))
    gates = jnp.concatenate(gates, axis=0)
    return gates / jnp.sum(gates, axis=0, keepdims=True), group_hot


def _moe_route_kernel(x_ref, g_ref, m_ref, wr_ref, br_ref, tri_ref, tiles_ref, gate_ref, meta_ref, count_ref,
                      *, layer):
    y = _rms(x_ref[...]) * g_ref[layer:layer + 1, :]
    h2 = y * (1.0 + m_ref[0, 0, 4:5, :]) + m_ref[0, 0, 3:4, :]
    gates, group_hot = _route(h2, wr_ref[...], br_ref[...])
    for s in range(SUBLANES):
        tiles_ref[pl.ds(s, TM, stride=SUBLANES), :] = h2[:, s * LANES:(s + 1) * LANES]
    padded = jnp.concatenate([gates, jnp.zeros((LANES - N_EXPERTS, TM), F32)], axis=0)
    gate_ref[...] = padded.T
    row = lax.broadcasted_iota(jnp.int32, (SUBLANES, TM), 0)
    hot = jnp.zeros((SUBLANES, TM), F32)
    for g in range(N_GROUPS):
        hot = jnp.where(row == g, group_hot[g], hot)
    before = _dot(hot.astype(BF16), tri_ref[...])
    rank = jnp.sum(hot * before, axis=0, keepdims=True)
    gid = group_hot[1] + 2.0 * group_hot[2] + 3.0 * group_hot[3]
    meta = jnp.where(row == 0, gid, jnp.where(row == 1, rank, 0.0))
    meta_ref[0] = meta.astype(jnp.int32)
    counts = jnp.sum(hot, axis=1, keepdims=True) + jnp.zeros((SUBLANES, LANES), F32)
    count_ref[0] = counts.astype(jnp.int32)


def moe_route(x, gain, mods, wr_t, b_r, tri, layer):
    return pl.pallas_call(
        functools.partial(_moe_route_kernel, layer=layer),
        grid=(N_TILES,),
        in_specs=[
            pl.BlockSpec((TM, D_MODEL), lambda i: (i, 0)),
            pl.BlockSpec((DEPTH, D_MODEL), lambda i: (0, 0)),
            pl.BlockSpec((1, 1, 6, D_MODEL), lambda i: (layer, _mod_row(i), 0, 0)),
            pl.BlockSpec((N_EXPERTS, D_MODEL), lambda i: (0, 0)),
            pl.BlockSpec((N_EXPERTS, 1), lambda i: (0, 0)),
            pl.BlockSpec((TM, TM), lambda i: (0, 0)),
        ],
        out_specs=[pl.BlockSpec((TM * SUBLANES, LANES), lambda i: (i, 0)),
                   pl.BlockSpec((TM, LANES), lambda i: (i, 0)),
                   pl.BlockSpec((1, SUBLANES, TM), lambda i: (i, 0, 0)),
                   pl.BlockSpec((1, SUBLANES, LANES), lambda i: (i, 0, 0))],
        out_shape=[jax.ShapeDtypeStruct((T_ALL * SUBLANES, LANES), F32),
                   jax.ShapeDtypeStruct((T_ALL, LANES), F32),
                   jax.ShapeDtypeStruct((N_TILES, SUBLANES, TM), jnp.int32),
                   jax.ShapeDtypeStruct((N_TILES, SUBLANES, LANES), jnp.int32)],
        compiler_params=_params(1),
        name="moe_route",
    )(x, gain, mods, wr_t, b_r, tri)


def _moe_positions(meta, counts):
    gid = meta[:, 0, :]
    rank = meta[:, 1, :]
    cnt = counts[:, :N_GROUPS, 0]
    total = jnp.sum(cnt, axis=0)
    padded = (total + (MOE_BLOCK - 1)) // MOE_BLOCK * MOE_BLOCK
    group_start = jnp.cumsum(padded) - padded
    base = group_start[None, :] + jnp.cumsum(cnt, axis=0) - cnt
    pos = rank
    for g in range(N_GROUPS):
        pos = pos + jnp.where(gid == g, base[:, g:g + 1], 0)
    first_block = group_start // MOE_BLOCK
    blk = jnp.arange(MOE_BLOCKS, dtype=jnp.int32)
    block_gid = sum((blk >= first_block[g]).astype(jnp.int32) for g in range(1, N_GROUPS))
    block_used = (blk * MOE_BLOCK < jnp.sum(padded)).astype(jnp.int32)
    return pos.reshape(T_ALL).astype(jnp.int32), block_gid, block_used


def _moe_permute_kernel(pos_ref, tiles_ref, gate_ref, sorted_ref, gsorted_ref):
    i = pl.program_id(0)

    @pl.when(i == 0)
    def _():
        def zero(b, carry):
            start = pl.multiple_of(b * TM, TM)
            sorted_ref[pl.ds(start, TM), :] = jnp.zeros((TM, LANES), F32)
            return carry
        lax.fori_loop(0, T_PAD * SUBLANES // TM, zero, 0)
        gsorted_ref[...] = jnp.zeros((T_PAD, LANES), F32)

    def move(t, carry):
        p = pos_ref[i * TM + t]
        dst = pl.multiple_of(p * SUBLANES, SUBLANES)
        src = pl.multiple_of(t * SUBLANES, SUBLANES)
        sorted_ref[pl.ds(dst, SUBLANES), :] = tiles_ref[pl.ds(src, SUBLANES), :]
        gsorted_ref[pl.ds(p, 1), :] = gate_ref[pl.ds(t, 1), :]
        return carry
    lax.fori_loop(0, TM, move, 0, unroll=8)


def moe_permute(pos, tiles, gates):
    return pl.pallas_call(
        _moe_permute_kernel,
        grid_spec=pltpu.PrefetchScalarGridSpec(
            num_scalar_prefetch=1,
            grid=(N_TILES,),
            in_specs=[pl.BlockSpec((TM * SUBLANES, LANES), lambda i, pos: (i, 0)),
                      pl.BlockSpec((TM, LANES), lambda i, pos: (i, 0))],
            out_specs=[pl.BlockSpec(memory_space=pltpu.VMEM), pl.BlockSpec(memory_space=pltpu.VMEM)],
        ),
        out_shape=[jax.ShapeDtypeStruct((T_PAD * SUBLANES, LANES), F32),
                   jax.ShapeDtypeStruct((T_PAD, LANES), F32)],
        compiler_params=_params(1),
        name="moe_permute",
    )(pos, tiles, gates)


def _group_changed(gid_ref, b):
    return (b == 0) | (gid_ref[b] != gid_ref[jnp.maximum(b - 1, 0)])


def _moe_expert_kernel(gid_ref, used_ref, s_ref, gate_ref, w1_ref, w3_ref, w2_ref, y_ref,
                       w1b_ref, w3b_ref, w2b_ref):
    b = pl.program_id(0)

    @pl.when(_group_changed(gid_ref, b))
    def _():
        for j in range(EXP_PER_GROUP):
            cols = slice(j * D_FF, (j + 1) * D_FF)
            w1b_ref[:, cols] = w1_ref[0, 0, j].astype(BF16)
            w3b_ref[:, cols] = w3_ref[0, 0, j].astype(BF16)
        w2b_ref[...] = w2_ref[0, 0].astype(BF16)

    @pl.when(used_ref[b] == 1)
    def _():
        lhs = jnp.concatenate([s_ref[pl.ds(s, MOE_BLOCK, stride=SUBLANES), :].astype(BF16)
                               for s in range(SUBLANES)], axis=1)
        a = _dot(lhs, w1b_ref[...])
        g = _dot(lhs, w3b_ref[...])
        gates = gate_ref[...]
        lane = lax.broadcasted_iota(jnp.int32, (MOE_BLOCK, LANES), 1)
        first = gid_ref[b] * EXP_PER_GROUP
        parts = []
        for j in range(EXP_PER_GROUP):
            cols = slice(j * D_FF, (j + 1) * D_FF)
            gate = jnp.sum(jnp.where(lane == first + j, gates, 0.0), axis=1, keepdims=True)
            parts.append((_silu(a[:, cols]) * g[:, cols] * gate).astype(BF16))
        act = jnp.concatenate(parts, axis=1)
        y = _dot(act, w2b_ref[...])
        for s in range(SUBLANES):
            y_ref[pl.ds(s, MOE_BLOCK, stride=SUBLANES), :] = y[:, s * LANES:(s + 1) * LANES]

    @pl.when(used_ref[b] == 0)
    def _():
        y_ref[...] = jnp.zeros_like(y_ref)


def moe_experts(block_gid, block_used, sorted_rows, sorted_gates, w1, w3, w2, layer):
    group_ff = EXP_PER_GROUP * D_FF
    w1g = w1.reshape(DEPTH, N_GROUPS, EXP_PER_GROUP, D_MODEL, D_FF)
    w3g = w3.reshape(DEPTH, N_GROUPS, EXP_PER_GROUP, D_MODEL, D_FF)
    w2g = w2.reshape(DEPTH, N_GROUPS, group_ff, D_MODEL)
    up_spec = pl.BlockSpec((1, 1, EXP_PER_GROUP, D_MODEL, D_FF), lambda b, gid, mask: (layer, gid[b], 0, 0, 0),
                           pipeline_mode=pl.Buffered(1))
    return pl.pallas_call(
        _moe_expert_kernel,
        grid_spec=pltpu.PrefetchScalarGridSpec(
            num_scalar_prefetch=2,
            grid=(MOE_BLOCKS,),
            in_specs=[pl.BlockSpec((MOE_BLOCK * SUBLANES, LANES), lambda b, gid, mask: (b, 0)),
                      pl.BlockSpec((MOE_BLOCK, LANES), lambda b, gid, mask: (b, 0)),
                      up_spec, up_spec,
                      pl.BlockSpec((1, 1, group_ff, D_MODEL), lambda b, gid, mask: (layer, gid[b], 0, 0))],
            out_specs=pl.BlockSpec((MOE_BLOCK * SUBLANES, LANES), lambda b, gid, mask: (b, 0)),
            scratch_shapes=[pltpu.VMEM((D_MODEL, group_ff), BF16), pltpu.VMEM((D_MODEL, group_ff), BF16),
                            pltpu.VMEM((group_ff, D_MODEL), BF16)],
        ),
        out_shape=jax.ShapeDtypeStruct((T_PAD * SUBLANES, LANES), F32),
        compiler_params=_params(1),
        name="moe_experts",
    )(block_gid, block_used, sorted_rows, sorted_gates, w1g, w3g, w2g)


def _moe_combine_kernel(pos_ref, ys_ref, x_ref, m_ref, o_ref, buf_ref):
    i = pl.program_id(0)

    def move(t, carry):
        src = pl.multiple_of(pos_ref[i * TM + t] * SUBLANES, SUBLANES)
        dst = pl.multiple_of(t * SUBLANES, SUBLANES)
        buf_ref[pl.ds(dst, SUBLANES), :] = ys_ref[pl.ds(src, SUBLANES), :]
        return carry
    lax.fori_loop(0, TM, move, 0, unroll=8)

    for s in range(SUBLANES):
        cols = slice(s * LANES, (s + 1) * LANES)
        y = buf_ref[pl.ds(s, TM, stride=SUBLANES), :]
        o_ref[:, cols] = x_ref[:, cols] + m_ref[0, 0, 5:6, cols] * y


def moe_combine(pos, y_sorted, x, mods, layer):
    return pl.pallas_call(
        _moe_combine_kernel,
        grid_spec=pltpu.PrefetchScalarGridSpec(
            num_scalar_prefetch=1,
            grid=(N_TILES,),
            in_specs=[
                pl.BlockSpec(memory_space=pltpu.VMEM),
                pl.BlockSpec((TM, D_MODEL), lambda i, pos: (i, 0)),
                pl.BlockSpec((1, 1, 6, D_MODEL), lambda i, pos: (layer, _mod_row(i), 0, 0)),
            ],
            out_specs=pl.BlockSpec((TM, D_MODEL), lambda i, pos: (i, 0)),
            scratch_shapes=[pltpu.VMEM((TM * SUBLANES, LANES), F32)],
        ),
        out_shape=jax.ShapeDtypeStruct((T_ALL, D_MODEL), F32),
        compiler_params=_params(1),
        name="moe_combine",
    )(pos, y_sorted, x, mods)


def moe(x, gain, mods, wr_t, b_r, tri, w1, w3, w2, layer):
    tiles, gates, meta, counts = moe_route(x, gain, mods, wr_t, b_r, tri, layer)
    pos, block_gid, block_used = _moe_positions(meta, counts)
    sorted_rows, sorted_gates = moe_permute(pos, tiles, gates)
    y_sorted = moe_experts(block_gid, block_used, sorted_rows, sorted_gates, w1, w3, w2, layer)
    return moe_combine(pos, y_sorted, x, mods, layer)


def _final_norm_kernel(x_ref, g_ref, o_ref):
    o_ref[...] = _rms(x_ref[...]) * g_ref[...]


def final_norm(x, gain, tile0, ntiles):
    return pl.pallas_call(
        _final_norm_kernel,
        grid=(ntiles,),
        in_specs=[pl.BlockSpec((TM, D_MODEL), lambda i: (tile0 + i, 0)),
                  pl.BlockSpec((1, D_MODEL), lambda i: (0, 0))],
        out_specs=pl.BlockSpec((TM, D_MODEL), lambda i: (i, 0)),
        out_shape=jax.ShapeDtypeStruct((ntiles * TM, D_MODEL), F32),
        compiler_params=_params(1),
        name="final_norm",
    )(x, gain.reshape(1, D_MODEL))


def kernel(x_prompt, x_sample, cache_attn_k, cache_attn_v, state_retention, c, c_ctx, w_ada, b_ada, norm1_g, norm2_g, final_g, w_in, attn_lambda, attn_subln_g, hy_conv_w, hy_conv_b, hy_f_w1, hy_f_b1, hy_f_w2, hy_f_b2, hy_f_w3, hy_skip, ret_decay_exp, w_branch_a, w_branch_b, w_branch_c, w_gate, b_gate, w_out, w_router, b_router, moe_w1, moe_w3, moe_w2):
    x = jnp.concatenate([x_prompt.reshape(T_CTX, D_MODEL), x_sample.reshape(T_LAT, D_MODEL)], axis=0)
    cond8 = jnp.concatenate([c_ctx[None, :], c, jnp.zeros((8 - 1 - DEC_BATCH, D_MODEL), F32)], axis=0)
    mods = ada_modulation(cond8, w_ada, b_ada).reshape(DEPTH, 8, 6, D_MODEL)

    cos, sin = _rope_tables()
    cos_c, sin_c = cos[:, :H_C * DK_C], sin[:, :H_C * DK_C]
    cache_k = cache_attn_k.reshape(DEC_BATCH, DEPTH, PAST_LEN, W_A)
    cache_v = cache_attn_v.reshape(DEC_BATCH, DEPTH, PAST_LEN, W_A)
    decay_rates = _filter_decay_rates()
    tables_ctx = _dft_tables(SEQ)
    tables_lat = _dft_tables(DEC_SEQ)
    filt_ctx = hyena_filters(SEQ, _filter_features(SEQ), decay_rates, tables_ctx[0], tables_ctx[1],
                             hy_f_w1, hy_f_b1, hy_f_w2, hy_f_b2, hy_f_w3)
    filt_lat = hyena_filters(DEC_SEQ, _filter_features(DEC_SEQ), decay_rates, tables_lat[0], tables_lat[1],
                             hy_f_w1, hy_f_b1, hy_f_w2, hy_f_b2, hy_f_w3)
    wr_t = w_router.T
    b_r = b_router.reshape(N_EXPERTS, 1)
    tri = jnp.asarray(np.triu(np.ones((TM, TM), np.float32), 1), dtype=BF16)

    caches = ()
    states = None
    for l in range(DEPTH):
        p, kv = in_projection(x, norm1_g, mods, w_in, w_gate, b_gate, l)

        ya, *caches = attention_ctx(p, kv, attn_lambda, attn_subln_g, l, caches)
        ya = attention_lat(ya, p, kv, cache_k, cache_v, cos, sin, attn_lambda, attn_subln_g, l)
        yb = hyena(None, p, hy_conv_w, hy_conv_b, hy_skip, filt_ctx[0], filt_ctx[1], tables_ctx,
                   l, SEQ, 0, BATCH)
        yb = hyena(yb, p, hy_conv_w, hy_conv_b, hy_skip, filt_lat[0], filt_lat[1], tables_lat,
                   l, DEC_SEQ, T_CTX // DEC_SEQ, DEC_BATCH)
        yc, states = retention(None, p, ret_decay_exp, None, None, states, l, SEQ, 0, BATCH, False)
        yc = retention(yc, p, ret_decay_exp, cos_c, sin_c, state_retention, l, DEC_SEQ,
                       T_CTX // DEC_SEQ, DEC_BATCH, True)[0]

        x = merge_branches(x, ya, yb, yc, p, mods, w_branch_a, w_branch_b, w_branch_c, w_out, l)
        x = moe(x, norm2_g, mods, wr_t, b_r, tri, moe_w1, moe_w3, moe_w2, l)

    new_k, new_v = caches
    y_prompt = final_norm(x, final_g, 0, CTX_TILES).reshape(BATCH, SEQ, D_MODEL)
    y_sample = final_norm(x, final_g, CTX_TILES, N_TILES - CTX_TILES).reshape(DEC_BATCH, DEC_SEQ, D_MODEL)
    return (y_prompt, y_sample, new_k, new_v, states)
```

```python
import functools
import math

import jax
import jax.numpy as jnp
import numpy as np
from jax import lax
from jax.experimental import pallas as pl
from jax.experimental.pallas import tpu as pltpu

F32 = jnp.float32
BF16 = jnp.bfloat16

D_MODEL = 1024
BATCH = 16
SEQ = 256
DEPTH = 4
DEC_BATCH = 2
DEC_SEQ = 1024
PAST_LEN = 256
GRID_W = 64
EPS = 1e-6
ROPE_BASE = 10000.0
H_A = 4
DH_A = 64
DV_A = 128
W_A = 512
C_B = 512
SHORT_K = 3
H_EMB = 33
H_FFN = 64
H_FAST_DECAY = 0.3
H_SLOW_DECAY = 1.5
H_TARGET = 1e-2
H_C = 4
DK_C = 64
DV_C = 128
W_C = 512
N_EXPERTS = 16
N_GROUPS = 4
EXP_PER_GROUP = 4
D_FF = 512
D_IN = 4608

T_CTX = BATCH * SEQ
T_LAT = DEC_BATCH * DEC_SEQ
T_ALL = T_CTX + T_LAT
TM = 1024
N_TILES = T_ALL // TM
CTX_TILES = T_CTX // TM
LANES = 128
SUBLANES = 8
MOE_BLOCK = 256
MOE_BLOCKS = T_ALL // MOE_BLOCK + N_GROUPS
T_PAD = MOE_BLOCKS * MOE_BLOCK
VMEM_LIMIT = 56 * 1024 * 1024


def _params(n_axes):
    return pltpu.CompilerParams(
        dimension_semantics=("arbitrary",) * n_axes, vmem_limit_bytes=VMEM_LIMIT)


def _mod_row(i):
    return jnp.maximum(i - (CTX_TILES - 1), 0)


def _rms(x):
    return x * lax.rsqrt(jnp.mean(x * x, axis=-1, keepdims=True) + EPS)


def _silu(x):
    return x * jax.nn.sigmoid(x)


def _dot(a, b):
    return jnp.dot(a, b, preferred_element_type=F32)


def _dot_nt(a, b):
    return lax.dot_general(a, b, (((1,), (1,)), ((), ())), preferred_element_type=F32)


def _dot_tn(a, b):
    return lax.dot_general(a, b, (((0,), (0,)), ((), ())), preferred_element_type=F32)


def _split3(x):
    hi = x.astype(BF16)
    lo = (x - hi.astype(F32)).astype(BF16)
    return hi, lo


def _dot3(a, b):
    ah, al = _split3(a)
    bh, bl = _split3(b)
    return _dot(ah, bh) + (_dot(ah, bl) + _dot(al, bh))


def _dot3_nt(a, b):
    ah, al = _split3(a)
    bh, bl = _split3(b)
    return _dot_nt(ah, bh) + (_dot_nt(ah, bl) + _dot_nt(al, bh))


def _ada_kernel(c_ref, w_ref, b_ref, o_ref):
    s = _silu(c_ref[...])
    o_ref[0] = _dot(s.astype(BF16), w_ref[0].astype(BF16)) + b_ref[0]


def ada_modulation(cond8, w_ada, b_ada):
    tn = 1536
    n = 6 * D_MODEL
    return pl.pallas_call(
        _ada_kernel,
        grid=(DEPTH, n // tn),
        in_specs=[
            pl.BlockSpec((8, D_MODEL), lambda l, j: (0, 0)),
            pl.BlockSpec((1, D_MODEL, tn), lambda l, j: (l, 0, j)),
            pl.BlockSpec((1, 1, tn), lambda l, j: (l, 0, j)),
        ],
        out_specs=pl.BlockSpec((1, 8, tn), lambda l, j: (l, 0, j)),
        out_shape=jax.ShapeDtypeStruct((DEPTH, 8, n), F32),
        compiler_params=_params(2),
        name="ada_modulation",
    )(cond8, w_ada, b_ada.reshape(DEPTH, 1, n))


PROJ_TM = 3072
PROJ_TN = 512
GATE_BLOCKS = 3 * D_MODEL // PROJ_TN
IN_BLOCKS = D_IN // PROJ_TN
P_WIDTH = 3 * D_MODEL + D_IN
KV_FIRST = GATE_BLOCKS + 1
P_Q = 3 * D_MODEL // W_A
P_HYENA = (3 * D_MODEL + 3 * W_A) // (3 * C_B)
P_RET_Q = (3 * D_MODEL + 3 * W_A + 3 * C_B) // (H_C * DK_C)
P_RET_V = (3 * D_MODEL + 3 * W_A + 3 * C_B + 2 * H_C * DK_C) // W_C


def _in_proj_kernel(x_ref, g_ref, m_ref, win_ref, wg_ref, bg_ref, p_ref, kv_ref, h_ref, *, layer):
    i = pl.program_id(0)
    j = pl.program_id(1)

    @pl.when(j == 0)
    def _():
        for s in range(PROJ_TM // TM):
            rows = slice(s * TM, (s + 1) * TM)
            mod = m_ref[layer, _mod_row(i * (PROJ_TM // TM) + s)]
            y = _rms(x_ref[rows, :]) * g_ref[layer:layer + 1, :]
            h_ref[rows, :] = (y * (1.0 + mod[1:2, :]) + mod[0:1, :]).astype(BF16)

    @pl.when(j < GATE_BLOCKS)
    def _():
        acc = _dot(h_ref[...], wg_ref[0].astype(BF16)) + bg_ref[0]
        p_ref[...] = jax.nn.sigmoid(acc).astype(BF16)

    @pl.when(j >= GATE_BLOCKS)
    def _():
        acc = _dot(h_ref[...], win_ref[0].astype(BF16))
        p_ref[...] = acc.astype(BF16)

        @pl.when((j == KV_FIRST) | (j == KV_FIRST + 1))
        def _():
            kv_ref[...] = acc


def in_projection(x, gain, mods, w_in, w_gate, b_gate, layer):
    return pl.pallas_call(
        functools.partial(_in_proj_kernel, layer=layer),
        grid=(T_ALL // PROJ_TM, GATE_BLOCKS + IN_BLOCKS),
        in_specs=[
            pl.BlockSpec((PROJ_TM, D_MODEL), lambda i, j: (i, 0), pipeline_mode=pl.Buffered(1)),
            pl.BlockSpec((DEPTH, D_MODEL), lambda i, j: (0, 0)),
            pl.BlockSpec((DEPTH, 8, 6, D_MODEL), lambda i, j: (0, 0, 0, 0)),
            pl.BlockSpec((1, D_MODEL, PROJ_TN), lambda i, j: (layer, 0, jnp.maximum(j - GATE_BLOCKS, 0))),
            pl.BlockSpec((1, D_MODEL, PROJ_TN), lambda i, j: (layer, 0, jnp.minimum(j, GATE_BLOCKS - 1))),
            pl.BlockSpec((1, 1, PROJ_TN), lambda i, j: (layer, 0, jnp.minimum(j, GATE_BLOCKS - 1))),
        ],
        out_specs=[
            pl.BlockSpec((PROJ_TM, PROJ_TN), lambda i, j: (i, j)),
            pl.BlockSpec((PROJ_TM, PROJ_TN), lambda i, j: (i, jnp.clip(j - KV_FIRST, 0, 1))),
        ],
        out_shape=[jax.ShapeDtypeStruct((T_ALL, P_WIDTH), BF16),
                   jax.ShapeDtypeStruct((T_ALL, 2 * W_A), F32)],
        scratch_shapes=[pltpu.VMEM((PROJ_TM, D_MODEL), BF16)],
        compiler_params=_params(2),
        name="in_projection",
    )(x, gain, mods, w_in, w_gate, b_gate.reshape(DEPTH, 1, 3 * D_MODEL))


def _rope_tables():
    n = DEC_SEQ
    rows = n // GRID_W
    row = jnp.repeat(jnp.arange(rows, dtype=F32), GRID_W)
    col = jnp.tile(jnp.arange(GRID_W, dtype=F32), rows)
    quarter = DH_A // 4
    inv = ROPE_BASE ** (-jnp.arange(quarter, dtype=F32) / quarter)
    ar = row[:, None] * inv
    ac = col[:, None] * inv
    ang = jnp.concatenate([ar, ar, ac, ac], axis=-1)
    cos = jnp.tile(jnp.cos(ang), (1, 512 // DH_A))
    sgn = jnp.tile(jnp.concatenate([-jnp.ones((quarter,), F32), jnp.ones((quarter,), F32)]), 512 // (2 * quarter))
    sin = jnp.tile(jnp.sin(ang), (1, 512 // DH_A)) * sgn
    return cos, sin


def _rope(x, cos, sin):
    w = x.shape[-1]
    lane = lax.broadcasted_iota(jnp.int32, x.shape, 1)
    first = (lane % 32) < 16
    rot = jnp.where(first, pltpu.roll(x, w - 16, 1), pltpu.roll(x, 16, 1))
    return x * cos + rot * sin


def _lambda(al_ref, layer):
    al = al_ref[layer]
    a = jnp.sum(al[0:1, :] * al[1:2, :], axis=-1, keepdims=True)
    b = jnp.sum(al[2:3, :] * al[3:4, :], axis=-1, keepdims=True)
    lam_init = 0.8 - 0.6 * math.exp(-0.3 * layer)
    return jnp.exp(a) - jnp.exp(b) + lam_init, lam_init


def _softmax_parts(s):
    m = jnp.max(s, axis=-1, keepdims=True)
    e = jnp.exp2(s - m)
    return e.astype(BF16), 1.0 / jnp.sum(e, axis=-1, keepdims=True)


def _diff_attention(q, k, v, lam, lam_init, subln, o_ref):
    lane = lax.broadcasted_iota(jnp.int32, q.shape, 1)
    q = q * (DH_A ** -0.5 * math.log2(math.e))
    q0 = jnp.where((lane % LANES) < DH_A, q, 0.0).astype(BF16)
    q1 = jnp.where((lane % LANES) >= DH_A, q, 0.0).astype(BF16)
    for h in range(H_A):
        sl = slice(h * LANES, (h + 1) * LANES)
        kh = k[:, sl]
        vh = v[:, sl]
        e0, r0 = _softmax_parts(_dot_nt(q0[:, sl], kh))
        e1, r1 = _softmax_parts(_dot_nt(q1[:, sl], kh))
        o = _dot(e0, vh) * r0 - _dot(e1, vh) * (lam * r1)
        o = _rms(o) * subln * (1.0 - lam_init)
        o_ref[:, sl] = o.astype(o_ref.dtype)


def _attn_ctx_kernel(*refs, layer, n_caches):
    q_ref, k_ref, v_ref, al_ref, g_ref, o_ref, knew_ref, vnew_ref = refs[n_caches:]
    lam, lam_init = _lambda(al_ref, layer)
    subln = g_ref[layer:layer + 1, :]
    k = k_ref[...]
    v = v_ref[...]
    for h in range(H_A):
        knew_ref[0, 0, :, h, :] = k[:, h * LANES:(h + 1) * LANES]
        vnew_ref[0, 0, :, h, :] = v[:, h * LANES:(h + 1) * LANES]
    _diff_attention(q_ref[...].astype(F32), k.astype(BF16), v.astype(BF16), lam, lam_init, subln, o_ref)


def attention_ctx(q, kv, attn_lambda, subln_g, layer, caches):
    cache_shape = jax.ShapeDtypeStruct((BATCH, DEPTH, SEQ, H_A, DV_A), F32)
    cache_spec = pl.BlockSpec((1, 1, SEQ, H_A, DV_A), lambda b: (b, layer, 0, 0, 0))
    n_caches = len(caches)
    return pl.pallas_call(
        functools.partial(_attn_ctx_kernel, layer=layer, n_caches=n_caches),
        grid=(BATCH,),
        in_specs=[pl.BlockSpec(memory_space=pl.ANY)] * n_caches + [
            pl.BlockSpec((SEQ, W_A), lambda b: (b, P_Q)),
            pl.BlockSpec((SEQ, W_A), lambda b: (b, 0)),
            pl.BlockSpec((SEQ, W_A), lambda b: (b, 1)),
            pl.BlockSpec((DEPTH, 4, DH_A), lambda b: (0, 0, 0)),
            pl.BlockSpec((DEPTH, DV_A), lambda b: (0, 0)),
        ],
        out_specs=[pl.BlockSpec((SEQ, W_A), lambda b: (b, 0)), cache_spec, cache_spec],
        out_shape=[jax.ShapeDtypeStruct((T_ALL, W_A), BF16), cache_shape, cache_shape],
        input_output_aliases={i: i + 1 for i in range(n_caches)},
        compiler_params=_params(1),
        name="attention_ctx",
    )(*caches, q, kv, kv, attn_lambda, subln_g)


QB = 256


def _attn_lat_kernel(ya_ref, q_ref, k_ref, v_ref, ck_ref, cv_ref, cosq_ref, sinq_ref, cos_ref, sin_ref,
                     al_ref, g_ref, o_ref, kall_ref, vall_ref, *, layer):
    del ya_ref

    @pl.when(pl.program_id(1) == 0)
    def _():
        kall_ref[0:PAST_LEN, :] = ck_ref[0, 0].astype(BF16)
        vall_ref[0:PAST_LEN, :] = cv_ref[0, 0].astype(BF16)
        kall_ref[PAST_LEN:, :] = _rope(k_ref[...], cos_ref[...], sin_ref[...]).astype(BF16)
        vall_ref[PAST_LEN:, :] = v_ref[...].astype(BF16)

    lam, lam_init = _lambda(al_ref, layer)
    subln = g_ref[layer:layer + 1, :]
    q = _rope(q_ref[...].astype(F32), cosq_ref[...], sinq_ref[...])
    _diff_attention(q, kall_ref[...], vall_ref[...], lam, lam_init, subln, o_ref)


def attention_lat(ya, q, kv, cache_k, cache_v, cos, sin, attn_lambda, subln_g, layer):
    nqb = DEC_SEQ // QB
    row0 = T_CTX // QB
    seq0 = T_CTX // DEC_SEQ
    return pl.pallas_call(
        functools.partial(_attn_lat_kernel, layer=layer),
        grid=(DEC_BATCH, nqb),
        in_specs=[
            pl.BlockSpec(memory_space=pl.ANY),
            pl.BlockSpec((QB, W_A), lambda b, i: (row0 + b * nqb + i, P_Q)),
            pl.BlockSpec((DEC_SEQ, W_A), lambda b, i: (seq0 + b, 0)),
            pl.BlockSpec((DEC_SEQ, W_A), lambda b, i: (seq0 + b, 1)),
            pl.BlockSpec((1, 1, PAST_LEN, W_A), lambda b, i: (b, layer, 0, 0)),
            pl.BlockSpec((1, 1, PAST_LEN, W_A), lambda b, i: (b, layer, 0, 0)),
            pl.BlockSpec((QB, W_A), lambda b, i: (i, 0)),
            pl.BlockSpec((QB, W_A), lambda b, i: (i, 0)),
            pl.BlockSpec((DEC_SEQ, W_A), lambda b, i: (0, 0)),
            pl.BlockSpec((DEC_SEQ, W_A), lambda b, i: (0, 0)),
            pl.BlockSpec((DEPTH, 4, DH_A), lambda b, i: (0, 0, 0)),
            pl.BlockSpec((DEPTH, DV_A), lambda b, i: (0, 0)),
        ],
        out_specs=pl.BlockSpec((QB, W_A), lambda b, i: (row0 + b * nqb + i, 0)),
        out_shape=jax.ShapeDtypeStruct((T_ALL, W_A), BF16),
        scratch_shapes=[pltpu.VMEM((PAST_LEN + DEC_SEQ, W_A), BF16),
                        pltpu.VMEM((PAST_LEN + DEC_SEQ, W_A), BF16)],
        input_output_aliases={0: 0},
        compiler_params=_params(2),
        name="attention_lat",
    )(ya, q, kv, kv, cache_k, cache_v, cos, sin, cos, sin, attn_lambda, subln_g)


def _dft_tables(n):
    k = np.arange(n, dtype=np.int64)
    prod = (2 * k[:, None] + 1) * k[None, :]
    ang = (prod % (4 * n)).astype(np.float64) * (math.pi / (2 * n))
    c = np.cos(ang).astype(np.float32)
    s = np.sin(ang).astype(np.float32)
    return tuple(jnp.asarray(t).astype(BF16) for t in (c, s, c.T, s.T))


def _filter_features(n):
    t = jnp.linspace(0.0, 1.0, n, dtype=F32)[:, None]
    bands = (H_EMB - 1) // 2
    w = 2.0 * math.pi * jnp.arange(n, dtype=F32)[:, None] / n
    fr = jnp.linspace(1e-4, bands - 1, bands, dtype=F32)
    z = jnp.concatenate([t, jnp.cos(w * fr), -jnp.sin(w * fr)], axis=-1)
    return jnp.pad(z, ((0, 0), (0, LANES - H_EMB)))


def _filter_decay_rates():
    max_decay = math.log(H_TARGET) / H_FAST_DECAY
    min_decay = math.log(H_TARGET) / H_SLOW_DECAY
    return jnp.abs(jnp.linspace(min_decay, max_decay, C_B, dtype=F32))[None, :]


def _hyena_filter_kernel(z_ref, w1_ref, b1_ref, w2_ref, b2_ref, w3_ref, dr_ref, c_ref, s_ref,
                         gre_ref, gim_ref):
    z = z_ref[...]
    hid = jnp.sin(_dot3(z, w1_ref[0]) + b1_ref[0])
    hid = jnp.sin(_dot3(hid, w2_ref[0]) + b2_ref[0])
    h = _dot3(hid, w3_ref[0])
    window = jnp.exp(-z[:, 0:1] * dr_ref[...])
    hf = h[:, :C_B] * window
    hb = h[:, C_B:] * window
    total = jnp.sum(jnp.abs(hf) + jnp.abs(hb), axis=0, keepdims=True)
    hf = hf / total
    hb = hb / total
    row = lax.broadcasted_iota(jnp.int32, hb.shape, 0)
    hb = jnp.where(row == 0, 0.0, hb)
    ah, al = _split3(hf + hb)
    dh, dl = _split3(hb - hf)
    gre_ref[0] = _dot(c_ref[...], ah) + _dot(c_ref[...], al)
    gim_ref[0] = _dot(s_ref[...], dh) + _dot(s_ref[...], dl)


def hyena_filters(n, z, decay_rates, cf, sf, f_w1, f_b1, f_w2, f_b2, f_w3):
    w1 = jnp.pad(f_w1, ((0, 0), (0, LANES - H_EMB), (0, 0)))
    full = lambda shape: pl.BlockSpec(shape, lambda l: (0,) * len(shape))
    per_layer = lambda shape: pl.BlockSpec((1,) + shape, lambda l: (l,) + (0,) * len(shape))
    return pl.pallas_call(
        _hyena_filter_kernel,
        grid=(DEPTH,),
        in_specs=[
            full((n, LANES)),
            per_layer((LANES, H_FFN)), per_layer((1, H_FFN)),
            per_layer((H_FFN, H_FFN)), per_layer((1, H_FFN)),
            per_layer((H_FFN, 2 * C_B)),
            full((1, C_B)), full((n, n)), full((n, n)),
        ],
        out_specs=[per_layer((n, C_B)), per_layer((n, C_B))],
        out_shape=[jax.ShapeDtypeStruct((DEPTH, n, C_B), F32)] * 2,
        compiler_params=_params(1),
        name="hyena_filters",
    )(z, w1, f_b1.reshape(DEPTH, 1, H_FFN), f_w2, f_b2.reshape(DEPTH, 1, H_FFN), f_w3,
      decay_rates, cf, sf)


def _hyena_kernel(*refs, layer, n, aliased):
    if aliased:
        refs = refs[1:]
    u_ref, cw_ref, cb_ref, skip_ref, gre_ref, gim_ref, c_ref, s_ref, ct_ref, st_ref, o_ref = refs
    u = u_ref[...].astype(F32)
    row = lax.broadcasted_iota(jnp.int32, u.shape, 0)
    prev = jnp.where(row == 0, 0.0, pltpu.roll(u, 1, 0))
    nxt = jnp.where(row == n - 1, 0.0, pltpu.roll(u, n - 1, 0))
    cw = cw_ref[layer]
    uc = cb_ref[layer:layer + 1, :] + prev * cw[0:1, :] + u * cw[1:2, :] + nxt * cw[2:3, :]
    x0 = uc[:, :C_B]
    x1 = uc[:, C_B:2 * C_B]
    v = uc[:, 2 * C_B:]
    w = v * x1
    wb = w.astype(BF16)
    ure = _dot(c_ref[...], wb)
    uim = -_dot(s_ref[...], wb)
    gre = gre_ref[0]
    gim = gim_ref[0]
    yre = (ure * gre - uim * gim).astype(BF16)
    yim = (ure * gim + uim * gre).astype(BF16)
    y = (_dot(ct_ref[...], yre) - _dot(st_ref[...], yim)) * (1.0 / n)
    y = y + w * skip_ref[layer:layer + 1, :]
    o_ref[...] = (x0 * y).astype(o_ref.dtype)


def hyena(yb, r, conv_w, conv_b, skip, gre, gim, tables, layer, n, seq0, nseq):
    cf, sf, cft, sft = tables
    aliased = yb is not None
    full = lambda shape: pl.BlockSpec(shape, lambda b: (0,) * len(shape))
    in_specs = [
        pl.BlockSpec((n, 3 * C_B), lambda b: (seq0 + b, P_HYENA)),
        full((DEPTH, SHORT_K, 3 * C_B)), full((DEPTH, 3 * C_B)), full((DEPTH, C_B)),
        pl.BlockSpec((1, n, C_B), lambda b: (layer, 0, 0)),
        pl.BlockSpec((1, n, C_B), lambda b: (layer, 0, 0)),
        full((n, n)), full((n, n)), full((n, n)), full((n, n)),
    ]
    args = [r, conv_w, conv_b, skip, gre, gim, cf, sf, cft, sft]
    if aliased:
        in_specs = [pl.BlockSpec(memory_space=pl.ANY)] + in_specs
        args = [yb] + args
    return pl.pallas_call(
        functools.partial(_hyena_kernel, layer=layer, n=n, aliased=aliased),
        grid=(nseq,),
        in_specs=in_specs,
        out_specs=pl.BlockSpec((n, C_B), lambda b: (seq0 + b, 0)),
        out_shape=jax.ShapeDtypeStruct((T_ALL, C_B), BF16),
        input_output_aliases={0: 0} if aliased else {},
        compiler_params=_params(1),
        name="hyena",
    )(*args)


RET_QB = 256


def _retention_kernel(*refs, layer, n, latent):
    if latent:
        _, q_ref, k_ref, v_ref, g_ref, de_ref, cos_ref, sin_ref, s0_ref, o_ref, w_ref = refs
        st_ref = None
    else:
        q_ref, k_ref, v_ref, g_ref, de_ref, o_ref, st_ref, w_ref = refs[-8:]
    log_gamma = jnp.log1p(-jnp.exp2(-de_ref[layer]))

    @pl.when(pl.program_id(0) == 0)
    def _():
        t = lax.broadcasted_iota(jnp.int32, (n, n), 0)
        s = lax.broadcasted_iota(jnp.int32, (n, n), 1)
        lag = (t - s).astype(F32)
        for h in range(H_C):
            rate = jnp.where(lag > 0, log_gamma[0:1, h:h + 1], log_gamma[1:2, h:h + 1])
            w_ref[h] = jnp.where(lag == 0, 2.0, jnp.exp(rate * jnp.abs(lag)))

    q = q_ref[...].astype(F32)
    k = k_ref[...].astype(F32) * (DK_C ** -0.5)
    if latent:
        q = _rope(q, cos_ref[...], sin_ref[...])
        k = _rope(k, cos_ref[...], sin_ref[...])
    v = v_ref[...].astype(BF16)
    gate = g_ref[...].astype(F32)
    lane = lax.broadcasted_iota(jnp.int32, (n, LANES), 1)
    pos = lax.broadcasted_iota(jnp.int32, (n, 1), 0).astype(F32)
    zeros64 = jnp.zeros((DK_C, DV_C), F32)

    for h in range(H_C):
        pair = slice((h // 2) * LANES, (h // 2 + 1) * LANES)
        lo = (h % 2) * DK_C
        own = (lane >= lo) & (lane < lo + DK_C)
        vs = slice(h * DV_C, (h + 1) * DV_C)
        gf = log_gamma[0:1, h:h + 1]
        gb = log_gamma[1:2, h:h + 1]
        qh = jnp.where(own, q[:, pair], 0.0).astype(BF16)
        kh = jnp.where(own, k[:, pair], 0.0)
        kb = kh.astype(BF16)
        vh = v[:, vs]
        if latent:
            s0 = [jnp.concatenate([s0_ref[0, 0, d, h], zeros64] if lo == 0 else [zeros64, s0_ref[0, 0, d, h]],
                                  axis=0).astype(BF16) for d in range(2)]
            carry = (_dot(qh, s0[0]) * jnp.exp(gf * (pos + 1.0))
                     + _dot(qh, s0[1]) * jnp.exp(gb * (float(n) - pos)))
        for r0 in range(0, n, RET_QB):
            rows = slice(r0, r0 + RET_QB)
            att = _dot_nt(qh[rows], kb) * w_ref[h, rows, :]
            out = _dot(att.astype(BF16), vh)
            if latent:
                out = out + carry[rows]
            o_ref[rows, vs] = (_rms(out) * _silu(gate[rows, vs])).astype(o_ref.dtype)
        if st_ref is not None:
            sf = _dot_tn((kh * jnp.exp(gf * (float(n - 1) - pos))).astype(BF16), vh)
            sb = _dot_tn((kh * jnp.exp(gb * pos)).astype(BF16), vh)
            st_ref[0, 0, 0, h] = sf[lo:lo + DK_C, :]
            st_ref[0, 0, 1, h] = sb[lo:lo + DK_C, :]


def retention(yc, r, decay_exp, cos, sin, state0, layer, n, seq0, nseq, latent):
    full = lambda shape: pl.BlockSpec(shape, lambda b: (0,) * len(shape))
    in_specs = [
        pl.BlockSpec((n, H_C * DK_C), lambda b: (seq0 + b, P_RET_Q)),
        pl.BlockSpec((n, H_C * DK_C), lambda b: (seq0 + b, P_RET_Q + 1)),
        pl.BlockSpec((n, W_C), lambda b: (seq0 + b, P_RET_V)),
        pl.BlockSpec((n, W_C), lambda b: (seq0 + b, P_RET_V + 1)),
        full((DEPTH, 2, H_C)),
    ]
    args = [r, r, r, r, decay_exp]
    out_specs = [pl.BlockSpec((n, W_C), lambda b: (seq0 + b, 0))]
    out_shape = [jax.ShapeDtypeStruct((T_ALL, W_C), BF16)]
    aliases = {}
    if latent:
        in_specs = [pl.BlockSpec(memory_space=pl.ANY)] + in_specs + [
            full((n, H_C * DK_C)), full((n, H_C * DK_C)),
            pl.BlockSpec((1, 1, 2, H_C, DK_C, DV_C), lambda b: (b, layer, 0, 0, 0, 0)),
        ]
        args = [yc] + args + [cos, sin, state0]
        aliases = {0: 0}
    else:
        out_specs.append(pl.BlockSpec((1, 1, 2, H_C, DK_C, DV_C), lambda b: (b, layer, 0, 0, 0, 0)))
        out_shape.append(jax.ShapeDtypeStruct((nseq, DEPTH, 2, H_C, DK_C, DV_C), F32))
        if state0 is not None:
            in_specs = [pl.BlockSpec(memory_space=pl.ANY)] + in_specs
            args = [state0] + args
            aliases = {0: 1}
    return pl.pallas_call(
        functools.partial(_retention_kernel, layer=layer, n=n, latent=latent),
        grid=(nseq,),
        in_specs=in_specs,
        out_specs=out_specs,
        out_shape=out_shape,
        scratch_shapes=[pltpu.VMEM((H_C, n, n), F32)],
        input_output_aliases=aliases,
        compiler_params=_params(1),
        name="retention",
    )(*args)


def _merge_kernel(x_ref, ya_ref, yb_ref, yc_ref, g_ref, m_ref, wa_ref, wb_ref, wc_ref, wo_ref,
                  o_ref, wbf_ref, wobf_ref):
    @pl.when(pl.program_id(0) == 0)
    def _():
        wbf_ref[0] = wa_ref[0].astype(BF16)
        wbf_ref[1] = wb_ref[0].astype(BF16)
        wbf_ref[2] = wc_ref[0].astype(BF16)
        wobf_ref[...] = wo_ref[0].astype(BF16)

    g = g_ref[...]
    merged = (g[:, :D_MODEL] * _dot(ya_ref[...], wbf_ref[0])
              + g[:, D_MODEL:2 * D_MODEL] * _dot(yb_ref[...], wbf_ref[1])
              + g[:, 2 * D_MODEL:] * _dot(yc_ref[...], wbf_ref[2]))
    g1 = m_ref[0, 0, 2:3, :]
    o_ref[...] = x_ref[...] + g1 * _dot(merged.astype(BF16), wobf_ref[...])


def merge_branches(x, ya, yb, yc, gates, mods, w_a, w_b, w_c, w_out, layer):
    tile = lambda w: pl.BlockSpec((TM, w), lambda i: (i, 0))
    wspec = lambda k: pl.BlockSpec((1, k, D_MODEL), lambda i: (layer, 0, 0))
    return pl.pallas_call(
        _merge_kernel,
        grid=(N_TILES,),
        in_specs=[
            tile(D_MODEL), tile(W_A), tile(C_B), tile(W_C), tile(3 * D_MODEL),
            pl.BlockSpec((1, 1, 6, D_MODEL), lambda i: (layer, _mod_row(i), 0, 0)),
            wspec(W_A), wspec(C_B), wspec(W_C), wspec(D_MODEL),
        ],
        out_specs=tile(D_MODEL),
        out_shape=jax.ShapeDtypeStruct((T_ALL, D_MODEL), F32),
        scratch_shapes=[pltpu.VMEM((3, W_A, D_MODEL), BF16), pltpu.VMEM((D_MODEL, D_MODEL), BF16)],
        compiler_params=_params(1),
        name="merge_branches",
    )(x, ya, yb, yc, gates, mods, w_a, w_b, w_c, w_out)


def _route(h2, wr_t, b_r):
    logits = _dot3_nt(wr_t, h2)
    m = jnp.max(logits, axis=0, keepdims=True)
    e = jnp.exp(logits - m)
    scores = e / jnp.sum(e, axis=0, keepdims=True)
    sel = scores + b_r
    rows = [sel[i:i + 1, :] for i in range(N_EXPERTS)]
    in_group = []
    gscore = []
    for g in range(N_GROUPS):
        members = range(g * EXP_PER_GROUP, (g + 1) * EXP_PER_GROUP)
        total = None
        for i in members:
            rank = None
            for j in members:
                if j == i:
                    continue
                ahead = (rows[j] >= rows[i]) if j < i else (rows[j] > rows[i])
                ahead = ahead.astype(F32)
                rank = ahead if rank is None else rank + ahead
            chosen = rank < 2.0
            in_group.append(chosen)
            part = jnp.where(chosen, rows[i], 0.0)
            total = part if total is None else total + part
        gscore.append(total)
    gates = []
    group_hot = []
    for g in range(N_GROUPS):
        best = None
        for g2 in range(N_GROUPS):
            if g2 == g:
                continue
            wins = gscore[g] > gscore[g2] if g2 < g else gscore[g] >= gscore[g2]
            best = wins if best is None else best & wins
        group_hot.append(best.astype(F32))
        for i in range(g * EXP_PER_GROUP, (g + 1) * EXP_PER_GROUP):
            gates.append(jnp.where(best & in_group[i], scores[i:i + 1, :], 0.0))
    gates = jnp.concatenate(gates, axis=0)
    return gates / jnp.sum(gates, axis=0, keepdims=True), group_hot


def _moe_route_kernel(x_ref, g_ref, m_ref, wr_ref, br_ref, tri_ref, tiles_ref, gate_ref, meta_ref, count_ref,
                      *, layer):
    y = _rms(x_ref[...]) * g_ref[layer:layer + 1, :]
    h2 = y * (1.0 + m_ref[0, 0, 4:5, :]) + m_ref[0, 0, 3:4, :]
    gates, group_hot = _route(h2, wr_ref[...], br_ref[...])
    for s in range(SUBLANES):
        tiles_ref[pl.ds(s, TM, stride=SUBLANES), :] = h2[:, s * LANES:(s + 1) * LANES]
    padded = jnp.concatenate([gates, jnp.zeros((LANES - N_EXPERTS, TM), F32)], axis=0)
    gate_ref[...] = padded.T
    row = lax.broadcasted_iota(jnp.int32, (SUBLANES, TM), 0)
    hot = jnp.zeros((SUBLANES, TM), F32)
    for g in range(N_GROUPS):
        hot = jnp.where(row == g, group_hot[g], hot)
    before = _dot(hot.astype(BF16), tri_ref[...])
    rank = jnp.sum(hot * before, axis=0, keepdims=True)
    gid = group_hot[1] + 2.0 * group_hot[2] + 3.0 * group_hot[3]
    meta = jnp.where(row == 0, gid, jnp.where(row == 1, rank, 0.0))
    meta_ref[0] = meta.astype(jnp.int32)
    counts = jnp.sum(hot, axis=1, keepdims=True) + jnp.zeros((SUBLANES, LANES), F32)
    count_ref[0] = counts.astype(jnp.int32)


def moe_route(x, gain, mods, wr_t, b_r, tri, layer):
    return pl.pallas_call(
        functools.partial(_moe_route_kernel, layer=layer),
        grid=(N_TILES,),
        in_specs=[
            pl.BlockSpec((TM, D_MODEL), lambda i: (i, 0)),
            pl.BlockSpec((DEPTH, D_MODEL), lambda i: (0, 0)),
            pl.BlockSpec((1, 1, 6, D_MODEL), lambda i: (layer, _mod_row(i), 0, 0)),
            pl.BlockSpec((N_EXPERTS, D_MODEL), lambda i: (0, 0)),
            pl.BlockSpec((N_EXPERTS, 1), lambda i: (0, 0)),
            pl.BlockSpec((TM, TM), lambda i: (0, 0)),
        ],
        out_specs=[pl.BlockSpec((TM * SUBLANES, LANES), lambda i: (i, 0)),
                   pl.BlockSpec((TM, LANES), lambda i: (i, 0)),
                   pl.BlockSpec((1, SUBLANES, TM), lambda i: (i, 0, 0)),
                   pl.BlockSpec((1, SUBLANES, LANES), lambda i: (i, 0, 0))],
        out_shape=[jax.ShapeDtypeStruct((T_ALL * SUBLANES, LANES), F32),
                   jax.ShapeDtypeStruct((T_ALL, LANES), F32),
                   jax.ShapeDtypeStruct((N_TILES, SUBLANES, TM), jnp.int32),
                   jax.ShapeDtypeStruct((N_TILES, SUBLANES, LANES), jnp.int32)],
        compiler_params=_params(1),
        name="moe_route",
    )(x, gain, mods, wr_t, b_r, tri)


def _moe_positions(meta, counts):
    gid = meta[:, 0, :]
    rank = meta[:, 1, :]
    cnt = counts[:, :N_GROUPS, 0]
    total = jnp.sum(cnt, axis=0)
    padded = (total + (MOE_BLOCK - 1)) // MOE_BLOCK * MOE_BLOCK
    group_start = jnp.cumsum(padded) - padded
    base = group_start[None, :] + jnp.cumsum(cnt, axis=0) - cnt
    pos = rank
    for g in range(N_GROUPS):
        pos = pos + jnp.where(gid == g, base[:, g:g + 1], 0)
    first_block = group_start // MOE_BLOCK
    blk = jnp.arange(MOE_BLOCKS, dtype=jnp.int32)
    block_gid = sum((blk >= first_block[g]).astype(jnp.int32) for g in range(1, N_GROUPS))
    block_used = (blk * MOE_BLOCK < jnp.sum(padded)).astype(jnp.int32)
    later = jnp.where(block_gid[None, :] > block_gid[:, None], block_gid[None, :], N_GROUPS)
    next_group = jnp.min(later, axis=1)
    next_group = jnp.where(next_group == N_GROUPS, -1, next_group).astype(jnp.int32)
    return pos.reshape(T_ALL).astype(jnp.int32), block_gid, block_used, next_group


def _moe_permute_kernel(pos_ref, tiles_ref, gate_ref, sorted_ref, gsorted_ref):
    i = pl.program_id(0)

    @pl.when(i == 0)
    def _():
        def zero(b, carry):
            start = pl.multiple_of(b * TM, TM)
            sorted_ref[pl.ds(start, TM), :] = jnp.zeros((TM, LANES), F32)
            return carry
        lax.fori_loop(0, T_PAD * SUBLANES // TM, zero, 0)
        gsorted_ref[...] = jnp.zeros((T_PAD, LANES), F32)

    def move(t, carry):
        p = pos_ref[i * TM + t]
        dst = pl.multiple_of(p * SUBLANES, SUBLANES)
        src = pl.multiple_of(t * SUBLANES, SUBLANES)
        sorted_ref[pl.ds(dst, SUBLANES), :] = tiles_ref[pl.ds(src, SUBLANES), :]
        gsorted_ref[pl.ds(p, 1), :] = gate_ref[pl.ds(t, 1), :]
        return carry
    lax.fori_loop(0, TM, move, 0, unroll=8)


def moe_permute(pos, tiles, gates):
    return pl.pallas_call(
        _moe_permute_kernel,
        grid_spec=pltpu.PrefetchScalarGridSpec(
            num_scalar_prefetch=1,
            grid=(N_TILES,),
            in_specs=[pl.BlockSpec((TM * SUBLANES, LANES), lambda i, pos: (i, 0)),
                      pl.BlockSpec((TM, LANES), lambda i, pos: (i, 0))],
            out_specs=[pl.BlockSpec(memory_space=pltpu.VMEM), pl.BlockSpec(memory_space=pltpu.VMEM)],
        ),
        out_shape=[jax.ShapeDtypeStruct((T_PAD * SUBLANES, LANES), F32),
                   jax.ShapeDtypeStruct((T_PAD, LANES), F32)],
        compiler_params=_params(1),
        name="moe_permute",
    )(pos, tiles, gates)


def _group_changed(gid_ref, b):
    return (b == 0) | (gid_ref[b] != gid_ref[jnp.maximum(b - 1, 0)])


def _moe_expert_kernel(gid_ref, used_ref, next_ref, s_ref, gate_ref, w1_hbm, w3_hbm, w2_ref, y_ref,
                       w1f_ref, w3f_ref, w1b_ref, w3b_ref, w2b_ref, sem, *, layer):
    b = pl.program_id(0)

    def up_weight_copies(group):
        return (pltpu.make_async_copy(w1_hbm.at[layer, group], w1f_ref, sem.at[0]),
                pltpu.make_async_copy(w3_hbm.at[layer, group], w3f_ref, sem.at[1]))

    @pl.when(b == 0)
    def _():
        for copy in up_weight_copies(gid_ref[0]):
            copy.start()

    @pl.when(_group_changed(gid_ref, b))
    def _():
        for copy in up_weight_copies(gid_ref[b]):
            copy.wait()
        for j in range(EXP_PER_GROUP):
            cols = slice(j * D_FF, (j + 1) * D_FF)
            w1b_ref[:, cols] = w1f_ref[j].astype(BF16)
            w3b_ref[:, cols] = w3f_ref[j].astype(BF16)
        w2b_ref[...] = w2_ref[0, 0].astype(BF16)

        @pl.when(next_ref[b] >= 0)
        def _():
            for copy in up_weight_copies(next_ref[b]):
                copy.start()

    @pl.when(used_ref[b] == 1)
    def _():
        lhs = jnp.concatenate([s_ref[pl.ds(s, MOE_BLOCK, stride=SUBLANES), :].astype(BF16)
                               for s in range(SUBLANES)], axis=1)
        a = _dot(lhs, w1b_ref[...])
        g = _dot(lhs, w3b_ref[...])
        gates = gate_ref[...]
        lane = lax.broadcasted_iota(jnp.int32, (MOE_BLOCK, LANES), 1)
        first = gid_ref[b] * EXP_PER_GROUP
        parts = []
        for j in range(EXP_PER_GROUP):
            cols = slice(j * D_FF, (j + 1) * D_FF)
            gate = jnp.sum(jnp.where(lane == first + j, gates, 0.0), axis=1, keepdims=True)
            parts.append((_silu(a[:, cols]) * g[:, cols] * gate).astype(BF16))
        act = jnp.concatenate(parts, axis=1)
        y = _dot(act, w2b_ref[...])
        for s in range(SUBLANES):
            y_ref[pl.ds(s, MOE_BLOCK, stride=SUBLANES), :] = y[:, s * LANES:(s + 1) * LANES]

    @pl.when(used_ref[b] == 0)
    def _():
        y_ref[...] = jnp.zeros_like(y_ref)


def moe_experts(block_gid, block_used, next_group, sorted_rows, sorted_gates, w1, w3, w2, layer):
    group_ff = EXP_PER_GROUP * D_FF
    w1g = w1.reshape(DEPTH, N_GROUPS, EXP_PER_GROUP, D_MODEL, D_FF)
    w3g = w3.reshape(DEPTH, N_GROUPS, EXP_PER_GROUP, D_MODEL, D_FF)
    w2g = w2.reshape(DEPTH, N_GROUPS, group_ff, D_MODEL)
    block = lambda b, gid, used, nxt: (b, 0)
    return pl.pallas_call(
        functools.partial(_moe_expert_kernel, layer=layer),
        grid_spec=pltpu.PrefetchScalarGridSpec(
            num_scalar_prefetch=3,
            grid=(MOE_BLOCKS,),
            in_specs=[pl.BlockSpec((MOE_BLOCK * SUBLANES, LANES), block),
                      pl.BlockSpec((MOE_BLOCK, LANES), block),
                      pl.BlockSpec(memory_space=pl.ANY), pl.BlockSpec(memory_space=pl.ANY),
                      pl.BlockSpec((1, 1, group_ff, D_MODEL), lambda b, gid, used, nxt: (layer, gid[b], 0, 0))],
            out_specs=pl.BlockSpec((MOE_BLOCK * SUBLANES, LANES), block),
            scratch_shapes=[pltpu.VMEM((EXP_PER_GROUP, D_MODEL, D_FF), F32),
                            pltpu.VMEM((EXP_PER_GROUP, D_MODEL, D_FF), F32),
                            pltpu.VMEM((D_MODEL, group_ff), BF16), pltpu.VMEM((D_MODEL, group_ff), BF16),
                            pltpu.VMEM((group_ff, D_MODEL), BF16),
                            pltpu.SemaphoreType.DMA((2,))],
        ),
        out_shape=jax.ShapeDtypeStruct((T_PAD * SUBLANES, LANES), F32),
        compiler_params=_params(1),
        name="moe_experts",
    )(block_gid, block_used, next_group, sorted_rows, sorted_gates, w1g, w3g, w2g)


def _moe_combine_kernel(pos_ref, ys_ref, x_ref, m_ref, o_ref, buf_ref):
    i = pl.program_id(0)

    def move(t, carry):
        src = pl.multiple_of(pos_ref[i * TM + t] * SUBLANES, SUBLANES)
        dst = pl.multiple_of(t * SUBLANES, SUBLANES)
        buf_ref[pl.ds(dst, SUBLANES), :] = ys_ref[pl.ds(src, SUBLANES), :]
        return carry
    lax.fori_loop(0, TM, move, 0, unroll=8)

    for s in range(SUBLANES):
        cols = slice(s * LANES, (s + 1) * LANES)
        y = buf_ref[pl.ds(s, TM, stride=SUBLANES), :]
        o_ref[:, cols] = x_ref[:, cols] + m_ref[0, 0, 5:6, cols] * y


def moe_combine(pos, y_sorted, x, mods, layer):
    return pl.pallas_call(
        _moe_combine_kernel,
        grid_spec=pltpu.PrefetchScalarGridSpec(
            num_scalar_prefetch=1,
            grid=(N_TILES,),
            in_specs=[
                pl.BlockSpec(memory_space=pltpu.VMEM),
                pl.BlockSpec((TM, D_MODEL), lambda i, pos: (i, 0)),
                pl.BlockSpec((1, 1, 6, D_MODEL), lambda i, pos: (layer, _mod_row(i), 0, 0)),
            ],
            out_specs=pl.BlockSpec((TM, D_MODEL), lambda i, pos: (i, 0)),
            scratch_shapes=[pltpu.VMEM((TM * SUBLANES, LANES), F32)],
        ),
        out_shape=jax.ShapeDtypeStruct((T_ALL, D_MODEL), F32),
        compiler_params=_params(1),
        name="moe_combine",
    )(pos, y_sorted, x, mods)


def moe(x, gain, mods, wr_t, b_r, tri, w1, w3, w2, layer):
    tiles, gates, meta, counts = moe_route(x, gain, mods, wr_t, b_r, tri, layer)
    pos, block_gid, block_used, next_group = _moe_positions(meta, counts)
    sorted_rows, sorted_gates = moe_permute(pos, tiles, gates)
    y_sorted = moe_experts(block_gid, block_used, next_group, sorted_rows, sorted_gates, w1, w3, w2, layer)
    return moe_combine(pos, y_sorted, x, mods, layer)


def _final_norm_kernel(x_ref, g_ref, o_ref):
    o_ref[...] = _rms(x_ref[...]) * g_ref[...]


def final_norm(x, gain, tile0, ntiles):
    return pl.pallas_call(
        _final_norm_kernel,
        grid=(ntiles,),
        in_specs=[pl.BlockSpec((TM, D_MODEL), lambda i: (tile0 + i, 0)),
                  pl.BlockSpec((1, D_MODEL), lambda i: (0, 0))],
        out_specs=pl.BlockSpec((TM, D_MODEL), lambda i: (i, 0)),
        out_shape=jax.ShapeDtypeStruct((ntiles * TM, D_MODEL), F32),
        compiler_params=_params(1),
        name="final_norm",
    )(x, gain.reshape(1, D_MODEL))


def kernel(x_prompt, x_sample, cache_attn_k, cache_attn_v, state_retention, c, c_ctx, w_ada, b_ada, norm1_g, norm2_g, final_g, w_in, attn_lambda, attn_subln_g, hy_conv_w, hy_conv_b, hy_f_w1, hy_f_b1, hy_f_w2, hy_f_b2, hy_f_w3, hy_skip, ret_decay_exp, w_branch_a, w_branch_b, w_branch_c, w_gate, b_gate, w_out, w_router, b_router, moe_w1, moe_w3, moe_w2):
    x = jnp.concatenate([x_prompt.reshape(T_CTX, D_MODEL), x_sample.reshape(T_LAT, D_MODEL)], axis=0)
    cond8 = jnp.concatenate([c_ctx[None, :], c, jnp.zeros((8 - 1 - DEC_BATCH, D_MODEL), F32)], axis=0)
    mods = ada_modulation(cond8, w_ada, b_ada).reshape(DEPTH, 8, 6, D_MODEL)

    cos, sin = _rope_tables()
    cos_c, sin_c = cos[:, :H_C * DK_C], sin[:, :H_C * DK_C]
    cache_k = cache_attn_k.reshape(DEC_BATCH, DEPTH, PAST_LEN, W_A)
    cache_v = cache_attn_v.reshape(DEC_BATCH, DEPTH, PAST_LEN, W_A)
    decay_rates = _filter_decay_rates()
    tables_ctx = _dft_tables(SEQ)
    tables_lat = _dft_tables(DEC_SEQ)
    filt_ctx = hyena_filters(SEQ, _filter_features(SEQ), decay_rates, tables_ctx[0], tables_ctx[1],
                             hy_f_w1, hy_f_b1, hy_f_w2, hy_f_b2, hy_f_w3)
    filt_lat = hyena_filters(DEC_SEQ, _filter_features(DEC_SEQ), decay_rates, tables_lat[0], tables_lat[1],
                             hy_f_w1, hy_f_b1, hy_f_w2, hy_f_b2, hy_f_w3)
    wr_t = w_router.T
    b_r = b_router.reshape(N_EXPERTS, 1)
    tri = jnp.asarray(np.triu(np.ones((TM, TM), np.float32), 1), dtype=BF16)

    caches = ()
    states = None
    for l in range(DEPTH):
        p, kv = in_projection(x, norm1_g, mods, w_in, w_gate, b_gate, l)

        ya, *caches = attention_ctx(p, kv, attn_lambda, attn_subln_g, l, caches)
        ya = attention_lat(ya, p, kv, cache_k, cache_v, cos, sin, attn_lambda, attn_subln_g, l)
        yb = hyena(None, p, hy_conv_w, hy_conv_b, hy_skip, filt_ctx[0], filt_ctx[1], tables_ctx,
                   l, SEQ, 0, BATCH)
        yb = hyena(yb, p, hy_conv_w, hy_conv_b, hy_skip, filt_lat[0], filt_lat[1], tables_lat,
                   l, DEC_SEQ, T_CTX // DEC_SEQ, DEC_BATCH)
        yc, states = retention(None, p, ret_decay_exp, None, None, states, l, SEQ, 0, BATCH, False)
        yc = retention(yc, p, ret_decay_exp, cos_c, sin_c, state_retention, l, DEC_SEQ,
                       T_CTX // DEC_SEQ, DEC_BATCH, True)[0]

        x = merge_branches(x, ya, yb, yc, p, mods, w_branch_a, w_branch_b, w_branch_c, w_out, l)
        x = moe(x, norm2_g, mods, wr_t, b_r, tri, moe_w1, moe_w3, moe_w2, l)

    new_k, new_v = caches
    y_prompt = final_norm(x, final_g, 0, CTX_TILES).reshape(BATCH, SEQ, D_MODEL)
    y_sample = final_norm(x, final_g, CTX_TILES, N_TILES - CTX_TILES).reshape(DEC_BATCH, DEC_SEQ, D_MODEL)
    return (y_prompt, y_sample, new_k, new_v, states)
```

```python
import functools
import math

import jax
import jax.numpy as jnp
import numpy as np
from jax import lax
from jax.experimental import pallas as pl
from jax.experimental.pallas import tpu as pltpu

F32 = jnp.float32
BF16 = jnp.bfloat16

D_MODEL = 1024
BATCH = 16
SEQ = 256
DEPTH = 4
DEC_BATCH = 2
DEC_SEQ = 1024
PAST_LEN = 256
GRID_W = 64
EPS = 1e-6
ROPE_BASE = 10000.0
H_A = 4
DH_A = 64
DV_A = 128
W_A = 512
C_B = 512
SHORT_K = 3
H_EMB = 33
H_FFN = 64
H_FAST_DECAY = 0.3
H_SLOW_DECAY = 1.5
H_TARGET = 1e-2
H_C = 4
DK_C = 64
DV_C = 128
W_C = 512
N_EXPERTS = 16
N_GROUPS = 4
EXP_PER_GROUP = 4
D_FF = 512
D_IN = 4608

T_CTX = BATCH * SEQ
T_LAT = DEC_BATCH * DEC_SEQ
T_ALL = T_CTX + T_LAT
TM = 1024
N_TILES = T_ALL // TM
CTX_TILES = T_CTX // TM
LANES = 128
SUBLANES = 8
MOE_BLOCK = 256
MOE_BLOCKS = T_ALL // MOE_BLOCK + N_GROUPS
T_PAD = MOE_BLOCKS * MOE_BLOCK
VMEM_LIMIT = 56 * 1024 * 1024


def _params(n_axes):
    return pltpu.CompilerParams(
        dimension_semantics=("arbitrary",) * n_axes, vmem_limit_bytes=VMEM_LIMIT)


def _mod_row(i):
    return jnp.maximum(i - (CTX_TILES - 1), 0)


def _rms(x):
    return x * lax.rsqrt(jnp.mean(x * x, axis=-1, keepdims=True) + EPS)


def _sigmoid(x):
    return 0.5 * jnp.tanh(0.5 * x) + 0.5


def _silu(x):
    return x * _sigmoid(x)


def _dot(a, b):
    return jnp.dot(a, b, preferred_element_type=F32)


def _dot_nt(a, b):
    return lax.dot_general(a, b, (((1,), (1,)), ((), ())), preferred_element_type=F32)


def _dot_tn(a, b):
    return lax.dot_general(a, b, (((0,), (0,)), ((), ())), preferred_element_type=F32)


def _split3(x):
    hi = x.astype(BF16)
    lo = (x - hi.astype(F32)).astype(BF16)
    return hi, lo


def _dot3(a, b):
    ah, al = _split3(a)
    bh, bl = _split3(b)
    return _dot(ah, bh) + (_dot(ah, bl) + _dot(al, bh))


def _dot3_nt(a, b):
    ah, al = _split3(a)
    bh, bl = _split3(b)
    return _dot_nt(ah, bh) + (_dot_nt(ah, bl) + _dot_nt(al, bh))


def _ada_kernel(c_ref, w_ref, b_ref, o_ref):
    s = _silu(c_ref[...])
    o_ref[0] = _dot(s.astype(BF16), w_ref[0].astype(BF16)) + b_ref[0]


def ada_modulation(cond8, w_ada, b_ada):
    tn = 1536
    n = 6 * D_MODEL
    return pl.pallas_call(
        _ada_kernel,
        grid=(DEPTH, n // tn),
        in_specs=[
            pl.BlockSpec((8, D_MODEL), lambda l, j: (0, 0)),
            pl.BlockSpec((1, D_MODEL, tn), lambda l, j: (l, 0, j)),
            pl.BlockSpec((1, 1, tn), lambda l, j: (l, 0, j)),
        ],
        out_specs=pl.BlockSpec((1, 8, tn), lambda l, j: (l, 0, j)),
        out_shape=jax.ShapeDtypeStruct((DEPTH, 8, n), F32),
        compiler_params=_params(2),
        name="ada_modulation",
    )(cond8, w_ada, b_ada.reshape(DEPTH, 1, n))


PROJ_TM = 3072
PROJ_TN = 512
GATE_BLOCKS = 3 * D_MODEL // PROJ_TN
IN_BLOCKS = D_IN // PROJ_TN
P_WIDTH = 3 * D_MODEL + D_IN
KV_FIRST = GATE_BLOCKS + 1
P_Q = 3 * D_MODEL // W_A
P_HYENA = (3 * D_MODEL + 3 * W_A) // (3 * C_B)
P_RET_Q = (3 * D_MODEL + 3 * W_A + 3 * C_B) // (H_C * DK_C)
P_RET_V = (3 * D_MODEL + 3 * W_A + 3 * C_B + 2 * H_C * DK_C) // W_C


def _in_proj_kernel(x_ref, g_ref, m_ref, win_ref, wg_ref, bg_ref, p_ref, kv_ref, h_ref, *, layer):
    i = pl.program_id(0)
    j = pl.program_id(1)

    @pl.when(j == 0)
    def _():
        for s in range(PROJ_TM // TM):
            rows = slice(s * TM, (s + 1) * TM)
            mod = m_ref[layer, _mod_row(i * (PROJ_TM // TM) + s)]
            y = _rms(x_ref[rows, :]) * g_ref[layer:layer + 1, :]
            h_ref[rows, :] = (y * (1.0 + mod[1:2, :]) + mod[0:1, :]).astype(BF16)

    @pl.when(j < GATE_BLOCKS)
    def _():
        acc = _dot(h_ref[...], wg_ref[0].astype(BF16)) + bg_ref[0]
        p_ref[...] = _sigmoid(acc).astype(BF16)

    @pl.when(j >= GATE_BLOCKS)
    def _():
        acc = _dot(h_ref[...], win_ref[0].astype(BF16))
        p_ref[...] = acc.astype(BF16)

        @pl.when((j == KV_FIRST) | (j == KV_FIRST + 1))
        def _():
            kv_ref[...] = acc


def in_projection(x, gain, mods, w_in, w_gate, b_gate, layer):
    return pl.pallas_call(
        functools.partial(_in_proj_kernel, layer=layer),
        grid=(T_ALL // PROJ_TM, GATE_BLOCKS + IN_BLOCKS),
        in_specs=[
            pl.BlockSpec((PROJ_TM, D_MODEL), lambda i, j: (i, 0), pipeline_mode=pl.Buffered(1)),
            pl.BlockSpec((DEPTH, D_MODEL), lambda i, j: (0, 0)),
            pl.BlockSpec((DEPTH, 8, 6, D_MODEL), lambda i, j: (0, 0, 0, 0)),
            pl.BlockSpec((1, D_MODEL, PROJ_TN), lambda i, j: (layer, 0, jnp.maximum(j - GATE_BLOCKS, 0))),
            pl.BlockSpec((1, D_MODEL, PROJ_TN), lambda i, j: (layer, 0, jnp.minimum(j, GATE_BLOCKS - 1))),
            pl.BlockSpec((1, 1, PROJ_TN), lambda i, j: (layer, 0, jnp.minimum(j, GATE_BLOCKS - 1))),
        ],
        out_specs=[
            pl.BlockSpec((PROJ_TM, PROJ_TN), lambda i, j: (i, j)),
            pl.BlockSpec((PROJ_TM, PROJ_TN), lambda i, j: (i, jnp.clip(j - KV_FIRST, 0, 1))),
        ],
        out_shape=[jax.ShapeDtypeStruct((T_ALL, P_WIDTH), BF16),
                   jax.ShapeDtypeStruct((T_ALL, 2 * W_A), F32)],
        scratch_shapes=[pltpu.VMEM((PROJ_TM, D_MODEL), BF16)],
        compiler_params=_params(2),
        name="in_projection",
    )(x, gain, mods, w_in, w_gate, b_gate.reshape(DEPTH, 1, 3 * D_MODEL))


def _rope_tables():
    n = DEC_SEQ
    rows = n // GRID_W
    row = jnp.repeat(jnp.arange(rows, dtype=F32), GRID_W)
    col = jnp.tile(jnp.arange(GRID_W, dtype=F32), rows)
    quarter = DH_A // 4
    inv = ROPE_BASE ** (-jnp.arange(quarter, dtype=F32) / quarter)
    ar = row[:, None] * inv
    ac = col[:, None] * inv
    ang = jnp.concatenate([ar, ar, ac, ac], axis=-1)
    cos = jnp.tile(jnp.cos(ang), (1, 512 // DH_A))
    sgn = jnp.tile(jnp.concatenate([-jnp.ones((quarter,), F32), jnp.ones((quarter,), F32)]), 512 // (2 * quarter))
    sin = jnp.tile(jnp.sin(ang), (1, 512 // DH_A)) * sgn
    return cos, sin


def _rope(x, cos, sin):
    w = x.shape[-1]
    lane = lax.broadcasted_iota(jnp.int32, x.shape, 1)
    first = (lane % 32) < 16
    rot = jnp.where(first, pltpu.roll(x, w - 16, 1), pltpu.roll(x, 16, 1))
    return x * cos + rot * sin


def _lambda(al_ref, layer):
    al = al_ref[layer]
    a = jnp.sum(al[0:1, :] * al[1:2, :], axis=-1, keepdims=True)
    b = jnp.sum(al[2:3, :] * al[3:4, :], axis=-1, keepdims=True)
    lam_init = 0.8 - 0.6 * math.exp(-0.3 * layer)
    return jnp.exp(a) - jnp.exp(b) + lam_init, lam_init


def _softmax_parts(s):
    m = jnp.max(s, axis=-1, keepdims=True)
    e = jnp.exp2(s - m)
    return e.astype(BF16), 1.0 / jnp.sum(e, axis=-1, keepdims=True)


def _diff_attention(q, k, v, lam, lam_init, subln, o_ref):
    lane = lax.broadcasted_iota(jnp.int32, q.shape, 1)
    q = q * (DH_A ** -0.5 * math.log2(math.e))
    q0 = jnp.where((lane % LANES) < DH_A, q, 0.0).astype(BF16)
    q1 = jnp.where((lane % LANES) >= DH_A, q, 0.0).astype(BF16)
    for h in range(H_A):
        sl = slice(h * LANES, (h + 1) * LANES)
        kh = k[:, sl]
        vh = v[:, sl]
        e0, r0 = _softmax_parts(_dot_nt(q0[:, sl], kh))
        e1, r1 = _softmax_parts(_dot_nt(q1[:, sl], kh))
        o = _dot(e0, vh) * r0 - _dot(e1, vh) * (lam * r1)
        o = _rms(o) * subln * (1.0 - lam_init)
        o_ref[:, sl] = o.astype(o_ref.dtype)


def _attn_ctx_kernel(q_ref, k_ref, v_ref, al_ref, g_ref, o_ref, knew_ref, vnew_ref, *, layer):
    lam, lam_init = _lambda(al_ref, layer)
    subln = g_ref[layer:layer + 1, :]
    k = k_ref[...]
    v = v_ref[...]
    for h in range(H_A):
        knew_ref[0, 0, :, h, :] = k[:, h * LANES:(h + 1) * LANES]
        vnew_ref[0, 0, :, h, :] = v[:, h * LANES:(h + 1) * LANES]
    _diff_attention(q_ref[...].astype(F32), k.astype(BF16), v.astype(BF16), lam, lam_init, subln, o_ref)


QB = 256


def _attn_lat_kernel(ya_ref, q_ref, k_ref, v_ref, ck_ref, cv_ref, cosq_ref, sinq_ref, cos_ref, sin_ref,
                     al_ref, g_ref, o_ref, kall_ref, vall_ref, *, layer):
    del ya_ref

    @pl.when(pl.program_id(1) == 0)
    def _():
        kall_ref[0:PAST_LEN, :] = ck_ref[0, 0].astype(BF16)
        vall_ref[0:PAST_LEN, :] = cv_ref[0, 0].astype(BF16)
        kall_ref[PAST_LEN:, :] = _rope(k_ref[...], cos_ref[...], sin_ref[...]).astype(BF16)
        vall_ref[PAST_LEN:, :] = v_ref[...].astype(BF16)

    lam, lam_init = _lambda(al_ref, layer)
    subln = g_ref[layer:layer + 1, :]
    q = _rope(q_ref[...].astype(F32), cosq_ref[...], sinq_ref[...])
    _diff_attention(q, kall_ref[...], vall_ref[...], lam, lam_init, subln, o_ref)


def attention_lat(ya, q, kv, cache_k, cache_v, cos, sin, attn_lambda, subln_g, layer):
    nqb = DEC_SEQ // QB
    row0 = T_CTX // QB
    seq0 = T_CTX // DEC_SEQ
    return pl.pallas_call(
        functools.partial(_attn_lat_kernel, layer=layer),
        grid=(DEC_BATCH, nqb),
        in_specs=[
            pl.BlockSpec(memory_space=pl.ANY),
            pl.BlockSpec((QB, W_A), lambda b, i: (row0 + b * nqb + i, P_Q)),
            pl.BlockSpec((DEC_SEQ, W_A), lambda b, i: (seq0 + b, 0)),
            pl.BlockSpec((DEC_SEQ, W_A), lambda b, i: (seq0 + b, 1)),
            pl.BlockSpec((1, 1, PAST_LEN, W_A), lambda b, i: (b, layer, 0, 0)),
            pl.BlockSpec((1, 1, PAST_LEN, W_A), lambda b, i: (b, layer, 0, 0)),
            pl.BlockSpec((QB, W_A), lambda b, i: (i, 0)),
            pl.BlockSpec((QB, W_A), lambda b, i: (i, 0)),
            pl.BlockSpec((DEC_SEQ, W_A), lambda b, i: (0, 0)),
            pl.BlockSpec((DEC_SEQ, W_A), lambda b, i: (0, 0)),
            pl.BlockSpec((DEPTH, 4, DH_A), lambda b, i: (0, 0, 0)),
            pl.BlockSpec((DEPTH, DV_A), lambda b, i: (0, 0)),
        ],
        out_specs=pl.BlockSpec((QB, W_A), lambda b, i: (row0 + b * nqb + i, 0)),
        out_shape=jax.ShapeDtypeStruct((T_ALL, W_A), BF16),
        scratch_shapes=[pltpu.VMEM((PAST_LEN + DEC_SEQ, W_A), BF16),
                        pltpu.VMEM((PAST_LEN + DEC_SEQ, W_A), BF16)],
        input_output_aliases={0: 0},
        compiler_params=_params(2),
        name="attention_lat",
    )(ya, q, kv, kv, cache_k, cache_v, cos, sin, cos, sin, attn_lambda, subln_g)


def _dft_tables(n):
    k = np.arange(n, dtype=np.int64)
    prod = (2 * k[:, None] + 1) * k[None, :]
    ang = (prod % (4 * n)).astype(np.float64) * (math.pi / (2 * n))
    c = np.cos(ang).astype(np.float32)
    s = np.sin(ang).astype(np.float32)
    return tuple(jnp.asarray(t).astype(BF16) for t in (c, s, c.T, s.T))


def _filter_features(n):
    t = jnp.linspace(0.0, 1.0, n, dtype=F32)[:, None]
    bands = (H_EMB - 1) // 2
    w = 2.0 * math.pi * jnp.arange(n, dtype=F32)[:, None] / n
    fr = jnp.linspace(1e-4, bands - 1, bands, dtype=F32)
    z = jnp.concatenate([t, jnp.cos(w * fr), -jnp.sin(w * fr)], axis=-1)
    return jnp.pad(z, ((0, 0), (0, LANES - H_EMB)))


def _filter_decay_rates():
    max_decay = math.log(H_TARGET) / H_FAST_DECAY
    min_decay = math.log(H_TARGET) / H_SLOW_DECAY
    return jnp.abs(jnp.linspace(min_decay, max_decay, C_B, dtype=F32))[None, :]


def _hyena_filter_kernel(z_ref, w1_ref, b1_ref, w2_ref, b2_ref, w3_ref, dr_ref, c_ref, s_ref,
                         gre_ref, gim_ref):
    z = z_ref[...]
    hid = jnp.sin(_dot3(z, w1_ref[0]) + b1_ref[0])
    hid = jnp.sin(_dot3(hid, w2_ref[0]) + b2_ref[0])
    h = _dot(hid.astype(BF16), w3_ref[0].astype(BF16))
    window = jnp.exp(-z[:, 0:1] * dr_ref[...])
    hf = h[:, :C_B] * window
    hb = h[:, C_B:] * window
    total = jnp.sum(jnp.abs(hf) + jnp.abs(hb), axis=0, keepdims=True)
    hf = hf / total
    hb = hb / total
    row = lax.broadcasted_iota(jnp.int32, hb.shape, 0)
    hb = jnp.where(row == 0, 0.0, hb)
    gre_ref[0] = _dot(c_ref[...], (hf + hb).astype(BF16))
    gim_ref[0] = _dot(s_ref[...], (hb - hf).astype(BF16))


def hyena_filters(n, z, decay_rates, cf, sf, f_w1, f_b1, f_w2, f_b2, f_w3):
    w1 = jnp.pad(f_w1, ((0, 0), (0, LANES - H_EMB), (0, 0)))
    full = lambda shape: pl.BlockSpec(shape, lambda l: (0,) * len(shape))
    per_layer = lambda shape: pl.BlockSpec((1,) + shape, lambda l: (l,) + (0,) * len(shape))
    return pl.pallas_call(
        _hyena_filter_kernel,
        grid=(DEPTH,),
        in_specs=[
            full((n, LANES)),
            per_layer((LANES, H_FFN)), per_layer((1, H_FFN)),
            per_layer((H_FFN, H_FFN)), per_layer((1, H_FFN)),
            per_layer((H_FFN, 2 * C_B)),
            full((1, C_B)), full((n, n)), full((n, n)),
        ],
        out_specs=[per_layer((n, C_B)), per_layer((n, C_B))],
        out_shape=[jax.ShapeDtypeStruct((DEPTH, n, C_B), F32)] * 2,
        compiler_params=_params(1),
        name="hyena_filters",
    )(z, w1, f_b1.reshape(DEPTH, 1, H_FFN), f_w2, f_b2.reshape(DEPTH, 1, H_FFN), f_w3,
      decay_rates, cf, sf)


def _hyena_kernel(*refs, layer, n, aliased):
    if aliased:
        refs = refs[1:]
    u_ref, cw_ref, cb_ref, skip_ref, gre_ref, gim_ref, c_ref, s_ref, ct_ref, st_ref, o_ref = refs
    u = u_ref[...].astype(F32)
    row = lax.broadcasted_iota(jnp.int32, u.shape, 0)
    prev = jnp.where(row == 0, 0.0, pltpu.roll(u, 1, 0))
    nxt = jnp.where(row == n - 1, 0.0, pltpu.roll(u, n - 1, 0))
    cw = cw_ref[layer]
    uc = cb_ref[layer:layer + 1, :] + prev * cw[0:1, :] + u * cw[1:2, :] + nxt * cw[2:3, :]
    x0 = uc[:, :C_B]
    x1 = uc[:, C_B:2 * C_B]
    v = uc[:, 2 * C_B:]
    w = v * x1
    wb = w.astype(BF16)
    ure = _dot(c_ref[...], wb)
    uim = -_dot(s_ref[...], wb)
    gre = gre_ref[0]
    gim = gim_ref[0]
    yre = (ure * gre - uim * gim).astype(BF16)
    yim = (ure * gim + uim * gre).astype(BF16)
    y = (_dot(ct_ref[...], yre) - _dot(st_ref[...], yim)) * (1.0 / n)
    y = y + w * skip_ref[layer:layer + 1, :]
    o_ref[...] = (x0 * y).astype(o_ref.dtype)


def _full(shape):
    return pl.BlockSpec(shape, lambda *_: (0,) * len(shape))


def _hyena_specs(n, seq0, layer):
    return [
        pl.BlockSpec((n, 3 * C_B), lambda b, *_: (seq0 + b, P_HYENA)),
        _full((DEPTH, SHORT_K, 3 * C_B)), _full((DEPTH, 3 * C_B)), _full((DEPTH, C_B)),
        pl.BlockSpec((1, n, C_B), lambda *_: (layer, 0, 0)),
        pl.BlockSpec((1, n, C_B), lambda *_: (layer, 0, 0)),
        _full((n, n)), _full((n, n)), _full((n, n)), _full((n, n)),
    ]


def hyena_lat(yb, p, conv_w, conv_b, skip, gre, gim, tables, layer):
    seq0 = T_CTX // DEC_SEQ
    return pl.pallas_call(
        functools.partial(_hyena_kernel, layer=layer, n=DEC_SEQ, aliased=True),
        grid=(DEC_BATCH,),
        in_specs=[pl.BlockSpec(memory_space=pl.ANY)] + _hyena_specs(DEC_SEQ, seq0, layer),
        out_specs=pl.BlockSpec((DEC_SEQ, C_B), lambda b: (seq0 + b, 0)),
        out_shape=jax.ShapeDtypeStruct((T_ALL, C_B), BF16),
        input_output_aliases={0: 0},
        compiler_params=_params(1),
        name="hyena_lat",
    )(yb, p, conv_w, conv_b, skip, gre, gim, *tables)


RET_QB = 256


def _retention_kernel(*refs, layer, n, latent):
    if latent:
        _, q_ref, k_ref, v_ref, g_ref, de_ref, cos_ref, sin_ref, s0_ref, o_ref, w_ref = refs
        st_ref = None
    else:
        q_ref, k_ref, v_ref, g_ref, de_ref, o_ref, st_ref, w_ref = refs[-8:]
    log_gamma = jnp.log1p(-jnp.exp2(-de_ref[layer]))

    @pl.when(pl.program_id(0) == 0)
    def _():
        t = lax.broadcasted_iota(jnp.int32, (n, n), 0)
        s = lax.broadcasted_iota(jnp.int32, (n, n), 1)
        lag = (t - s).astype(F32)
        for h in range(H_C):
            rate = jnp.where(lag > 0, log_gamma[0:1, h:h + 1], log_gamma[1:2, h:h + 1])
            w_ref[h] = jnp.where(lag == 0, 2.0, jnp.exp(rate * jnp.abs(lag)))

    q = q_ref[...].astype(F32)
    k = k_ref[...].astype(F32) * (DK_C ** -0.5)
    if latent:
        q = _rope(q, cos_ref[...], sin_ref[...])
        k = _rope(k, cos_ref[...], sin_ref[...])
    v = v_ref[...].astype(BF16)
    gate = g_ref[...].astype(F32)
    lane = lax.broadcasted_iota(jnp.int32, (n, LANES), 1)
    pos = lax.broadcasted_iota(jnp.int32, (n, 1), 0).astype(F32)
    zeros64 = jnp.zeros((DK_C, DV_C), F32)

    for h in range(H_C):
        pair = slice((h // 2) * LANES, (h // 2 + 1) * LANES)
        lo = (h % 2) * DK_C
        own = (lane >= lo) & (lane < lo + DK_C)
        vs = slice(h * DV_C, (h + 1) * DV_C)
        gf = log_gamma[0:1, h:h + 1]
        gb = log_gamma[1:2, h:h + 1]
        qh = jnp.where(own, q[:, pair], 0.0).astype(BF16)
        kh = jnp.where(own, k[:, pair], 0.0)
        kb = kh.astype(BF16)
        vh = v[:, vs]
        if latent:
            s0 = [jnp.concatenate([s0_ref[0, 0, d, h], zeros64] if lo == 0 else [zeros64, s0_ref[0, 0, d, h]],
                                  axis=0).astype(BF16) for d in range(2)]
            carry = (_dot(qh, s0[0]) * jnp.exp(gf * (pos + 1.0))
                     + _dot(qh, s0[1]) * jnp.exp(gb * (float(n) - pos)))
        for r0 in range(0, n, RET_QB):
            rows = slice(r0, r0 + RET_QB)
            att = _dot_nt(qh[rows], kb) * w_ref[h, rows, :]
            out = _dot(att.astype(BF16), vh)
            if latent:
                out = out + carry[rows]
            o_ref[rows, vs] = (_rms(out) * _silu(gate[rows, vs])).astype(o_ref.dtype)
        if st_ref is not None:
            sf = _dot_tn((kh * jnp.exp(gf * (float(n - 1) - pos))).astype(BF16), vh)
            sb = _dot_tn((kh * jnp.exp(gb * pos)).astype(BF16), vh)
            st_ref[0, 0, 0, h] = sf[lo:lo + DK_C, :]
            st_ref[0, 0, 1, h] = sb[lo:lo + DK_C, :]


def _retention_specs(n, seq0):
    return [
        pl.BlockSpec((n, H_C * DK_C), lambda b, *_: (seq0 + b, P_RET_Q)),
        pl.BlockSpec((n, H_C * DK_C), lambda b, *_: (seq0 + b, P_RET_Q + 1)),
        pl.BlockSpec((n, W_C), lambda b, *_: (seq0 + b, P_RET_V)),
        pl.BlockSpec((n, W_C), lambda b, *_: (seq0 + b, P_RET_V + 1)),
        _full((DEPTH, 2, H_C)),
    ]


def retention_lat(yc, p, decay_exp, cos, sin, state0, layer):
    n = DEC_SEQ
    seq0 = T_CTX // DEC_SEQ
    return pl.pallas_call(
        functools.partial(_retention_kernel, layer=layer, n=n, latent=True),
        grid=(DEC_BATCH,),
        in_specs=[pl.BlockSpec(memory_space=pl.ANY)] + _retention_specs(n, seq0) + [
            _full((n, H_C * DK_C)), _full((n, H_C * DK_C)),
            pl.BlockSpec((1, 1, 2, H_C, DK_C, DV_C), lambda b: (b, layer, 0, 0, 0, 0)),
        ],
        out_specs=pl.BlockSpec((n, W_C), lambda b: (seq0 + b, 0)),
        out_shape=jax.ShapeDtypeStruct((T_ALL, W_C), BF16),
        scratch_shapes=[pltpu.VMEM((H_C, n, n), F32)],
        input_output_aliases={0: 0},
        compiler_params=_params(1),
        name="retention_lat",
    )(yc, p, p, p, p, decay_exp, cos, sin, state0)


N_ATTN_IN = 5
N_HYENA_IN = 10
N_RET_IN = 5


def _mixers_ctx_kernel(*refs, layer, n_threaded):
    refs = refs[n_threaded:]
    attn_in = refs[:N_ATTN_IN]
    hyena_in = refs[N_ATTN_IN:N_ATTN_IN + N_HYENA_IN]
    ret_in = refs[N_ATTN_IN + N_HYENA_IN:N_ATTN_IN + N_HYENA_IN + N_RET_IN]
    ya_ref, knew_ref, vnew_ref, yb_ref, yc_ref, st_ref, w_ref = refs[N_ATTN_IN + N_HYENA_IN + N_RET_IN:]
    _attn_ctx_kernel(*attn_in, ya_ref, knew_ref, vnew_ref, layer=layer)
    _hyena_kernel(*hyena_in, yb_ref, layer=layer, n=SEQ, aliased=False)
    _retention_kernel(*ret_in, yc_ref, st_ref, w_ref, layer=layer, n=SEQ, latent=False)


def mixers_ctx(p, kv, attn_lambda, subln_g, conv_w, conv_b, skip, gre, gim, tables, decay_exp, layer, threaded):
    cache_shape = jax.ShapeDtypeStruct((BATCH, DEPTH, SEQ, H_A, DV_A), F32)
    cache_spec = pl.BlockSpec((1, 1, SEQ, H_A, DV_A), lambda b: (b, layer, 0, 0, 0))
    branch_shape = jax.ShapeDtypeStruct((T_ALL, W_A), BF16)
    branch_spec = pl.BlockSpec((SEQ, W_A), lambda b: (b, 0))
    n_threaded = len(threaded)
    attn_specs = [
        pl.BlockSpec((SEQ, W_A), lambda b: (b, P_Q)),
        pl.BlockSpec((SEQ, W_A), lambda b: (b, 0)),
        pl.BlockSpec((SEQ, W_A), lambda b: (b, 1)),
        _full((DEPTH, 4, DH_A)), _full((DEPTH, DV_A)),
    ]
    return pl.pallas_call(
        functools.partial(_mixers_ctx_kernel, layer=layer, n_threaded=n_threaded),
        grid=(BATCH,),
        in_specs=([pl.BlockSpec(memory_space=pl.ANY)] * n_threaded + attn_specs
                  + _hyena_specs(SEQ, 0, layer) + _retention_specs(SEQ, 0)),
        out_specs=[branch_spec, cache_spec, cache_spec, branch_spec, branch_spec,
                   pl.BlockSpec((1, 1, 2, H_C, DK_C, DV_C), lambda b: (b, layer, 0, 0, 0, 0))],
        out_shape=[branch_shape, cache_shape, cache_shape, branch_shape, branch_shape,
                   jax.ShapeDtypeStruct((BATCH, DEPTH, 2, H_C, DK_C, DV_C), F32)],
        scratch_shapes=[pltpu.VMEM((H_C, SEQ, SEQ), F32)],
        input_output_aliases=dict(zip(range(n_threaded), (1, 2, 5))),
        compiler_params=_params(1),
        name="mixers_ctx",
    )(*threaded, p, kv, kv, attn_lambda, subln_g, p, conv_w, conv_b, skip, gre, gim, *tables,
      p, p, p, p, decay_exp)


def _merge_kernel(x_ref, ya_ref, yb_ref, yc_ref, g_ref, m_ref, wa_ref, wb_ref, wc_ref, wo_ref,
                  o_ref, wbf_ref, wobf_ref):
    @pl.when(pl.program_id(0) == 0)
    def _():
        wbf_ref[0] = wa_ref[0].astype(BF16)
        wbf_ref[1] = wb_ref[0].astype(BF16)
        wbf_ref[2] = wc_ref[0].astype(BF16)
        wobf_ref[...] = wo_ref[0].astype(BF16)

    g = g_ref[...]
    merged = (g[:, :D_MODEL] * _dot(ya_ref[...], wbf_ref[0])
              + g[:, D_MODEL:2 * D_MODEL] * _dot(yb_ref[...], wbf_ref[1])
              + g[:, 2 * D_MODEL:] * _dot(yc_ref[...], wbf_ref[2]))
    g1 = m_ref[0, 0, 2:3, :]
    o_ref[...] = x_ref[...] + g1 * _dot(merged.astype(BF16), wobf_ref[...])


def merge_branches(x, ya, yb, yc, gates, mods, w_a, w_b, w_c, w_out, layer):
    tile = lambda w: pl.BlockSpec((TM, w), lambda i: (i, 0))
    wspec = lambda k: pl.BlockSpec((1, k, D_MODEL), lambda i: (layer, 0, 0))
    return pl.pallas_call(
        _merge_kernel,
        grid=(N_TILES,),
        in_specs=[
            tile(D_MODEL), tile(W_A), tile(C_B), tile(W_C), tile(3 * D_MODEL),
            pl.BlockSpec((1, 1, 6, D_MODEL), lambda i: (layer, _mod_row(i), 0, 0)),
            wspec(W_A), wspec(C_B), wspec(W_C), wspec(D_MODEL),
        ],
        out_specs=tile(D_MODEL),
        out_shape=jax.ShapeDtypeStruct((T_ALL, D_MODEL), F32),
        scratch_shapes=[pltpu.VMEM((3, W_A, D_MODEL), BF16), pltpu.VMEM((D_MODEL, D_MODEL), BF16)],
        compiler_params=_params(1),
        name="merge_branches",
    )(x, ya, yb, yc, gates, mods, w_a, w_b, w_c, w_out)


def _route(h2, wr_t, b_r):
    logits = _dot3_nt(wr_t, h2)
    m = jnp.max(logits, axis=0, keepdims=True)
    e = jnp.exp(logits - m)
    scores = e / jnp.sum(e, axis=0, keepdims=True)
    sel = scores + b_r
    rows = [sel[i:i + 1, :] for i in range(N_EXPERTS)]
    in_group = []
    gscore = []
    for g in range(N_GROUPS):
        members = range(g * EXP_PER_GROUP, (g + 1) * EXP_PER_GROUP)
        total = None
        for i in members:
            rank = None
            for j in members:
                if j == i:
                    continue
                ahead = (rows[j] >= rows[i]) if j < i else (rows[j] > rows[i])
                ahead = ahead.astype(F32)
                rank = ahead if rank is None else rank + ahead
            chosen = rank < 2.0
            in_group.append(chosen)
            part = jnp.where(chosen, rows[i], 0.0)
            total = part if total is None else total + part
        gscore.append(total)
    gates = []
    group_hot = []
    for g in range(N_GROUPS):
        best = None
        for g2 in range(N_GROUPS):
            if g2 == g:
                continue
            wins = gscore[g] > gscore[g2] if g2 < g else gscore[g] >= gscore[g2]
            best = wins if best is None else best & wins
        group_hot.append(best.astype(F32))
        for i in range(g * EXP_PER_GROUP, (g + 1) * EXP_PER_GROUP):
            gates.append(jnp.where(best & in_group[i], scores[i:i + 1, :], 0.0))
    gates = jnp.concatenate(gates, axis=0)
    return gates / jnp.sum(gates, axis=0, keepdims=True), group_hot


def _moe_route_kernel(x_ref, g_ref, m_ref, wr_ref, br_ref, tri_ref, tiles_ref, gate_ref, meta_ref, count_ref,
                      *, layer):
    y = _rms(x_ref[...]) * g_ref[layer:layer + 1, :]
    h2 = y * (1.0 + m_ref[0, 0, 4:5, :]) + m_ref[0, 0, 3:4, :]
    gates, group_hot = _route(h2, wr_ref[...], br_ref[...])
    for s in range(SUBLANES):
        tiles_ref[pl.ds(s, TM, stride=SUBLANES), :] = h2[:, s * LANES:(s + 1) * LANES]
    padded = jnp.concatenate([gates, jnp.zeros((LANES - N_EXPERTS, TM), F32)], axis=0)
    gate_ref[...] = padded.T
    row = lax.broadcasted_iota(jnp.int32, (SUBLANES, TM), 0)
    hot = jnp.zeros((SUBLANES, TM), F32)
    for g in range(N_GROUPS):
        hot = jnp.where(row == g, group_hot[g], hot)
    before = _dot(hot.astype(BF16), tri_ref[...])
    rank = jnp.sum(hot * before, axis=0, keepdims=True)
    gid = group_hot[1] + 2.0 * group_hot[2] + 3.0 * group_hot[3]
    meta = jnp.where(row == 0, gid, jnp.where(row == 1, rank, 0.0))
    meta_ref[0] = meta.astype(jnp.int32)
    counts = jnp.sum(hot, axis=1, keepdims=True) + jnp.zeros((SUBLANES, LANES), F32)
    count_ref[0] = counts.astype(jnp.int32)


def moe_route(x, gain, mods, wr_t, b_r, tri, layer):
    return pl.pallas_call(
        functools.partial(_moe_route_kernel, layer=layer),
        grid=(N_TILES,),
        in_specs=[
            pl.BlockSpec((TM, D_MODEL), lambda i: (i, 0)),
            pl.BlockSpec((DEPTH, D_MODEL), lambda i: (0, 0)),
            pl.BlockSpec((1, 1, 6, D_MODEL), lambda i: (layer, _mod_row(i), 0, 0)),
            pl.BlockSpec((N_EXPERTS, D_MODEL), lambda i: (0, 0)),
            pl.BlockSpec((N_EXPERTS, 1), lambda i: (0, 0)),
            pl.BlockSpec((TM, TM), lambda i: (0, 0)),
        ],
        out_specs=[pl.BlockSpec((TM * SUBLANES, LANES), lambda i: (i, 0)),
                   pl.BlockSpec((TM, LANES), lambda i: (i, 0)),
                   pl.BlockSpec((1, SUBLANES, TM), lambda i: (i, 0, 0)),
                   pl.BlockSpec((1, SUBLANES, LANES), lambda i: (i, 0, 0))],
        out_shape=[jax.ShapeDtypeStruct((T_ALL * SUBLANES, LANES), F32),
                   jax.ShapeDtypeStruct((T_ALL, LANES), F32),
                   jax.ShapeDtypeStruct((N_TILES, SUBLANES, TM), jnp.int32),
                   jax.ShapeDtypeStruct((N_TILES, SUBLANES, LANES), jnp.int32)],
        compiler_params=_params(1),
        name="moe_route",
    )(x, gain, mods, wr_t, b_r, tri)


def _moe_positions(meta, counts):
    gid = meta[:, 0, :]
    rank = meta[:, 1, :]
    cnt = counts[:, :N_GROUPS, 0]
    total = jnp.sum(cnt, axis=0)
    padded = (total + (MOE_BLOCK - 1)) // MOE_BLOCK * MOE_BLOCK
    group_start = jnp.cumsum(padded) - padded
    base = group_start[None, :] + jnp.cumsum(cnt, axis=0) - cnt
    pos = rank
    for g in range(N_GROUPS):
        pos = pos + jnp.where(gid == g, base[:, g:g + 1], 0)
    first_block = group_start // MOE_BLOCK
    blk = jnp.arange(MOE_BLOCKS, dtype=jnp.int32)
    block_gid = sum((blk >= first_block[g]).astype(jnp.int32) for g in range(1, N_GROUPS))
    block_used = (blk * MOE_BLOCK < jnp.sum(padded)).astype(jnp.int32)
    later = jnp.where(block_gid[None, :] > block_gid[:, None], block_gid[None, :], N_GROUPS)
    next_group = jnp.min(later, axis=1)
    next_group = jnp.where(next_group == N_GROUPS, -1, next_group).astype(jnp.int32)
    return pos.reshape(T_ALL).astype(jnp.int32), block_gid, block_used, next_group


def _moe_permute_kernel(pos_ref, tiles_ref, gate_ref, sorted_ref, gsorted_ref):
    i = pl.program_id(0)

    @pl.when(i == 0)
    def _():
        def zero(b, carry):
            start = pl.multiple_of(b * TM, TM)
            sorted_ref[pl.ds(start, TM), :] = jnp.zeros((TM, LANES), F32)
            return carry
        lax.fori_loop(0, T_PAD * SUBLANES // TM, zero, 0)
        gsorted_ref[...] = jnp.zeros((T_PAD, LANES), F32)

    def move(t, carry):
        p = pos_ref[i * TM + t]
        dst = pl.multiple_of(p * SUBLANES, SUBLANES)
        src = pl.multiple_of(t * SUBLANES, SUBLANES)
        sorted_ref[pl.ds(dst, SUBLANES), :] = tiles_ref[pl.ds(src, SUBLANES), :]
        gsorted_ref[pl.ds(p, 1), :] = gate_ref[pl.ds(t, 1), :]
        return carry
    lax.fori_loop(0, TM, move, 0, unroll=8)


def moe_permute(pos, tiles, gates):
    return pl.pallas_call(
        _moe_permute_kernel,
        grid_spec=pltpu.PrefetchScalarGridSpec(
            num_scalar_prefetch=1,
            grid=(N_TILES,),
            in_specs=[pl.BlockSpec((TM * SUBLANES, LANES), lambda i, pos: (i, 0)),
                      pl.BlockSpec((TM, LANES), lambda i, pos: (i, 0))],
            out_specs=[pl.BlockSpec(memory_space=pltpu.VMEM), pl.BlockSpec(memory_space=pltpu.VMEM)],
        ),
        out_shape=[jax.ShapeDtypeStruct((T_PAD * SUBLANES, LANES), F32),
                   jax.ShapeDtypeStruct((T_PAD, LANES), F32)],
        compiler_params=_params(1),
        name="moe_permute",
    )(pos, tiles, gates)


def _group_changed(gid_ref, b):
    return (b == 0) | (gid_ref[b] != gid_ref[jnp.maximum(b - 1, 0)])


def _moe_expert_kernel(gid_ref, used_ref, next_ref, s_ref, gate_ref, w1_hbm, w3_hbm, w2_ref, y_ref,
                       w1f_ref, w3f_ref, w1b_ref, w3b_ref, w2b_ref, sem, *, layer):
    b = pl.program_id(0)

    def up_weight_copies(group):
        return (pltpu.make_async_copy(w1_hbm.at[layer, group], w1f_ref, sem.at[0]),
                pltpu.make_async_copy(w3_hbm.at[layer, group], w3f_ref, sem.at[1]))

    @pl.when(b == 0)
    def _():
        for copy in up_weight_copies(gid_ref[0]):
            copy.start()

    @pl.when(_group_changed(gid_ref, b))
    def _():
        for copy in up_weight_copies(gid_ref[b]):
            copy.wait()
        for j in range(EXP_PER_GROUP):
            cols = slice(j * D_FF, (j + 1) * D_FF)
            w1b_ref[:, cols] = w1f_ref[j].astype(BF16)
            w3b_ref[:, cols] = w3f_ref[j].astype(BF16)
        w2b_ref[...] = w2_ref[0, 0].astype(BF16)

        @pl.when(next_ref[b] >= 0)
        def _():
            for copy in up_weight_copies(next_ref[b]):
                copy.start()

    @pl.when(used_ref[b] == 1)
    def _():
        lhs = jnp.concatenate([s_ref[pl.ds(s, MOE_BLOCK, stride=SUBLANES), :].astype(BF16)
                               for s in range(SUBLANES)], axis=1)
        a = _dot(lhs, w1b_ref[...])
        g = _dot(lhs, w3b_ref[...])
        gates = gate_ref[...]
        lane = lax.broadcasted_iota(jnp.int32, (MOE_BLOCK, LANES), 1)
        first = gid_ref[b] * EXP_PER_GROUP
        parts = []
        for j in range(EXP_PER_GROUP):
            cols = slice(j * D_FF, (j + 1) * D_FF)
            gate = jnp.sum(jnp.where(lane == first + j, gates, 0.0), axis=1, keepdims=True)
            parts.append((_silu(a[:, cols]) * g[:, cols] * gate).astype(BF16))
        act = jnp.concatenate(parts, axis=1)
        y = _dot(act, w2b_ref[...])
        for s in range(SUBLANES):
            y_ref[pl.ds(s, MOE_BLOCK, stride=SUBLANES), :] = y[:, s * LANES:(s + 1) * LANES]

    @pl.when(used_ref[b] == 0)
    def _():
        y_ref[...] = jnp.zeros_like(y_ref)


def moe_experts(block_gid, block_used, next_group, sorted_rows, sorted_gates, w1, w3, w2, layer):
    group_ff = EXP_PER_GROUP * D_FF
    w1g = w1.reshape(DEPTH, N_GROUPS, EXP_PER_GROUP, D_MODEL, D_FF)
    w3g = w3.reshape(DEPTH, N_GROUPS, EXP_PER_GROUP, D_MODEL, D_FF)
    w2g = w2.reshape(DEPTH, N_GROUPS, group_ff, D_MODEL)
    block = lambda b, gid, used, nxt: (b, 0)
    return pl.pallas_call(
        functools.partial(_moe_expert_kernel, layer=layer),
        grid_spec=pltpu.PrefetchScalarGridSpec(
            num_scalar_prefetch=3,
            grid=(MOE_BLOCKS,),
            in_specs=[pl.BlockSpec((MOE_BLOCK * SUBLANES, LANES), block),
                      pl.BlockSpec((MOE_BLOCK, LANES), block),
                      pl.BlockSpec(memory_space=pl.ANY), pl.BlockSpec(memory_space=pl.ANY),
                      pl.BlockSpec((1, 1, group_ff, D_MODEL), lambda b, gid, used, nxt: (layer, gid[b], 0, 0))],
            out_specs=pl.BlockSpec((MOE_BLOCK * SUBLANES, LANES), block),
            scratch_shapes=[pltpu.VMEM((EXP_PER_GROUP, D_MODEL, D_FF), F32),
                            pltpu.VMEM((EXP_PER_GROUP, D_MODEL, D_FF), F32),
                            pltpu.VMEM((D_MODEL, group_ff), BF16), pltpu.VMEM((D_MODEL, group_ff), BF16),
                            pltpu.VMEM((group_ff, D_MODEL), BF16),
                            pltpu.SemaphoreType.DMA((2,))],
        ),
        out_shape=jax.ShapeDtypeStruct((T_PAD * SUBLANES, LANES), F32),
        compiler_params=_params(1),
        name="moe_experts",
    )(block_gid, block_used, next_group, sorted_rows, sorted_gates, w1g, w3g, w2g)


def _moe_combine_kernel(pos_ref, ys_ref, x_ref, m_ref, o_ref, buf_ref):
    i = pl.program_id(0)

    def move(t, carry):
        src = pl.multiple_of(pos_ref[i * TM + t] * SUBLANES, SUBLANES)
        dst = pl.multiple_of(t * SUBLANES, SUBLANES)
        buf_ref[pl.ds(dst, SUBLANES), :] = ys_ref[pl.ds(src, SUBLANES), :]
        return carry
    lax.fori_loop(0, TM, move, 0, unroll=8)

    for s in range(SUBLANES):
        cols = slice(s * LANES, (s + 1) * LANES)
        y = buf_ref[pl.ds(s, TM, stride=SUBLANES), :]
        o_ref[:, cols] = x_ref[:, cols] + m_ref[0, 0, 5:6, cols] * y


def moe_combine(pos, y_sorted, x, mods, layer):
    return pl.pallas_call(
        _moe_combine_kernel,
        grid_spec=pltpu.PrefetchScalarGridSpec(
            num_scalar_prefetch=1,
            grid=(N_TILES,),
            in_specs=[
                pl.BlockSpec(memory_space=pltpu.VMEM),
                pl.BlockSpec((TM, D_MODEL), lambda i, pos: (i, 0)),
                pl.BlockSpec((1, 1, 6, D_MODEL), lambda i, pos: (layer, _mod_row(i), 0, 0)),
            ],
            out_specs=pl.BlockSpec((TM, D_MODEL), lambda i, pos: (i, 0)),
            scratch_shapes=[pltpu.VMEM((TM * SUBLANES, LANES), F32)],
        ),
        out_shape=jax.ShapeDtypeStruct((T_ALL, D_MODEL), F32),
        compiler_params=_params(1),
        name="moe_combine",
    )(pos, y_sorted, x, mods)


def moe(x, gain, mods, wr_t, b_r, tri, w1, w3, w2, layer):
    tiles, gates, meta, counts = moe_route(x, gain, mods, wr_t, b_r, tri, layer)
    pos, block_gid, block_used, next_group = _moe_positions(meta, counts)
    sorted_rows, sorted_gates = moe_permute(pos, tiles, gates)
    y_sorted = moe_experts(block_gid, block_used, next_group, sorted_rows, sorted_gates, w1, w3, w2, layer)
    return moe_combine(pos, y_sorted, x, mods, layer)


def _final_norm_kernel(x_ref, g_ref, o_ref):
    o_ref[...] = _rms(x_ref[...]) * g_ref[...]


def final_norm(x, gain, tile0, ntiles):
    return pl.pallas_call(
        _final_norm_kernel,
        grid=(ntiles,),
        in_specs=[pl.BlockSpec((TM, D_MODEL), lambda i: (tile0 + i, 0)),
                  pl.BlockSpec((1, D_MODEL), lambda i: (0, 0))],
        out_specs=pl.BlockSpec((TM, D_MODEL), lambda i: (i, 0)),
        out_shape=jax.ShapeDtypeStruct((ntiles * TM, D_MODEL), F32),
        compiler_params=_params(1),
        name="final_norm",
    )(x, gain.reshape(1, D_MODEL))


def kernel(x_prompt, x_sample, cache_attn_k, cache_attn_v, state_retention, c, c_ctx, w_ada, b_ada, norm1_g, norm2_g, final_g, w_in, attn_lambda, attn_subln_g, hy_conv_w, hy_conv_b, hy_f_w1, hy_f_b1, hy_f_w2, hy_f_b2, hy_f_w3, hy_skip, ret_decay_exp, w_branch_a, w_branch_b, w_branch_c, w_gate, b_gate, w_out, w_router, b_router, moe_w1, moe_w3, moe_w2):
    x = jnp.concatenate([x_prompt.reshape(T_CTX, D_MODEL), x_sample.reshape(T_LAT, D_MODEL)], axis=0)
    cond8 = jnp.concatenate([c_ctx[None, :], c, jnp.zeros((8 - 1 - DEC_BATCH, D_MODEL), F32)], axis=0)
    mods = ada_modulation(cond8, w_ada, b_ada).reshape(DEPTH, 8, 6, D_MODEL)

    cos, sin = _rope_tables()
    cos_c, sin_c = cos[:, :H_C * DK_C], sin[:, :H_C * DK_C]
    cache_k = cache_attn_k.reshape(DEC_BATCH, DEPTH, PAST_LEN, W_A)
    cache_v = cache_attn_v.reshape(DEC_BATCH, DEPTH, PAST_LEN, W_A)
    decay_rates = _filter_decay_rates()
    tables_ctx = _dft_tables(SEQ)
    tables_lat = _dft_tables(DEC_SEQ)
    filt_ctx = hyena_filters(SEQ, _filter_features(SEQ), decay_rates, tables_ctx[0], tables_ctx[1],
                             hy_f_w1, hy_f_b1, hy_f_w2, hy_f_b2, hy_f_w3)
    filt_lat = hyena_filters(DEC_SEQ, _filter_features(DEC_SEQ), decay_rates, tables_lat[0], tables_lat[1],
                             hy_f_w1, hy_f_b1, hy_f_w2, hy_f_b2, hy_f_w3)
    wr_t = w_router.T
    b_r = b_router.reshape(N_EXPERTS, 1)
    tri = jnp.asarray(np.triu(np.ones((TM, TM), np.float32), 1), dtype=BF16)

    threaded = ()
    for l in range(DEPTH):
        p, kv = in_projection(x, norm1_g, mods, w_in, w_gate, b_gate, l)

        ya, new_k, new_v, yb, yc, states = mixers_ctx(
            p, kv, attn_lambda, attn_subln_g, hy_conv_w, hy_conv_b, hy_skip, filt_ctx[0], filt_ctx[1],
            tables_ctx, ret_decay_exp, l, threaded)
        threaded = (new_k, new_v, states)
        ya = attention_lat(ya, p, kv, cache_k, cache_v, cos, sin, attn_lambda, attn_subln_g, l)
        yb = hyena_lat(yb, p, hy_conv_w, hy_conv_b, hy_skip, filt_lat[0], filt_lat[1], tables_lat, l)
        yc = retention_lat(yc, p, ret_decay_exp, cos_c, sin_c, state_retention, l)

        x = merge_branches(x, ya, yb, yc, p, mods, w_branch_a, w_branch_b, w_branch_c, w_out, l)
        x = moe(x, norm2_g, mods, wr_t, b_r, tri, moe_w1, moe_w3, moe_w2, l)

    y_prompt = final_norm(x, final_g, 0, CTX_TILES).reshape(BATCH, SEQ, D_MODEL)
    y_sample = final_norm(x, final_g, CTX_TILES, N_TILES - CTX_TILES).reshape(DEC_BATCH, DEC_SEQ, D_MODEL)
    return (y_prompt, y_sample, new_k, new_v, states)
```

```python
import functools
import math

import jax
import jax.numpy as jnp
import numpy as np
from jax import lax
from jax.experimental import pallas as pl
from jax.experimental.pallas import tpu as pltpu

F32 = jnp.float32
BF16 = jnp.bfloat16

D_MODEL = 1024
BATCH = 16
SEQ = 256
DEPTH = 4
DEC_BATCH = 2
DEC_SEQ = 1024
PAST_LEN = 256
GRID_W = 64
EPS = 1e-6
ROPE_BASE = 10000.0
H_A = 4
DH_A = 64
DV_A = 128
W_A = 512
C_B = 512
SHORT_K = 3
H_EMB = 33
H_FFN = 64
H_FAST_DECAY = 0.3
H_SLOW_DECAY = 1.5
H_TARGET = 1e-2
H_C = 4
DK_C = 64
DV_C = 128
W_C = 512
N_EXPERTS = 16
N_GROUPS = 4
EXP_PER_GROUP = 4
D_FF = 512
D_IN = 4608

T_CTX = BATCH * SEQ
T_LAT = DEC_BATCH * DEC_SEQ
T_ALL = T_CTX + T_LAT
TM = 1024
N_TILES = T_ALL // TM
CTX_TILES = T_CTX // TM
LANES = 128
SUBLANES = 8
MOE_BLOCK = 256
MOE_BLOCKS = T_ALL // MOE_BLOCK + N_GROUPS
T_PAD = MOE_BLOCKS * MOE_BLOCK
VMEM_LIMIT = 56 * 1024 * 1024


def _params(n_axes):
    return pltpu.CompilerParams(
        dimension_semantics=("arbitrary",) * n_axes, vmem_limit_bytes=VMEM_LIMIT)


def _mod_row(i):
    return jnp.maximum(i - (CTX_TILES - 1), 0)


def _rms(x):
    return x * lax.rsqrt(jnp.mean(x * x, axis=-1, keepdims=True) + EPS)


def _sigmoid(x):
    return 0.5 * jnp.tanh(0.5 * x) + 0.5


def _silu(x):
    return x * _sigmoid(x)


def _dot(a, b):
    return jnp.dot(a, b, preferred_element_type=F32)


def _dot_nt(a, b):
    return lax.dot_general(a, b, (((1,), (1,)), ((), ())), preferred_element_type=F32)


def _dot_tn(a, b):
    return lax.dot_general(a, b, (((0,), (0,)), ((), ())), preferred_element_type=F32)


def _split3(x):
    hi = x.astype(BF16)
    lo = (x - hi.astype(F32)).astype(BF16)
    return hi, lo


def _dot3(a, b):
    ah, al = _split3(a)
    bh, bl = _split3(b)
    return _dot(ah, bh) + (_dot(ah, bl) + _dot(al, bh))


def _dot3_nt(a, b):
    ah, al = _split3(a)
    bh, bl = _split3(b)
    return _dot_nt(ah, bh) + (_dot_nt(ah, bl) + _dot_nt(al, bh))


def _ada_kernel(c_ref, w_ref, b_ref, o_ref):
    s = _silu(c_ref[...])
    o_ref[0] = _dot(s.astype(BF16), w_ref[0].astype(BF16)) + b_ref[0]


def ada_modulation(cond8, w_ada, b_ada):
    tn = 1536
    n = 6 * D_MODEL
    return pl.pallas_call(
        _ada_kernel,
        grid=(DEPTH, n // tn),
        in_specs=[
            pl.BlockSpec((8, D_MODEL), lambda l, j: (0, 0)),
            pl.BlockSpec((1, D_MODEL, tn), lambda l, j: (l, 0, j)),
            pl.BlockSpec((1, 1, tn), lambda l, j: (l, 0, j)),
        ],
        out_specs=pl.BlockSpec((1, 8, tn), lambda l, j: (l, 0, j)),
        out_shape=jax.ShapeDtypeStruct((DEPTH, 8, n), F32),
        compiler_params=_params(2),
        name="ada_modulation",
    )(cond8, w_ada, b_ada.reshape(DEPTH, 1, n))


PROJ_TM = 3072
PROJ_TN = 512
GATE_BLOCKS = 3 * D_MODEL // PROJ_TN
IN_BLOCKS = D_IN // PROJ_TN
P_WIDTH = 3 * D_MODEL + D_IN
KV_FIRST = GATE_BLOCKS + 1
P_Q = 3 * D_MODEL // W_A
P_HYENA = (3 * D_MODEL + 3 * W_A) // (3 * C_B)
P_RET_Q = (3 * D_MODEL + 3 * W_A + 3 * C_B) // (H_C * DK_C)
P_RET_V = (3 * D_MODEL + 3 * W_A + 3 * C_B + 2 * H_C * DK_C) // W_C


def _in_proj_kernel(x_ref, g_ref, m_ref, win_ref, wg_ref, bg_ref, p_ref, kv_ref, h_ref, *, layer):
    i = pl.program_id(0)
    j = pl.program_id(1)

    @pl.when(j == 0)
    def _():
        for s in range(PROJ_TM // TM):
            rows = slice(s * TM, (s + 1) * TM)
            mod = m_ref[layer, _mod_row(i * (PROJ_TM // TM) + s)]
            y = _rms(x_ref[rows, :]) * g_ref[layer:layer + 1, :]
            h_ref[rows, :] = (y * (1.0 + mod[1:2, :]) + mod[0:1, :]).astype(BF16)

    @pl.when(j < GATE_BLOCKS)
    def _():
        acc = _dot(h_ref[...], wg_ref[0].astype(BF16)) + bg_ref[0]
        p_ref[...] = _sigmoid(acc).astype(BF16)

    @pl.when(j >= GATE_BLOCKS)
    def _():
        acc = _dot(h_ref[...], win_ref[0].astype(BF16))
        p_ref[...] = acc.astype(BF16)

        @pl.when((j == KV_FIRST) | (j == KV_FIRST + 1))
        def _():
            kv_ref[...] = acc


def in_projection(x, gain, mods, w_in, w_gate, b_gate, layer):
    return pl.pallas_call(
        functools.partial(_in_proj_kernel, layer=layer),
        grid=(T_ALL // PROJ_TM, GATE_BLOCKS + IN_BLOCKS),
        in_specs=[
            pl.BlockSpec((PROJ_TM, D_MODEL), lambda i, j: (i, 0), pipeline_mode=pl.Buffered(1)),
            pl.BlockSpec((DEPTH, D_MODEL), lambda i, j: (0, 0)),
            pl.BlockSpec((DEPTH, 8, 6, D_MODEL), lambda i, j: (0, 0, 0, 0)),
            pl.BlockSpec((1, D_MODEL, PROJ_TN), lambda i, j: (layer, 0, jnp.maximum(j - GATE_BLOCKS, 0))),
            pl.BlockSpec((1, D_MODEL, PROJ_TN), lambda i, j: (layer, 0, jnp.minimum(j, GATE_BLOCKS - 1))),
            pl.BlockSpec((1, 1, PROJ_TN), lambda i, j: (layer, 0, jnp.minimum(j, GATE_BLOCKS - 1))),
        ],
        out_specs=[
            pl.BlockSpec((PROJ_TM, PROJ_TN), lambda i, j: (i, j)),
            pl.BlockSpec((PROJ_TM, PROJ_TN), lambda i, j: (i, jnp.clip(j - KV_FIRST, 0, 1))),
        ],
        out_shape=[jax.ShapeDtypeStruct((T_ALL, P_WIDTH), BF16),
                   jax.ShapeDtypeStruct((T_ALL, 2 * W_A), F32)],
        scratch_shapes=[pltpu.VMEM((PROJ_TM, D_MODEL), BF16)],
        compiler_params=_params(2),
        name="in_projection",
    )(x, gain, mods, w_in, w_gate, b_gate.reshape(DEPTH, 1, 3 * D_MODEL))


def _rope_tables():
    n = DEC_SEQ
    rows = n // GRID_W
    row = jnp.repeat(jnp.arange(rows, dtype=F32), GRID_W)
    col = jnp.tile(jnp.arange(GRID_W, dtype=F32), rows)
    quarter = DH_A // 4
    inv = ROPE_BASE ** (-jnp.arange(quarter, dtype=F32) / quarter)
    ar = row[:, None] * inv
    ac = col[:, None] * inv
    ang = jnp.concatenate([ar, ar, ac, ac], axis=-1)
    cos = jnp.tile(jnp.cos(ang), (1, 512 // DH_A))
    sgn = jnp.tile(jnp.concatenate([-jnp.ones((quarter,), F32), jnp.ones((quarter,), F32)]), 512 // (2 * quarter))
    sin = jnp.tile(jnp.sin(ang), (1, 512 // DH_A)) * sgn
    return cos, sin


def _rope(x, cos, sin):
    w = x.shape[-1]
    lane = lax.broadcasted_iota(jnp.int32, x.shape, 1)
    first = (lane % 32) < 16
    rot = jnp.where(first, pltpu.roll(x, w - 16, 1), pltpu.roll(x, 16, 1))
    return x * cos + rot * sin


def _lambda(al_ref, layer):
    al = al_ref[layer]
    a = jnp.sum(al[0:1, :] * al[1:2, :], axis=-1, keepdims=True)
    b = jnp.sum(al[2:3, :] * al[3:4, :], axis=-1, keepdims=True)
    lam_init = 0.8 - 0.6 * math.exp(-0.3 * layer)
    return jnp.exp(a) - jnp.exp(b) + lam_init, lam_init


def _softmax_numerators(s):
    return jnp.exp2(s - jnp.max(s, axis=-1, keepdims=True)).astype(BF16)


def _diff_attention(q, k, v, lam, lam_init, subln, o_ref):
    lane = lax.broadcasted_iota(jnp.int32, q.shape, 1)
    q = q * (DH_A ** -0.5 * math.log2(math.e))
    q0 = jnp.where((lane % LANES) < DH_A, q, 0.0).astype(BF16)
    q1 = jnp.where((lane % LANES) >= DH_A, q, 0.0).astype(BF16)
    ones = jnp.ones((k.shape[0], LANES), BF16)
    for h in range(H_A):
        sl = slice(h * LANES, (h + 1) * LANES)
        kh = k[:, sl]
        v_ones = jnp.concatenate([v[:, sl], ones], axis=1)
        o0 = _dot(_softmax_numerators(_dot_nt(q0[:, sl], kh)), v_ones)
        o1 = _dot(_softmax_numerators(_dot_nt(q1[:, sl], kh)), v_ones)
        r0 = 1.0 / o0[:, DV_A:DV_A + 1]
        r1 = lam / o1[:, DV_A:DV_A + 1]
        o = o0[:, :DV_A] * r0 - o1[:, :DV_A] * r1
        o = _rms(o) * subln * (1.0 - lam_init)
        o_ref[:, sl] = o.astype(o_ref.dtype)


def _attn_ctx_kernel(q_ref, k_ref, v_ref, al_ref, g_ref, o_ref, knew_ref, vnew_ref, *, layer):
    lam, lam_init = _lambda(al_ref, layer)
    subln = g_ref[layer:layer + 1, :]
    k = k_ref[...]
    v = v_ref[...]
    for h in range(H_A):
        knew_ref[0, 0, :, h, :] = k[:, h * LANES:(h + 1) * LANES]
        vnew_ref[0, 0, :, h, :] = v[:, h * LANES:(h + 1) * LANES]
    _diff_attention(q_ref[...].astype(F32), k.astype(BF16), v.astype(BF16), lam, lam_init, subln, o_ref)


QB = 256


def _attn_lat_kernel(ya_ref, q_ref, k_ref, v_ref, ck_ref, cv_ref, cosq_ref, sinq_ref, cos_ref, sin_ref,
                     al_ref, g_ref, o_ref, kall_ref, vall_ref, *, layer):
    del ya_ref

    @pl.when(pl.program_id(1) == 0)
    def _():
        kall_ref[0:PAST_LEN, :] = ck_ref[0, 0].astype(BF16)
        vall_ref[0:PAST_LEN, :] = cv_ref[0, 0].astype(BF16)
        kall_ref[PAST_LEN:, :] = _rope(k_ref[...], cos_ref[...], sin_ref[...]).astype(BF16)
        vall_ref[PAST_LEN:, :] = v_ref[...].astype(BF16)

    lam, lam_init = _lambda(al_ref, layer)
    subln = g_ref[layer:layer + 1, :]
    q = _rope(q_ref[...].astype(F32), cosq_ref[...], sinq_ref[...])
    _diff_attention(q, kall_ref[...], vall_ref[...], lam, lam_init, subln, o_ref)


def attention_lat(ya, q, kv, cache_k, cache_v, cos, sin, attn_lambda, subln_g, layer):
    nqb = DEC_SEQ // QB
    row0 = T_CTX // QB
    seq0 = T_CTX // DEC_SEQ
    return pl.pallas_call(
        functools.partial(_attn_lat_kernel, layer=layer),
        grid=(DEC_BATCH, nqb),
        in_specs=[
            pl.BlockSpec(memory_space=pl.ANY),
            pl.BlockSpec((QB, W_A), lambda b, i: (row0 + b * nqb + i, P_Q)),
            pl.BlockSpec((DEC_SEQ, W_A), lambda b, i: (seq0 + b, 0)),
            pl.BlockSpec((DEC_SEQ, W_A), lambda b, i: (seq0 + b, 1)),
            pl.BlockSpec((1, 1, PAST_LEN, W_A), lambda b, i: (b, layer, 0, 0)),
            pl.BlockSpec((1, 1, PAST_LEN, W_A), lambda b, i: (b, layer, 0, 0)),
            pl.BlockSpec((QB, W_A), lambda b, i: (i, 0)),
            pl.BlockSpec((QB, W_A), lambda b, i: (i, 0)),
            pl.BlockSpec((DEC_SEQ, W_A), lambda b, i: (0, 0)),
            pl.BlockSpec((DEC_SEQ, W_A), lambda b, i: (0, 0)),
            pl.BlockSpec((DEPTH, 4, DH_A), lambda b, i: (0, 0, 0)),
            pl.BlockSpec((DEPTH, DV_A), lambda b, i: (0, 0)),
        ],
        out_specs=pl.BlockSpec((QB, W_A), lambda b, i: (row0 + b * nqb + i, 0)),
        out_shape=jax.ShapeDtypeStruct((T_ALL, W_A), BF16),
        scratch_shapes=[pltpu.VMEM((PAST_LEN + DEC_SEQ, W_A), BF16),
                        pltpu.VMEM((PAST_LEN + DEC_SEQ, W_A), BF16)],
        input_output_aliases={0: 0},
        compiler_params=_params(2),
        name="attention_lat",
    )(ya, q, kv, kv, cache_k, cache_v, cos, sin, cos, sin, attn_lambda, subln_g)


def _dft_tables(n):
    k = np.arange(n, dtype=np.int64)
    prod = (2 * k[:, None] + 1) * k[None, :]
    ang = (prod % (4 * n)).astype(np.float64) * (math.pi / (2 * n))
    c = np.cos(ang).astype(np.float32)
    s = np.sin(ang).astype(np.float32)
    return tuple(jnp.asarray(t).astype(BF16) for t in (c, s, c.T, s.T))


def _filter_features(n):
    t = jnp.linspace(0.0, 1.0, n, dtype=F32)[:, None]
    bands = (H_EMB - 1) // 2
    w = 2.0 * math.pi * jnp.arange(n, dtype=F32)[:, None] / n
    fr = jnp.linspace(1e-4, bands - 1, bands, dtype=F32)
    z = jnp.concatenate([t, jnp.cos(w * fr), -jnp.sin(w * fr)], axis=-1)
    return jnp.pad(z, ((0, 0), (0, LANES - H_EMB)))


def _filter_decay_rates():
    max_decay = math.log(H_TARGET) / H_FAST_DECAY
    min_decay = math.log(H_TARGET) / H_SLOW_DECAY
    return jnp.abs(jnp.linspace(min_decay, max_decay, C_B, dtype=F32))[None, :]


def _hyena_filter_kernel(z_ref, w1_ref, b1_ref, w2_ref, b2_ref, w3_ref, dr_ref, c_ref, s_ref,
                         gre_ref, gim_ref):
    z = z_ref[...]
    hid = jnp.sin(_dot3(z, w1_ref[0]) + b1_ref[0])
    hid = jnp.sin(_dot3(hid, w2_ref[0]) + b2_ref[0])
    h = _dot(hid.astype(BF16), w3_ref[0].astype(BF16))
    window = jnp.exp(-z[:, 0:1] * dr_ref[...])
    hf = h[:, :C_B] * window
    hb = h[:, C_B:] * window
    total = jnp.sum(jnp.abs(hf) + jnp.abs(hb), axis=0, keepdims=True)
    hf = hf / total
    hb = hb / total
    row = lax.broadcasted_iota(jnp.int32, hb.shape, 0)
    hb = jnp.where(row == 0, 0.0, hb)
    gre_ref[0] = _dot(c_ref[...], (hf + hb).astype(BF16))
    gim_ref[0] = _dot(s_ref[...], (hb - hf).astype(BF16))


def hyena_filters(n, z, decay_rates, cf, sf, f_w1, f_b1, f_w2, f_b2, f_w3):
    w1 = jnp.pad(f_w1, ((0, 0), (0, LANES - H_EMB), (0, 0)))
    full = lambda shape: pl.BlockSpec(shape, lambda l: (0,) * len(shape))
    per_layer = lambda shape: pl.BlockSpec((1,) + shape, lambda l: (l,) + (0,) * len(shape))
    return pl.pallas_call(
        _hyena_filter_kernel,
        grid=(DEPTH,),
        in_specs=[
            full((n, LANES)),
            per_layer((LANES, H_FFN)), per_layer((1, H_FFN)),
            per_layer((H_FFN, H_FFN)), per_layer((1, H_FFN)),
            per_layer((H_FFN, 2 * C_B)),
            full((1, C_B)), full((n, n)), full((n, n)),
        ],
        out_specs=[per_layer((n, C_B)), per_layer((n, C_B))],
        out_shape=[jax.ShapeDtypeStruct((DEPTH, n, C_B), F32)] * 2,
        compiler_params=_params(1),
        name="hyena_filters",
    )(z, w1, f_b1.reshape(DEPTH, 1, H_FFN), f_w2, f_b2.reshape(DEPTH, 1, H_FFN), f_w3,
      decay_rates, cf, sf)


def _hyena_kernel(*refs, layer, n, aliased):
    if aliased:
        refs = refs[1:]
    u_ref, cw_ref, cb_ref, skip_ref, gre_ref, gim_ref, c_ref, s_ref, ct_ref, st_ref, o_ref = refs
    u = u_ref[...].astype(F32)
    row = lax.broadcasted_iota(jnp.int32, u.shape, 0)
    prev = jnp.where(row == 0, 0.0, pltpu.roll(u, 1, 0))
    nxt = jnp.where(row == n - 1, 0.0, pltpu.roll(u, n - 1, 0))
    cw = cw_ref[layer]
    uc = cb_ref[layer:layer + 1, :] + prev * cw[0:1, :] + u * cw[1:2, :] + nxt * cw[2:3, :]
    x0 = uc[:, :C_B]
    x1 = uc[:, C_B:2 * C_B]
    v = uc[:, 2 * C_B:]
    w = v * x1
    wb = w.astype(BF16)
    ure = _dot(c_ref[...], wb)
    uim = -_dot(s_ref[...], wb)
    gre = gre_ref[0]
    gim = gim_ref[0]
    yre = (ure * gre - uim * gim).astype(BF16)
    yim = (ure * gim + uim * gre).astype(BF16)
    y = (_dot(ct_ref[...], yre) - _dot(st_ref[...], yim)) * (1.0 / n)
    y = y + w * skip_ref[layer:layer + 1, :]
    o_ref[...] = (x0 * y).astype(o_ref.dtype)


def _full(shape):
    return pl.BlockSpec(shape, lambda *_: (0,) * len(shape))


def _hyena_specs(n, seq0, layer):
    return [
        pl.BlockSpec((n, 3 * C_B), lambda b, *_: (seq0 + b, P_HYENA)),
        _full((DEPTH, SHORT_K, 3 * C_B)), _full((DEPTH, 3 * C_B)), _full((DEPTH, C_B)),
        pl.BlockSpec((1, n, C_B), lambda *_: (layer, 0, 0)),
        pl.BlockSpec((1, n, C_B), lambda *_: (layer, 0, 0)),
        _full((n, n)), _full((n, n)), _full((n, n)), _full((n, n)),
    ]


def hyena_lat(yb, p, conv_w, conv_b, skip, gre, gim, tables, layer):
    seq0 = T_CTX // DEC_SEQ
    return pl.pallas_call(
        functools.partial(_hyena_kernel, layer=layer, n=DEC_SEQ, aliased=True),
        grid=(DEC_BATCH,),
        in_specs=[pl.BlockSpec(memory_space=pl.ANY)] + _hyena_specs(DEC_SEQ, seq0, layer),
        out_specs=pl.BlockSpec((DEC_SEQ, C_B), lambda b: (seq0 + b, 0)),
        out_shape=jax.ShapeDtypeStruct((T_ALL, C_B), BF16),
        input_output_aliases={0: 0},
        compiler_params=_params(1),
        name="hyena_lat",
    )(yb, p, conv_w, conv_b, skip, gre, gim, *tables)


RET_QB = 256


def _retention_kernel(*refs, layer, n, latent):
    if latent:
        _, q_ref, k_ref, v_ref, g_ref, de_ref, cos_ref, sin_ref, s0_ref, o_ref, w_ref, vec_ref = refs
        st_ref = None
    else:
        q_ref, k_ref, v_ref, g_ref, de_ref, o_ref, st_ref, w_ref, vec_ref = refs[-9:]
    log_gamma = jnp.log1p(-jnp.exp2(-de_ref[layer]))

    @pl.when(pl.program_id(0) == 0)
    def _():
        t = lax.broadcasted_iota(jnp.int32, (n, n), 0)
        s = lax.broadcasted_iota(jnp.int32, (n, n), 1)
        lag = (t - s).astype(F32)
        for h in range(H_C):
            rate = jnp.where(lag > 0, log_gamma[0:1, h:h + 1], log_gamma[1:2, h:h + 1])
            w_ref[h] = jnp.where(lag == 0, 2.0, jnp.exp(rate * jnp.abs(lag)))
            pos = lax.broadcasted_iota(jnp.int32, (n, LANES), 0).astype(F32)
            steps = (pos + 1.0, float(n) - pos) if latent else (float(n - 1) - pos, pos)
            for d in range(2):
                vec_ref[h, d] = jnp.exp(log_gamma[d:d + 1, h:h + 1] * steps[d])

    q = q_ref[...].astype(F32)
    k = k_ref[...].astype(F32) * (DK_C ** -0.5)
    if latent:
        q = _rope(q, cos_ref[...], sin_ref[...])
        k = _rope(k, cos_ref[...], sin_ref[...])
    v = v_ref[...].astype(BF16)
    gate = g_ref[...].astype(F32)
    lane = lax.broadcasted_iota(jnp.int32, (n, LANES), 1)
    zeros64 = jnp.zeros((DK_C, DV_C), F32)

    for h in range(H_C):
        pair = slice((h // 2) * LANES, (h // 2 + 1) * LANES)
        lo = (h % 2) * DK_C
        own = (lane >= lo) & (lane < lo + DK_C)
        vs = slice(h * DV_C, (h + 1) * DV_C)
        qh = jnp.where(own, q[:, pair], 0.0).astype(BF16)
        kh = jnp.where(own, k[:, pair], 0.0)
        kb = kh.astype(BF16)
        vh = v[:, vs]
        if latent:
            s0 = [jnp.concatenate([s0_ref[0, 0, d, h], zeros64] if lo == 0 else [zeros64, s0_ref[0, 0, d, h]],
                                  axis=0).astype(BF16) for d in range(2)]
            carry = _dot(qh, s0[0]) * vec_ref[h, 0] + _dot(qh, s0[1]) * vec_ref[h, 1]
        for r0 in range(0, n, RET_QB):
            rows = slice(r0, r0 + RET_QB)
            att = _dot_nt(qh[rows], kb) * w_ref[h, rows, :]
            out = _dot(att.astype(BF16), vh)
            if latent:
                out = out + carry[rows]
            o_ref[rows, vs] = (_rms(out) * _silu(gate[rows, vs])).astype(o_ref.dtype)
        if st_ref is not None:
            sf = _dot_tn((kh * vec_ref[h, 0]).astype(BF16), vh)
            sb = _dot_tn((kh * vec_ref[h, 1]).astype(BF16), vh)
            st_ref[0, 0, 0, h] = sf[lo:lo + DK_C, :]
            st_ref[0, 0, 1, h] = sb[lo:lo + DK_C, :]


def _retention_specs(n, seq0):
    return [
        pl.BlockSpec((n, H_C * DK_C), lambda b, *_: (seq0 + b, P_RET_Q)),
        pl.BlockSpec((n, H_C * DK_C), lambda b, *_: (seq0 + b, P_RET_Q + 1)),
        pl.BlockSpec((n, W_C), lambda b, *_: (seq0 + b, P_RET_V)),
        pl.BlockSpec((n, W_C), lambda b, *_: (seq0 + b, P_RET_V + 1)),
        _full((DEPTH, 2, H_C)),
    ]


def retention_lat(yc, p, decay_exp, cos, sin, state0, layer):
    n = DEC_SEQ
    seq0 = T_CTX // DEC_SEQ
    return pl.pallas_call(
        functools.partial(_retention_kernel, layer=layer, n=n, latent=True),
        grid=(DEC_BATCH,),
        in_specs=[pl.BlockSpec(memory_space=pl.ANY)] + _retention_specs(n, seq0) + [
            _full((n, H_C * DK_C)), _full((n, H_C * DK_C)),
            pl.BlockSpec((1, 1, 2, H_C, DK_C, DV_C), lambda b: (b, layer, 0, 0, 0, 0)),
        ],
        out_specs=pl.BlockSpec((n, W_C), lambda b: (seq0 + b, 0)),
        out_shape=jax.ShapeDtypeStruct((T_ALL, W_C), BF16),
        scratch_shapes=[pltpu.VMEM((H_C, n, n), F32), pltpu.VMEM((H_C, 2, n, LANES), F32)],
        input_output_aliases={0: 0},
        compiler_params=_params(1),
        name="retention_lat",
    )(yc, p, p, p, p, decay_exp, cos, sin, state0)


N_ATTN_IN = 5
N_HYENA_IN = 10
N_RET_IN = 5


def _mixers_ctx_kernel(*refs, layer, n_threaded):
    refs = refs[n_threaded:]
    attn_in = refs[:N_ATTN_IN]
    hyena_in = refs[N_ATTN_IN:N_ATTN_IN + N_HYENA_IN]
    ret_in = refs[N_ATTN_IN + N_HYENA_IN:N_ATTN_IN + N_HYENA_IN + N_RET_IN]
    ya_ref, knew_ref, vnew_ref, yb_ref, yc_ref, st_ref, w_ref, vec_ref = refs[N_ATTN_IN + N_HYENA_IN + N_RET_IN:]
    _attn_ctx_kernel(*attn_in, ya_ref, knew_ref, vnew_ref, layer=layer)
    _hyena_kernel(*hyena_in, yb_ref, layer=layer, n=SEQ, aliased=False)
    _retention_kernel(*ret_in, yc_ref, st_ref, w_ref, vec_ref, layer=layer, n=SEQ, latent=False)


def mixers_ctx(p, kv, attn_lambda, subln_g, conv_w, conv_b, skip, gre, gim, tables, decay_exp, layer, threaded):
    cache_shape = jax.ShapeDtypeStruct((BATCH, DEPTH, SEQ, H_A, DV_A), F32)
    cache_spec = pl.BlockSpec((1, 1, SEQ, H_A, DV_A), lambda b: (b, layer, 0, 0, 0))
    branch_shape = jax.ShapeDtypeStruct((T_ALL, W_A), BF16)
    branch_spec = pl.BlockSpec((SEQ, W_A), lambda b: (b, 0))
    n_threaded = len(threaded)
    attn_specs = [
        pl.BlockSpec((SEQ, W_A), lambda b: (b, P_Q)),
        pl.BlockSpec((SEQ, W_A), lambda b: (b, 0)),
        pl.BlockSpec((SEQ, W_A), lambda b: (b, 1)),
        _full((DEPTH, 4, DH_A)), _full((DEPTH, DV_A)),
    ]
    return pl.pallas_call(
        functools.partial(_mixers_ctx_kernel, layer=layer, n_threaded=n_threaded),
        grid=(BATCH,),
        in_specs=([pl.BlockSpec(memory_space=pl.ANY)] * n_threaded + attn_specs
                  + _hyena_specs(SEQ, 0, layer) + _retention_specs(SEQ, 0)),
        out_specs=[branch_spec, cache_spec, cache_spec, branch_spec, branch_spec,
                   pl.BlockSpec((1, 1, 2, H_C, DK_C, DV_C), lambda b: (b, layer, 0, 0, 0, 0))],
        out_shape=[branch_shape, cache_shape, cache_shape, branch_shape, branch_shape,
                   jax.ShapeDtypeStruct((BATCH, DEPTH, 2, H_C, DK_C, DV_C), F32)],
        scratch_shapes=[pltpu.VMEM((H_C, SEQ, SEQ), F32), pltpu.VMEM((H_C, 2, SEQ, LANES), F32)],
        input_output_aliases=dict(zip(range(n_threaded), (1, 2, 5))),
        compiler_params=_params(1),
        name="mixers_ctx",
    )(*threaded, p, kv, kv, attn_lambda, subln_g, p, conv_w, conv_b, skip, gre, gim, *tables,
      p, p, p, p, decay_exp)


def _merge_kernel(x_ref, ya_ref, yb_ref, yc_ref, g_ref, m_ref, wa_ref, wb_ref, wc_ref, wo_ref,
                  o_ref, wbf_ref, wobf_ref):
    @pl.when(pl.program_id(0) == 0)
    def _():
        wbf_ref[0] = wa_ref[0].astype(BF16)
        wbf_ref[1] = wb_ref[0].astype(BF16)
        wbf_ref[2] = wc_ref[0].astype(BF16)
        wobf_ref[...] = wo_ref[0].astype(BF16)

    g = g_ref[...]
    merged = (g[:, :D_MODEL] * _dot(ya_ref[...], wbf_ref[0])
              + g[:, D_MODEL:2 * D_MODEL] * _dot(yb_ref[...], wbf_ref[1])
              + g[:, 2 * D_MODEL:] * _dot(yc_ref[...], wbf_ref[2]))
    g1 = m_ref[0, 0, 2:3, :]
    o_ref[...] = x_ref[...] + g1 * _dot(merged.astype(BF16), wobf_ref[...])


def merge_branches(x, ya, yb, yc, gates, mods, w_a, w_b, w_c, w_out, layer):
    tile = lambda w: pl.BlockSpec((TM, w), lambda i: (i, 0))
    wspec = lambda k: pl.BlockSpec((1, k, D_MODEL), lambda i: (layer, 0, 0))
    return pl.pallas_call(
        _merge_kernel,
        grid=(N_TILES,),
        in_specs=[
            tile(D_MODEL), tile(W_A), tile(C_B), tile(W_C), tile(3 * D_MODEL),
            pl.BlockSpec((1, 1, 6, D_MODEL), lambda i: (layer, _mod_row(i), 0, 0)),
            wspec(W_A), wspec(C_B), wspec(W_C), wspec(D_MODEL),
        ],
        out_specs=tile(D_MODEL),
        out_shape=jax.ShapeDtypeStruct((T_ALL, D_MODEL), F32),
        scratch_shapes=[pltpu.VMEM((3, W_A, D_MODEL), BF16), pltpu.VMEM((D_MODEL, D_MODEL), BF16)],
        compiler_params=_params(1),
        name="merge_branches",
    )(x, ya, yb, yc, gates, mods, w_a, w_b, w_c, w_out)


def _route(h2, wr_t, b_r):
    logits = _dot3_nt(wr_t, h2)
    m = jnp.max(logits, axis=0, keepdims=True)
    e = jnp.exp(logits - m)
    scores = e / jnp.sum(e, axis=0, keepdims=True)
    sel = scores + b_r
    rows = [sel[i:i + 1, :] for i in range(N_EXPERTS)]
    in_group = []
    gscore = []
    for g in range(N_GROUPS):
        members = range(g * EXP_PER_GROUP, (g + 1) * EXP_PER_GROUP)
        total = None
        for i in members:
            rank = None
            for j in members:
                if j == i:
                    continue
                ahead = (rows[j] >= rows[i]) if j < i else (rows[j] > rows[i])
                ahead = ahead.astype(F32)
                rank = ahead if rank is None else rank + ahead
            chosen = rank < 2.0
            in_group.append(chosen)
            part = jnp.where(chosen, rows[i], 0.0)
            total = part if total is None else total + part
        gscore.append(total)
    gates = []
    group_hot = []
    for g in range(N_GROUPS):
        best = None
        for g2 in range(N_GROUPS):
            if g2 == g:
                continue
            wins = gscore[g] > gscore[g2] if g2 < g else gscore[g] >= gscore[g2]
            best = wins if best is None else best & wins
        group_hot.append(best.astype(F32))
        for i in range(g * EXP_PER_GROUP, (g + 1) * EXP_PER_GROUP):
            gates.append(jnp.where(best & in_group[i], scores[i:i + 1, :], 0.0))
    gates = jnp.concatenate(gates, axis=0)
    return gates / jnp.sum(gates, axis=0, keepdims=True), group_hot


def _moe_route_kernel(x_ref, g_ref, m_ref, wr_ref, br_ref, tri_ref, tiles_ref, gate_ref, meta_ref, count_ref,
                      *, layer):
    y = _rms(x_ref[...]) * g_ref[layer:layer + 1, :]
    h2 = y * (1.0 + m_ref[0, 0, 4:5, :]) + m_ref[0, 0, 3:4, :]
    gates, group_hot = _route(h2, wr_ref[...], br_ref[...])
    for s in range(SUBLANES):
        tiles_ref[pl.ds(s, TM, stride=SUBLANES), :] = h2[:, s * LANES:(s + 1) * LANES]
    padded = jnp.concatenate([gates, jnp.zeros((LANES - N_EXPERTS, TM), F32)], axis=0)
    gate_ref[...] = padded.T
    row = lax.broadcasted_iota(jnp.int32, (SUBLANES, TM), 0)
    hot = jnp.zeros((SUBLANES, TM), F32)
    for g in range(N_GROUPS):
        hot = jnp.where(row == g, group_hot[g], hot)
    before = _dot(hot.astype(BF16), tri_ref[...])
    rank = jnp.sum(hot * before, axis=0, keepdims=True)
    gid = group_hot[1] + 2.0 * group_hot[2] + 3.0 * group_hot[3]
    meta = jnp.where(row == 0, gid, jnp.where(row == 1, rank, 0.0))
    meta_ref[0] = meta.astype(jnp.int32)
    counts = jnp.sum(hot, axis=1, keepdims=True) + jnp.zeros((SUBLANES, LANES), F32)
    count_ref[0] = counts.astype(jnp.int32)


def moe_route(x, gain, mods, wr_t, b_r, tri, layer):
    return pl.pallas_call(
        functools.partial(_moe_route_kernel, layer=layer),
        grid=(N_TILES,),
        in_specs=[
            pl.BlockSpec((TM, D_MODEL), lambda i: (i, 0)),
            pl.BlockSpec((DEPTH, D_MODEL), lambda i: (0, 0)),
            pl.BlockSpec((1, 1, 6, D_MODEL), lambda i: (layer, _mod_row(i), 0, 0)),
            pl.BlockSpec((N_EXPERTS, D_MODEL), lambda i: (0, 0)),
            pl.BlockSpec((N_EXPERTS, 1), lambda i: (0, 0)),
            pl.BlockSpec((TM, TM), lambda i: (0, 0)),
        ],
        out_specs=[pl.BlockSpec((TM * SUBLANES, LANES), lambda i: (i, 0)),
                   pl.BlockSpec((TM, LANES), lambda i: (i, 0)),
                   pl.BlockSpec((1, SUBLANES, TM), lambda i: (i, 0, 0)),
                   pl.BlockSpec((1, SUBLANES, LANES), lambda i: (i, 0, 0))],
        out_shape=[jax.ShapeDtypeStruct((T_ALL * SUBLANES, LANES), F32),
                   jax.ShapeDtypeStruct((T_ALL, LANES), F32),
                   jax.ShapeDtypeStruct((N_TILES, SUBLANES, TM), jnp.int32),
                   jax.ShapeDtypeStruct((N_TILES, SUBLANES, LANES), jnp.int32)],
        compiler_params=_params(1),
        name="moe_route",
    )(x, gain, mods, wr_t, b_r, tri)


def _moe_positions(meta, counts):
    gid = meta[:, 0, :]
    rank = meta[:, 1, :]
    cnt = counts[:, :N_GROUPS, 0]
    total = jnp.sum(cnt, axis=0)
    padded = (total + (MOE_BLOCK - 1)) // MOE_BLOCK * MOE_BLOCK
    group_start = jnp.cumsum(padded) - padded
    base = group_start[None, :] + jnp.cumsum(cnt, axis=0) - cnt
    pos = rank
    for g in range(N_GROUPS):
        pos = pos + jnp.where(gid == g, base[:, g:g + 1], 0)
    first_block = group_start // MOE_BLOCK
    blk = jnp.arange(MOE_BLOCKS, dtype=jnp.int32)
    block_gid = sum((blk >= first_block[g]).astype(jnp.int32) for g in range(1, N_GROUPS))
    block_used = (blk * MOE_BLOCK < jnp.sum(padded)).astype(jnp.int32)
    later = jnp.where(block_gid[None, :] > block_gid[:, None], block_gid[None, :], N_GROUPS)
    next_group = jnp.min(later, axis=1)
    next_group = jnp.where(next_group == N_GROUPS, -1, next_group).astype(jnp.int32)
    return pos.reshape(T_ALL).astype(jnp.int32), block_gid, block_used, next_group


def _moe_permute_kernel(pos_ref, tiles_ref, gate_ref, sorted_ref, gsorted_ref):
    i = pl.program_id(0)

    @pl.when(i == 0)
    def _():
        def zero(b, carry):
            start = pl.multiple_of(b * TM, TM)
            sorted_ref[pl.ds(start, TM), :] = jnp.zeros((TM, LANES), F32)
            return carry
        lax.fori_loop(0, T_PAD * SUBLANES // TM, zero, 0)
        gsorted_ref[...] = jnp.zeros((T_PAD, LANES), F32)

    def move(t, carry):
        p = pos_ref[i * TM + t]
        dst = pl.multiple_of(p * SUBLANES, SUBLANES)
        src = pl.multiple_of(t * SUBLANES, SUBLANES)
        sorted_ref[pl.ds(dst, SUBLANES), :] = tiles_ref[pl.ds(src, SUBLANES), :]
        gsorted_ref[pl.ds(p, 1), :] = gate_ref[pl.ds(t, 1), :]
        return carry
    lax.fori_loop(0, TM, move, 0, unroll=8)


def moe_permute(pos, tiles, gates):
    return pl.pallas_call(
        _moe_permute_kernel,
        grid_spec=pltpu.PrefetchScalarGridSpec(
            num_scalar_prefetch=1,
            grid=(N_TILES,),
            in_specs=[pl.BlockSpec((TM * SUBLANES, LANES), lambda i, pos: (i, 0)),
                      pl.BlockSpec((TM, LANES), lambda i, pos: (i, 0))],
            out_specs=[pl.BlockSpec(memory_space=pltpu.VMEM), pl.BlockSpec(memory_space=pltpu.VMEM)],
        ),
        out_shape=[jax.ShapeDtypeStruct((T_PAD * SUBLANES, LANES), F32),
                   jax.ShapeDtypeStruct((T_PAD, LANES), F32)],
        compiler_params=_params(1),
        name="moe_permute",
    )(pos, tiles, gates)


def _group_changed(gid_ref, b):
    return (b == 0) | (gid_ref[b] != gid_ref[jnp.maximum(b - 1, 0)])


def _moe_expert_kernel(gid_ref, used_ref, next_ref, s_ref, gate_ref, w1_hbm, w3_hbm, w2_ref, y_ref,
                       w1f_ref, w3f_ref, w1b_ref, w3b_ref, w2b_ref, sem, *, layer):
    b = pl.program_id(0)

    def up_weight_copies(group):
        return (pltpu.make_async_copy(w1_hbm.at[layer, group], w1f_ref, sem.at[0]),
                pltpu.make_async_copy(w3_hbm.at[layer, group], w3f_ref, sem.at[1]))

    @pl.when(b == 0)
    def _():
        for copy in up_weight_copies(gid_ref[0]):
            copy.start()

    @pl.when(_group_changed(gid_ref, b))
    def _():
        for copy in up_weight_copies(gid_ref[b]):
            copy.wait()
        for j in range(EXP_PER_GROUP):
            cols = slice(j * D_FF, (j + 1) * D_FF)
            w1b_ref[:, cols] = w1f_ref[j].astype(BF16)
            w3b_ref[:, cols] = w3f_ref[j].astype(BF16)
        w2b_ref[...] = w2_ref[0, 0].astype(BF16)

        @pl.when(next_ref[b] >= 0)
        def _():
            for copy in up_weight_copies(next_ref[b]):
                copy.start()

    @pl.when(used_ref[b] == 1)
    def _():
        lhs = jnp.concatenate([s_ref[pl.ds(s, MOE_BLOCK, stride=SUBLANES), :].astype(BF16)
                               for s in range(SUBLANES)], axis=1)
        a = _dot(lhs, w1b_ref[...])
        g = _dot(lhs, w3b_ref[...])
        gates = gate_ref[...]
        lane = lax.broadcasted_iota(jnp.int32, (MOE_BLOCK, LANES), 1)
        first = gid_ref[b] * EXP_PER_GROUP
        parts = []
        for j in range(EXP_PER_GROUP):
            cols = slice(j * D_FF, (j + 1) * D_FF)
            gate = jnp.sum(jnp.where(lane == first + j, gates, 0.0), axis=1, keepdims=True)
            parts.append((_silu(a[:, cols]) * g[:, cols] * gate).astype(BF16))
        act = jnp.concatenate(parts, axis=1)
        y = _dot(act, w2b_ref[...])
        for s in range(SUBLANES):
            y_ref[pl.ds(s, MOE_BLOCK, stride=SUBLANES), :] = y[:, s * LANES:(s + 1) * LANES]

    @pl.when(used_ref[b] == 0)
    def _():
        y_ref[...] = jnp.zeros_like(y_ref)


def moe_experts(block_gid, block_used, next_group, sorted_rows, sorted_gates, w1, w3, w2, layer):
    group_ff = EXP_PER_GROUP * D_FF
    w1g = w1.reshape(DEPTH, N_GROUPS, EXP_PER_GROUP, D_MODEL, D_FF)
    w3g = w3.reshape(DEPTH, N_GROUPS, EXP_PER_GROUP, D_MODEL, D_FF)
    w2g = w2.reshape(DEPTH, N_GROUPS, group_ff, D_MODEL)
    block = lambda b, gid, used, nxt: (b, 0)
    return pl.pallas_call(
        functools.partial(_moe_expert_kernel, layer=layer),
        grid_spec=pltpu.PrefetchScalarGridSpec(
            num_scalar_prefetch=3,
            grid=(MOE_BLOCKS,),
            in_specs=[pl.BlockSpec((MOE_BLOCK * SUBLANES, LANES), block),
                      pl.BlockSpec((MOE_BLOCK, LANES), block),
                      pl.BlockSpec(memory_space=pl.ANY), pl.BlockSpec(memory_space=pl.ANY),
                      pl.BlockSpec((1, 1, group_ff, D_MODEL), lambda b, gid, used, nxt: (layer, gid[b], 0, 0))],
            out_specs=pl.BlockSpec((MOE_BLOCK * SUBLANES, LANES), block),
            scratch_shapes=[pltpu.VMEM((EXP_PER_GROUP, D_MODEL, D_FF), F32),
                            pltpu.VMEM((EXP_PER_GROUP, D_MODEL, D_FF), F32),
                            pltpu.VMEM((D_MODEL, group_ff), BF16), pltpu.VMEM((D_MODEL, group_ff), BF16),
                            pltpu.VMEM((group_ff, D_MODEL), BF16),
                            pltpu.SemaphoreType.DMA((2,))],
        ),
        out_shape=jax.ShapeDtypeStruct((T_PAD * SUBLANES, LANES), F32),
        compiler_params=_params(1),
        name="moe_experts",
    )(block_gid, block_used, next_group, sorted_rows, sorted_gates, w1g, w3g, w2g)


def _moe_combine_kernel(pos_ref, ys_ref, x_ref, m_ref, o_ref, buf_ref):
    i = pl.program_id(0)

    def move(t, carry):
        src = pl.multiple_of(pos_ref[i * TM + t] * SUBLANES, SUBLANES)
        dst = pl.multiple_of(t * SUBLANES, SUBLANES)
        buf_ref[pl.ds(dst, SUBLANES), :] = ys_ref[pl.ds(src, SUBLANES), :]
        return carry
    lax.fori_loop(0, TM, move, 0, unroll=8)

    for s in range(SUBLANES):
        cols = slice(s * LANES, (s + 1) * LANES)
        y = buf_ref[pl.ds(s, TM, stride=SUBLANES), :]
        o_ref[:, cols] = x_ref[:, cols] + m_ref[0, 0, 5:6, cols] * y


def moe_combine(pos, y_sorted, x, mods, layer):
    return pl.pallas_call(
        _moe_combine_kernel,
        grid_spec=pltpu.PrefetchScalarGridSpec(
            num_scalar_prefetch=1,
            grid=(N_TILES,),
            in_specs=[
                pl.BlockSpec(memory_space=pltpu.VMEM),
                pl.BlockSpec((TM, D_MODEL), lambda i, pos: (i, 0)),
                pl.BlockSpec((1, 1, 6, D_MODEL), lambda i, pos: (layer, _mod_row(i), 0, 0)),
            ],
            out_specs=pl.BlockSpec((TM, D_MODEL), lambda i, pos: (i, 0)),
            scratch_shapes=[pltpu.VMEM((TM * SUBLANES, LANES), F32)],
        ),
        out_shape=jax.ShapeDtypeStruct((T_ALL, D_MODEL), F32),
        compiler_params=_params(1),
        name="moe_combine",
    )(pos, y_sorted, x, mods)


def moe(x, gain, mods, wr_t, b_r, tri, w1, w3, w2, layer):
    tiles, gates, meta, counts = moe_route(x, gain, mods, wr_t, b_r, tri, layer)
    pos, block_gid, block_used, next_group = _moe_positions(meta, counts)
    sorted_rows, sorted_gates = moe_permute(pos, tiles, gates)
    y_sorted = moe_experts(block_gid, block_used, next_group, sorted_rows, sorted_gates, w1, w3, w2, layer)
    return moe_combine(pos, y_sorted, x, mods, layer)


def _final_norm_kernel(x_ref, g_ref, o_ref):
    o_ref[...] = _rms(x_ref[...]) * g_ref[...]


def final_norm(x, gain, tile0, ntiles):
    return pl.pallas_call(
        _final_norm_kernel,
        grid=(ntiles,),
        in_specs=[pl.BlockSpec((TM, D_MODEL), lambda i: (tile0 + i, 0)),
                  pl.BlockSpec((1, D_MODEL), lambda i: (0, 0))],
        out_specs=pl.BlockSpec((TM, D_MODEL), lambda i: (i, 0)),
        out_shape=jax.ShapeDtypeStruct((ntiles * TM, D_MODEL), F32),
        compiler_params=_params(1),
        name="final_norm",
    )(x, gain.reshape(1, D_MODEL))


def kernel(x_prompt, x_sample, cache_attn_k, cache_attn_v, state_retention, c, c_ctx, w_ada, b_ada, norm1_g, norm2_g, final_g, w_in, attn_lambda, attn_subln_g, hy_conv_w, hy_conv_b, hy_f_w1, hy_f_b1, hy_f_w2, hy_f_b2, hy_f_w3, hy_skip, ret_decay_exp, w_branch_a, w_branch_b, w_branch_c, w_gate, b_gate, w_out, w_router, b_router, moe_w1, moe_w3, moe_w2):
    x = jnp.concatenate([x_prompt.reshape(T_CTX, D_MODEL), x_sample.reshape(T_LAT, D_MODEL)], axis=0)
    cond8 = jnp.concatenate([c_ctx[None, :], c, jnp.zeros((8 - 1 - DEC_BATCH, D_MODEL), F32)], axis=0)
    mods = ada_modulation(cond8, w_ada, b_ada).reshape(DEPTH, 8, 6, D_MODEL)

    cos, sin = _rope_tables()
    cos_c, sin_c = cos[:, :H_C * DK_C], sin[:, :H_C * DK_C]
    cache_k = cache_attn_k.reshape(DEC_BATCH, DEPTH, PAST_LEN, W_A)
    cache_v = cache_attn_v.reshape(DEC_BATCH, DEPTH, PAST_LEN, W_A)
    decay_rates = _filter_decay_rates()
    tables_ctx = _dft_tables(SEQ)
    tables_lat = _dft_tables(DEC_SEQ)
    filt_ctx = hyena_filters(SEQ, _filter_features(SEQ), decay_rates, tables_ctx[0], tables_ctx[1],
                             hy_f_w1, hy_f_b1, hy_f_w2, hy_f_b2, hy_f_w3)
    filt_lat = hyena_filters(DEC_SEQ, _filter_features(DEC_SEQ), decay_rates, tables_lat[0], tables_lat[1],
                             hy_f_w1, hy_f_b1, hy_f_w2, hy_f_b2, hy_f_w3)
    wr_t = w_router.T
    b_r = b_router.reshape(N_EXPERTS, 1)
    tri = jnp.asarray(np.triu(np.ones((TM, TM), np.float32), 1), dtype=BF16)

    threaded = ()
    for l in range(DEPTH):
        p, kv = in_projection(x, norm1_g, mods, w_in, w_gate, b_gate, l)

        ya, new_k, new_v, yb, yc, states = mixers_ctx(
            p, kv, attn_lambda, attn_subln_g, hy_conv_w, hy_conv_b, hy_skip, filt_ctx[0], filt_ctx[1],
            tables_ctx, ret_decay_exp, l, threaded)
        threaded = (new_k, new_v, states)
        ya = attention_lat(ya, p, kv, cache_k, cache_v, cos, sin, attn_lambda, attn_subln_g, l)
        yb = hyena_lat(yb, p, hy_conv_w, hy_conv_b, hy_skip, filt_lat[0], filt_lat[1], tables_lat, l)
        yc = retention_lat(yc, p, ret_decay_exp, cos_c, sin_c, state_retention, l)

        x = merge_branches(x, ya, yb, yc, p, mods, w_branch_a, w_branch_b, w_branch_c, w_out, l)
        x = moe(x, norm2_g, mods, wr_t, b_r, tri, moe_w1, moe_w3, moe_w2, l)

    y_prompt = final_norm(x, final_g, 0, CTX_TILES).reshape(BATCH, SEQ, D_MODEL)
    y_sample = final_norm(x, final_g, CTX_TILES, N_TILES - CTX_TILES).reshape(DEC_BATCH, DEC_SEQ, D_MODEL)
    return (y_prompt, y_sample, new_k, new_v, states)
```

```python
import functools
import math

import jax
import jax.numpy as jnp
import numpy as np
from jax import lax
from jax.experimental import pallas as pl
from jax.experimental.pallas import tpu as pltpu

F32 = jnp.float32
BF16 = jnp.bfloat16

D_MODEL = 1024
BATCH = 16
SEQ = 256
DEPTH = 4
DEC_BATCH = 2
DEC_SEQ = 1024
PAST_LEN = 256
GRID_W = 64
EPS = 1e-6
ROPE_BASE = 10000.0
H_A = 4
DH_A = 64
DV_A = 128
W_A = 512
C_B = 512
SHORT_K = 3
H_EMB = 33
H_FFN = 64
H_FAST_DECAY = 0.3
H_SLOW_DECAY = 1.5
H_TARGET = 1e-2
H_C = 4
DK_C = 64
DV_C = 128
W_C = 512
N_EXPERTS = 16
N_GROUPS = 4
EXP_PER_GROUP = 4
D_FF = 512
D_IN = 4608

T_CTX = BATCH * SEQ
T_LAT = DEC_BATCH * DEC_SEQ
T_ALL = T_CTX + T_LAT
TM = 1024
N_TILES = T_ALL // TM
CTX_TILES = T_CTX // TM
LANES = 128
SUBLANES = 8
MOE_BLOCK = 256
MOE_BLOCKS = T_ALL // MOE_BLOCK + N_GROUPS
T_PAD = MOE_BLOCKS * MOE_BLOCK
VMEM_LIMIT = 56 * 1024 * 1024


def _params(n_axes):
    return pltpu.CompilerParams(
        dimension_semantics=("arbitrary",) * n_axes, vmem_limit_bytes=VMEM_LIMIT)


def _mod_row(i):
    return jnp.maximum(i - (CTX_TILES - 1), 0)


def _rms(x):
    return x * lax.rsqrt(jnp.mean(x * x, axis=-1, keepdims=True) + EPS)


def _sigmoid(x):
    return 0.5 * jnp.tanh(0.5 * x) + 0.5


def _silu(x):
    return x * _sigmoid(x)


def _dot(a, b):
    return jnp.dot(a, b, preferred_element_type=F32)


def _dot_nt(a, b):
    return lax.dot_general(a, b, (((1,), (1,)), ((), ())), preferred_element_type=F32)


def _dot_tn(a, b):
    return lax.dot_general(a, b, (((0,), (0,)), ((), ())), preferred_element_type=F32)


def _split3(x):
    hi = x.astype(BF16)
    lo = (x - hi.astype(F32)).astype(BF16)
    return hi, lo


def _dot3(a, b):
    ah, al = _split3(a)
    bh, bl = _split3(b)
    return _dot(ah, bh) + (_dot(ah, bl) + _dot(al, bh))


def _dot3_nt(a, b):
    ah, al = _split3(a)
    bh, bl = _split3(b)
    return _dot_nt(ah, bh) + (_dot_nt(ah, bl) + _dot_nt(al, bh))


def _ada_kernel(c_ref, w_ref, b_ref, o_ref):
    s = _silu(c_ref[...])
    o_ref[0] = _dot(s.astype(BF16), w_ref[0].astype(BF16)) + b_ref[0]


def ada_modulation(cond8, w_ada, b_ada):
    tn = 1536
    n = 6 * D_MODEL
    return pl.pallas_call(
        _ada_kernel,
        grid=(DEPTH, n // tn),
        in_specs=[
            pl.BlockSpec((8, D_MODEL), lambda l, j: (0, 0)),
            pl.BlockSpec((1, D_MODEL, tn), lambda l, j: (l, 0, j)),
            pl.BlockSpec((1, 1, tn), lambda l, j: (l, 0, j)),
        ],
        out_specs=pl.BlockSpec((1, 8, tn), lambda l, j: (l, 0, j)),
        out_shape=jax.ShapeDtypeStruct((DEPTH, 8, n), F32),
        compiler_params=_params(2),
        name="ada_modulation",
    )(cond8, w_ada, b_ada.reshape(DEPTH, 1, n))


PROJ_TM = 3072
PROJ_TN = 512
GATE_BLOCKS = 3 * D_MODEL // PROJ_TN
IN_BLOCKS = D_IN // PROJ_TN
P_WIDTH = 3 * D_MODEL + D_IN
KV_FIRST = GATE_BLOCKS + 1
P_Q = 3 * D_MODEL // W_A
P_HYENA = (3 * D_MODEL + 3 * W_A) // (3 * C_B)
P_RET_Q = (3 * D_MODEL + 3 * W_A + 3 * C_B) // (H_C * DK_C)
P_RET_V = (3 * D_MODEL + 3 * W_A + 3 * C_B + 2 * H_C * DK_C) // W_C


def _in_proj_kernel(x_ref, g_ref, m_ref, win_ref, wg_ref, bg_ref, p_ref, kv_ref, h_ref, *, layer):
    i = pl.program_id(0)
    j = pl.program_id(1)

    @pl.when(j == 0)
    def _():
        for s in range(PROJ_TM // TM):
            rows = slice(s * TM, (s + 1) * TM)
            mod = m_ref[layer, _mod_row(i * (PROJ_TM // TM) + s)]
            y = _rms(x_ref[rows, :]) * g_ref[layer:layer + 1, :]
            h_ref[rows, :] = (y * (1.0 + mod[1:2, :]) + mod[0:1, :]).astype(BF16)

    @pl.when(j < GATE_BLOCKS)
    def _():
        acc = _dot(h_ref[...], wg_ref[0].astype(BF16)) + bg_ref[0]
        p_ref[...] = _sigmoid(acc).astype(BF16)

    @pl.when(j >= GATE_BLOCKS)
    def _():
        acc = _dot(h_ref[...], win_ref[0].astype(BF16))
        p_ref[...] = acc.astype(BF16)

        @pl.when((j == KV_FIRST) | (j == KV_FIRST + 1))
        def _():
            kv_ref[...] = acc


def in_projection(x, gain, mods, w_in, w_gate, b_gate, layer):
    return pl.pallas_call(
        functools.partial(_in_proj_kernel, layer=layer),
        grid=(T_ALL // PROJ_TM, GATE_BLOCKS + IN_BLOCKS),
        in_specs=[
            pl.BlockSpec((PROJ_TM, D_MODEL), lambda i, j: (i, 0), pipeline_mode=pl.Buffered(1)),
            pl.BlockSpec((DEPTH, D_MODEL), lambda i, j: (0, 0)),
            pl.BlockSpec((DEPTH, 8, 6, D_MODEL), lambda i, j: (0, 0, 0, 0)),
            pl.BlockSpec((1, D_MODEL, PROJ_TN), lambda i, j: (layer, 0, jnp.maximum(j - GATE_BLOCKS, 0))),
            pl.BlockSpec((1, D_MODEL, PROJ_TN), lambda i, j: (layer, 0, jnp.minimum(j, GATE_BLOCKS - 1))),
            pl.BlockSpec((1, 1, PROJ_TN), lambda i, j: (layer, 0, jnp.minimum(j, GATE_BLOCKS - 1))),
        ],
        out_specs=[
            pl.BlockSpec((PROJ_TM, PROJ_TN), lambda i, j: (i, j)),
            pl.BlockSpec((PROJ_TM, PROJ_TN), lambda i, j: (i, jnp.clip(j - KV_FIRST, 0, 1))),
        ],
        out_shape=[jax.ShapeDtypeStruct((T_ALL, P_WIDTH), BF16),
                   jax.ShapeDtypeStruct((T_ALL, 2 * W_A), F32)],
        scratch_shapes=[pltpu.VMEM((PROJ_TM, D_MODEL), BF16)],
        compiler_params=_params(2),
        name="in_projection",
    )(x, gain, mods, w_in, w_gate, b_gate.reshape(DEPTH, 1, 3 * D_MODEL))


def _rope_tables():
    n = DEC_SEQ
    rows = n // GRID_W
    row = jnp.repeat(jnp.arange(rows, dtype=F32), GRID_W)
    col = jnp.tile(jnp.arange(GRID_W, dtype=F32), rows)
    quarter = DH_A // 4
    inv = ROPE_BASE ** (-jnp.arange(quarter, dtype=F32) / quarter)
    ar = row[:, None] * inv
    ac = col[:, None] * inv
    ang = jnp.concatenate([ar, ar, ac, ac], axis=-1)
    cos = jnp.tile(jnp.cos(ang), (1, 512 // DH_A))
    sgn = jnp.tile(jnp.concatenate([-jnp.ones((quarter,), F32), jnp.ones((quarter,), F32)]), 512 // (2 * quarter))
    sin = jnp.tile(jnp.sin(ang), (1, 512 // DH_A)) * sgn
    return cos, sin


def _rope(x, cos, sin):
    w = x.shape[-1]
    lane = lax.broadcasted_iota(jnp.int32, x.shape, 1)
    first = (lane % 32) < 16
    rot = jnp.where(first, pltpu.roll(x, w - 16, 1), pltpu.roll(x, 16, 1))
    return x * cos + rot * sin


def _lambda(al_ref, layer):
    al = al_ref[layer]
    a = jnp.sum(al[0:1, :] * al[1:2, :], axis=-1, keepdims=True)
    b = jnp.sum(al[2:3, :] * al[3:4, :], axis=-1, keepdims=True)
    lam_init = 0.8 - 0.6 * math.exp(-0.3 * layer)
    return jnp.exp(a) - jnp.exp(b) + lam_init, lam_init


def _scores(q, k):
    return _dot_nt(q, k).astype(BF16)


def _softmax_numerators(s):
    return jnp.exp2(s - jnp.max(s, axis=-1, keepdims=True))


def _diff_attention(q, k, v, lam, lam_init, subln, o_ref):
    lane = lax.broadcasted_iota(jnp.int32, q.shape, 1)
    q = q * (DH_A ** -0.5 * math.log2(math.e))
    q0 = jnp.where((lane % LANES) < DH_A, q, 0.0).astype(BF16)
    q1 = jnp.where((lane % LANES) >= DH_A, q, 0.0).astype(BF16)
    ones = jnp.ones((k.shape[0], LANES), BF16)
    for h in range(H_A):
        sl = slice(h * LANES, (h + 1) * LANES)
        kh = k[:, sl]
        v_ones = jnp.concatenate([v[:, sl], ones], axis=1)
        o0 = _dot(_softmax_numerators(_scores(q0[:, sl], kh)), v_ones)
        o1 = _dot(_softmax_numerators(_scores(q1[:, sl], kh)), v_ones)
        r0 = 1.0 / o0[:, DV_A:DV_A + 1]
        r1 = lam / o1[:, DV_A:DV_A + 1]
        o = o0[:, :DV_A] * r0 - o1[:, :DV_A] * r1
        o = _rms(o) * subln * (1.0 - lam_init)
        o_ref[:, sl] = o.astype(o_ref.dtype)


def _attn_ctx_kernel(q_ref, k_ref, v_ref, al_ref, g_ref, o_ref, knew_ref, vnew_ref, *, layer):
    lam, lam_init = _lambda(al_ref, layer)
    subln = g_ref[layer:layer + 1, :]
    k = k_ref[...]
    v = v_ref[...]
    for h in range(H_A):
        knew_ref[0, 0, :, h, :] = k[:, h * LANES:(h + 1) * LANES]
        vnew_ref[0, 0, :, h, :] = v[:, h * LANES:(h + 1) * LANES]
    _diff_attention(q_ref[...].astype(F32), k.astype(BF16), v.astype(BF16), lam, lam_init, subln, o_ref)


QB = 256


def _attn_lat_kernel(ya_ref, q_ref, k_ref, v_ref, ck_ref, cv_ref, cosq_ref, sinq_ref, cos_ref, sin_ref,
                     al_ref, g_ref, o_ref, kall_ref, vall_ref, *, layer):
    del ya_ref

    @pl.when(pl.program_id(1) == 0)
    def _():
        kall_ref[0:PAST_LEN, :] = ck_ref[0, 0].astype(BF16)
        vall_ref[0:PAST_LEN, :] = cv_ref[0, 0].astype(BF16)
        kall_ref[PAST_LEN:, :] = _rope(k_ref[...], cos_ref[...], sin_ref[...]).astype(BF16)
        vall_ref[PAST_LEN:, :] = v_ref[...].astype(BF16)

    lam, lam_init = _lambda(al_ref, layer)
    subln = g_ref[layer:layer + 1, :]
    q = _rope(q_ref[...].astype(F32), cosq_ref[...], sinq_ref[...])
    _diff_attention(q, kall_ref[...], vall_ref[...], lam, lam_init, subln, o_ref)


def attention_lat(ya, q, kv, cache_k, cache_v, cos, sin, attn_lambda, subln_g, layer):
    nqb = DEC_SEQ // QB
    row0 = T_CTX // QB
    seq0 = T_CTX // DEC_SEQ
    return pl.pallas_call(
        functools.partial(_attn_lat_kernel, layer=layer),
        grid=(DEC_BATCH, nqb),
        in_specs=[
            pl.BlockSpec(memory_space=pl.ANY),
            pl.BlockSpec((QB, W_A), lambda b, i: (row0 + b * nqb + i, P_Q)),
            pl.BlockSpec((DEC_SEQ, W_A), lambda b, i: (seq0 + b, 0)),
            pl.BlockSpec((DEC_SEQ, W_A), lambda b, i: (seq0 + b, 1)),
            pl.BlockSpec((1, 1, PAST_LEN, W_A), lambda b, i: (b, layer, 0, 0)),
            pl.BlockSpec((1, 1, PAST_LEN, W_A), lambda b, i: (b, layer, 0, 0)),
            pl.BlockSpec((QB, W_A), lambda b, i: (i, 0)),
            pl.BlockSpec((QB, W_A), lambda b, i: (i, 0)),
            pl.BlockSpec((DEC_SEQ, W_A), lambda b, i: (0, 0)),
            pl.BlockSpec((DEC_SEQ, W_A), lambda b, i: (0, 0)),
            pl.BlockSpec((DEPTH, 4, DH_A), lambda b, i: (0, 0, 0)),
            pl.BlockSpec((DEPTH, DV_A), lambda b, i: (0, 0)),
        ],
        out_specs=pl.BlockSpec((QB, W_A), lambda b, i: (row0 + b * nqb + i, 0)),
        out_shape=jax.ShapeDtypeStruct((T_ALL, W_A), BF16),
        scratch_shapes=[pltpu.VMEM((PAST_LEN + DEC_SEQ, W_A), BF16),
                        pltpu.VMEM((PAST_LEN + DEC_SEQ, W_A), BF16)],
        input_output_aliases={0: 0},
        compiler_params=_params(2),
        name="attention_lat",
    )(ya, q, kv, kv, cache_k, cache_v, cos, sin, cos, sin, attn_lambda, subln_g)


def _dft_tables(n):
    k = np.arange(n, dtype=np.int64)
    prod = (2 * k[:, None] + 1) * k[None, :]
    ang = (prod % (4 * n)).astype(np.float64) * (math.pi / (2 * n))
    c = np.cos(ang).astype(np.float32)
    s = np.sin(ang).astype(np.float32)
    return tuple(jnp.asarray(t).astype(BF16) for t in (c, s, c.T, s.T))


def _filter_features(n):
    t = jnp.linspace(0.0, 1.0, n, dtype=F32)[:, None]
    bands = (H_EMB - 1) // 2
    w = 2.0 * math.pi * jnp.arange(n, dtype=F32)[:, None] / n
    fr = jnp.linspace(1e-4, bands - 1, bands, dtype=F32)
    z = jnp.concatenate([t, jnp.cos(w * fr), -jnp.sin(w * fr)], axis=-1)
    return jnp.pad(z, ((0, 0), (0, LANES - H_EMB)))


def _filter_decay_rates():
    max_decay = math.log(H_TARGET) / H_FAST_DECAY
    min_decay = math.log(H_TARGET) / H_SLOW_DECAY
    return jnp.abs(jnp.linspace(min_decay, max_decay, C_B, dtype=F32))[None, :]


def _hyena_filter_kernel(z_ref, w1_ref, b1_ref, w2_ref, b2_ref, w3_ref, dr_ref, c_ref, s_ref,
                         gre_ref, gim_ref):
    z = z_ref[...]
    hid = jnp.sin(_dot3(z, w1_ref[0]) + b1_ref[0])
    hid = jnp.sin(_dot3(hid, w2_ref[0]) + b2_ref[0])
    h = _dot(hid.astype(BF16), w3_ref[0].astype(BF16))
    window = jnp.exp(-z[:, 0:1] * dr_ref[...])
    hf = h[:, :C_B] * window
    hb = h[:, C_B:] * window
    total = jnp.sum(jnp.abs(hf) + jnp.abs(hb), axis=0, keepdims=True)
    hf = hf / total
    hb = hb / total
    row = lax.broadcasted_iota(jnp.int32, hb.shape, 0)
    hb = jnp.where(row == 0, 0.0, hb)
    gre_ref[0] = _dot(c_ref[...], (hf + hb).astype(BF16))
    gim_ref[0] = _dot(s_ref[...], (hb - hf).astype(BF16))


def hyena_filters(n, z, decay_rates, cf, sf, f_w1, f_b1, f_w2, f_b2, f_w3):
    w1 = jnp.pad(f_w1, ((0, 0), (0, LANES - H_EMB), (0, 0)))
    full = lambda shape: pl.BlockSpec(shape, lambda l: (0,) * len(shape))
    per_layer = lambda shape: pl.BlockSpec((1,) + shape, lambda l: (l,) + (0,) * len(shape))
    return pl.pallas_call(
        _hyena_filter_kernel,
        grid=(DEPTH,),
        in_specs=[
            full((n, LANES)),
            per_layer((LANES, H_FFN)), per_layer((1, H_FFN)),
            per_layer((H_FFN, H_FFN)), per_layer((1, H_FFN)),
            per_layer((H_FFN, 2 * C_B)),
            full((1, C_B)), full((n, n)), full((n, n)),
        ],
        out_specs=[per_layer((n, C_B)), per_layer((n, C_B))],
        out_shape=[jax.ShapeDtypeStruct((DEPTH, n, C_B), F32)] * 2,
        compiler_params=_params(1),
        name="hyena_filters",
    )(z, w1, f_b1.reshape(DEPTH, 1, H_FFN), f_w2, f_b2.reshape(DEPTH, 1, H_FFN), f_w3,
      decay_rates, cf, sf)


def _hyena_kernel(*refs, layer, n, aliased):
    if aliased:
        refs = refs[1:]
    u_ref, cw_ref, cb_ref, skip_ref, gre_ref, gim_ref, c_ref, s_ref, ct_ref, st_ref, o_ref = refs
    u = u_ref[...].astype(F32)
    row = lax.broadcasted_iota(jnp.int32, u.shape, 0)
    prev = jnp.where(row == 0, 0.0, pltpu.roll(u, 1, 0))
    nxt = jnp.where(row == n - 1, 0.0, pltpu.roll(u, n - 1, 0))
    cw = cw_ref[layer]
    uc = cb_ref[layer:layer + 1, :] + prev * cw[0:1, :] + u * cw[1:2, :] + nxt * cw[2:3, :]
    x0 = uc[:, :C_B]
    x1 = uc[:, C_B:2 * C_B]
    v = uc[:, 2 * C_B:]
    w = v * x1
    wb = w.astype(BF16)
    ure = _dot(c_ref[...], wb)
    uim = -_dot(s_ref[...], wb)
    gre = gre_ref[0]
    gim = gim_ref[0]
    yre = (ure * gre - uim * gim).astype(BF16)
    yim = (ure * gim + uim * gre).astype(BF16)
    y = (_dot(ct_ref[...], yre) - _dot(st_ref[...], yim)) * (1.0 / n)
    y = y + w * skip_ref[layer:layer + 1, :]
    o_ref[...] = (x0 * y).astype(o_ref.dtype)


def _full(shape):
    return pl.BlockSpec(shape, lambda *_: (0,) * len(shape))


def _hyena_specs(n, seq0, layer):
    return [
        pl.BlockSpec((n, 3 * C_B), lambda b, *_: (seq0 + b, P_HYENA)),
        _full((DEPTH, SHORT_K, 3 * C_B)), _full((DEPTH, 3 * C_B)), _full((DEPTH, C_B)),
        pl.BlockSpec((1, n, C_B), lambda *_: (layer, 0, 0)),
        pl.BlockSpec((1, n, C_B), lambda *_: (layer, 0, 0)),
        _full((n, n)), _full((n, n)), _full((n, n)), _full((n, n)),
    ]


def hyena_lat(yb, p, conv_w, conv_b, skip, gre, gim, tables, layer):
    seq0 = T_CTX // DEC_SEQ
    return pl.pallas_call(
        functools.partial(_hyena_kernel, layer=layer, n=DEC_SEQ, aliased=True),
        grid=(DEC_BATCH,),
        in_specs=[pl.BlockSpec(memory_space=pl.ANY)] + _hyena_specs(DEC_SEQ, seq0, layer),
        out_specs=pl.BlockSpec((DEC_SEQ, C_B), lambda b: (seq0 + b, 0)),
        out_shape=jax.ShapeDtypeStruct((T_ALL, C_B), BF16),
        input_output_aliases={0: 0},
        compiler_params=_params(1),
        name="hyena_lat",
    )(yb, p, conv_w, conv_b, skip, gre, gim, *tables)


RET_QB = 256


def _retention_kernel(*refs, layer, n, latent):
    if latent:
        _, q_ref, k_ref, v_ref, g_ref, de_ref, cos_ref, sin_ref, s0_ref, o_ref, w_ref, vec_ref = refs
        st_ref = None
    else:
        q_ref, k_ref, v_ref, g_ref, de_ref, o_ref, st_ref, w_ref, vec_ref = refs[-9:]
    log_gamma = jnp.log1p(-jnp.exp2(-de_ref[layer]))

    @pl.when(pl.program_id(0) == 0)
    def _():
        t = lax.broadcasted_iota(jnp.int32, (n, n), 0)
        s = lax.broadcasted_iota(jnp.int32, (n, n), 1)
        lag = (t - s).astype(F32)
        for h in range(H_C):
            rate = jnp.where(lag > 0, log_gamma[0:1, h:h + 1], log_gamma[1:2, h:h + 1])
            w_ref[h] = jnp.where(lag == 0, 2.0, jnp.exp(rate * jnp.abs(lag)))
            pos = lax.broadcasted_iota(jnp.int32, (n, LANES), 0).astype(F32)
            steps = (pos + 1.0, float(n) - pos) if latent else (float(n - 1) - pos, pos)
            for d in range(2):
                vec_ref[h, d] = jnp.exp(log_gamma[d:d + 1, h:h + 1] * steps[d])

    q = q_ref[...].astype(F32)
    k = k_ref[...].astype(F32) * (DK_C ** -0.5)
    if latent:
        q = _rope(q, cos_ref[...], sin_ref[...])
        k = _rope(k, cos_ref[...], sin_ref[...])
    v = v_ref[...].astype(BF16)
    gate = g_ref[...].astype(F32)
    lane = lax.broadcasted_iota(jnp.int32, (n, LANES), 1)
    zeros64 = jnp.zeros((DK_C, DV_C), F32)

    for h in range(H_C):
        pair = slice((h // 2) * LANES, (h // 2 + 1) * LANES)
        lo = (h % 2) * DK_C
        own = (lane >= lo) & (lane < lo + DK_C)
        vs = slice(h * DV_C, (h + 1) * DV_C)
        qh = jnp.where(own, q[:, pair], 0.0).astype(BF16)
        kh = jnp.where(own, k[:, pair], 0.0)
        kb = kh.astype(BF16)
        vh = v[:, vs]
        if latent:
            s0 = [jnp.concatenate([s0_ref[0, 0, d, h], zeros64] if lo == 0 else [zeros64, s0_ref[0, 0, d, h]],
                                  axis=0).astype(BF16) for d in range(2)]
            carry = _dot(qh, s0[0]) * vec_ref[h, 0] + _dot(qh, s0[1]) * vec_ref[h, 1]
        for r0 in range(0, n, RET_QB):
            rows = slice(r0, r0 + RET_QB)
            att = _dot_nt(qh[rows], kb) * w_ref[h, rows, :]
            out = _dot(att.astype(BF16), vh)
            if latent:
                out = out + carry[rows]
            o_ref[rows, vs] = (_rms(out) * _silu(gate[rows, vs])).astype(o_ref.dtype)
        if st_ref is not None:
            sf = _dot_tn((kh * vec_ref[h, 0]).astype(BF16), vh)
            sb = _dot_tn((kh * vec_ref[h, 1]).astype(BF16), vh)
            st_ref[0, 0, 0, h] = sf[lo:lo + DK_C, :]
            st_ref[0, 0, 1, h] = sb[lo:lo + DK_C, :]


def _retention_specs(n, seq0):
    return [
        pl.BlockSpec((n, H_C * DK_C), lambda b, *_: (seq0 + b, P_RET_Q)),
        pl.BlockSpec((n, H_C * DK_C), lambda b, *_: (seq0 + b, P_RET_Q + 1)),
        pl.BlockSpec((n, W_C), lambda b, *_: (seq0 + b, P_RET_V)),
        pl.BlockSpec((n, W_C), lambda b, *_: (seq0 + b, P_RET_V + 1)),
        _full((DEPTH, 2, H_C)),
    ]


def retention_lat(yc, p, decay_exp, cos, sin, state0, layer):
    n = DEC_SEQ
    seq0 = T_CTX // DEC_SEQ
    return pl.pallas_call(
        functools.partial(_retention_kernel, layer=layer, n=n, latent=True),
        grid=(DEC_BATCH,),
        in_specs=[pl.BlockSpec(memory_space=pl.ANY)] + _retention_specs(n, seq0) + [
            _full((n, H_C * DK_C)), _full((n, H_C * DK_C)),
            pl.BlockSpec((1, 1, 2, H_C, DK_C, DV_C), lambda b: (b, layer, 0, 0, 0, 0)),
        ],
        out_specs=pl.BlockSpec((n, W_C), lambda b: (seq0 + b, 0)),
        out_shape=jax.ShapeDtypeStruct((T_ALL, W_C), BF16),
        scratch_shapes=[pltpu.VMEM((H_C, n, n), F32), pltpu.VMEM((H_C, 2, n, LANES), F32)],
        input_output_aliases={0: 0},
        compiler_params=_params(1),
        name="retention_lat",
    )(yc, p, p, p, p, decay_exp, cos, sin, state0)


N_ATTN_IN = 5
N_HYENA_IN = 10
N_RET_IN = 5


def _mixers_ctx_kernel(*refs, layer, n_threaded):
    refs = refs[n_threaded:]
    attn_in = refs[:N_ATTN_IN]
    hyena_in = refs[N_ATTN_IN:N_ATTN_IN + N_HYENA_IN]
    ret_in = refs[N_ATTN_IN + N_HYENA_IN:N_ATTN_IN + N_HYENA_IN + N_RET_IN]
    ya_ref, knew_ref, vnew_ref, yb_ref, yc_ref, st_ref, w_ref, vec_ref = refs[N_ATTN_IN + N_HYENA_IN + N_RET_IN:]
    _attn_ctx_kernel(*attn_in, ya_ref, knew_ref, vnew_ref, layer=layer)
    _hyena_kernel(*hyena_in, yb_ref, layer=layer, n=SEQ, aliased=False)
    _retention_kernel(*ret_in, yc_ref, st_ref, w_ref, vec_ref, layer=layer, n=SEQ, latent=False)


def mixers_ctx(p, kv, attn_lambda, subln_g, conv_w, conv_b, skip, gre, gim, tables, decay_exp, layer, threaded):
    cache_shape = jax.ShapeDtypeStruct((BATCH, DEPTH, SEQ, H_A, DV_A), F32)
    cache_spec = pl.BlockSpec((1, 1, SEQ, H_A, DV_A), lambda b: (b, layer, 0, 0, 0))
    branch_shape = jax.ShapeDtypeStruct((T_ALL, W_A), BF16)
    branch_spec = pl.BlockSpec((SEQ, W_A), lambda b: (b, 0))
    n_threaded = len(threaded)
    attn_specs = [
        pl.BlockSpec((SEQ, W_A), lambda b: (b, P_Q)),
        pl.BlockSpec((SEQ, W_A), lambda b: (b, 0)),
        pl.BlockSpec((SEQ, W_A), lambda b: (b, 1)),
        _full((DEPTH, 4, DH_A)), _full((DEPTH, DV_A)),
    ]
    return pl.pallas_call(
        functools.partial(_mixers_ctx_kernel, layer=layer, n_threaded=n_threaded),
        grid=(BATCH,),
        in_specs=([pl.BlockSpec(memory_space=pl.ANY)] * n_threaded + attn_specs
                  + _hyena_specs(SEQ, 0, layer) + _retention_specs(SEQ, 0)),
        out_specs=[branch_spec, cache_spec, cache_spec, branch_spec, branch_spec,
                   pl.BlockSpec((1, 1, 2, H_C, DK_C, DV_C), lambda b: (b, layer, 0, 0, 0, 0))],
        out_shape=[branch_shape, cache_shape, cache_shape, branch_shape, branch_shape,
                   jax.ShapeDtypeStruct((BATCH, DEPTH, 2, H_C, DK_C, DV_C), F32)],
        scratch_shapes=[pltpu.VMEM((H_C, SEQ, SEQ), F32), pltpu.VMEM((H_C, 2, SEQ, LANES), F32)],
        input_output_aliases=dict(zip(range(n_threaded), (1, 2, 5))),
        compiler_params=_params(1),
        name="mixers_ctx",
    )(*threaded, p, kv, kv, attn_lambda, subln_g, p, conv_w, conv_b, skip, gre, gim, *tables,
      p, p, p, p, decay_exp)


def _merge_kernel(x_ref, ya_ref, yb_ref, yc_ref, g_ref, m_ref, wa_ref, wb_ref, wc_ref, wo_ref,
                  o_ref, wbf_ref, wobf_ref):
    @pl.when(pl.program_id(0) == 0)
    def _():
        wbf_ref[0] = wa_ref[0].astype(BF16)
        wbf_ref[1] = wb_ref[0].astype(BF16)
        wbf_ref[2] = wc_ref[0].astype(BF16)
        wobf_ref[...] = wo_ref[0].astype(BF16)

    g = g_ref[...]
    merged = (g[:, :D_MODEL] * _dot(ya_ref[...], wbf_ref[0])
              + g[:, D_MODEL:2 * D_MODEL] * _dot(yb_ref[...], wbf_ref[1])
              + g[:, 2 * D_MODEL:] * _dot(yc_ref[...], wbf_ref[2]))
    g1 = m_ref[0, 0, 2:3, :]
    o_ref[...] = x_ref[...] + g1 * _dot(merged.astype(BF16), wobf_ref[...])


def merge_branches(x, ya, yb, yc, gates, mods, w_a, w_b, w_c, w_out, layer):
    tile = lambda w: pl.BlockSpec((TM, w), lambda i: (i, 0))
    wspec = lambda k: pl.BlockSpec((1, k, D_MODEL), lambda i: (layer, 0, 0))
    return pl.pallas_call(
        _merge_kernel,
        grid=(N_TILES,),
        in_specs=[
            tile(D_MODEL), tile(W_A), tile(C_B), tile(W_C), tile(3 * D_MODEL),
            pl.BlockSpec((1, 1, 6, D_MODEL), lambda i: (layer, _mod_row(i), 0, 0)),
            wspec(W_A), wspec(C_B), wspec(W_C), wspec(D_MODEL),
        ],
        out_specs=tile(D_MODEL),
        out_shape=jax.ShapeDtypeStruct((T_ALL, D_MODEL), F32),
        scratch_shapes=[pltpu.VMEM((3, W_A, D_MODEL), BF16), pltpu.VMEM((D_MODEL, D_MODEL), BF16)],
        compiler_params=_params(1),
        name="merge_branches",
    )(x, ya, yb, yc, gates, mods, w_a, w_b, w_c, w_out)


def _route(h2, wr_t, b_r):
    logits = _dot3_nt(wr_t, h2)
    m = jnp.max(logits, axis=0, keepdims=True)
    e = jnp.exp(logits - m)
    scores = e / jnp.sum(e, axis=0, keepdims=True)
    sel = scores + b_r
    rows = [sel[i:i + 1, :] for i in range(N_EXPERTS)]
    in_group = []
    gscore = []
    for g in range(N_GROUPS):
        members = range(g * EXP_PER_GROUP, (g + 1) * EXP_PER_GROUP)
        total = None
        for i in members:
            rank = None
            for j in members:
                if j == i:
                    continue
                ahead = (rows[j] >= rows[i]) if j < i else (rows[j] > rows[i])
                ahead = ahead.astype(F32)
                rank = ahead if rank is None else rank + ahead
            chosen = rank < 2.0
            in_group.append(chosen)
            part = jnp.where(chosen, rows[i], 0.0)
            total = part if total is None else total + part
        gscore.append(total)
    gates = []
    group_hot = []
    for g in range(N_GROUPS):
        best = None
        for g2 in range(N_GROUPS):
            if g2 == g:
                continue
            wins = gscore[g] > gscore[g2] if g2 < g else gscore[g] >= gscore[g2]
            best = wins if best is None else best & wins
        group_hot.append(best.astype(F32))
        for i in range(g * EXP_PER_GROUP, (g + 1) * EXP_PER_GROUP):
            gates.append(jnp.where(best & in_group[i], scores[i:i + 1, :], 0.0))
    gates = jnp.concatenate(gates, axis=0)
    return gates / jnp.sum(gates, axis=0, keepdims=True), group_hot


def _moe_route_kernel(x_ref, g_ref, m_ref, wr_ref, br_ref, tri_ref, tiles_ref, gate_ref, meta_ref, count_ref,
                      *, layer):
    y = _rms(x_ref[...]) * g_ref[layer:layer + 1, :]
    h2 = y * (1.0 + m_ref[0, 0, 4:5, :]) + m_ref[0, 0, 3:4, :]
    gates, group_hot = _route(h2, wr_ref[...], br_ref[...])
    for s in range(SUBLANES):
        tiles_ref[pl.ds(s, TM, stride=SUBLANES), :] = h2[:, s * LANES:(s + 1) * LANES]
    padded = jnp.concatenate([gates, jnp.zeros((LANES - N_EXPERTS, TM), F32)], axis=0)
    gate_ref[...] = padded.T
    row = lax.broadcasted_iota(jnp.int32, (SUBLANES, TM), 0)
    hot = jnp.zeros((SUBLANES, TM), F32)
    for g in range(N_GROUPS):
        hot = jnp.where(row == g, group_hot[g], hot)
    before = _dot(hot.astype(BF16), tri_ref[...])
    rank = jnp.sum(hot * before, axis=0, keepdims=True)
    gid = group_hot[1] + 2.0 * group_hot[2] + 3.0 * group_hot[3]
    meta = jnp.where(row == 0, gid, jnp.where(row == 1, rank, 0.0))
    meta_ref[0] = meta.astype(jnp.int32)
    counts = jnp.sum(hot, axis=1, keepdims=True) + jnp.zeros((SUBLANES, LANES), F32)
    count_ref[0] = counts.astype(jnp.int32)


def moe_route(x, gain, mods, wr_t, b_r, tri, layer):
    return pl.pallas_call(
        functools.partial(_moe_route_kernel, layer=layer),
        grid=(N_TILES,),
        in_specs=[
            pl.BlockSpec((TM, D_MODEL), lambda i: (i, 0)),
            pl.BlockSpec((DEPTH, D_MODEL), lambda i: (0, 0)),
            pl.BlockSpec((1, 1, 6, D_MODEL), lambda i: (layer, _mod_row(i), 0, 0)),
            pl.BlockSpec((N_EXPERTS, D_MODEL), lambda i: (0, 0)),
            pl.BlockSpec((N_EXPERTS, 1), lambda i: (0, 0)),
            pl.BlockSpec((TM, TM), lambda i: (0, 0)),
        ],
        out_specs=[pl.BlockSpec((TM * SUBLANES, LANES), lambda i: (i, 0)),
                   pl.BlockSpec((TM, LANES), lambda i: (i, 0)),
                   pl.BlockSpec((1, SUBLANES, TM), lambda i: (i, 0, 0)),
                   pl.BlockSpec((1, SUBLANES, LANES), lambda i: (i, 0, 0))],
        out_shape=[jax.ShapeDtypeStruct((T_ALL * SUBLANES, LANES), F32),
                   jax.ShapeDtypeStruct((T_ALL, LANES), F32),
                   jax.ShapeDtypeStruct((N_TILES, SUBLANES, TM), jnp.int32),
                   jax.ShapeDtypeStruct((N_TILES, SUBLANES, LANES), jnp.int32)],
        compiler_params=_params(1),
        name="moe_route",
    )(x, gain, mods, wr_t, b_r, tri)


def _moe_positions(meta, counts):
    gid = meta[:, 0, :]
    rank = meta[:, 1, :]
    cnt = counts[:, :N_GROUPS, 0]
    total = jnp.sum(cnt, axis=0)
    padded = (total + (MOE_BLOCK - 1)) // MOE_BLOCK * MOE_BLOCK
    group_start = jnp.cumsum(padded) - padded
    base = group_start[None, :] + jnp.cumsum(cnt, axis=0) - cnt
    pos = rank
    for g in range(N_GROUPS):
        pos = pos + jnp.where(gid == g, base[:, g:g + 1], 0)
    first_block = group_start // MOE_BLOCK
    blk = jnp.arange(MOE_BLOCKS, dtype=jnp.int32)
    block_gid = sum((blk >= first_block[g]).astype(jnp.int32) for g in range(1, N_GROUPS))
    block_used = (blk * MOE_BLOCK < jnp.sum(padded)).astype(jnp.int32)
    later = jnp.where(block_gid[None, :] > block_gid[:, None], block_gid[None, :], N_GROUPS)
    next_group = jnp.min(later, axis=1)
    next_group = jnp.where(next_group == N_GROUPS, -1, next_group).astype(jnp.int32)
    return pos.reshape(T_ALL).astype(jnp.int32), block_gid, block_used, next_group


def _moe_permute_kernel(pos_ref, tiles_ref, gate_ref, sorted_ref, gsorted_ref):
    i = pl.program_id(0)

    @pl.when(i == 0)
    def _():
        def zero(b, carry):
            start = pl.multiple_of(b * TM, TM)
            sorted_ref[pl.ds(start, TM), :] = jnp.zeros((TM, LANES), F32)
            return carry
        lax.fori_loop(0, T_PAD * SUBLANES // TM, zero, 0)
        gsorted_ref[...] = jnp.zeros((T_PAD, LANES), F32)

    def move(t, carry):
        p = pos_ref[i * TM + t]
        dst = pl.multiple_of(p * SUBLANES, SUBLANES)
        src = pl.multiple_of(t * SUBLANES, SUBLANES)
        sorted_ref[pl.ds(dst, SUBLANES), :] = tiles_ref[pl.ds(src, SUBLANES), :]
        gsorted_ref[pl.ds(p, 1), :] = gate_ref[pl.ds(t, 1), :]
        return carry
    lax.fori_loop(0, TM, move, 0, unroll=8)


def moe_permute(pos, tiles, gates):
    return pl.pallas_call(
        _moe_permute_kernel,
        grid_spec=pltpu.PrefetchScalarGridSpec(
            num_scalar_prefetch=1,
            grid=(N_TILES,),
            in_specs=[pl.BlockSpec((TM * SUBLANES, LANES), lambda i, pos: (i, 0)),
                      pl.BlockSpec((TM, LANES), lambda i, pos: (i, 0))],
            out_specs=[pl.BlockSpec(memory_space=pltpu.VMEM), pl.BlockSpec(memory_space=pltpu.VMEM)],
        ),
        out_shape=[jax.ShapeDtypeStruct((T_PAD * SUBLANES, LANES), F32),
                   jax.ShapeDtypeStruct((T_PAD, LANES), F32)],
        compiler_params=_params(1),
        name="moe_permute",
    )(pos, tiles, gates)


def _group_changed(gid_ref, b):
    return (b == 0) | (gid_ref[b] != gid_ref[jnp.maximum(b - 1, 0)])


def _moe_expert_kernel(gid_ref, used_ref, next_ref, s_ref, gate_ref, w1_hbm, w3_hbm, w2_ref, y_ref,
                       w1f_ref, w3f_ref, w1b_ref, w3b_ref, w2b_ref, sem, *, layer):
    b = pl.program_id(0)

    def up_weight_copies(group):
        return (pltpu.make_async_copy(w1_hbm.at[layer, group], w1f_ref, sem.at[0]),
                pltpu.make_async_copy(w3_hbm.at[layer, group], w3f_ref, sem.at[1]))

    @pl.when(b == 0)
    def _():
        for copy in up_weight_copies(gid_ref[0]):
            copy.start()

    @pl.when(_group_changed(gid_ref, b))
    def _():
        for copy in up_weight_copies(gid_ref[b]):
            copy.wait()
        for j in range(EXP_PER_GROUP):
            cols = slice(j * D_FF, (j + 1) * D_FF)
            w1b_ref[:, cols] = w1f_ref[j].astype(BF16)
            w3b_ref[:, cols] = w3f_ref[j].astype(BF16)
        w2b_ref[...] = w2_ref[0, 0].astype(BF16)

        @pl.when(next_ref[b] >= 0)
        def _():
            for copy in up_weight_copies(next_ref[b]):
                copy.start()

    @pl.when(used_ref[b] == 1)
    def _():
        lhs = jnp.concatenate([s_ref[pl.ds(s, MOE_BLOCK, stride=SUBLANES), :].astype(BF16)
                               for s in range(SUBLANES)], axis=1)
        a = _dot(lhs, w1b_ref[...])
        g = _dot(lhs, w3b_ref[...])
        gates = gate_ref[...]
        lane = lax.broadcasted_iota(jnp.int32, (MOE_BLOCK, LANES), 1)
        first = gid_ref[b] * EXP_PER_GROUP
        parts = []
        for j in range(EXP_PER_GROUP):
            cols = slice(j * D_FF, (j + 1) * D_FF)
            gate = jnp.sum(jnp.where(lane == first + j, gates, 0.0), axis=1, keepdims=True)
            parts.append((_silu(a[:, cols]) * g[:, cols] * gate).astype(BF16))
        act = jnp.concatenate(parts, axis=1)
        y = _dot(act, w2b_ref[...])
        for s in range(SUBLANES):
            y_ref[pl.ds(s, MOE_BLOCK, stride=SUBLANES), :] = y[:, s * LANES:(s + 1) * LANES]

    @pl.when(used_ref[b] == 0)
    def _():
        y_ref[...] = jnp.zeros_like(y_ref)


def moe_experts(block_gid, block_used, next_group, sorted_rows, sorted_gates, w1, w3, w2, layer):
    group_ff = EXP_PER_GROUP * D_FF
    w1g = w1.reshape(DEPTH, N_GROUPS, EXP_PER_GROUP, D_MODEL, D_FF)
    w3g = w3.reshape(DEPTH, N_GROUPS, EXP_PER_GROUP, D_MODEL, D_FF)
    w2g = w2.reshape(DEPTH, N_GROUPS, group_ff, D_MODEL)
    block = lambda b, gid, used, nxt: (b, 0)
    return pl.pallas_call(
        functools.partial(_moe_expert_kernel, layer=layer),
        grid_spec=pltpu.PrefetchScalarGridSpec(
            num_scalar_prefetch=3,
            grid=(MOE_BLOCKS,),
            in_specs=[pl.BlockSpec((MOE_BLOCK * SUBLANES, LANES), block),
                      pl.BlockSpec((MOE_BLOCK, LANES), block),
                      pl.BlockSpec(memory_space=pl.ANY), pl.BlockSpec(memory_space=pl.ANY),
                      pl.BlockSpec((1, 1, group_ff, D_MODEL), lambda b, gid, used, nxt: (layer, gid[b], 0, 0))],
            out_specs=pl.BlockSpec((MOE_BLOCK * SUBLANES, LANES), block),
            scratch_shapes=[pltpu.VMEM((EXP_PER_GROUP, D_MODEL, D_FF), F32),
                            pltpu.VMEM((EXP_PER_GROUP, D_MODEL, D_FF), F32),
                            pltpu.VMEM((D_MODEL, group_ff), BF16), pltpu.VMEM((D_MODEL, group_ff), BF16),
                            pltpu.VMEM((group_ff, D_MODEL), BF16),
                            pltpu.SemaphoreType.DMA((2,))],
        ),
        out_shape=jax.ShapeDtypeStruct((T_PAD * SUBLANES, LANES), F32),
        compiler_params=_params(1),
        name="moe_experts",
    )(block_gid, block_used, next_group, sorted_rows, sorted_gates, w1g, w3g, w2g)


def _moe_combine_kernel(pos_ref, ys_ref, x_ref, m_ref, o_ref, buf_ref):
    i = pl.program_id(0)

    def move(t, carry):
        src = pl.multiple_of(pos_ref[i * TM + t] * SUBLANES, SUBLANES)
        dst = pl.multiple_of(t * SUBLANES, SUBLANES)
        buf_ref[pl.ds(dst, SUBLANES), :] = ys_ref[pl.ds(src, SUBLANES), :]
        return carry
    lax.fori_loop(0, TM, move, 0, unroll=8)

    for s in range(SUBLANES):
        cols = slice(s * LANES, (s + 1) * LANES)
        y = buf_ref[pl.ds(s, TM, stride=SUBLANES), :]
        o_ref[:, cols] = x_ref[:, cols] + m_ref[0, 0, 5:6, cols] * y


def moe_combine(pos, y_sorted, x, mods, layer):
    return pl.pallas_call(
        _moe_combine_kernel,
        grid_spec=pltpu.PrefetchScalarGridSpec(
            num_scalar_prefetch=1,
            grid=(N_TILES,),
            in_specs=[
                pl.BlockSpec(memory_space=pltpu.VMEM),
                pl.BlockSpec((TM, D_MODEL), lambda i, pos: (i, 0)),
                pl.BlockSpec((1, 1, 6, D_MODEL), lambda i, pos: (layer, _mod_row(i), 0, 0)),
            ],
            out_specs=pl.BlockSpec((TM, D_MODEL), lambda i, pos: (i, 0)),
            scratch_shapes=[pltpu.VMEM((TM * SUBLANES, LANES), F32)],
        ),
        out_shape=jax.ShapeDtypeStruct((T_ALL, D_MODEL), F32),
        compiler_params=_params(1),
        name="moe_combine",
    )(pos, y_sorted, x, mods)


def moe(x, gain, mods, wr_t, b_r, tri, w1, w3, w2, layer):
    tiles, gates, meta, counts = moe_route(x, gain, mods, wr_t, b_r, tri, layer)
    pos, block_gid, block_used, next_group = _moe_positions(meta, counts)
    sorted_rows, sorted_gates = moe_permute(pos, tiles, gates)
    y_sorted = moe_experts(block_gid, block_used, next_group, sorted_rows, sorted_gates, w1, w3, w2, layer)
    return moe_combine(pos, y_sorted, x, mods, layer)


def _final_norm_kernel(x_ref, g_ref, o_ref):
    o_ref[...] = _rms(x_ref[...]) * g_ref[...]


def final_norm(x, gain, tile0, ntiles):
    return pl.pallas_call(
        _final_norm_kernel,
        grid=(ntiles,),
        in_specs=[pl.BlockSpec((TM, D_MODEL), lambda i: (tile0 + i, 0)),
                  pl.BlockSpec((1, D_MODEL), lambda i: (0, 0))],
        out_specs=pl.BlockSpec((TM, D_MODEL), lambda i: (i, 0)),
        out_shape=jax.ShapeDtypeStruct((ntiles * TM, D_MODEL), F32),
        compiler_params=_params(1),
        name="final_norm",
    )(x, gain.reshape(1, D_MODEL))


def kernel(x_prompt, x_sample, cache_attn_k, cache_attn_v, state_retention, c, c_ctx, w_ada, b_ada, norm1_g, norm2_g, final_g, w_in, attn_lambda, attn_subln_g, hy_conv_w, hy_conv_b, hy_f_w1, hy_f_b1, hy_f_w2, hy_f_b2, hy_f_w3, hy_skip, ret_decay_exp, w_branch_a, w_branch_b, w_branch_c, w_gate, b_gate, w_out, w_router, b_router, moe_w1, moe_w3, moe_w2):
    x = jnp.concatenate([x_prompt.reshape(T_CTX, D_MODEL), x_sample.reshape(T_LAT, D_MODEL)], axis=0)
    cond8 = jnp.concatenate([c_ctx[None, :], c, jnp.zeros((8 - 1 - DEC_BATCH, D_MODEL), F32)], axis=0)
    mods = ada_modulation(cond8, w_ada, b_ada).reshape(DEPTH, 8, 6, D_MODEL)

    cos, sin = _rope_tables()
    cos_c, sin_c = cos[:, :H_C * DK_C], sin[:, :H_C * DK_C]
    cache_k = cache_attn_k.reshape(DEC_BATCH, DEPTH, PAST_LEN, W_A)
    cache_v = cache_attn_v.reshape(DEC_BATCH, DEPTH, PAST_LEN, W_A)
    decay_rates = _filter_decay_rates()
    tables_ctx = _dft_tables(SEQ)
    tables_lat = _dft_tables(DEC_SEQ)
    filt_ctx = hyena_filters(SEQ, _filter_features(SEQ), decay_rates, tables_ctx[0], tables_ctx[1],
                             hy_f_w1, hy_f_b1, hy_f_w2, hy_f_b2, hy_f_w3)
    filt_lat = hyena_filters(DEC_SEQ, _filter_features(DEC_SEQ), decay_rates, tables_lat[0], tables_lat[1],
                             hy_f_w1, hy_f_b1, hy_f_w2, hy_f_b2, hy_f_w3)
    wr_t = w_router.T
    b_r = b_router.reshape(N_EXPERTS, 1)
    tri = jnp.asarray(np.triu(np.ones((TM, TM), np.float32), 1), dtype=BF16)

    threaded = ()
    for l in range(DEPTH):
        p, kv = in_projection(x, norm1_g, mods, w_in, w_gate, b_gate, l)

        ya, new_k, new_v, yb, yc, states = mixers_ctx(
            p, kv, attn_lambda, attn_subln_g, hy_conv_w, hy_conv_b, hy_skip, filt_ctx[0], filt_ctx[1],
            tables_ctx, ret_decay_exp, l, threaded)
        threaded = (new_k, new_v, states)
        ya = attention_lat(ya, p, kv, cache_k, cache_v, cos, sin, attn_lambda, attn_subln_g, l)
        yb = hyena_lat(yb, p, hy_conv_w, hy_conv_b, hy_skip, filt_lat[0], filt_lat[1], tables_lat, l)
        yc = retention_lat(yc, p, ret_decay_exp, cos_c, sin_c, state_retention, l)

        x = merge_branches(x, ya, yb, yc, p, mods, w_branch_a, w_branch_b, w_branch_c, w_out, l)
        x = moe(x, norm2_g, mods, wr_t, b_r, tri, moe_w1, moe_w3, moe_w2, l)

    y_prompt = final_norm(x, final_g, 0, CTX_TILES).reshape(BATCH, SEQ, D_MODEL)
    y_sample = final_norm(x, final_g, CTX_TILES, N_TILES - CTX_TILES).reshape(DEC_BATCH, DEC_SEQ, D_MODEL)
    return (y_prompt, y_sample, new_k, new_v, states)
```

```python
import functools
import math

import jax
import jax.numpy as jnp
import numpy as np
from jax import lax
from jax.experimental import pallas as pl
from jax.experimental.pallas import tpu as pltpu

F32 = jnp.float32
BF16 = jnp.bfloat16

D_MODEL = 1024
BATCH = 16
SEQ = 256
DEPTH = 4
DEC_BATCH = 2
DEC_SEQ = 1024
PAST_LEN = 256
GRID_W = 64
EPS = 1e-6
ROPE_BASE = 10000.0
H_A = 4
DH_A = 64
DV_A = 128
W_A = 512
C_B = 512
SHORT_K = 3
H_EMB = 33
H_FFN = 64
H_FAST_DECAY = 0.3
H_SLOW_DECAY = 1.5
H_TARGET = 1e-2
H_C = 4
DK_C = 64
DV_C = 128
W_C = 512
N_EXPERTS = 16
N_GROUPS = 4
EXP_PER_GROUP = 4
D_FF = 512
D_IN = 4608

T_CTX = BATCH * SEQ
T_LAT = DEC_BATCH * DEC_SEQ
T_ALL = T_CTX + T_LAT
TM = 1024
N_TILES = T_ALL // TM
CTX_TILES = T_CTX // TM
LANES = 128
SUBLANES = 8
MOE_BLOCK = 256
MOE_BLOCKS = T_ALL // MOE_BLOCK + N_GROUPS
T_PAD = MOE_BLOCKS * MOE_BLOCK
VMEM_LIMIT = 56 * 1024 * 1024


def _params(n_axes):
    return pltpu.CompilerParams(
        dimension_semantics=("arbitrary",) * n_axes, vmem_limit_bytes=VMEM_LIMIT)


def _mod_row(i):
    return jnp.maximum(i - (CTX_TILES - 1), 0)


def _rms(x):
    return x * lax.rsqrt(jnp.mean(x * x, axis=-1, keepdims=True) + EPS)


def _sigmoid(x):
    return 0.5 * jnp.tanh(0.5 * x) + 0.5


def _silu(x):
    return x * _sigmoid(x)


def _dot(a, b):
    return jnp.dot(a, b, preferred_element_type=F32)


def _dot_nt(a, b):
    return lax.dot_general(a, b, (((1,), (1,)), ((), ())), preferred_element_type=F32)


def _dot_tn(a, b):
    return lax.dot_general(a, b, (((0,), (0,)), ((), ())), preferred_element_type=F32)


def _split3(x):
    hi = x.astype(BF16)
    lo = (x - hi.astype(F32)).astype(BF16)
    return hi, lo


def _dot3(a, b):
    ah, al = _split3(a)
    bh, bl = _split3(b)
    return _dot(ah, bh) + (_dot(ah, bl) + _dot(al, bh))


def _dot3_nt(a, b):
    ah, al = _split3(a)
    bh, bl = _split3(b)
    return _dot_nt(ah, bh) + (_dot_nt(ah, bl) + _dot_nt(al, bh))


def _ada_kernel(c_ref, w_ref, b_ref, o_ref):
    s = _silu(c_ref[...])
    o_ref[0] = _dot(s.astype(BF16), w_ref[0].astype(BF16)) + b_ref[0]


def ada_modulation(cond8, w_ada, b_ada):
    tn = 1536
    n = 6 * D_MODEL
    return pl.pallas_call(
        _ada_kernel,
        grid=(DEPTH, n // tn),
        in_specs=[
            pl.BlockSpec((8, D_MODEL), lambda l, j: (0, 0)),
            pl.BlockSpec((1, D_MODEL, tn), lambda l, j: (l, 0, j)),
            pl.BlockSpec((1, 1, tn), lambda l, j: (l, 0, j)),
        ],
        out_specs=pl.BlockSpec((1, 8, tn), lambda l, j: (l, 0, j)),
        out_shape=jax.ShapeDtypeStruct((DEPTH, 8, n), F32),
        compiler_params=_params(2),
        name="ada_modulation",
    )(cond8, w_ada, b_ada.reshape(DEPTH, 1, n))


PROJ_TM = 3072
PROJ_TN = 512
GATE_BLOCKS = 3 * D_MODEL // PROJ_TN
IN_BLOCKS = D_IN // PROJ_TN
P_WIDTH = 3 * D_MODEL + D_IN
KV_FIRST = GATE_BLOCKS + 1
P_Q = 3 * D_MODEL // W_A
P_HYENA = (3 * D_MODEL + 3 * W_A) // (3 * C_B)
P_RET_Q = (3 * D_MODEL + 3 * W_A + 3 * C_B) // (H_C * DK_C)
P_RET_V = (3 * D_MODEL + 3 * W_A + 3 * C_B + 2 * H_C * DK_C) // W_C


def _in_proj_kernel(x_ref, g_ref, m_ref, win_ref, wg_ref, bg_ref, p_ref, kv_ref, h_ref, *, layer):
    i = pl.program_id(0)
    j = pl.program_id(1)

    @pl.when(j == 0)
    def _():
        for s in range(PROJ_TM // TM):
            rows = slice(s * TM, (s + 1) * TM)
            mod = m_ref[layer, _mod_row(i * (PROJ_TM // TM) + s)]
            y = _rms(x_ref[rows, :]) * g_ref[layer:layer + 1, :]
            h_ref[rows, :] = (y * (1.0 + mod[1:2, :]) + mod[0:1, :]).astype(BF16)

    @pl.when(j < GATE_BLOCKS)
    def _():
        acc = _dot(h_ref[...], wg_ref[0].astype(BF16)) + bg_ref[0]
        p_ref[...] = _sigmoid(acc).astype(BF16)

    @pl.when(j >= GATE_BLOCKS)
    def _():
        acc = _dot(h_ref[...], win_ref[0].astype(BF16))
        p_ref[...] = acc.astype(BF16)

        @pl.when((j == KV_FIRST) | (j == KV_FIRST + 1))
        def _():
            kv_ref[...] = acc


def in_projection(x, gain, mods, w_in, w_gate, b_gate, layer):
    return pl.pallas_call(
        functools.partial(_in_proj_kernel, layer=layer),
        grid=(T_ALL // PROJ_TM, GATE_BLOCKS + IN_BLOCKS),
        in_specs=[
            pl.BlockSpec((PROJ_TM, D_MODEL), lambda i, j: (i, 0), pipeline_mode=pl.Buffered(1)),
            pl.BlockSpec((DEPTH, D_MODEL), lambda i, j: (0, 0)),
            pl.BlockSpec((DEPTH, 8, 6, D_MODEL), lambda i, j: (0, 0, 0, 0)),
            pl.BlockSpec((1, D_MODEL, PROJ_TN), lambda i, j: (layer, 0, jnp.maximum(j - GATE_BLOCKS, 0))),
            pl.BlockSpec((1, D_MODEL, PROJ_TN), lambda i, j: (layer, 0, jnp.minimum(j, GATE_BLOCKS - 1))),
            pl.BlockSpec((1, 1, PROJ_TN), lambda i, j: (layer, 0, jnp.minimum(j, GATE_BLOCKS - 1))),
        ],
        out_specs=[
            pl.BlockSpec((PROJ_TM, PROJ_TN), lambda i, j: (i, j)),
            pl.BlockSpec((PROJ_TM, PROJ_TN), lambda i, j: (i, jnp.clip(j - KV_FIRST, 0, 1))),
        ],
        out_shape=[jax.ShapeDtypeStruct((T_ALL, P_WIDTH), BF16),
                   jax.ShapeDtypeStruct((T_ALL, 2 * W_A), F32)],
        scratch_shapes=[pltpu.VMEM((PROJ_TM, D_MODEL), BF16)],
        compiler_params=_params(2),
        name="in_projection",
    )(x, gain, mods, w_in, w_gate, b_gate.reshape(DEPTH, 1, 3 * D_MODEL))


def _rope_tables():
    n = DEC_SEQ
    rows = n // GRID_W
    row = jnp.repeat(jnp.arange(rows, dtype=F32), GRID_W)
    col = jnp.tile(jnp.arange(GRID_W, dtype=F32), rows)
    quarter = DH_A // 4
    inv = ROPE_BASE ** (-jnp.arange(quarter, dtype=F32) / quarter)
    ar = row[:, None] * inv
    ac = col[:, None] * inv
    ang = jnp.concatenate([ar, ar, ac, ac], axis=-1)
    cos = jnp.tile(jnp.cos(ang), (1, 512 // DH_A))
    sgn = jnp.tile(jnp.concatenate([-jnp.ones((quarter,), F32), jnp.ones((quarter,), F32)]), 512 // (2 * quarter))
    sin = jnp.tile(jnp.sin(ang), (1, 512 // DH_A)) * sgn
    return cos, sin


def _rope(x, cos, sin):
    w = x.shape[-1]
    lane = lax.broadcasted_iota(jnp.int32, x.shape, 1)
    first = (lane % 32) < 16
    rot = jnp.where(first, pltpu.roll(x, w - 16, 1), pltpu.roll(x, 16, 1))
    return x * cos + rot * sin


def _lambda(al_ref, layer):
    al = al_ref[layer]
    a = jnp.sum(al[0:1, :] * al[1:2, :], axis=-1, keepdims=True)
    b = jnp.sum(al[2:3, :] * al[3:4, :], axis=-1, keepdims=True)
    lam_init = 0.8 - 0.6 * math.exp(-0.3 * layer)
    return jnp.exp(a) - jnp.exp(b) + lam_init, lam_init


def _scores(q, k):
    return _dot_nt(q, k).astype(BF16)


def _softmax_numerators(s):
    return jnp.exp2(s - jnp.max(s, axis=-1, keepdims=True))


def _diff_attention(q, k, v, lam, lam_init, subln, o_ref):
    lane = lax.broadcasted_iota(jnp.int32, q.shape, 1)
    q = q * (DH_A ** -0.5 * math.log2(math.e))
    q0 = jnp.where((lane % LANES) < DH_A, q, 0.0).astype(BF16)
    q1 = jnp.where((lane % LANES) >= DH_A, q, 0.0).astype(BF16)
    ones = jnp.ones((k.shape[0], LANES), BF16)
    for h in range(H_A):
        sl = slice(h * LANES, (h + 1) * LANES)
        kh = k[:, sl]
        v_ones = jnp.concatenate([v[:, sl], ones], axis=1)
        o0 = _dot(_softmax_numerators(_scores(q0[:, sl], kh)), v_ones)
        o1 = _dot(_softmax_numerators(_scores(q1[:, sl], kh)), v_ones)
        r0 = 1.0 / o0[:, DV_A:DV_A + 1]
        r1 = lam / o1[:, DV_A:DV_A + 1]
        o = o0[:, :DV_A] * r0 - o1[:, :DV_A] * r1
        o = _rms(o) * subln * (1.0 - lam_init)
        o_ref[:, sl] = o.astype(o_ref.dtype)


def _attn_ctx_kernel(q_ref, k_ref, v_ref, al_ref, g_ref, o_ref, knew_ref, vnew_ref, *, layer):
    lam, lam_init = _lambda(al_ref, layer)
    subln = g_ref[layer:layer + 1, :]
    k = k_ref[...]
    v = v_ref[...]
    for h in range(H_A):
        knew_ref[0, 0, :, h, :] = k[:, h * LANES:(h + 1) * LANES]
        vnew_ref[0, 0, :, h, :] = v[:, h * LANES:(h + 1) * LANES]
    _diff_attention(q_ref[...].astype(F32), k.astype(BF16), v.astype(BF16), lam, lam_init, subln, o_ref)


QB = 256


def _attn_lat_kernel(ya_ref, q_ref, k_ref, v_ref, ck_ref, cv_ref, cosq_ref, sinq_ref, cos_ref, sin_ref,
                     al_ref, g_ref, o_ref, kall_ref, vall_ref, *, layer):
    del ya_ref

    @pl.when(pl.program_id(1) == 0)
    def _():
        kall_ref[0:PAST_LEN, :] = ck_ref[0, 0].astype(BF16)
        vall_ref[0:PAST_LEN, :] = cv_ref[0, 0].astype(BF16)
        kall_ref[PAST_LEN:, :] = _rope(k_ref[...], cos_ref[...], sin_ref[...]).astype(BF16)
        vall_ref[PAST_LEN:, :] = v_ref[...].astype(BF16)

    lam, lam_init = _lambda(al_ref, layer)
    subln = g_ref[layer:layer + 1, :]
    q = _rope(q_ref[...].astype(F32), cosq_ref[...], sinq_ref[...])
    _diff_attention(q, kall_ref[...], vall_ref[...], lam, lam_init, subln, o_ref)


def attention_lat(ya, q, kv, cache_k, cache_v, cos, sin, attn_lambda, subln_g, layer):
    nqb = DEC_SEQ // QB
    row0 = T_CTX // QB
    seq0 = T_CTX // DEC_SEQ
    return pl.pallas_call(
        functools.partial(_attn_lat_kernel, layer=layer),
        grid=(DEC_BATCH, nqb),
        in_specs=[
            pl.BlockSpec(memory_space=pl.ANY),
            pl.BlockSpec((QB, W_A), lambda b, i: (row0 + b * nqb + i, P_Q)),
            pl.BlockSpec((DEC_SEQ, W_A), lambda b, i: (seq0 + b, 0)),
            pl.BlockSpec((DEC_SEQ, W_A), lambda b, i: (seq0 + b, 1)),
            pl.BlockSpec((1, 1, PAST_LEN, W_A), lambda b, i: (b, layer, 0, 0)),
            pl.BlockSpec((1, 1, PAST_LEN, W_A), lambda b, i: (b, layer, 0, 0)),
            pl.BlockSpec((QB, W_A), lambda b, i: (i, 0)),
            pl.BlockSpec((QB, W_A), lambda b, i: (i, 0)),
            pl.BlockSpec((DEC_SEQ, W_A), lambda b, i: (0, 0)),
            pl.BlockSpec((DEC_SEQ, W_A), lambda b, i: (0, 0)),
            pl.BlockSpec((DEPTH, 4, DH_A), lambda b, i: (0, 0, 0)),
            pl.BlockSpec((DEPTH, DV_A), lambda b, i: (0, 0)),
        ],
        out_specs=pl.BlockSpec((QB, W_A), lambda b, i: (row0 + b * nqb + i, 0)),
        out_shape=jax.ShapeDtypeStruct((T_ALL, W_A), BF16),
        scratch_shapes=[pltpu.VMEM((PAST_LEN + DEC_SEQ, W_A), BF16),
                        pltpu.VMEM((PAST_LEN + DEC_SEQ, W_A), BF16)],
        input_output_aliases={0: 0},
        compiler_params=_params(2),
        name="attention_lat",
    )(ya, q, kv, kv, cache_k, cache_v, cos, sin, cos, sin, attn_lambda, subln_g)


def _dft_tables(n):
    k = np.arange(n, dtype=np.int64)
    prod = (2 * k[:, None] + 1) * k[None, :]
    ang = (prod % (4 * n)).astype(np.float64) * (math.pi / (2 * n))
    c = np.cos(ang).astype(np.float32)
    s = np.sin(ang).astype(np.float32)
    return tuple(jnp.asarray(t).astype(BF16) for t in (c, s, c.T, s.T))


def _filter_features(n):
    t = jnp.linspace(0.0, 1.0, n, dtype=F32)[:, None]
    bands = (H_EMB - 1) // 2
    w = 2.0 * math.pi * jnp.arange(n, dtype=F32)[:, None] / n
    fr = jnp.linspace(1e-4, bands - 1, bands, dtype=F32)
    z = jnp.concatenate([t, jnp.cos(w * fr), -jnp.sin(w * fr)], axis=-1)
    return jnp.pad(z, ((0, 0), (0, LANES - H_EMB)))


def _filter_decay_rates():
    max_decay = math.log(H_TARGET) / H_FAST_DECAY
    min_decay = math.log(H_TARGET) / H_SLOW_DECAY
    return jnp.abs(jnp.linspace(min_decay, max_decay, C_B, dtype=F32))[None, :]


def _hyena_filter_kernel(z_ref, w1_ref, b1_ref, w2_ref, b2_ref, w3_ref, dr_ref, c_ref, s_ref,
                         gre_ref, gim_ref):
    z = z_ref[...]
    hid = jnp.sin(_dot3(z, w1_ref[0]) + b1_ref[0])
    hid = jnp.sin(_dot3(hid, w2_ref[0]) + b2_ref[0])
    h = _dot(hid.astype(BF16), w3_ref[0].astype(BF16))
    window = jnp.exp(-z[:, 0:1] * dr_ref[...])
    hf = h[:, :C_B] * window
    hb = h[:, C_B:] * window
    total = jnp.sum(jnp.abs(hf) + jnp.abs(hb), axis=0, keepdims=True)
    hf = hf / total
    hb = hb / total
    row = lax.broadcasted_iota(jnp.int32, hb.shape, 0)
    hb = jnp.where(row == 0, 0.0, hb)
    gre_ref[0] = _dot(c_ref[...], (hf + hb).astype(BF16))
    gim_ref[0] = _dot(s_ref[...], (hb - hf).astype(BF16))


def hyena_filters(n, z, decay_rates, cf, sf, f_w1, f_b1, f_w2, f_b2, f_w3):
    w1 = jnp.pad(f_w1, ((0, 0), (0, LANES - H_EMB), (0, 0)))
    full = lambda shape: pl.BlockSpec(shape, lambda l: (0,) * len(shape))
    per_layer = lambda shape: pl.BlockSpec((1,) + shape, lambda l: (l,) + (0,) * len(shape))
    return pl.pallas_call(
        _hyena_filter_kernel,
        grid=(DEPTH,),
        in_specs=[
            full((n, LANES)),
            per_layer((LANES, H_FFN)), per_layer((1, H_FFN)),
            per_layer((H_FFN, H_FFN)), per_layer((1, H_FFN)),
            per_layer((H_FFN, 2 * C_B)),
            full((1, C_B)), full((n, n)), full((n, n)),
        ],
        out_specs=[per_layer((n, C_B)), per_layer((n, C_B))],
        out_shape=[jax.ShapeDtypeStruct((DEPTH, n, C_B), F32)] * 2,
        compiler_params=_params(1),
        name="hyena_filters",
    )(z, w1, f_b1.reshape(DEPTH, 1, H_FFN), f_w2, f_b2.reshape(DEPTH, 1, H_FFN), f_w3,
      decay_rates, cf, sf)


def _hyena_kernel(*refs, layer, n, aliased):
    if aliased:
        refs = refs[1:]
    u_ref, cw_ref, cb_ref, skip_ref, gre_ref, gim_ref, c_ref, s_ref, ct_ref, st_ref, o_ref = refs
    u = u_ref[...].astype(F32)
    row = lax.broadcasted_iota(jnp.int32, u.shape, 0)
    prev = jnp.where(row == 0, 0.0, pltpu.roll(u, 1, 0))
    nxt = jnp.where(row == n - 1, 0.0, pltpu.roll(u, n - 1, 0))
    cw = cw_ref[layer]
    uc = cb_ref[layer:layer + 1, :] + prev * cw[0:1, :] + u * cw[1:2, :] + nxt * cw[2:3, :]
    x0 = uc[:, :C_B]
    x1 = uc[:, C_B:2 * C_B]
    v = uc[:, 2 * C_B:]
    w = v * x1
    wb = w.astype(BF16)
    ure = _dot(c_ref[...], wb)
    uim = -_dot(s_ref[...], wb)
    gre = gre_ref[0]
    gim = gim_ref[0]
    yre = (ure * gre - uim * gim).astype(BF16)
    yim = (ure * gim + uim * gre).astype(BF16)
    y = (_dot(ct_ref[...], yre) - _dot(st_ref[...], yim)) * (1.0 / n)
    y = y + w * skip_ref[layer:layer + 1, :]
    o_ref[...] = (x0 * y).astype(o_ref.dtype)


def _full(shape):
    return pl.BlockSpec(shape, lambda *_: (0,) * len(shape))


def _hyena_specs(n, seq0, layer):
    return [
        pl.BlockSpec((n, 3 * C_B), lambda b, *_: (seq0 + b, P_HYENA)),
        _full((DEPTH, SHORT_K, 3 * C_B)), _full((DEPTH, 3 * C_B)), _full((DEPTH, C_B)),
        pl.BlockSpec((1, n, C_B), lambda *_: (layer, 0, 0)),
        pl.BlockSpec((1, n, C_B), lambda *_: (layer, 0, 0)),
        _full((n, n)), _full((n, n)), _full((n, n)), _full((n, n)),
    ]


def hyena_lat(yb, p, conv_w, conv_b, skip, gre, gim, tables, layer):
    seq0 = T_CTX // DEC_SEQ
    return pl.pallas_call(
        functools.partial(_hyena_kernel, layer=layer, n=DEC_SEQ, aliased=True),
        grid=(DEC_BATCH,),
        in_specs=[pl.BlockSpec(memory_space=pl.ANY)] + _hyena_specs(DEC_SEQ, seq0, layer),
        out_specs=pl.BlockSpec((DEC_SEQ, C_B), lambda b: (seq0 + b, 0)),
        out_shape=jax.ShapeDtypeStruct((T_ALL, C_B), BF16),
        input_output_aliases={0: 0},
        compiler_params=_params(1),
        name="hyena_lat",
    )(yb, p, conv_w, conv_b, skip, gre, gim, *tables)


RET_QB = 256


def _retention_kernel(*refs, layer, n, latent):
    if latent:
        _, q_ref, k_ref, v_ref, g_ref, de_ref, cos_ref, sin_ref, s0_ref, o_ref, w_ref, vec_ref = refs
        st_ref = None
    else:
        q_ref, k_ref, v_ref, g_ref, de_ref, o_ref, st_ref, w_ref, vec_ref = refs[-9:]
    log_gamma = jnp.log1p(-jnp.exp2(-de_ref[layer]))

    @pl.when(pl.program_id(0) == 0)
    def _():
        t = lax.broadcasted_iota(jnp.int32, (n, n), 0)
        s = lax.broadcasted_iota(jnp.int32, (n, n), 1)
        lag = (t - s).astype(F32)
        for h in range(H_C):
            rate = jnp.where(lag > 0, log_gamma[0:1, h:h + 1], log_gamma[1:2, h:h + 1])
            w_ref[h] = jnp.where(lag == 0, 2.0, jnp.exp(rate * jnp.abs(lag)))
            pos = lax.broadcasted_iota(jnp.int32, (n, LANES), 0).astype(F32)
            steps = (pos + 1.0, float(n) - pos) if latent else (float(n - 1) - pos, pos)
            for d in range(2):
                vec_ref[h, d] = jnp.exp(log_gamma[d:d + 1, h:h + 1] * steps[d])

    q = q_ref[...].astype(F32)
    k = k_ref[...].astype(F32) * (DK_C ** -0.5)
    if latent:
        q = _rope(q, cos_ref[...], sin_ref[...])
        k = _rope(k, cos_ref[...], sin_ref[...])
    v = v_ref[...].astype(BF16)
    gate = g_ref[...].astype(F32)
    lane = lax.broadcasted_iota(jnp.int32, (n, LANES), 1)
    zeros64 = jnp.zeros((DK_C, DV_C), F32)

    for h in range(H_C):
        pair = slice((h // 2) * LANES, (h // 2 + 1) * LANES)
        lo = (h % 2) * DK_C
        own = (lane >= lo) & (lane < lo + DK_C)
        vs = slice(h * DV_C, (h + 1) * DV_C)
        qh = jnp.where(own, q[:, pair], 0.0).astype(BF16)
        kh = jnp.where(own, k[:, pair], 0.0)
        kb = kh.astype(BF16)
        vh = v[:, vs]
        if latent:
            s0 = [jnp.concatenate([s0_ref[0, 0, d, h], zeros64] if lo == 0 else [zeros64, s0_ref[0, 0, d, h]],
                                  axis=0).astype(BF16) for d in range(2)]
            carry = _dot(qh, s0[0]) * vec_ref[h, 0] + _dot(qh, s0[1]) * vec_ref[h, 1]
        for r0 in range(0, n, RET_QB):
            rows = slice(r0, r0 + RET_QB)
            att = _dot_nt(qh[rows], kb) * w_ref[h, rows, :]
            out = _dot(att.astype(BF16), vh)
            if latent:
                out = out + carry[rows]
            o_ref[rows, vs] = (_rms(out) * _silu(gate[rows, vs])).astype(o_ref.dtype)
        if st_ref is not None:
            sf = _dot_tn((kh * vec_ref[h, 0]).astype(BF16), vh)
            sb = _dot_tn((kh * vec_ref[h, 1]).astype(BF16), vh)
            st_ref[0, 0, 0, h] = sf[lo:lo + DK_C, :]
            st_ref[0, 0, 1, h] = sb[lo:lo + DK_C, :]


def _retention_specs(n, seq0):
    return [
        pl.BlockSpec((n, H_C * DK_C), lambda b, *_: (seq0 + b, P_RET_Q)),
        pl.BlockSpec((n, H_C * DK_C), lambda b, *_: (seq0 + b, P_RET_Q + 1)),
        pl.BlockSpec((n, W_C), lambda b, *_: (seq0 + b, P_RET_V)),
        pl.BlockSpec((n, W_C), lambda b, *_: (seq0 + b, P_RET_V + 1)),
        _full((DEPTH, 2, H_C)),
    ]


def retention_lat(yc, p, decay_exp, cos, sin, state0, layer):
    n = DEC_SEQ
    seq0 = T_CTX // DEC_SEQ
    return pl.pallas_call(
        functools.partial(_retention_kernel, layer=layer, n=n, latent=True),
        grid=(DEC_BATCH,),
        in_specs=[pl.BlockSpec(memory_space=pl.ANY)] + _retention_specs(n, seq0) + [
            _full((n, H_C * DK_C)), _full((n, H_C * DK_C)),
            pl.BlockSpec((1, 1, 2, H_C, DK_C, DV_C), lambda b: (b, layer, 0, 0, 0, 0)),
        ],
        out_specs=pl.BlockSpec((n, W_C), lambda b: (seq0 + b, 0)),
        out_shape=jax.ShapeDtypeStruct((T_ALL, W_C), BF16),
        scratch_shapes=[pltpu.VMEM((H_C, n, n), F32), pltpu.VMEM((H_C, 2, n, LANES), F32)],
        input_output_aliases={0: 0},
        compiler_params=_params(1),
        name="retention_lat",
    )(yc, p, p, p, p, decay_exp, cos, sin, state0)


N_ATTN_IN = 5
N_HYENA_IN = 10
N_RET_IN = 5


def _mixers_ctx_kernel(*refs, layer, n_threaded):
    refs = refs[n_threaded:]
    attn_in = refs[:N_ATTN_IN]
    hyena_in = refs[N_ATTN_IN:N_ATTN_IN + N_HYENA_IN]
    ret_in = refs[N_ATTN_IN + N_HYENA_IN:N_ATTN_IN + N_HYENA_IN + N_RET_IN]
    ya_ref, knew_ref, vnew_ref, yb_ref, yc_ref, st_ref, w_ref, vec_ref = refs[N_ATTN_IN + N_HYENA_IN + N_RET_IN:]
    _attn_ctx_kernel(*attn_in, ya_ref, knew_ref, vnew_ref, layer=layer)
    _hyena_kernel(*hyena_in, yb_ref, layer=layer, n=SEQ, aliased=False)
    _retention_kernel(*ret_in, yc_ref, st_ref, w_ref, vec_ref, layer=layer, n=SEQ, latent=False)


def mixers_ctx(p, kv, attn_lambda, subln_g, conv_w, conv_b, skip, gre, gim, tables, decay_exp, layer, threaded):
    cache_shape = jax.ShapeDtypeStruct((BATCH, DEPTH, SEQ, H_A, DV_A), F32)
    cache_spec = pl.BlockSpec((1, 1, SEQ, H_A, DV_A), lambda b: (b, layer, 0, 0, 0))
    branch_shape = jax.ShapeDtypeStruct((T_ALL, W_A), BF16)
    branch_spec = pl.BlockSpec((SEQ, W_A), lambda b: (b, 0))
    n_threaded = len(threaded)
    attn_specs = [
        pl.BlockSpec((SEQ, W_A), lambda b: (b, P_Q)),
        pl.BlockSpec((SEQ, W_A), lambda b: (b, 0)),
        pl.BlockSpec((SEQ, W_A), lambda b: (b, 1)),
        _full((DEPTH, 4, DH_A)), _full((DEPTH, DV_A)),
    ]
    return pl.pallas_call(
        functools.partial(_mixers_ctx_kernel, layer=layer, n_threaded=n_threaded),
        grid=(BATCH,),
        in_specs=([pl.BlockSpec(memory_space=pl.ANY)] * n_threaded + attn_specs
                  + _hyena_specs(SEQ, 0, layer) + _retention_specs(SEQ, 0)),
        out_specs=[branch_spec, cache_spec, cache_spec, branch_spec, branch_spec,
                   pl.BlockSpec((1, 1, 2, H_C, DK_C, DV_C), lambda b: (b, layer, 0, 0, 0, 0))],
        out_shape=[branch_shape, cache_shape, cache_shape, branch_shape, branch_shape,
                   jax.ShapeDtypeStruct((BATCH, DEPTH, 2, H_C, DK_C, DV_C), F32)],
        scratch_shapes=[pltpu.VMEM((H_C, SEQ, SEQ), F32), pltpu.VMEM((H_C, 2, SEQ, LANES), F32)],
        input_output_aliases=dict(zip(range(n_threaded), (1, 2, 5))),
        compiler_params=_params(1),
        name="mixers_ctx",
    )(*threaded, p, kv, kv, attn_lambda, subln_g, p, conv_w, conv_b, skip, gre, gim, *tables,
      p, p, p, p, decay_exp)


def _merge_kernel(x_ref, ya_ref, yb_ref, yc_ref, g_ref, m_ref, wa_ref, wb_ref, wc_ref, wo_ref,
                  o_ref, wbf_ref, wobf_ref):
    @pl.when(pl.program_id(0) == 0)
    def _():
        wbf_ref[0] = wa_ref[0].astype(BF16)
        wbf_ref[1] = wb_ref[0].astype(BF16)
        wbf_ref[2] = wc_ref[0].astype(BF16)
        wobf_ref[...] = wo_ref[0].astype(BF16)

    g = g_ref[...]
    merged = (g[:, :D_MODEL] * _dot(ya_ref[...], wbf_ref[0])
              + g[:, D_MODEL:2 * D_MODEL] * _dot(yb_ref[...], wbf_ref[1])
              + g[:, 2 * D_MODEL:] * _dot(yc_ref[...], wbf_ref[2]))
    g1 = m_ref[0, 0, 2:3, :]
    o_ref[...] = x_ref[...] + g1 * _dot(merged.astype(BF16), wobf_ref[...])


def merge_branches(x, ya, yb, yc, gates, mods, w_a, w_b, w_c, w_out, layer):
    tile = lambda w: pl.BlockSpec((TM, w), lambda i: (i, 0))
    wspec = lambda k: pl.BlockSpec((1, k, D_MODEL), lambda i: (layer, 0, 0))
    return pl.pallas_call(
        _merge_kernel,
        grid=(N_TILES,),
        in_specs=[
            tile(D_MODEL), tile(W_A), tile(C_B), tile(W_C), tile(3 * D_MODEL),
            pl.BlockSpec((1, 1, 6, D_MODEL), lambda i: (layer, _mod_row(i), 0, 0)),
            wspec(W_A), wspec(C_B), wspec(W_C), wspec(D_MODEL),
        ],
        out_specs=tile(D_MODEL),
        out_shape=jax.ShapeDtypeStruct((T_ALL, D_MODEL), F32),
        scratch_shapes=[pltpu.VMEM((3, W_A, D_MODEL), BF16), pltpu.VMEM((D_MODEL, D_MODEL), BF16)],
        compiler_params=_params(1),
        name="merge_branches",
    )(x, ya, yb, yc, gates, mods, w_a, w_b, w_c, w_out)


def _route(h2, wr_t, b_r):
    logits = _dot3_nt(wr_t, h2)
    m = jnp.max(logits, axis=0, keepdims=True)
    e = jnp.exp(logits - m)
    scores = e / jnp.sum(e, axis=0, keepdims=True)
    sel = scores + b_r
    rows = [sel[i:i + 1, :] for i in range(N_EXPERTS)]
    in_group = []
    gscore = []
    for g in range(N_GROUPS):
        members = range(g * EXP_PER_GROUP, (g + 1) * EXP_PER_GROUP)
        total = None
        for i in members:
            rank = None
            for j in members:
                if j == i:
                    continue
                ahead = (rows[j] >= rows[i]) if j < i else (rows[j] > rows[i])
                ahead = ahead.astype(F32)
                rank = ahead if rank is None else rank + ahead
            chosen = rank < 2.0
            in_group.append(chosen)
            part = jnp.where(chosen, rows[i], 0.0)
            total = part if total is None else total + part
        gscore.append(total)
    gates = []
    group_hot = []
    for g in range(N_GROUPS):
        best = None
        for g2 in range(N_GROUPS):
            if g2 == g:
                continue
            wins = gscore[g] > gscore[g2] if g2 < g else gscore[g] >= gscore[g2]
            best = wins if best is None else best & wins
        group_hot.append(best.astype(F32))
        for i in range(g * EXP_PER_GROUP, (g + 1) * EXP_PER_GROUP):
            gates.append(jnp.where(best & in_group[i], scores[i:i + 1, :], 0.0))
    gates = jnp.concatenate(gates, axis=0)
    return gates / jnp.sum(gates, axis=0, keepdims=True), group_hot


def _moe_route_kernel(x_ref, g_ref, m_ref, wr_ref, br_ref, tri_ref, tiles_ref, gate_ref, meta_ref, count_ref,
                      *, layer):
    y = _rms(x_ref[...]) * g_ref[layer:layer + 1, :]
    h2 = y * (1.0 + m_ref[0, 0, 4:5, :]) + m_ref[0, 0, 3:4, :]
    gates, group_hot = _route(h2, wr_ref[...], br_ref[...])
    for s in range(SUBLANES):
        tiles_ref[pl.ds(s, TM, stride=SUBLANES), :] = h2[:, s * LANES:(s + 1) * LANES]
    padded = jnp.concatenate([gates, jnp.zeros((LANES - N_EXPERTS, TM), F32)], axis=0)
    gate_ref[...] = padded.T
    row = lax.broadcasted_iota(jnp.int32, (SUBLANES, TM), 0)
    hot = jnp.zeros((SUBLANES, TM), F32)
    for g in range(N_GROUPS):
        hot = jnp.where(row == g, group_hot[g], hot)
    before = _dot(hot.astype(BF16), tri_ref[...])
    rank = jnp.sum(hot * before, axis=0, keepdims=True)
    gid = group_hot[1] + 2.0 * group_hot[2] + 3.0 * group_hot[3]
    meta = jnp.where(row == 0, gid, jnp.where(row == 1, rank, 0.0))
    meta_ref[0] = meta.astype(jnp.int32)
    counts = jnp.sum(hot, axis=1, keepdims=True) + jnp.zeros((SUBLANES, LANES), F32)
    count_ref[0] = counts.astype(jnp.int32)


def moe_route(x, gain, mods, wr_t, b_r, tri, layer):
    return pl.pallas_call(
        functools.partial(_moe_route_kernel, layer=layer),
        grid=(N_TILES,),
        in_specs=[
            pl.BlockSpec((TM, D_MODEL), lambda i: (i, 0)),
            pl.BlockSpec((DEPTH, D_MODEL), lambda i: (0, 0)),
            pl.BlockSpec((1, 1, 6, D_MODEL), lambda i: (layer, _mod_row(i), 0, 0)),
            pl.BlockSpec((N_EXPERTS, D_MODEL), lambda i: (0, 0)),
            pl.BlockSpec((N_EXPERTS, 1), lambda i: (0, 0)),
            pl.BlockSpec((TM, TM), lambda i: (0, 0)),
        ],
        out_specs=[pl.BlockSpec((TM * SUBLANES, LANES), lambda i: (i, 0)),
                   pl.BlockSpec((TM, LANES), lambda i: (i, 0)),
                   pl.BlockSpec((1, SUBLANES, TM), lambda i: (i, 0, 0)),
                   pl.BlockSpec((1, SUBLANES, LANES), lambda i: (i, 0, 0))],
        out_shape=[jax.ShapeDtypeStruct((T_ALL * SUBLANES, LANES), F32),
                   jax.ShapeDtypeStruct((T_ALL, LANES), F32),
                   jax.ShapeDtypeStruct((N_TILES, SUBLANES, TM), jnp.int32),
                   jax.ShapeDtypeStruct((N_TILES, SUBLANES, LANES), jnp.int32)],
        compiler_params=_params(1),
        name="moe_route",
    )(x, gain, mods, wr_t, b_r, tri)


def _moe_positions(meta, counts):
    gid = meta[:, 0, :]
    rank = meta[:, 1, :]
    cnt = counts[:, :N_GROUPS, 0]
    total = jnp.sum(cnt, axis=0)
    padded = (total + (MOE_BLOCK - 1)) // MOE_BLOCK * MOE_BLOCK
    group_start = jnp.cumsum(padded) - padded
    base = group_start[None, :] + jnp.cumsum(cnt, axis=0) - cnt
    pos = rank
    for g in range(N_GROUPS):
        pos = pos + jnp.where(gid == g, base[:, g:g + 1], 0)
    first_block = group_start // MOE_BLOCK
    blk = jnp.arange(MOE_BLOCKS, dtype=jnp.int32)
    block_gid = sum((blk >= first_block[g]).astype(jnp.int32) for g in range(1, N_GROUPS))
    block_used = (blk * MOE_BLOCK < jnp.sum(padded)).astype(jnp.int32)
    later = jnp.where(block_gid[None, :] > block_gid[:, None], block_gid[None, :], N_GROUPS)
    next_group = jnp.min(later, axis=1)
    next_group = jnp.where(next_group == N_GROUPS, -1, next_group).astype(jnp.int32)
    return pos.reshape(T_ALL).astype(jnp.int32), block_gid, block_used, next_group


def _moe_permute_kernel(pos_ref, tiles_ref, gate_ref, sorted_ref, gsorted_ref):
    i = pl.program_id(0)

    @pl.when(i == 0)
    def _():
        def zero(b, carry):
            start = pl.multiple_of(b * TM, TM)
            sorted_ref[pl.ds(start, TM), :] = jnp.zeros((TM, LANES), F32)
            return carry
        lax.fori_loop(0, T_PAD * SUBLANES // TM, zero, 0)
        gsorted_ref[...] = jnp.zeros((T_PAD, LANES), F32)

    def move(t, carry):
        p = pos_ref[i * TM + t]
        dst = pl.multiple_of(p * SUBLANES, SUBLANES)
        src = pl.multiple_of(t * SUBLANES, SUBLANES)
        sorted_ref[pl.ds(dst, SUBLANES), :] = tiles_ref[pl.ds(src, SUBLANES), :]
        gsorted_ref[pl.ds(p, 1), :] = gate_ref[pl.ds(t, 1), :]
        return carry
    lax.fori_loop(0, TM, move, 0, unroll=8)


def moe_permute(pos, tiles, gates):
    return pl.pallas_call(
        _moe_permute_kernel,
        grid_spec=pltpu.PrefetchScalarGridSpec(
            num_scalar_prefetch=1,
            grid=(N_TILES,),
            in_specs=[pl.BlockSpec((TM * SUBLANES, LANES), lambda i, pos: (i, 0)),
                      pl.BlockSpec((TM, LANES), lambda i, pos: (i, 0))],
            out_specs=[pl.BlockSpec(memory_space=pltpu.VMEM), pl.BlockSpec(memory_space=pltpu.VMEM)],
        ),
        out_shape=[jax.ShapeDtypeStruct((T_PAD * SUBLANES, LANES), F32),
                   jax.ShapeDtypeStruct((T_PAD, LANES), F32)],
        compiler_params=_params(1),
        name="moe_permute",
    )(pos, tiles, gates)


def _group_changed(gid_ref, b):
    return (b == 0) | (gid_ref[b] != gid_ref[jnp.maximum(b - 1, 0)])


def _moe_expert_kernel(gid_ref, used_ref, next_ref, s_ref, gate_ref, w1_hbm, w3_hbm, w2_ref, y_ref,
                       w1f_ref, w3f_ref, w1b_ref, w3b_ref, w2b_ref, sem, *, layer):
    b = pl.program_id(0)

    def up_weight_copies(group):
        return (pltpu.make_async_copy(w1_hbm.at[layer, group], w1f_ref, sem.at[0]),
                pltpu.make_async_copy(w3_hbm.at[layer, group], w3f_ref, sem.at[1]))

    @pl.when(b == 0)
    def _():
        for copy in up_weight_copies(gid_ref[0]):
            copy.start()

    @pl.when(_group_changed(gid_ref, b))
    def _():
        for copy in up_weight_copies(gid_ref[b]):
            copy.wait()
        for j in range(EXP_PER_GROUP):
            cols = slice(j * D_FF, (j + 1) * D_FF)
            w1b_ref[:, cols] = w1f_ref[j].astype(BF16)
            w3b_ref[:, cols] = w3f_ref[j].astype(BF16)
        w2b_ref[...] = w2_ref[0, 0].astype(BF16)

        @pl.when(next_ref[b] >= 0)
        def _():
            for copy in up_weight_copies(next_ref[b]):
                copy.start()

    @pl.when(used_ref[b] == 1)
    def _():
        lhs = jnp.concatenate([s_ref[pl.ds(s, MOE_BLOCK, stride=SUBLANES), :].astype(BF16)
                               for s in range(SUBLANES)], axis=1)
        a = _dot(lhs, w1b_ref[...])
        g = _dot(lhs, w3b_ref[...])
        gates = gate_ref[...]
        lane = lax.broadcasted_iota(jnp.int32, (MOE_BLOCK, LANES), 1)
        first = gid_ref[b] * EXP_PER_GROUP
        parts = []
        for j in range(EXP_PER_GROUP):
            cols = slice(j * D_FF, (j + 1) * D_FF)
            gate = jnp.sum(jnp.where(lane == first + j, gates, 0.0), axis=1, keepdims=True)
            parts.append((_silu(a[:, cols]) * g[:, cols] * gate).astype(BF16))
        act = jnp.concatenate(parts, axis=1)
        y = _dot(act, w2b_ref[...])
        for s in range(SUBLANES):
            y_ref[pl.ds(s, MOE_BLOCK, stride=SUBLANES), :] = y[:, s * LANES:(s + 1) * LANES]

    @pl.when(used_ref[b] == 0)
    def _():
        y_ref[...] = jnp.zeros_like(y_ref)


def moe_experts(block_gid, block_used, next_group, sorted_rows, sorted_gates, w1, w3, w2, layer):
    group_ff = EXP_PER_GROUP * D_FF
    w1g = w1.reshape(DEPTH, N_GROUPS, EXP_PER_GROUP, D_MODEL, D_FF)
    w3g = w3.reshape(DEPTH, N_GROUPS, EXP_PER_GROUP, D_MODEL, D_FF)
    w2g = w2.reshape(DEPTH, N_GROUPS, group_ff, D_MODEL)
    block = lambda b, gid, used, nxt: (b, 0)
    return pl.pallas_call(
        functools.partial(_moe_expert_kernel, layer=layer),
        grid_spec=pltpu.PrefetchScalarGridSpec(
            num_scalar_prefetch=3,
            grid=(MOE_BLOCKS,),
            in_specs=[pl.BlockSpec((MOE_BLOCK * SUBLANES, LANES), block),
                      pl.BlockSpec((MOE_BLOCK, LANES), block),
                      pl.BlockSpec(memory_space=pl.ANY), pl.BlockSpec(memory_space=pl.ANY),
                      pl.BlockSpec((1, 1, group_ff, D_MODEL), lambda b, gid, used, nxt: (layer, gid[b], 0, 0))],
            out_specs=pl.BlockSpec((MOE_BLOCK * SUBLANES, LANES), block),
            scratch_shapes=[pltpu.VMEM((EXP_PER_GROUP, D_MODEL, D_FF), F32),
                            pltpu.VMEM((EXP_PER_GROUP, D_MODEL, D_FF), F32),
                            pltpu.VMEM((D_MODEL, group_ff), BF16), pltpu.VMEM((D_MODEL, group_ff), BF16),
                            pltpu.VMEM((group_ff, D_MODEL), BF16),
                            pltpu.SemaphoreType.DMA((2,))],
        ),
        out_shape=jax.ShapeDtypeStruct((T_PAD * SUBLANES, LANES), F32),
        compiler_params=_params(1),
        name="moe_experts",
    )(block_gid, block_used, next_group, sorted_rows, sorted_gates, w1g, w3g, w2g)


FINAL_TM = 512


def _moe_combine_kernel(pos_ref, ys_ref, x_ref, m_ref, *rest, tm, final):
    i = pl.program_id(0)
    buf_ref = rest[-2] if final else rest[-1]

    def move(t, carry):
        src = pl.multiple_of(pos_ref[i * tm + t] * SUBLANES, SUBLANES)
        dst = pl.multiple_of(t * SUBLANES, SUBLANES)
        buf_ref[pl.ds(dst, SUBLANES), :] = ys_ref[pl.ds(src, SUBLANES), :]
        return carry
    lax.fori_loop(0, tm, move, 0, unroll=8)

    x_new_ref = rest[-1] if final else rest[0]
    for s in range(SUBLANES):
        cols = slice(s * LANES, (s + 1) * LANES)
        y = buf_ref[pl.ds(s, tm, stride=SUBLANES), :]
        x_new_ref[:, cols] = x_ref[:, cols] + m_ref[0, 0, 5:6, cols] * y

    if final:
        gain_ref, ctx_ref, lat_ref = rest[:3]
        out = _rms(x_new_ref[...]) * gain_ref[...]

        @pl.when(i < T_CTX // tm)
        def _():
            ctx_ref[...] = out

        @pl.when(i >= T_CTX // tm)
        def _():
            lat_ref[...] = out


def moe_combine(pos, y_sorted, x, mods, layer, final_gain=None):
    final = final_gain is not None
    tm = FINAL_TM if final else TM
    ctx_tiles = T_CTX // tm
    tile = pl.BlockSpec((tm, D_MODEL), lambda i, pos: (i, 0))
    in_specs = [
        pl.BlockSpec(memory_space=pltpu.VMEM),
        tile,
        pl.BlockSpec((1, 1, 6, D_MODEL), lambda i, pos: (layer, _mod_row(i * tm // TM), 0, 0)),
    ]
    args = [pos, y_sorted, x, mods]
    scratch = [pltpu.VMEM((tm * SUBLANES, LANES), F32)]
    if final:
        in_specs.append(pl.BlockSpec((1, D_MODEL), lambda i, pos: (0, 0)))
        args.append(final_gain.reshape(1, D_MODEL))
        out_specs = [pl.BlockSpec((tm, D_MODEL), lambda i, pos: (jnp.minimum(i, ctx_tiles - 1), 0)),
                     pl.BlockSpec((tm, D_MODEL), lambda i, pos: (jnp.maximum(i - ctx_tiles, 0), 0))]
        out_shape = [jax.ShapeDtypeStruct((T_CTX, D_MODEL), F32), jax.ShapeDtypeStruct((T_LAT, D_MODEL), F32)]
        scratch.append(pltpu.VMEM((tm, D_MODEL), F32))
    else:
        out_specs = tile
        out_shape = jax.ShapeDtypeStruct((T_ALL, D_MODEL), F32)
    return pl.pallas_call(
        functools.partial(_moe_combine_kernel, tm=tm, final=final),
        grid_spec=pltpu.PrefetchScalarGridSpec(
            num_scalar_prefetch=1,
            grid=(T_ALL // tm,),
            in_specs=in_specs,
            out_specs=out_specs,
            scratch_shapes=scratch,
        ),
        out_shape=out_shape,
        compiler_params=_params(1),
        name="moe_combine",
    )(*args)


def moe(x, gain, mods, wr_t, b_r, tri, w1, w3, w2, layer, final_gain=None):
    tiles, gates, meta, counts = moe_route(x, gain, mods, wr_t, b_r, tri, layer)
    pos, block_gid, block_used, next_group = _moe_positions(meta, counts)
    sorted_rows, sorted_gates = moe_permute(pos, tiles, gates)
    y_sorted = moe_experts(block_gid, block_used, next_group, sorted_rows, sorted_gates, w1, w3, w2, layer)
    return moe_combine(pos, y_sorted, x, mods, layer, final_gain)


def kernel(x_prompt, x_sample, cache_attn_k, cache_attn_v, state_retention, c, c_ctx, w_ada, b_ada, norm1_g, norm2_g, final_g, w_in, attn_lambda, attn_subln_g, hy_conv_w, hy_conv_b, hy_f_w1, hy_f_b1, hy_f_w2, hy_f_b2, hy_f_w3, hy_skip, ret_decay_exp, w_branch_a, w_branch_b, w_branch_c, w_gate, b_gate, w_out, w_router, b_router, moe_w1, moe_w3, moe_w2):
    x = jnp.concatenate([x_prompt.reshape(T_CTX, D_MODEL), x_sample.reshape(T_LAT, D_MODEL)], axis=0)
    cond8 = jnp.concatenate([c_ctx[None, :], c, jnp.zeros((8 - 1 - DEC_BATCH, D_MODEL), F32)], axis=0)
    mods = ada_modulation(cond8, w_ada, b_ada).reshape(DEPTH, 8, 6, D_MODEL)

    cos, sin = _rope_tables()
    cos_c, sin_c = cos[:, :H_C * DK_C], sin[:, :H_C * DK_C]
    cache_k = cache_attn_k.reshape(DEC_BATCH, DEPTH, PAST_LEN, W_A)
    cache_v = cache_attn_v.reshape(DEC_BATCH, DEPTH, PAST_LEN, W_A)
    decay_rates = _filter_decay_rates()
    tables_ctx = _dft_tables(SEQ)
    tables_lat = _dft_tables(DEC_SEQ)
    filt_ctx = hyena_filters(SEQ, _filter_features(SEQ), decay_rates, tables_ctx[0], tables_ctx[1],
                             hy_f_w1, hy_f_b1, hy_f_w2, hy_f_b2, hy_f_w3)
    filt_lat = hyena_filters(DEC_SEQ, _filter_features(DEC_SEQ), decay_rates, tables_lat[0], tables_lat[1],
                             hy_f_w1, hy_f_b1, hy_f_w2, hy_f_b2, hy_f_w3)
    wr_t = w_router.T
    b_r = b_router.reshape(N_EXPERTS, 1)
    tri = jnp.asarray(np.triu(np.ones((TM, TM), np.float32), 1), dtype=BF16)

    threaded = ()
    for l in range(DEPTH):
        p, kv = in_projection(x, norm1_g, mods, w_in, w_gate, b_gate, l)

        ya, new_k, new_v, yb, yc, states = mixers_ctx(
            p, kv, attn_lambda, attn_subln_g, hy_conv_w, hy_conv_b, hy_skip, filt_ctx[0], filt_ctx[1],
            tables_ctx, ret_decay_exp, l, threaded)
        threaded = (new_k, new_v, states)
        ya = attention_lat(ya, p, kv, cache_k, cache_v, cos, sin, attn_lambda, attn_subln_g, l)
        yb = hyena_lat(yb, p, hy_conv_w, hy_conv_b, hy_skip, filt_lat[0], filt_lat[1], tables_lat, l)
        yc = retention_lat(yc, p, ret_decay_exp, cos_c, sin_c, state_retention, l)

        x = merge_branches(x, ya, yb, yc, p, mods, w_branch_a, w_branch_b, w_branch_c, w_out, l)
        x = moe(x, norm2_g, mods, wr_t, b_r, tri, moe_w1, moe_w3, moe_w2, l,
                final_g if l == DEPTH - 1 else None)

    y_prompt, y_sample = x
    return (y_prompt.reshape(BATCH, SEQ, D_MODEL), y_sample.reshape(DEC_BATCH, DEC_SEQ, D_MODEL),
            new_k, new_v, states)
```

```python
import functools
import math

import jax
import jax.numpy as jnp
import numpy as np
from jax import lax
from jax.experimental import pallas as pl
from jax.experimental.pallas import tpu as pltpu

F32 = jnp.float32
BF16 = jnp.bfloat16

D_MODEL = 1024
BATCH = 16
SEQ = 256
DEPTH = 4
DEC_BATCH = 2
DEC_SEQ = 1024
PAST_LEN = 256
GRID_W = 64
EPS = 1e-6
ROPE_BASE = 10000.0
H_A = 4
DH_A = 64
DV_A = 128
W_A = 512
C_B = 512
SHORT_K = 3
H_EMB = 33
H_FFN = 64
H_FAST_DECAY = 0.3
H_SLOW_DECAY = 1.5
H_TARGET = 1e-2
H_C = 4
DK_C = 64
DV_C = 128
W_C = 512
N_EXPERTS = 16
N_GROUPS = 4
EXP_PER_GROUP = 4
D_FF = 512
D_IN = 4608

T_CTX = BATCH * SEQ
T_LAT = DEC_BATCH * DEC_SEQ
T_ALL = T_CTX + T_LAT
TM = 1024
N_TILES = T_ALL // TM
CTX_TILES = T_CTX // TM
LANES = 128
SUBLANES = 8
MOE_BLOCK = 256
MOE_BLOCKS = T_ALL // MOE_BLOCK + N_GROUPS
T_PAD = MOE_BLOCKS * MOE_BLOCK
VMEM_LIMIT = 56 * 1024 * 1024


def _params(n_axes):
    return pltpu.CompilerParams(
        dimension_semantics=("arbitrary",) * n_axes, vmem_limit_bytes=VMEM_LIMIT)


def _mod_row(i):
    return jnp.maximum(i - (CTX_TILES - 1), 0)


def _rms(x):
    return x * lax.rsqrt(jnp.mean(x * x, axis=-1, keepdims=True) + EPS)


def _sigmoid(x):
    return 0.5 * jnp.tanh(0.5 * x) + 0.5


def _silu(x):
    return x * _sigmoid(x)


def _dot(a, b):
    return jnp.dot(a, b, preferred_element_type=F32)


def _dot_nt(a, b):
    return lax.dot_general(a, b, (((1,), (1,)), ((), ())), preferred_element_type=F32)


def _dot_tn(a, b):
    return lax.dot_general(a, b, (((0,), (0,)), ((), ())), preferred_element_type=F32)


def _split3(x):
    hi = x.astype(BF16)
    lo = (x - hi.astype(F32)).astype(BF16)
    return hi, lo


def _dot3(a, b):
    ah, al = _split3(a)
    bh, bl = _split3(b)
    return _dot(ah, bh) + (_dot(ah, bl) + _dot(al, bh))


def _dot3_nt(a, b):
    ah, al = _split3(a)
    bh, bl = _split3(b)
    return _dot_nt(ah, bh) + (_dot_nt(ah, bl) + _dot_nt(al, bh))


def _ada_kernel(c_ref, w_ref, b_ref, o_ref):
    s = _silu(c_ref[...])
    o_ref[0] = _dot(s.astype(BF16), w_ref[0].astype(BF16)) + b_ref[0]


def ada_modulation(cond8, w_ada, b_ada):
    tn = 1536
    n = 6 * D_MODEL
    return pl.pallas_call(
        _ada_kernel,
        grid=(DEPTH, n // tn),
        in_specs=[
            pl.BlockSpec((8, D_MODEL), lambda l, j: (0, 0)),
            pl.BlockSpec((1, D_MODEL, tn), lambda l, j: (l, 0, j)),
            pl.BlockSpec((1, 1, tn), lambda l, j: (l, 0, j)),
        ],
        out_specs=pl.BlockSpec((1, 8, tn), lambda l, j: (l, 0, j)),
        out_shape=jax.ShapeDtypeStruct((DEPTH, 8, n), F32),
        compiler_params=_params(2),
        name="ada_modulation",
    )(cond8, w_ada, b_ada.reshape(DEPTH, 1, n))


PROJ_TM = 3072
PROJ_TN = 512
GATE_BLOCKS = 3 * D_MODEL // PROJ_TN
IN_BLOCKS = D_IN // PROJ_TN
P_WIDTH = 3 * D_MODEL + D_IN
KV_FIRST = GATE_BLOCKS + 1
P_Q = 3 * D_MODEL // W_A
P_HYENA = (3 * D_MODEL + 3 * W_A) // (3 * C_B)
P_RET_Q = (3 * D_MODEL + 3 * W_A + 3 * C_B) // (H_C * DK_C)
P_RET_V = (3 * D_MODEL + 3 * W_A + 3 * C_B + 2 * H_C * DK_C) // W_C


def _in_proj_kernel(x_hbm, g_ref, m_ref, win_ref, wg_ref, bg_ref, p_ref, kv_ref, h_ref, x_ref, sem, *, layer):
    i = pl.program_id(0)
    j = pl.program_id(1)

    def x_copy(tile):
        start = pl.multiple_of(tile * PROJ_TM, PROJ_TM)
        return pltpu.make_async_copy(x_hbm.at[pl.ds(start, PROJ_TM), :], x_ref, sem.at[0])

    @pl.when((i == 0) & (j == 0))
    def _():
        x_copy(0).start()

    @pl.when(j == 0)
    def _():
        x_copy(i).wait()
        for s in range(PROJ_TM // TM):
            rows = slice(s * TM, (s + 1) * TM)
            mod = m_ref[layer, _mod_row(i * (PROJ_TM // TM) + s)]
            y = _rms(x_ref[rows, :]) * g_ref[layer:layer + 1, :]
            h_ref[rows, :] = (y * (1.0 + mod[1:2, :]) + mod[0:1, :]).astype(BF16)

        @pl.when(i + 1 < T_ALL // PROJ_TM)
        def _():
            x_copy(i + 1).start()

    @pl.when(j < GATE_BLOCKS)
    def _():
        acc = _dot(h_ref[...], wg_ref[0].astype(BF16)) + bg_ref[0]
        p_ref[...] = _sigmoid(acc).astype(BF16)

    @pl.when(j >= GATE_BLOCKS)
    def _():
        acc = _dot(h_ref[...], win_ref[0].astype(BF16))
        p_ref[...] = acc.astype(BF16)

        @pl.when((j == KV_FIRST) | (j == KV_FIRST + 1))
        def _():
            kv_ref[...] = acc


def in_projection(x, gain, mods, w_in, w_gate, b_gate, layer):
    return pl.pallas_call(
        functools.partial(_in_proj_kernel, layer=layer),
        grid=(T_ALL // PROJ_TM, GATE_BLOCKS + IN_BLOCKS),
        in_specs=[
            pl.BlockSpec(memory_space=pl.ANY),
            pl.BlockSpec((DEPTH, D_MODEL), lambda i, j: (0, 0)),
            pl.BlockSpec((DEPTH, 8, 6, D_MODEL), lambda i, j: (0, 0, 0, 0)),
            pl.BlockSpec((1, D_MODEL, PROJ_TN), lambda i, j: (layer, 0, jnp.maximum(j - GATE_BLOCKS, 0))),
            pl.BlockSpec((1, D_MODEL, PROJ_TN), lambda i, j: (layer, 0, jnp.minimum(j, GATE_BLOCKS - 1))),
            pl.BlockSpec((1, 1, PROJ_TN), lambda i, j: (layer, 0, jnp.minimum(j, GATE_BLOCKS - 1))),
        ],
        out_specs=[
            pl.BlockSpec((PROJ_TM, PROJ_TN), lambda i, j: (i, j)),
            pl.BlockSpec((PROJ_TM, PROJ_TN), lambda i, j: (i, jnp.clip(j - KV_FIRST, 0, 1))),
        ],
        out_shape=[jax.ShapeDtypeStruct((T_ALL, P_WIDTH), BF16),
                   jax.ShapeDtypeStruct((T_ALL, 2 * W_A), F32)],
        scratch_shapes=[pltpu.VMEM((PROJ_TM, D_MODEL), BF16), pltpu.VMEM((PROJ_TM, D_MODEL), F32),
                        pltpu.SemaphoreType.DMA((1,))],
        compiler_params=_params(2),
        name="in_projection",
    )(x, gain, mods, w_in, w_gate, b_gate.reshape(DEPTH, 1, 3 * D_MODEL))


def _rope_tables():
    n = DEC_SEQ
    rows = n // GRID_W
    row = jnp.repeat(jnp.arange(rows, dtype=F32), GRID_W)
    col = jnp.tile(jnp.arange(GRID_W, dtype=F32), rows)
    quarter = DH_A // 4
    inv = ROPE_BASE ** (-jnp.arange(quarter, dtype=F32) / quarter)
    ar = row[:, None] * inv
    ac = col[:, None] * inv
    ang = jnp.concatenate([ar, ar, ac, ac], axis=-1)
    cos = jnp.tile(jnp.cos(ang), (1, 512 // DH_A))
    sgn = jnp.tile(jnp.concatenate([-jnp.ones((quarter,), F32), jnp.ones((quarter,), F32)]), 512 // (2 * quarter))
    sin = jnp.tile(jnp.sin(ang), (1, 512 // DH_A)) * sgn
    return cos, sin


def _rope(x, cos, sin):
    w = x.shape[-1]
    lane = lax.broadcasted_iota(jnp.int32, x.shape, 1)
    first = (lane % 32) < 16
    rot = jnp.where(first, pltpu.roll(x, w - 16, 1), pltpu.roll(x, 16, 1))
    return x * cos + rot * sin


def _lambda(al_ref, layer):
    al = al_ref[layer]
    a = jnp.sum(al[0:1, :] * al[1:2, :], axis=-1, keepdims=True)
    b = jnp.sum(al[2:3, :] * al[3:4, :], axis=-1, keepdims=True)
    lam_init = 0.8 - 0.6 * math.exp(-0.3 * layer)
    return jnp.exp(a) - jnp.exp(b) + lam_init, lam_init


def _scores(q, k):
    return _dot_nt(q, k).astype(BF16)


def _softmax_numerators(s):
    return jnp.exp2(s - jnp.max(s, axis=-1, keepdims=True))


def _diff_attention(q, k, v, lam, lam_init, subln, o_ref):
    lane = lax.broadcasted_iota(jnp.int32, q.shape, 1)
    q = q * (DH_A ** -0.5 * math.log2(math.e))
    q0 = jnp.where((lane % LANES) < DH_A, q, 0.0).astype(BF16)
    q1 = jnp.where((lane % LANES) >= DH_A, q, 0.0).astype(BF16)
    ones = jnp.ones((k.shape[0], LANES), BF16)
    for h in range(H_A):
        sl = slice(h * LANES, (h + 1) * LANES)
        kh = k[:, sl]
        v_ones = jnp.concatenate([v[:, sl], ones], axis=1)
        o0 = _dot(_softmax_numerators(_scores(q0[:, sl], kh)), v_ones)
        o1 = _dot(_softmax_numerators(_scores(q1[:, sl], kh)), v_ones)
        r0 = 1.0 / o0[:, DV_A:DV_A + 1]
        r1 = lam / o1[:, DV_A:DV_A + 1]
        o = o0[:, :DV_A] * r0 - o1[:, :DV_A] * r1
        o = _rms(o) * subln * (1.0 - lam_init)
        o_ref[:, sl] = o.astype(o_ref.dtype)


def _attn_ctx_kernel(q_ref, k_ref, v_ref, al_ref, g_ref, o_ref, knew_ref, vnew_ref, *, layer):
    lam, lam_init = _lambda(al_ref, layer)
    subln = g_ref[layer:layer + 1, :]
    k = k_ref[...]
    v = v_ref[...]
    for h in range(H_A):
        knew_ref[0, 0, :, h, :] = k[:, h * LANES:(h + 1) * LANES]
        vnew_ref[0, 0, :, h, :] = v[:, h * LANES:(h + 1) * LANES]
    _diff_attention(q_ref[...].astype(F32), k.astype(BF16), v.astype(BF16), lam, lam_init, subln, o_ref)


QB = 256


def _attn_lat_kernel(ya_ref, q_ref, k_ref, v_ref, ck_ref, cv_ref, cosq_ref, sinq_ref, cos_ref, sin_ref,
                     al_ref, g_ref, o_ref, kall_ref, vall_ref, *, layer):
    del ya_ref

    @pl.when(pl.program_id(1) == 0)
    def _():
        kall_ref[0:PAST_LEN, :] = ck_ref[0, 0].astype(BF16)
        vall_ref[0:PAST_LEN, :] = cv_ref[0, 0].astype(BF16)
        kall_ref[PAST_LEN:, :] = _rope(k_ref[...], cos_ref[...], sin_ref[...]).astype(BF16)
        vall_ref[PAST_LEN:, :] = v_ref[...].astype(BF16)

    lam, lam_init = _lambda(al_ref, layer)
    subln = g_ref[layer:layer + 1, :]
    q = _rope(q_ref[...].astype(F32), cosq_ref[...], sinq_ref[...])
    _diff_attention(q, kall_ref[...], vall_ref[...], lam, lam_init, subln, o_ref)


def attention_lat(ya, q, kv, cache_k, cache_v, cos, sin, attn_lambda, subln_g, layer):
    nqb = DEC_SEQ // QB
    row0 = T_CTX // QB
    seq0 = T_CTX // DEC_SEQ
    return pl.pallas_call(
        functools.partial(_attn_lat_kernel, layer=layer),
        grid=(DEC_BATCH, nqb),
        in_specs=[
            pl.BlockSpec(memory_space=pl.ANY),
            pl.BlockSpec((QB, W_A), lambda b, i: (row0 + b * nqb + i, P_Q)),
            pl.BlockSpec((DEC_SEQ, W_A), lambda b, i: (seq0 + b, 0)),
            pl.BlockSpec((DEC_SEQ, W_A), lambda b, i: (seq0 + b, 1)),
            pl.BlockSpec((1, 1, PAST_LEN, W_A), lambda b, i: (b, layer, 0, 0)),
            pl.BlockSpec((1, 1, PAST_LEN, W_A), lambda b, i: (b, layer, 0, 0)),
            pl.BlockSpec((QB, W_A), lambda b, i: (i, 0)),
            pl.BlockSpec((QB, W_A), lambda b, i: (i, 0)),
            pl.BlockSpec((DEC_SEQ, W_A), lambda b, i: (0, 0)),
            pl.BlockSpec((DEC_SEQ, W_A), lambda b, i: (0, 0)),
            pl.BlockSpec((DEPTH, 4, DH_A), lambda b, i: (0, 0, 0)),
            pl.BlockSpec((DEPTH, DV_A), lambda b, i: (0, 0)),
        ],
        out_specs=pl.BlockSpec((QB, W_A), lambda b, i: (row0 + b * nqb + i, 0)),
        out_shape=jax.ShapeDtypeStruct((T_ALL, W_A), BF16),
        scratch_shapes=[pltpu.VMEM((PAST_LEN + DEC_SEQ, W_A), BF16),
                        pltpu.VMEM((PAST_LEN + DEC_SEQ, W_A), BF16)],
        input_output_aliases={0: 0},
        compiler_params=_params(2),
        name="attention_lat",
    )(ya, q, kv, kv, cache_k, cache_v, cos, sin, cos, sin, attn_lambda, subln_g)


def _dft_tables(n):
    k = np.arange(n, dtype=np.int64)
    prod = (2 * k[:, None] + 1) * k[None, :]
    ang = (prod % (4 * n)).astype(np.float64) * (math.pi / (2 * n))
    c = np.cos(ang).astype(np.float32)
    s = np.sin(ang).astype(np.float32)
    return tuple(jnp.asarray(t).astype(BF16) for t in (c, s, c.T, s.T))


def _filter_features(n):
    t = jnp.linspace(0.0, 1.0, n, dtype=F32)[:, None]
    bands = (H_EMB - 1) // 2
    w = 2.0 * math.pi * jnp.arange(n, dtype=F32)[:, None] / n
    fr = jnp.linspace(1e-4, bands - 1, bands, dtype=F32)
    z = jnp.concatenate([t, jnp.cos(w * fr), -jnp.sin(w * fr)], axis=-1)
    return jnp.pad(z, ((0, 0), (0, LANES - H_EMB)))


def _filter_decay_rates():
    max_decay = math.log(H_TARGET) / H_FAST_DECAY
    min_decay = math.log(H_TARGET) / H_SLOW_DECAY
    return jnp.abs(jnp.linspace(min_decay, max_decay, C_B, dtype=F32))[None, :]


def _hyena_filter_kernel(z_ref, w1_ref, b1_ref, w2_ref, b2_ref, w3_ref, dr_ref, c_ref, s_ref,
                         gre_ref, gim_ref):
    z = z_ref[...]
    hid = jnp.sin(_dot3(z, w1_ref[0]) + b1_ref[0])
    hid = jnp.sin(_dot3(hid, w2_ref[0]) + b2_ref[0])
    h = _dot(hid.astype(BF16), w3_ref[0].astype(BF16))
    window = jnp.exp(-z[:, 0:1] * dr_ref[...])
    hf = h[:, :C_B] * window
    hb = h[:, C_B:] * window
    total = jnp.sum(jnp.abs(hf) + jnp.abs(hb), axis=0, keepdims=True)
    hf = hf / total
    hb = hb / total
    row = lax.broadcasted_iota(jnp.int32, hb.shape, 0)
    hb = jnp.where(row == 0, 0.0, hb)
    gre_ref[0] = _dot(c_ref[...], (hf + hb).astype(BF16))
    gim_ref[0] = _dot(s_ref[...], (hb - hf).astype(BF16))


def hyena_filters(n, z, decay_rates, cf, sf, f_w1, f_b1, f_w2, f_b2, f_w3):
    w1 = jnp.pad(f_w1, ((0, 0), (0, LANES - H_EMB), (0, 0)))
    full = lambda shape: pl.BlockSpec(shape, lambda l: (0,) * len(shape))
    per_layer = lambda shape: pl.BlockSpec((1,) + shape, lambda l: (l,) + (0,) * len(shape))
    return pl.pallas_call(
        _hyena_filter_kernel,
        grid=(DEPTH,),
        in_specs=[
            full((n, LANES)),
            per_layer((LANES, H_FFN)), per_layer((1, H_FFN)),
            per_layer((H_FFN, H_FFN)), per_layer((1, H_FFN)),
            per_layer((H_FFN, 2 * C_B)),
            full((1, C_B)), full((n, n)), full((n, n)),
        ],
        out_specs=[per_layer((n, C_B)), per_layer((n, C_B))],
        out_shape=[jax.ShapeDtypeStruct((DEPTH, n, C_B), F32)] * 2,
        compiler_params=_params(1),
        name="hyena_filters",
    )(z, w1, f_b1.reshape(DEPTH, 1, H_FFN), f_w2, f_b2.reshape(DEPTH, 1, H_FFN), f_w3,
      decay_rates, cf, sf)


def _hyena_kernel(*refs, layer, n, aliased):
    if aliased:
        refs = refs[1:]
    u_ref, cw_ref, cb_ref, skip_ref, gre_ref, gim_ref, c_ref, s_ref, ct_ref, st_ref, o_ref = refs
    u = u_ref[...].astype(F32)
    row = lax.broadcasted_iota(jnp.int32, u.shape, 0)
    prev = jnp.where(row == 0, 0.0, pltpu.roll(u, 1, 0))
    nxt = jnp.where(row == n - 1, 0.0, pltpu.roll(u, n - 1, 0))
    cw = cw_ref[layer]
    uc = cb_ref[layer:layer + 1, :] + prev * cw[0:1, :] + u * cw[1:2, :] + nxt * cw[2:3, :]
    x0 = uc[:, :C_B]
    x1 = uc[:, C_B:2 * C_B]
    v = uc[:, 2 * C_B:]
    w = v * x1
    wb = w.astype(BF16)
    ure = _dot(c_ref[...], wb)
    uim = -_dot(s_ref[...], wb)
    gre = gre_ref[0]
    gim = gim_ref[0]
    yre = (ure * gre - uim * gim).astype(BF16)
    yim = (ure * gim + uim * gre).astype(BF16)
    y = (_dot(ct_ref[...], yre) - _dot(st_ref[...], yim)) * (1.0 / n)
    y = y + w * skip_ref[layer:layer + 1, :]
    o_ref[...] = (x0 * y).astype(o_ref.dtype)


def _full(shape):
    return pl.BlockSpec(shape, lambda *_: (0,) * len(shape))


def _hyena_specs(n, seq0, layer):
    return [
        pl.BlockSpec((n, 3 * C_B), lambda b, *_: (seq0 + b, P_HYENA)),
        _full((DEPTH, SHORT_K, 3 * C_B)), _full((DEPTH, 3 * C_B)), _full((DEPTH, C_B)),
        pl.BlockSpec((1, n, C_B), lambda *_: (layer, 0, 0)),
        pl.BlockSpec((1, n, C_B), lambda *_: (layer, 0, 0)),
        _full((n, n)), _full((n, n)), _full((n, n)), _full((n, n)),
    ]


def hyena_lat(yb, p, conv_w, conv_b, skip, gre, gim, tables, layer):
    seq0 = T_CTX // DEC_SEQ
    return pl.pallas_call(
        functools.partial(_hyena_kernel, layer=layer, n=DEC_SEQ, aliased=True),
        grid=(DEC_BATCH,),
        in_specs=[pl.BlockSpec(memory_space=pl.ANY)] + _hyena_specs(DEC_SEQ, seq0, layer),
        out_specs=pl.BlockSpec((DEC_SEQ, C_B), lambda b: (seq0 + b, 0)),
        out_shape=jax.ShapeDtypeStruct((T_ALL, C_B), BF16),
        input_output_aliases={0: 0},
        compiler_params=_params(1),
        name="hyena_lat",
    )(yb, p, conv_w, conv_b, skip, gre, gim, *tables)


RET_QB = 256


def _retention_kernel(*refs, layer, n, latent):
    if latent:
        _, q_ref, k_ref, v_ref, g_ref, de_ref, cos_ref, sin_ref, s0_ref, o_ref, w_ref, vec_ref = refs
        st_ref = None
    else:
        q_ref, k_ref, v_ref, g_ref, de_ref, o_ref, st_ref, w_ref, vec_ref = refs[-9:]
    log_gamma = jnp.log1p(-jnp.exp2(-de_ref[layer]))

    @pl.when(pl.program_id(0) == 0)
    def _():
        t = lax.broadcasted_iota(jnp.int32, (n, n), 0)
        s = lax.broadcasted_iota(jnp.int32, (n, n), 1)
        lag = (t - s).astype(F32)
        for h in range(H_C):
            rate = jnp.where(lag > 0, log_gamma[0:1, h:h + 1], log_gamma[1:2, h:h + 1])
            w_ref[h] = jnp.where(lag == 0, 2.0, jnp.exp(rate * jnp.abs(lag)))
            pos = lax.broadcasted_iota(jnp.int32, (n, LANES), 0).astype(F32)
            steps = (pos + 1.0, float(n) - pos) if latent else (float(n - 1) - pos, pos)
            for d in range(2):
                vec_ref[h, d] = jnp.exp(log_gamma[d:d + 1, h:h + 1] * steps[d])

    q = q_ref[...].astype(F32)
    k = k_ref[...].astype(F32) * (DK_C ** -0.5)
    if latent:
        q = _rope(q, cos_ref[...], sin_ref[...])
        k = _rope(k, cos_ref[...], sin_ref[...])
    v = v_ref[...].astype(BF16)
    gate = g_ref[...].astype(F32)
    lane = lax.broadcasted_iota(jnp.int32, (n, LANES), 1)
    zeros64 = jnp.zeros((DK_C, DV_C), F32)

    for h in range(H_C):
        pair = slice((h // 2) * LANES, (h // 2 + 1) * LANES)
        lo = (h % 2) * DK_C
        own = (lane >= lo) & (lane < lo + DK_C)
        vs = slice(h * DV_C, (h + 1) * DV_C)
        qh = jnp.where(own, q[:, pair], 0.0).astype(BF16)
        kh = jnp.where(own, k[:, pair], 0.0)
        kb = kh.astype(BF16)
        vh = v[:, vs]
        if latent:
            s0 = [jnp.concatenate([s0_ref[0, 0, d, h], zeros64] if lo == 0 else [zeros64, s0_ref[0, 0, d, h]],
                                  axis=0).astype(BF16) for d in range(2)]
            carry = _dot(qh, s0[0]) * vec_ref[h, 0] + _dot(qh, s0[1]) * vec_ref[h, 1]
        for r0 in range(0, n, RET_QB):
            rows = slice(r0, r0 + RET_QB)
            att = _dot_nt(qh[rows], kb) * w_ref[h, rows, :]
            out = _dot(att.astype(BF16), vh)
            if latent:
                out = out + carry[rows]
            o_ref[rows, vs] = (_rms(out) * _silu(gate[rows, vs])).astype(o_ref.dtype)
        if st_ref is not None:
            sf = _dot_tn((kh * vec_ref[h, 0]).astype(BF16), vh)
            sb = _dot_tn((kh * vec_ref[h, 1]).astype(BF16), vh)
            st_ref[0, 0, 0, h] = sf[lo:lo + DK_C, :]
            st_ref[0, 0, 1, h] = sb[lo:lo + DK_C, :]


def _retention_specs(n, seq0):
    return [
        pl.BlockSpec((n, H_C * DK_C), lambda b, *_: (seq0 + b, P_RET_Q)),
        pl.BlockSpec((n, H_C * DK_C), lambda b, *_: (seq0 + b, P_RET_Q + 1)),
        pl.BlockSpec((n, W_C), lambda b, *_: (seq0 + b, P_RET_V)),
        pl.BlockSpec((n, W_C), lambda b, *_: (seq0 + b, P_RET_V + 1)),
        _full((DEPTH, 2, H_C)),
    ]


def retention_lat(yc, p, decay_exp, cos, sin, state0, layer):
    n = DEC_SEQ
    seq0 = T_CTX // DEC_SEQ
    return pl.pallas_call(
        functools.partial(_retention_kernel, layer=layer, n=n, latent=True),
        grid=(DEC_BATCH,),
        in_specs=[pl.BlockSpec(memory_space=pl.ANY)] + _retention_specs(n, seq0) + [
            _full((n, H_C * DK_C)), _full((n, H_C * DK_C)),
            pl.BlockSpec((1, 1, 2, H_C, DK_C, DV_C), lambda b: (b, layer, 0, 0, 0, 0)),
        ],
        out_specs=pl.BlockSpec((n, W_C), lambda b: (seq0 + b, 0)),
        out_shape=jax.ShapeDtypeStruct((T_ALL, W_C), BF16),
        scratch_shapes=[pltpu.VMEM((H_C, n, n), F32), pltpu.VMEM((H_C, 2, n, LANES), F32)],
        input_output_aliases={0: 0},
        compiler_params=_params(1),
        name="retention_lat",
    )(yc, p, p, p, p, decay_exp, cos, sin, state0)


N_ATTN_IN = 5
N_HYENA_IN = 10
N_RET_IN = 5


def _mixers_ctx_kernel(*refs, layer, n_threaded):
    refs = refs[n_threaded:]
    attn_in = refs[:N_ATTN_IN]
    hyena_in = refs[N_ATTN_IN:N_ATTN_IN + N_HYENA_IN]
    ret_in = refs[N_ATTN_IN + N_HYENA_IN:N_ATTN_IN + N_HYENA_IN + N_RET_IN]
    ya_ref, knew_ref, vnew_ref, yb_ref, yc_ref, st_ref, w_ref, vec_ref = refs[N_ATTN_IN + N_HYENA_IN + N_RET_IN:]
    _attn_ctx_kernel(*attn_in, ya_ref, knew_ref, vnew_ref, layer=layer)
    _hyena_kernel(*hyena_in, yb_ref, layer=layer, n=SEQ, aliased=False)
    _retention_kernel(*ret_in, yc_ref, st_ref, w_ref, vec_ref, layer=layer, n=SEQ, latent=False)


def mixers_ctx(p, kv, attn_lambda, subln_g, conv_w, conv_b, skip, gre, gim, tables, decay_exp, layer, threaded):
    cache_shape = jax.ShapeDtypeStruct((BATCH, DEPTH, SEQ, H_A, DV_A), F32)
    cache_spec = pl.BlockSpec((1, 1, SEQ, H_A, DV_A), lambda b: (b, layer, 0, 0, 0))
    branch_shape = jax.ShapeDtypeStruct((T_ALL, W_A), BF16)
    branch_spec = pl.BlockSpec((SEQ, W_A), lambda b: (b, 0))
    n_threaded = len(threaded)
    attn_specs = [
        pl.BlockSpec((SEQ, W_A), lambda b: (b, P_Q)),
        pl.BlockSpec((SEQ, W_A), lambda b: (b, 0)),
        pl.BlockSpec((SEQ, W_A), lambda b: (b, 1)),
        _full((DEPTH, 4, DH_A)), _full((DEPTH, DV_A)),
    ]
    return pl.pallas_call(
        functools.partial(_mixers_ctx_kernel, layer=layer, n_threaded=n_threaded),
        grid=(BATCH,),
        in_specs=([pl.BlockSpec(memory_space=pl.ANY)] * n_threaded + attn_specs
                  + _hyena_specs(SEQ, 0, layer) + _retention_specs(SEQ, 0)),
        out_specs=[branch_spec, cache_spec, cache_spec, branch_spec, branch_spec,
                   pl.BlockSpec((1, 1, 2, H_C, DK_C, DV_C), lambda b: (b, layer, 0, 0, 0, 0))],
        out_shape=[branch_shape, cache_shape, cache_shape, branch_shape, branch_shape,
                   jax.ShapeDtypeStruct((BATCH, DEPTH, 2, H_C, DK_C, DV_C), F32)],
        scratch_shapes=[pltpu.VMEM((H_C, SEQ, SEQ), F32), pltpu.VMEM((H_C, 2, SEQ, LANES), F32)],
        input_output_aliases=dict(zip(range(n_threaded), (1, 2, 5))),
        compiler_params=_params(1),
        name="mixers_ctx",
    )(*threaded, p, kv, kv, attn_lambda, subln_g, p, conv_w, conv_b, skip, gre, gim, *tables,
      p, p, p, p, decay_exp)


def _merge_kernel(x_ref, ya_ref, yb_ref, yc_ref, g_ref, m_ref, wa_ref, wb_ref, wc_ref, wo_ref,
                  o_ref, wbf_ref, wobf_ref):
    @pl.when(pl.program_id(0) == 0)
    def _():
        wbf_ref[0] = wa_ref[0].astype(BF16)
        wbf_ref[1] = wb_ref[0].astype(BF16)
        wbf_ref[2] = wc_ref[0].astype(BF16)
        wobf_ref[...] = wo_ref[0].astype(BF16)

    g = g_ref[...]
    merged = (g[:, :D_MODEL] * _dot(ya_ref[...], wbf_ref[0])
              + g[:, D_MODEL:2 * D_MODEL] * _dot(yb_ref[...], wbf_ref[1])
              + g[:, 2 * D_MODEL:] * _dot(yc_ref[...], wbf_ref[2]))
    g1 = m_ref[0, 0, 2:3, :]
    o_ref[...] = x_ref[...] + g1 * _dot(merged.astype(BF16), wobf_ref[...])


def merge_branches(x, ya, yb, yc, gates, mods, w_a, w_b, w_c, w_out, layer):
    tile = lambda w: pl.BlockSpec((TM, w), lambda i: (i, 0))
    wspec = lambda k: pl.BlockSpec((1, k, D_MODEL), lambda i: (layer, 0, 0))
    return pl.pallas_call(
        _merge_kernel,
        grid=(N_TILES,),
        in_specs=[
            tile(D_MODEL), tile(W_A), tile(C_B), tile(W_C), tile(3 * D_MODEL),
            pl.BlockSpec((1, 1, 6, D_MODEL), lambda i: (layer, _mod_row(i), 0, 0)),
            wspec(W_A), wspec(C_B), wspec(W_C), wspec(D_MODEL),
        ],
        out_specs=tile(D_MODEL),
        out_shape=jax.ShapeDtypeStruct((T_ALL, D_MODEL), F32),
        scratch_shapes=[pltpu.VMEM((3, W_A, D_MODEL), BF16), pltpu.VMEM((D_MODEL, D_MODEL), BF16)],
        compiler_params=_params(1),
        name="merge_branches",
    )(x, ya, yb, yc, gates, mods, w_a, w_b, w_c, w_out)


def _route(h2, wr_t, b_r):
    logits = _dot3_nt(wr_t, h2)
    m = jnp.max(logits, axis=0, keepdims=True)
    e = jnp.exp(logits - m)
    scores = e / jnp.sum(e, axis=0, keepdims=True)
    sel = scores + b_r
    rows = [sel[i:i + 1, :] for i in range(N_EXPERTS)]
    in_group = []
    gscore = []
    for g in range(N_GROUPS):
        members = range(g * EXP_PER_GROUP, (g + 1) * EXP_PER_GROUP)
        total = None
        for i in members:
            rank = None
            for j in members:
                if j == i:
                    continue
                ahead = (rows[j] >= rows[i]) if j < i else (rows[j] > rows[i])
                ahead = ahead.astype(F32)
                rank = ahead if rank is None else rank + ahead
            chosen = rank < 2.0
            in_group.append(chosen)
            part = jnp.where(chosen, rows[i], 0.0)
            total = part if total is None else total + part
        gscore.append(total)
    gates = []
    group_hot = []
    for g in range(N_GROUPS):
        best = None
        for g2 in range(N_GROUPS):
            if g2 == g:
                continue
            wins = gscore[g] > gscore[g2] if g2 < g else gscore[g] >= gscore[g2]
            best = wins if best is None else best & wins
        group_hot.append(best.astype(F32))
        for i in range(g * EXP_PER_GROUP, (g + 1) * EXP_PER_GROUP):
            gates.append(jnp.where(best & in_group[i], scores[i:i + 1, :], 0.0))
    gates = jnp.concatenate(gates, axis=0)
    return gates / jnp.sum(gates, axis=0, keepdims=True), group_hot


def _moe_route_kernel(x_ref, g_ref, m_ref, wr_ref, br_ref, tri_ref, tiles_ref, gate_ref, meta_ref, count_ref,
                      *, layer):
    y = _rms(x_ref[...]) * g_ref[layer:layer + 1, :]
    h2 = y * (1.0 + m_ref[0, 0, 4:5, :]) + m_ref[0, 0, 3:4, :]
    gates, group_hot = _route(h2, wr_ref[...], br_ref[...])
    for s in range(SUBLANES):
        tiles_ref[pl.ds(s, TM, stride=SUBLANES), :] = h2[:, s * LANES:(s + 1) * LANES]
    padded = jnp.concatenate([gates, jnp.zeros((LANES - N_EXPERTS, TM), F32)], axis=0)
    gate_ref[...] = padded.T
    row = lax.broadcasted_iota(jnp.int32, (SUBLANES, TM), 0)
    hot = jnp.zeros((SUBLANES, TM), F32)
    for g in range(N_GROUPS):
        hot = jnp.where(row == g, group_hot[g], hot)
    before = _dot(hot.astype(BF16), tri_ref[...])
    rank = jnp.sum(hot * before, axis=0, keepdims=True)
    gid = group_hot[1] + 2.0 * group_hot[2] + 3.0 * group_hot[3]
    meta = jnp.where(row == 0, gid, jnp.where(row == 1, rank, 0.0))
    meta_ref[0] = meta.astype(jnp.int32)
    counts = jnp.sum(hot, axis=1, keepdims=True) + jnp.zeros((SUBLANES, LANES), F32)
    count_ref[0] = counts.astype(jnp.int32)


def moe_route(x, gain, mods, wr_t, b_r, tri, layer):
    return pl.pallas_call(
        functools.partial(_moe_route_kernel, layer=layer),
        grid=(N_TILES,),
        in_specs=[
            pl.BlockSpec((TM, D_MODEL), lambda i: (i, 0)),
            pl.BlockSpec((DEPTH, D_MODEL), lambda i: (0, 0)),
            pl.BlockSpec((1, 1, 6, D_MODEL), lambda i: (layer, _mod_row(i), 0, 0)),
            pl.BlockSpec((N_EXPERTS, D_MODEL), lambda i: (0, 0)),
            pl.BlockSpec((N_EXPERTS, 1), lambda i: (0, 0)),
            pl.BlockSpec((TM, TM), lambda i: (0, 0)),
        ],
        out_specs=[pl.BlockSpec((TM * SUBLANES, LANES), lambda i: (i, 0)),
                   pl.BlockSpec((TM, LANES), lambda i: (i, 0)),
                   pl.BlockSpec((1, SUBLANES, TM), lambda i: (i, 0, 0)),
                   pl.BlockSpec((1, SUBLANES, LANES), lambda i: (i, 0, 0))],
        out_shape=[jax.ShapeDtypeStruct((T_ALL * SUBLANES, LANES), F32),
                   jax.ShapeDtypeStruct((T_ALL, LANES), F32),
                   jax.ShapeDtypeStruct((N_TILES, SUBLANES, TM), jnp.int32),
                   jax.ShapeDtypeStruct((N_TILES, SUBLANES, LANES), jnp.int32)],
        compiler_params=_params(1),
        name="moe_route",
    )(x, gain, mods, wr_t, b_r, tri)


def _moe_positions(meta, counts):
    gid = meta[:, 0, :]
    rank = meta[:, 1, :]
    cnt = counts[:, :N_GROUPS, 0]
    shift = MOE_BLOCK.bit_length() - 1
    total = functools.reduce(lambda a, b: a + b, [cnt[i] for i in range(N_TILES)])
    padded = ((total + (MOE_BLOCK - 1)) >> shift) << shift
    starts = [jnp.zeros((), jnp.int32)]
    for g in range(N_GROUPS):
        starts.append(starts[-1] + padded[g])
    group_start = jnp.stack(starts[:N_GROUPS])
    rows = [group_start]
    for i in range(N_TILES - 1):
        rows.append(rows[-1] + cnt[i])
    base = jnp.stack(rows)
    pos = rank
    for g in range(N_GROUPS):
        pos = pos + jnp.where(gid == g, base[:, g:g + 1], 0)
    first_block = group_start >> shift
    blk = jnp.arange(MOE_BLOCKS, dtype=jnp.int32)
    block_gid = sum((blk >= first_block[g]).astype(jnp.int32) for g in range(1, N_GROUPS))
    block_used = (blk * MOE_BLOCK < starts[N_GROUPS]).astype(jnp.int32)
    later = jnp.where(block_gid[None, :] > block_gid[:, None], block_gid[None, :], N_GROUPS)
    next_group = jnp.min(later, axis=1)
    next_group = jnp.where(next_group == N_GROUPS, -1, next_group).astype(jnp.int32)
    return pos.reshape(T_ALL).astype(jnp.int32), block_gid, block_used, next_group


def _moe_permute_kernel(pos_ref, tiles_ref, gate_ref, sorted_ref, gsorted_ref):
    i = pl.program_id(0)

    @pl.when(i == 0)
    def _():
        def zero(b, carry):
            start = pl.multiple_of(b * TM, TM)
            sorted_ref[pl.ds(start, TM), :] = jnp.zeros((TM, LANES), F32)
            return carry
        lax.fori_loop(0, T_PAD * SUBLANES // TM, zero, 0)
        gsorted_ref[...] = jnp.zeros((T_PAD, LANES), F32)

    def move(t, carry):
        p = pos_ref[i * TM + t]
        dst = pl.multiple_of(p * SUBLANES, SUBLANES)
        src = pl.multiple_of(t * SUBLANES, SUBLANES)
        sorted_ref[pl.ds(dst, SUBLANES), :] = tiles_ref[pl.ds(src, SUBLANES), :]
        gsorted_ref[pl.ds(p, 1), :] = gate_ref[pl.ds(t, 1), :]
        return carry
    lax.fori_loop(0, TM, move, 0, unroll=8)


def moe_permute(pos, tiles, gates):
    return pl.pallas_call(
        _moe_permute_kernel,
        grid_spec=pltpu.PrefetchScalarGridSpec(
            num_scalar_prefetch=1,
            grid=(N_TILES,),
            in_specs=[pl.BlockSpec((TM * SUBLANES, LANES), lambda i, pos: (i, 0)),
                      pl.BlockSpec((TM, LANES), lambda i, pos: (i, 0))],
            out_specs=[pl.BlockSpec(memory_space=pltpu.VMEM), pl.BlockSpec(memory_space=pltpu.VMEM)],
        ),
        out_shape=[jax.ShapeDtypeStruct((T_PAD * SUBLANES, LANES), F32),
                   jax.ShapeDtypeStruct((T_PAD, LANES), F32)],
        compiler_params=_params(1),
        name="moe_permute",
    )(pos, tiles, gates)


def _group_changed(gid_ref, b):
    return (b == 0) | (gid_ref[b] != gid_ref[jnp.maximum(b - 1, 0)])


def _moe_expert_kernel(gid_ref, used_ref, next_ref, s_ref, gate_ref, w1_hbm, w3_hbm, w2_ref, y_ref,
                       w1f_ref, w3f_ref, w1b_ref, w3b_ref, w2b_ref, sem, *, layer):
    b = pl.program_id(0)

    def up_weight_copies(group):
        return (pltpu.make_async_copy(w1_hbm.at[layer, group], w1f_ref, sem.at[0]),
                pltpu.make_async_copy(w3_hbm.at[layer, group], w3f_ref, sem.at[1]))

    @pl.when(b == 0)
    def _():
        for copy in up_weight_copies(gid_ref[0]):
            copy.start()

    @pl.when(_group_changed(gid_ref, b))
    def _():
        for copy in up_weight_copies(gid_ref[b]):
            copy.wait()
        for j in range(EXP_PER_GROUP):
            cols = slice(j * D_FF, (j + 1) * D_FF)
            w1b_ref[:, cols] = w1f_ref[j].astype(BF16)
            w3b_ref[:, cols] = w3f_ref[j].astype(BF16)
        w2b_ref[...] = w2_ref[0, 0].astype(BF16)

        @pl.when(next_ref[b] >= 0)
        def _():
            for copy in up_weight_copies(next_ref[b]):
                copy.start()

    @pl.when(used_ref[b] == 1)
    def _():
        lhs = jnp.concatenate([s_ref[pl.ds(s, MOE_BLOCK, stride=SUBLANES), :].astype(BF16)
                               for s in range(SUBLANES)], axis=1)
        a = _dot(lhs, w1b_ref[...])
        g = _dot(lhs, w3b_ref[...])
        gates = gate_ref[...]
        lane = lax.broadcasted_iota(jnp.int32, (MOE_BLOCK, LANES), 1)
        first = gid_ref[b] * EXP_PER_GROUP
        parts = []
        for j in range(EXP_PER_GROUP):
            cols = slice(j * D_FF, (j + 1) * D_FF)
            gate = jnp.sum(jnp.where(lane == first + j, gates, 0.0), axis=1, keepdims=True)
            parts.append((_silu(a[:, cols]) * g[:, cols] * gate).astype(BF16))
        act = jnp.concatenate(parts, axis=1)
        y = _dot(act, w2b_ref[...])
        for s in range(SUBLANES):
            y_ref[pl.ds(s, MOE_BLOCK, stride=SUBLANES), :] = y[:, s * LANES:(s + 1) * LANES]

    @pl.when(used_ref[b] == 0)
    def _():
        y_ref[...] = jnp.zeros_like(y_ref)


def moe_experts(block_gid, block_used, next_group, sorted_rows, sorted_gates, w1, w3, w2, layer):
    group_ff = EXP_PER_GROUP * D_FF
    w1g = w1.reshape(DEPTH, N_GROUPS, EXP_PER_GROUP, D_MODEL, D_FF)
    w3g = w3.reshape(DEPTH, N_GROUPS, EXP_PER_GROUP, D_MODEL, D_FF)
    w2g = w2.reshape(DEPTH, N_GROUPS, group_ff, D_MODEL)
    block = lambda b, gid, used, nxt: (b, 0)
    return pl.pallas_call(
        functools.partial(_moe_expert_kernel, layer=layer),
        grid_spec=pltpu.PrefetchScalarGridSpec(
            num_scalar_prefetch=3,
            grid=(MOE_BLOCKS,),
            in_specs=[pl.BlockSpec((MOE_BLOCK * SUBLANES, LANES), block),
                      pl.BlockSpec((MOE_BLOCK, LANES), block),
                      pl.BlockSpec(memory_space=pl.ANY), pl.BlockSpec(memory_space=pl.ANY),
                      pl.BlockSpec((1, 1, group_ff, D_MODEL), lambda b, gid, used, nxt: (layer, gid[b], 0, 0))],
            out_specs=pl.BlockSpec((MOE_BLOCK * SUBLANES, LANES), block),
            scratch_shapes=[pltpu.VMEM((EXP_PER_GROUP, D_MODEL, D_FF), F32),
                            pltpu.VMEM((EXP_PER_GROUP, D_MODEL, D_FF), F32),
                            pltpu.VMEM((D_MODEL, group_ff), BF16), pltpu.VMEM((D_MODEL, group_ff), BF16),
                            pltpu.VMEM((group_ff, D_MODEL), BF16),
                            pltpu.SemaphoreType.DMA((2,))],
        ),
        out_shape=jax.ShapeDtypeStruct((T_PAD * SUBLANES, LANES), F32),
        compiler_params=_params(1),
        name="moe_experts",
    )(block_gid, block_used, next_group, sorted_rows, sorted_gates, w1g, w3g, w2g)


FINAL_TM = 512


def _moe_combine_kernel(pos_ref, ys_ref, x_ref, m_ref, *rest, tm, final):
    i = pl.program_id(0)
    buf_ref = rest[-2] if final else rest[-1]

    def move(t, carry):
        src = pl.multiple_of(pos_ref[i * tm + t] * SUBLANES, SUBLANES)
        dst = pl.multiple_of(t * SUBLANES, SUBLANES)
        buf_ref[pl.ds(dst, SUBLANES), :] = ys_ref[pl.ds(src, SUBLANES), :]
        return carry
    lax.fori_loop(0, tm, move, 0, unroll=8)

    x_new_ref = rest[-1] if final else rest[0]
    for s in range(SUBLANES):
        cols = slice(s * LANES, (s + 1) * LANES)
        y = buf_ref[pl.ds(s, tm, stride=SUBLANES), :]
        x_new_ref[:, cols] = x_ref[:, cols] + m_ref[0, 0, 5:6, cols] * y

    if final:
        gain_ref, ctx_ref, lat_ref = rest[:3]
        out = _rms(x_new_ref[...]) * gain_ref[...]

        @pl.when(i < T_CTX // tm)
        def _():
            ctx_ref[...] = out

        @pl.when(i >= T_CTX // tm)
        def _():
            lat_ref[...] = out


def moe_combine(pos, y_sorted, x, mods, layer, final_gain=None):
    final = final_gain is not None
    tm = FINAL_TM if final else TM
    ctx_tiles = T_CTX // tm
    tile = pl.BlockSpec((tm, D_MODEL), lambda i, pos: (i, 0))
    in_specs = [
        pl.BlockSpec(memory_space=pltpu.VMEM),
        tile,
        pl.BlockSpec((1, 1, 6, D_MODEL), lambda i, pos: (layer, _mod_row(i * tm // TM), 0, 0)),
    ]
    args = [pos, y_sorted, x, mods]
    scratch = [pltpu.VMEM((tm * SUBLANES, LANES), F32)]
    if final:
        in_specs.append(pl.BlockSpec((1, D_MODEL), lambda i, pos: (0, 0)))
        args.append(final_gain.reshape(1, D_MODEL))
        out_specs = [pl.BlockSpec((tm, D_MODEL), lambda i, pos: (jnp.minimum(i, ctx_tiles - 1), 0)),
                     pl.BlockSpec((tm, D_MODEL), lambda i, pos: (jnp.maximum(i - ctx_tiles, 0), 0))]
        out_shape = [jax.ShapeDtypeStruct((T_CTX, D_MODEL), F32), jax.ShapeDtypeStruct((T_LAT, D_MODEL), F32)]
        scratch.append(pltpu.VMEM((tm, D_MODEL), F32))
    else:
        out_specs = tile
        out_shape = jax.ShapeDtypeStruct((T_ALL, D_MODEL), F32)
    return pl.pallas_call(
        functools.partial(_moe_combine_kernel, tm=tm, final=final),
        grid_spec=pltpu.PrefetchScalarGridSpec(
            num_scalar_prefetch=1,
            grid=(T_ALL // tm,),
            in_specs=in_specs,
            out_specs=out_specs,
            scratch_shapes=scratch,
        ),
        out_shape=out_shape,
        compiler_params=_params(1),
        name="moe_combine",
    )(*args)


def moe(x, gain, mods, wr_t, b_r, tri, w1, w3, w2, layer, final_gain=None):
    tiles, gates, meta, counts = moe_route(x, gain, mods, wr_t, b_r, tri, layer)
    pos, block_gid, block_used, next_group = _moe_positions(meta, counts)
    sorted_rows, sorted_gates = moe_permute(pos, tiles, gates)
    y_sorted = moe_experts(block_gid, block_used, next_group, sorted_rows, sorted_gates, w1, w3, w2, layer)
    return moe_combine(pos, y_sorted, x, mods, layer, final_gain)


def kernel(x_prompt, x_sample, cache_attn_k, cache_attn_v, state_retention, c, c_ctx, w_ada, b_ada, norm1_g, norm2_g, final_g, w_in, attn_lambda, attn_subln_g, hy_conv_w, hy_conv_b, hy_f_w1, hy_f_b1, hy_f_w2, hy_f_b2, hy_f_w3, hy_skip, ret_decay_exp, w_branch_a, w_branch_b, w_branch_c, w_gate, b_gate, w_out, w_router, b_router, moe_w1, moe_w3, moe_w2):
    x = jnp.concatenate([x_prompt.reshape(T_CTX, D_MODEL), x_sample.reshape(T_LAT, D_MODEL)], axis=0)
    cond8 = jnp.concatenate([c_ctx[None, :], c, jnp.zeros((8 - 1 - DEC_BATCH, D_MODEL), F32)], axis=0)
    mods = ada_modulation(cond8, w_ada, b_ada).reshape(DEPTH, 8, 6, D_MODEL)

    cos, sin = _rope_tables()
    cos_c, sin_c = cos[:, :H_C * DK_C], sin[:, :H_C * DK_C]
    cache_k = cache_attn_k.reshape(DEC_BATCH, DEPTH, PAST_LEN, W_A)
    cache_v = cache_attn_v.reshape(DEC_BATCH, DEPTH, PAST_LEN, W_A)
    decay_rates = _filter_decay_rates()
    tables_ctx = _dft_tables(SEQ)
    tables_lat = _dft_tables(DEC_SEQ)
    filt_ctx = hyena_filters(SEQ, _filter_features(SEQ), decay_rates, tables_ctx[0], tables_ctx[1],
                             hy_f_w1, hy_f_b1, hy_f_w2, hy_f_b2, hy_f_w3)
    filt_lat = hyena_filters(DEC_SEQ, _filter_features(DEC_SEQ), decay_rates, tables_lat[0], tables_lat[1],
                             hy_f_w1, hy_f_b1, hy_f_w2, hy_f_b2, hy_f_w3)
    wr_t = w_router.T
    b_r = b_router.reshape(N_EXPERTS, 1)
    tri = jnp.asarray(np.triu(np.ones((TM, TM), np.float32), 1), dtype=BF16)

    threaded = ()
    for l in range(DEPTH):
        p, kv = in_projection(x, norm1_g, mods, w_in, w_gate, b_gate, l)

        ya, new_k, new_v, yb, yc, states = mixers_ctx(
            p, kv, attn_lambda, attn_subln_g, hy_conv_w, hy_conv_b, hy_skip, filt_ctx[0], filt_ctx[1],
            tables_ctx, ret_decay_exp, l, threaded)
        threaded = (new_k, new_v, states)
        ya = attention_lat(ya, p, kv, cache_k, cache_v, cos, sin, attn_lambda, attn_subln_g, l)
        yb = hyena_lat(yb, p, hy_conv_w, hy_conv_b, hy_skip, filt_lat[0], filt_lat[1], tables_lat, l)
        yc = retention_lat(yc, p, ret_decay_exp, cos_c, sin_c, state_retention, l)

        x = merge_branches(x, ya, yb, yc, p, mods, w_branch_a, w_branch_b, w_branch_c, w_out, l)
        x = moe(x, norm2_g, mods, wr_t, b_r, tri, moe_w1, moe_w3, moe_w2, l,
                final_g if l == DEPTH - 1 else None)

    y_prompt, y_sample = x
    return (y_prompt.reshape(BATCH, SEQ, D_MODEL), y_sample.reshape(DEC_BATCH, DEC_SEQ, D_MODEL),
            new_k, new_v, states)
```

```python
import functools
import math

import jax
import jax.numpy as jnp
import numpy as np
from jax import lax
from jax.experimental import pallas as pl
from jax.experimental.pallas import tpu as pltpu

F32 = jnp.float32
BF16 = jnp.bfloat16

D_MODEL = 1024
BATCH = 16
SEQ = 256
DEPTH = 4
DEC_BATCH = 2
DEC_SEQ = 1024
PAST_LEN = 256
GRID_W = 64
EPS = 1e-6
ROPE_BASE = 10000.0
H_A = 4
DH_A = 64
DV_A = 128
W_A = 512
C_B = 512
SHORT_K = 3
H_EMB = 33
H_FFN = 64
H_FAST_DECAY = 0.3
H_SLOW_DECAY = 1.5
H_TARGET = 1e-2
H_C = 4
DK_C = 64
DV_C = 128
W_C = 512
N_EXPERTS = 16
N_GROUPS = 4
EXP_PER_GROUP = 4
D_FF = 512
D_IN = 4608

T_CTX = BATCH * SEQ
T_LAT = DEC_BATCH * DEC_SEQ
T_ALL = T_CTX + T_LAT
TM = 1024
N_TILES = T_ALL // TM
CTX_TILES = T_CTX // TM
LANES = 128
SUBLANES = 8
MOE_BLOCK = 256
MOE_BLOCKS = T_ALL // MOE_BLOCK + N_GROUPS
T_PAD = MOE_BLOCKS * MOE_BLOCK
VMEM_LIMIT = 56 * 1024 * 1024


def _params(n_axes):
    return pltpu.CompilerParams(
        dimension_semantics=("arbitrary",) * n_axes, vmem_limit_bytes=VMEM_LIMIT)


def _mod_row(i):
    return jnp.maximum(i - (CTX_TILES - 1), 0)


def _rms(x):
    return x * lax.rsqrt(jnp.mean(x * x, axis=-1, keepdims=True) + EPS)


def _sigmoid(x):
    return 0.5 * jnp.tanh(0.5 * x) + 0.5


def _silu(x):
    return x * _sigmoid(x)


def _dot(a, b):
    return jnp.dot(a, b, preferred_element_type=F32)


def _dot_nt(a, b):
    return lax.dot_general(a, b, (((1,), (1,)), ((), ())), preferred_element_type=F32)


def _dot_tn(a, b):
    return lax.dot_general(a, b, (((0,), (0,)), ((), ())), preferred_element_type=F32)


def _split3(x):
    hi = x.astype(BF16)
    lo = (x - hi.astype(F32)).astype(BF16)
    return hi, lo


def _dot3(a, b):
    ah, al = _split3(a)
    bh, bl = _split3(b)
    return _dot(ah, bh) + (_dot(ah, bl) + _dot(al, bh))


def _dot3_nt(a, b):
    ah, al = _split3(a)
    bh, bl = _split3(b)
    return _dot_nt(ah, bh) + (_dot_nt(ah, bl) + _dot_nt(al, bh))


def _ada_kernel(c_ref, w_ref, b_ref, o_ref):
    s = _silu(c_ref[...])
    o_ref[0] = _dot(s.astype(BF16), w_ref[0].astype(BF16)) + b_ref[0]


def ada_modulation(cond8, w_ada, b_ada):
    tn = 1536
    n = 6 * D_MODEL
    return pl.pallas_call(
        _ada_kernel,
        grid=(DEPTH, n // tn),
        in_specs=[
            pl.BlockSpec((8, D_MODEL), lambda l, j: (0, 0)),
            pl.BlockSpec((1, D_MODEL, tn), lambda l, j: (l, 0, j)),
            pl.BlockSpec((1, 1, tn), lambda l, j: (l, 0, j)),
        ],
        out_specs=pl.BlockSpec((1, 8, tn), lambda l, j: (l, 0, j)),
        out_shape=jax.ShapeDtypeStruct((DEPTH, 8, n), F32),
        compiler_params=_params(2),
        name="ada_modulation",
    )(cond8, w_ada, b_ada.reshape(DEPTH, 1, n))


PROJ_TM = 3072
PROJ_TN = 512
GATE_BLOCKS = 3 * D_MODEL // PROJ_TN
IN_BLOCKS = D_IN // PROJ_TN
P_WIDTH = 3 * D_MODEL + D_IN
KV_FIRST = GATE_BLOCKS + 1
P_Q = 3 * D_MODEL // W_A
P_HYENA = (3 * D_MODEL + 3 * W_A) // (3 * C_B)
P_RET_Q = (3 * D_MODEL + 3 * W_A + 3 * C_B) // (H_C * DK_C)
P_RET_V = (3 * D_MODEL + 3 * W_A + 3 * C_B + 2 * H_C * DK_C) // W_C


def _in_proj_kernel(x_hbm, g_ref, m_ref, win_ref, wg_ref, bg_ref, p_ref, kv_ref, h_ref, x_ref, sem, *, layer):
    i = pl.program_id(0)
    j = pl.program_id(1)

    def x_copy(tile):
        start = pl.multiple_of(tile * PROJ_TM, PROJ_TM)
        return pltpu.make_async_copy(x_hbm.at[pl.ds(start, PROJ_TM), :], x_ref, sem.at[0])

    @pl.when((i == 0) & (j == 0))
    def _():
        x_copy(0).start()

    @pl.when(j == 0)
    def _():
        x_copy(i).wait()
        for s in range(PROJ_TM // TM):
            rows = slice(s * TM, (s + 1) * TM)
            mod = m_ref[layer, _mod_row(i * (PROJ_TM // TM) + s)]
            y = _rms(x_ref[rows, :]) * g_ref[layer:layer + 1, :]
            h_ref[rows, :] = (y * (1.0 + mod[1:2, :]) + mod[0:1, :]).astype(BF16)

        @pl.when(i + 1 < T_ALL // PROJ_TM)
        def _():
            x_copy(i + 1).start()

    @pl.when(j < GATE_BLOCKS)
    def _():
        acc = _dot(h_ref[...], wg_ref[0].astype(BF16)) + bg_ref[0]
        p_ref[...] = _sigmoid(acc).astype(BF16)

    @pl.when(j >= GATE_BLOCKS)
    def _():
        acc = _dot(h_ref[...], win_ref[0].astype(BF16))
        p_ref[...] = acc.astype(BF16)

        @pl.when((j == KV_FIRST) | (j == KV_FIRST + 1))
        def _():
            kv_ref[...] = acc


def in_projection(x, gain, mods, w_in, w_gate, b_gate, layer):
    return pl.pallas_call(
        functools.partial(_in_proj_kernel, layer=layer),
        grid=(T_ALL // PROJ_TM, GATE_BLOCKS + IN_BLOCKS),
        in_specs=[
            pl.BlockSpec(memory_space=pl.ANY),
            pl.BlockSpec((DEPTH, D_MODEL), lambda i, j: (0, 0)),
            pl.BlockSpec((DEPTH, 8, 6, D_MODEL), lambda i, j: (0, 0, 0, 0)),
            pl.BlockSpec((1, D_MODEL, PROJ_TN), lambda i, j: (layer, 0, jnp.maximum(j - GATE_BLOCKS, 0))),
            pl.BlockSpec((1, D_MODEL, PROJ_TN), lambda i, j: (layer, 0, jnp.minimum(j, GATE_BLOCKS - 1))),
            pl.BlockSpec((1, 1, PROJ_TN), lambda i, j: (layer, 0, jnp.minimum(j, GATE_BLOCKS - 1))),
        ],
        out_specs=[
            pl.BlockSpec((PROJ_TM, PROJ_TN), lambda i, j: (i, j)),
            pl.BlockSpec((PROJ_TM, PROJ_TN), lambda i, j: (i, jnp.clip(j - KV_FIRST, 0, 1))),
        ],
        out_shape=[jax.ShapeDtypeStruct((T_ALL, P_WIDTH), BF16),
                   jax.ShapeDtypeStruct((T_ALL, 2 * W_A), F32)],
        scratch_shapes=[pltpu.VMEM((PROJ_TM, D_MODEL), BF16), pltpu.VMEM((PROJ_TM, D_MODEL), F32),
                        pltpu.SemaphoreType.DMA((1,))],
        compiler_params=_params(2),
        name="in_projection",
    )(x, gain, mods, w_in, w_gate, b_gate.reshape(DEPTH, 1, 3 * D_MODEL))


def _rope_tables():
    n = DEC_SEQ
    rows = n // GRID_W
    row = jnp.repeat(jnp.arange(rows, dtype=F32), GRID_W)
    col = jnp.tile(jnp.arange(GRID_W, dtype=F32), rows)
    quarter = DH_A // 4
    inv = ROPE_BASE ** (-jnp.arange(quarter, dtype=F32) / quarter)
    ar = row[:, None] * inv
    ac = col[:, None] * inv
    ang = jnp.concatenate([ar, ar, ac, ac], axis=-1)
    cos = jnp.tile(jnp.cos(ang), (1, 512 // DH_A))
    sgn = jnp.tile(jnp.concatenate([-jnp.ones((quarter,), F32), jnp.ones((quarter,), F32)]), 512 // (2 * quarter))
    sin = jnp.tile(jnp.sin(ang), (1, 512 // DH_A)) * sgn
    return cos, sin


def _rope(x, cos, sin):
    w = x.shape[-1]
    lane = lax.broadcasted_iota(jnp.int32, x.shape, 1)
    first = (lane % 32) < 16
    rot = jnp.where(first, pltpu.roll(x, w - 16, 1), pltpu.roll(x, 16, 1))
    return x * cos + rot * sin


def _lambda(al_ref, layer):
    al = al_ref[layer]
    a = jnp.sum(al[0:1, :] * al[1:2, :], axis=-1, keepdims=True)
    b = jnp.sum(al[2:3, :] * al[3:4, :], axis=-1, keepdims=True)
    lam_init = 0.8 - 0.6 * math.exp(-0.3 * layer)
    return jnp.exp(a) - jnp.exp(b) + lam_init, lam_init


def _scores(q, k):
    return _dot_nt(q, k).astype(BF16)


def _softmax_numerators(s):
    return jnp.exp2(s - jnp.max(s, axis=-1, keepdims=True))


def _diff_attention(q, k, v, lam, lam_init, subln, o_ref):
    lane = lax.broadcasted_iota(jnp.int32, q.shape, 1)
    q = q * (DH_A ** -0.5 * math.log2(math.e))
    q0 = jnp.where((lane % LANES) < DH_A, q, 0.0).astype(BF16)
    q1 = jnp.where((lane % LANES) >= DH_A, q, 0.0).astype(BF16)
    ones = jnp.ones((k.shape[0], LANES), BF16)
    for h in range(H_A):
        sl = slice(h * LANES, (h + 1) * LANES)
        kh = k[:, sl]
        v_ones = jnp.concatenate([v[:, sl], ones], axis=1)
        o0 = _dot(_softmax_numerators(_scores(q0[:, sl], kh)), v_ones)
        o1 = _dot(_softmax_numerators(_scores(q1[:, sl], kh)), v_ones)
        r0 = 1.0 / o0[:, DV_A:DV_A + 1]
        r1 = lam / o1[:, DV_A:DV_A + 1]
        o = o0[:, :DV_A] * r0 - o1[:, :DV_A] * r1
        o = _rms(o) * subln * (1.0 - lam_init)
        o_ref[:, sl] = o.astype(o_ref.dtype)


def _attn_ctx_kernel(q_ref, k_ref, v_ref, al_ref, g_ref, o_ref, knew_ref, vnew_ref, *, layer):
    lam, lam_init = _lambda(al_ref, layer)
    subln = g_ref[layer:layer + 1, :]
    k = k_ref[...]
    v = v_ref[...]
    for h in range(H_A):
        knew_ref[0, 0, :, h, :] = k[:, h * LANES:(h + 1) * LANES]
        vnew_ref[0, 0, :, h, :] = v[:, h * LANES:(h + 1) * LANES]
    _diff_attention(q_ref[...].astype(F32), k.astype(BF16), v.astype(BF16), lam, lam_init, subln, o_ref)


QB = 256


def _attn_lat_kernel(ya_ref, q_ref, k_ref, v_ref, ck_ref, cv_ref, cosq_ref, sinq_ref, cos_ref, sin_ref,
                     al_ref, g_ref, o_ref, kall_ref, vall_ref, *, layer):
    del ya_ref

    @pl.when(pl.program_id(1) == 0)
    def _():
        for h in range(H_A):
            sl = slice(h * LANES, (h + 1) * LANES)
            kall_ref[0:PAST_LEN, sl] = ck_ref[0, 0, :, h, :].astype(BF16)
            vall_ref[0:PAST_LEN, sl] = cv_ref[0, 0, :, h, :].astype(BF16)
        kall_ref[PAST_LEN:, :] = _rope(k_ref[...], cos_ref[...], sin_ref[...]).astype(BF16)
        vall_ref[PAST_LEN:, :] = v_ref[...].astype(BF16)

    lam, lam_init = _lambda(al_ref, layer)
    subln = g_ref[layer:layer + 1, :]
    q = _rope(q_ref[...].astype(F32), cosq_ref[...], sinq_ref[...])
    _diff_attention(q, kall_ref[...], vall_ref[...], lam, lam_init, subln, o_ref)


def attention_lat(ya, q, kv, cache_k, cache_v, cos, sin, attn_lambda, subln_g, layer):
    nqb = DEC_SEQ // QB
    row0 = T_CTX // QB
    seq0 = T_CTX // DEC_SEQ
    return pl.pallas_call(
        functools.partial(_attn_lat_kernel, layer=layer),
        grid=(DEC_BATCH, nqb),
        in_specs=[
            pl.BlockSpec(memory_space=pl.ANY),
            pl.BlockSpec((QB, W_A), lambda b, i: (row0 + b * nqb + i, P_Q)),
            pl.BlockSpec((DEC_SEQ, W_A), lambda b, i: (seq0 + b, 0)),
            pl.BlockSpec((DEC_SEQ, W_A), lambda b, i: (seq0 + b, 1)),
            pl.BlockSpec((1, 1, PAST_LEN, H_A, DV_A), lambda b, i: (b, layer, 0, 0, 0)),
            pl.BlockSpec((1, 1, PAST_LEN, H_A, DV_A), lambda b, i: (b, layer, 0, 0, 0)),
            pl.BlockSpec((QB, W_A), lambda b, i: (i, 0)),
            pl.BlockSpec((QB, W_A), lambda b, i: (i, 0)),
            pl.BlockSpec((DEC_SEQ, W_A), lambda b, i: (0, 0)),
            pl.BlockSpec((DEC_SEQ, W_A), lambda b, i: (0, 0)),
            pl.BlockSpec((DEPTH, 4, DH_A), lambda b, i: (0, 0, 0)),
            pl.BlockSpec((DEPTH, DV_A), lambda b, i: (0, 0)),
        ],
        out_specs=pl.BlockSpec((QB, W_A), lambda b, i: (row0 + b * nqb + i, 0)),
        out_shape=jax.ShapeDtypeStruct((T_ALL, W_A), BF16),
        scratch_shapes=[pltpu.VMEM((PAST_LEN + DEC_SEQ, W_A), BF16),
                        pltpu.VMEM((PAST_LEN + DEC_SEQ, W_A), BF16)],
        input_output_aliases={0: 0},
        compiler_params=_params(2),
        name="attention_lat",
    )(ya, q, kv, kv, cache_k, cache_v, cos, sin, cos, sin, attn_lambda, subln_g)


def _dft_tables(n):
    k = np.arange(n, dtype=np.int64)
    prod = (2 * k[:, None] + 1) * k[None, :]
    ang = (prod % (4 * n)).astype(np.float64) * (math.pi / (2 * n))
    c = np.cos(ang).astype(np.float32)
    s = np.sin(ang).astype(np.float32)
    return tuple(jnp.asarray(t).astype(BF16) for t in (c, s, c.T, s.T))


def _filter_features(n):
    t = jnp.linspace(0.0, 1.0, n, dtype=F32)[:, None]
    bands = (H_EMB - 1) // 2
    w = 2.0 * math.pi * jnp.arange(n, dtype=F32)[:, None] / n
    fr = jnp.linspace(1e-4, bands - 1, bands, dtype=F32)
    z = jnp.concatenate([t, jnp.cos(w * fr), -jnp.sin(w * fr)], axis=-1)
    return jnp.pad(z, ((0, 0), (0, LANES - H_EMB)))


def _filter_decay_rates():
    max_decay = math.log(H_TARGET) / H_FAST_DECAY
    min_decay = math.log(H_TARGET) / H_SLOW_DECAY
    return jnp.abs(jnp.linspace(min_decay, max_decay, C_B, dtype=F32))[None, :]


def _hyena_filter_kernel(z_ref, w1_ref, b1_ref, w2_ref, b2_ref, w3_ref, dr_ref, c_ref, s_ref,
                         gre_ref, gim_ref):
    z = z_ref[...]
    hid = jnp.sin(_dot3(z, w1_ref[0]) + b1_ref[0])
    hid = jnp.sin(_dot3(hid, w2_ref[0]) + b2_ref[0])
    h = _dot(hid.astype(BF16), w3_ref[0].astype(BF16))
    window = jnp.exp(-z[:, 0:1] * dr_ref[...])
    hf = h[:, :C_B] * window
    hb = h[:, C_B:] * window
    total = jnp.sum(jnp.abs(hf) + jnp.abs(hb), axis=0, keepdims=True)
    hf = hf / total
    hb = hb / total
    row = lax.broadcasted_iota(jnp.int32, hb.shape, 0)
    hb = jnp.where(row == 0, 0.0, hb)
    gre_ref[0] = _dot(c_ref[...], (hf + hb).astype(BF16))
    gim_ref[0] = _dot(s_ref[...], (hb - hf).astype(BF16))


def hyena_filters(n, z, decay_rates, cf, sf, f_w1, f_b1, f_w2, f_b2, f_w3):
    w1 = jnp.pad(f_w1, ((0, 0), (0, LANES - H_EMB), (0, 0)))
    full = lambda shape: pl.BlockSpec(shape, lambda l: (0,) * len(shape))
    per_layer = lambda shape: pl.BlockSpec((1,) + shape, lambda l: (l,) + (0,) * len(shape))
    return pl.pallas_call(
        _hyena_filter_kernel,
        grid=(DEPTH,),
        in_specs=[
            full((n, LANES)),
            per_layer((LANES, H_FFN)), per_layer((1, H_FFN)),
            per_layer((H_FFN, H_FFN)), per_layer((1, H_FFN)),
            per_layer((H_FFN, 2 * C_B)),
            full((1, C_B)), full((n, n)), full((n, n)),
        ],
        out_specs=[per_layer((n, C_B)), per_layer((n, C_B))],
        out_shape=[jax.ShapeDtypeStruct((DEPTH, n, C_B), F32)] * 2,
        compiler_params=_params(1),
        name="hyena_filters",
    )(z, w1, f_b1.reshape(DEPTH, 1, H_FFN), f_w2, f_b2.reshape(DEPTH, 1, H_FFN), f_w3,
      decay_rates, cf, sf)


def _hyena_kernel(*refs, layer, n, aliased):
    if aliased:
        refs = refs[1:]
    u_ref, cw_ref, cb_ref, skip_ref, gre_ref, gim_ref, c_ref, s_ref, ct_ref, st_ref, o_ref = refs
    u = u_ref[...].astype(F32)
    row = lax.broadcasted_iota(jnp.int32, u.shape, 0)
    prev = jnp.where(row == 0, 0.0, pltpu.roll(u, 1, 0))
    nxt = jnp.where(row == n - 1, 0.0, pltpu.roll(u, n - 1, 0))
    cw = cw_ref[layer]
    uc = cb_ref[layer:layer + 1, :] + prev * cw[0:1, :] + u * cw[1:2, :] + nxt * cw[2:3, :]
    x0 = uc[:, :C_B]
    x1 = uc[:, C_B:2 * C_B]
    v = uc[:, 2 * C_B:]
    w = v * x1
    wb = w.astype(BF16)
    ure = _dot(c_ref[...], wb)
    uim = -_dot(s_ref[...], wb)
    gre = gre_ref[0]
    gim = gim_ref[0]
    yre = (ure * gre - uim * gim).astype(BF16)
    yim = (ure * gim + uim * gre).astype(BF16)
    y = (_dot(ct_ref[...], yre) - _dot(st_ref[...], yim)) * (1.0 / n)
    y = y + w * skip_ref[layer:layer + 1, :]
    o_ref[...] = (x0 * y).astype(o_ref.dtype)


def _full(shape):
    return pl.BlockSpec(shape, lambda *_: (0,) * len(shape))


def _hyena_specs(n, seq0, layer):
    return [
        pl.BlockSpec((n, 3 * C_B), lambda b, *_: (seq0 + b, P_HYENA)),
        _full((DEPTH, SHORT_K, 3 * C_B)), _full((DEPTH, 3 * C_B)), _full((DEPTH, C_B)),
        pl.BlockSpec((1, n, C_B), lambda *_: (layer, 0, 0)),
        pl.BlockSpec((1, n, C_B), lambda *_: (layer, 0, 0)),
        _full((n, n)), _full((n, n)), _full((n, n)), _full((n, n)),
    ]


def hyena_lat(yb, p, conv_w, conv_b, skip, gre, gim, tables, layer):
    seq0 = T_CTX // DEC_SEQ
    return pl.pallas_call(
        functools.partial(_hyena_kernel, layer=layer, n=DEC_SEQ, aliased=True),
        grid=(DEC_BATCH,),
        in_specs=[pl.BlockSpec(memory_space=pl.ANY)] + _hyena_specs(DEC_SEQ, seq0, layer),
        out_specs=pl.BlockSpec((DEC_SEQ, C_B), lambda b: (seq0 + b, 0)),
        out_shape=jax.ShapeDtypeStruct((T_ALL, C_B), BF16),
        input_output_aliases={0: 0},
        compiler_params=_params(1),
        name="hyena_lat",
    )(yb, p, conv_w, conv_b, skip, gre, gim, *tables)


RET_QB = 256


def _retention_kernel(*refs, layer, n, latent):
    if latent:
        _, q_ref, k_ref, v_ref, g_ref, de_ref, cos_ref, sin_ref, s0_ref, o_ref, w_ref, vec_ref = refs
        st_ref = None
    else:
        q_ref, k_ref, v_ref, g_ref, de_ref, o_ref, st_ref, w_ref, vec_ref = refs[-9:]
    log_gamma = jnp.log1p(-jnp.exp2(-de_ref[layer]))

    @pl.when(pl.program_id(0) == 0)
    def _():
        t = lax.broadcasted_iota(jnp.int32, (n, n), 0)
        s = lax.broadcasted_iota(jnp.int32, (n, n), 1)
        lag = (t - s).astype(F32)
        for h in range(H_C):
            rate = jnp.where(lag > 0, log_gamma[0:1, h:h + 1], log_gamma[1:2, h:h + 1])
            w_ref[h] = jnp.where(lag == 0, 2.0, jnp.exp(rate * jnp.abs(lag)))
            pos = lax.broadcasted_iota(jnp.int32, (n, LANES), 0).astype(F32)
            steps = (pos + 1.0, float(n) - pos) if latent else (float(n - 1) - pos, pos)
            for d in range(2):
                vec_ref[h, d] = jnp.exp(log_gamma[d:d + 1, h:h + 1] * steps[d])

    q = q_ref[...].astype(F32)
    k = k_ref[...].astype(F32) * (DK_C ** -0.5)
    if latent:
        q = _rope(q, cos_ref[...], sin_ref[...])
        k = _rope(k, cos_ref[...], sin_ref[...])
    v = v_ref[...].astype(BF16)
    gate = g_ref[...].astype(F32)
    lane = lax.broadcasted_iota(jnp.int32, (n, LANES), 1)
    zeros64 = jnp.zeros((DK_C, DV_C), F32)

    for h in range(H_C):
        pair = slice((h // 2) * LANES, (h // 2 + 1) * LANES)
        lo = (h % 2) * DK_C
        own = (lane >= lo) & (lane < lo + DK_C)
        vs = slice(h * DV_C, (h + 1) * DV_C)
        qh = jnp.where(own, q[:, pair], 0.0).astype(BF16)
        kh = jnp.where(own, k[:, pair], 0.0)
        kb = kh.astype(BF16)
        vh = v[:, vs]
        if latent:
            s0 = [jnp.concatenate([s0_ref[0, 0, d, h], zeros64] if lo == 0 else [zeros64, s0_ref[0, 0, d, h]],
                                  axis=0).astype(BF16) for d in range(2)]
            carry = _dot(qh, s0[0]) * vec_ref[h, 0] + _dot(qh, s0[1]) * vec_ref[h, 1]
        for r0 in range(0, n, RET_QB):
            rows = slice(r0, r0 + RET_QB)
            att = _dot_nt(qh[rows], kb) * w_ref[h, rows, :]
            out = _dot(att.astype(BF16), vh)
            if latent:
                out = out + carry[rows]
            o_ref[rows, vs] = (_rms(out) * _silu(gate[rows, vs])).astype(o_ref.dtype)
        if st_ref is not None:
            sf = _dot_tn((kh * vec_ref[h, 0]).astype(BF16), vh)
            sb = _dot_tn((kh * vec_ref[h, 1]).astype(BF16), vh)
            st_ref[0, 0, 0, h] = sf[lo:lo + DK_C, :]
            st_ref[0, 0, 1, h] = sb[lo:lo + DK_C, :]


def _retention_specs(n, seq0):
    return [
        pl.BlockSpec((n, H_C * DK_C), lambda b, *_: (seq0 + b, P_RET_Q)),
        pl.BlockSpec((n, H_C * DK_C), lambda b, *_: (seq0 + b, P_RET_Q + 1)),
        pl.BlockSpec((n, W_C), lambda b, *_: (seq0 + b, P_RET_V)),
        pl.BlockSpec((n, W_C), lambda b, *_: (seq0 + b, P_RET_V + 1)),
        _full((DEPTH, 2, H_C)),
    ]


def retention_lat(yc, p, decay_exp, cos, sin, state0, layer):
    n = DEC_SEQ
    seq0 = T_CTX // DEC_SEQ
    return pl.pallas_call(
        functools.partial(_retention_kernel, layer=layer, n=n, latent=True),
        grid=(DEC_BATCH,),
        in_specs=[pl.BlockSpec(memory_space=pl.ANY)] + _retention_specs(n, seq0) + [
            _full((n, H_C * DK_C)), _full((n, H_C * DK_C)),
            pl.BlockSpec((1, 1, 2, H_C, DK_C, DV_C), lambda b: (b, layer, 0, 0, 0, 0)),
        ],
        out_specs=pl.BlockSpec((n, W_C), lambda b: (seq0 + b, 0)),
        out_shape=jax.ShapeDtypeStruct((T_ALL, W_C), BF16),
        scratch_shapes=[pltpu.VMEM((H_C, n, n), F32), pltpu.VMEM((H_C, 2, n, LANES), F32)],
        input_output_aliases={0: 0},
        compiler_params=_params(1),
        name="retention_lat",
    )(yc, p, p, p, p, decay_exp, cos, sin, state0)


N_ATTN_IN = 5
N_HYENA_IN = 10
N_RET_IN = 5


def _mixers_ctx_kernel(*refs, layer, n_threaded):
    refs = refs[n_threaded:]
    attn_in = refs[:N_ATTN_IN]
    hyena_in = refs[N_ATTN_IN:N_ATTN_IN + N_HYENA_IN]
    ret_in = refs[N_ATTN_IN + N_HYENA_IN:N_ATTN_IN + N_HYENA_IN + N_RET_IN]
    ya_ref, knew_ref, vnew_ref, yb_ref, yc_ref, st_ref, w_ref, vec_ref = refs[N_ATTN_IN + N_HYENA_IN + N_RET_IN:]
    _attn_ctx_kernel(*attn_in, ya_ref, knew_ref, vnew_ref, layer=layer)
    _hyena_kernel(*hyena_in, yb_ref, layer=layer, n=SEQ, aliased=False)
    _retention_kernel(*ret_in, yc_ref, st_ref, w_ref, vec_ref, layer=layer, n=SEQ, latent=False)


def mixers_ctx(p, kv, attn_lambda, subln_g, conv_w, conv_b, skip, gre, gim, tables, decay_exp, layer, threaded):
    cache_shape = jax.ShapeDtypeStruct((BATCH, DEPTH, SEQ, H_A, DV_A), F32)
    cache_spec = pl.BlockSpec((1, 1, SEQ, H_A, DV_A), lambda b: (b, layer, 0, 0, 0))
    branch_shape = jax.ShapeDtypeStruct((T_ALL, W_A), BF16)
    branch_spec = pl.BlockSpec((SEQ, W_A), lambda b: (b, 0))
    n_threaded = len(threaded)
    attn_specs = [
        pl.BlockSpec((SEQ, W_A), lambda b: (b, P_Q)),
        pl.BlockSpec((SEQ, W_A), lambda b: (b, 0)),
        pl.BlockSpec((SEQ, W_A), lambda b: (b, 1)),
        _full((DEPTH, 4, DH_A)), _full((DEPTH, DV_A)),
    ]
    return pl.pallas_call(
        functools.partial(_mixers_ctx_kernel, layer=layer, n_threaded=n_threaded),
        grid=(BATCH,),
        in_specs=([pl.BlockSpec(memory_space=pl.ANY)] * n_threaded + attn_specs
                  + _hyena_specs(SEQ, 0, layer) + _retention_specs(SEQ, 0)),
        out_specs=[branch_spec, cache_spec, cache_spec, branch_spec, branch_spec,
                   pl.BlockSpec((1, 1, 2, H_C, DK_C, DV_C), lambda b: (b, layer, 0, 0, 0, 0))],
        out_shape=[branch_shape, cache_shape, cache_shape, branch_shape, branch_shape,
                   jax.ShapeDtypeStruct((BATCH, DEPTH, 2, H_C, DK_C, DV_C), F32)],
        scratch_shapes=[pltpu.VMEM((H_C, SEQ, SEQ), F32), pltpu.VMEM((H_C, 2, SEQ, LANES), F32)],
        input_output_aliases=dict(zip(range(n_threaded), (1, 2, 5))),
        compiler_params=_params(1),
        name="mixers_ctx",
    )(*threaded, p, kv, kv, attn_lambda, subln_g, p, conv_w, conv_b, skip, gre, gim, *tables,
      p, p, p, p, decay_exp)


def _merge_kernel(x_ref, ya_ref, yb_ref, yc_ref, g_ref, m_ref, wa_ref, wb_ref, wc_ref, wo_ref,
                  o_ref, wbf_ref, wobf_ref):
    @pl.when(pl.program_id(0) == 0)
    def _():
        wbf_ref[0] = wa_ref[0].astype(BF16)
        wbf_ref[1] = wb_ref[0].astype(BF16)
        wbf_ref[2] = wc_ref[0].astype(BF16)
        wobf_ref[...] = wo_ref[0].astype(BF16)

    g = g_ref[...]
    merged = (g[:, :D_MODEL] * _dot(ya_ref[...], wbf_ref[0])
              + g[:, D_MODEL:2 * D_MODEL] * _dot(yb_ref[...], wbf_ref[1])
              + g[:, 2 * D_MODEL:] * _dot(yc_ref[...], wbf_ref[2]))
    g1 = m_ref[0, 0, 2:3, :]
    o_ref[...] = x_ref[...] + g1 * _dot(merged.astype(BF16), wobf_ref[...])


def merge_branches(x, ya, yb, yc, gates, mods, w_a, w_b, w_c, w_out, layer):
    tile = lambda w: pl.BlockSpec((TM, w), lambda i: (i, 0))
    wspec = lambda k: pl.BlockSpec((1, k, D_MODEL), lambda i: (layer, 0, 0))
    return pl.pallas_call(
        _merge_kernel,
        grid=(N_TILES,),
        in_specs=[
            tile(D_MODEL), tile(W_A), tile(C_B), tile(W_C), tile(3 * D_MODEL),
            pl.BlockSpec((1, 1, 6, D_MODEL), lambda i: (layer, _mod_row(i), 0, 0)),
            wspec(W_A), wspec(C_B), wspec(W_C), wspec(D_MODEL),
        ],
        out_specs=tile(D_MODEL),
        out_shape=jax.ShapeDtypeStruct((T_ALL, D_MODEL), F32),
        scratch_shapes=[pltpu.VMEM((3, W_A, D_MODEL), BF16), pltpu.VMEM((D_MODEL, D_MODEL), BF16)],
        compiler_params=_params(1),
        name="merge_branches",
    )(x, ya, yb, yc, gates, mods, w_a, w_b, w_c, w_out)


def _route(h2, wr_t, b_r):
    logits = _dot3_nt(wr_t, h2)
    m = jnp.max(logits, axis=0, keepdims=True)
    e = jnp.exp(logits - m)
    scores = e / jnp.sum(e, axis=0, keepdims=True)
    sel = scores + b_r
    rows = [sel[i:i + 1, :] for i in range(N_EXPERTS)]
    in_group = []
    gscore = []
    for g in range(N_GROUPS):
        members = range(g * EXP_PER_GROUP, (g + 1) * EXP_PER_GROUP)
        total = None
        for i in members:
            rank = None
            for j in members:
                if j == i:
                    continue
                ahead = (rows[j] >= rows[i]) if j < i else (rows[j] > rows[i])
                ahead = ahead.astype(F32)
                rank = ahead if rank is None else rank + ahead
            chosen = rank < 2.0
            in_group.append(chosen)
            part = jnp.where(chosen, rows[i], 0.0)
            total = part if total is None else total + part
        gscore.append(total)
    gates = []
    group_hot = []
    for g in range(N_GROUPS):
        best = None
        for g2 in range(N_GROUPS):
            if g2 == g:
                continue
            wins = gscore[g] > gscore[g2] if g2 < g else gscore[g] >= gscore[g2]
            best = wins if best is None else best & wins
        group_hot.append(best.astype(F32))
        for i in range(g * EXP_PER_GROUP, (g + 1) * EXP_PER_GROUP):
            gates.append(jnp.where(best & in_group[i], scores[i:i + 1, :], 0.0))
    gates = jnp.concatenate(gates, axis=0)
    return gates / jnp.sum(gates, axis=0, keepdims=True), group_hot


def _moe_route_kernel(x_ref, g_ref, m_ref, wr_ref, br_ref, tri_ref, tiles_ref, gate_ref, meta_ref, count_ref,
                      *, layer):
    y = _rms(x_ref[...]) * g_ref[layer:layer + 1, :]
    h2 = y * (1.0 + m_ref[0, 0, 4:5, :]) + m_ref[0, 0, 3:4, :]
    gates, group_hot = _route(h2, wr_ref[...], br_ref[...])
    for s in range(SUBLANES):
        tiles_ref[pl.ds(s, TM, stride=SUBLANES), :] = h2[:, s * LANES:(s + 1) * LANES]
    padded = jnp.concatenate([gates, jnp.zeros((LANES - N_EXPERTS, TM), F32)], axis=0)
    gate_ref[...] = padded.T
    row = lax.broadcasted_iota(jnp.int32, (SUBLANES, TM), 0)
    hot = jnp.zeros((SUBLANES, TM), F32)
    for g in range(N_GROUPS):
        hot = jnp.where(row == g, group_hot[g], hot)
    before = _dot(hot.astype(BF16), tri_ref[...])
    rank = jnp.sum(hot * before, axis=0, keepdims=True)
    gid = group_hot[1] + 2.0 * group_hot[2] + 3.0 * group_hot[3]
    meta = jnp.where(row == 0, gid, jnp.where(row == 1, rank, 0.0))
    meta_ref[0] = meta.astype(jnp.int32)
    counts = jnp.sum(hot, axis=1, keepdims=True) + jnp.zeros((SUBLANES, LANES), F32)
    count_ref[0] = counts.astype(jnp.int32)


def moe_route(x, gain, mods, wr_t, b_r, tri, layer):
    return pl.pallas_call(
        functools.partial(_moe_route_kernel, layer=layer),
        grid=(N_TILES,),
        in_specs=[
            pl.BlockSpec((TM, D_MODEL), lambda i: (i, 0)),
            pl.BlockSpec((DEPTH, D_MODEL), lambda i: (0, 0)),
            pl.BlockSpec((1, 1, 6, D_MODEL), lambda i: (layer, _mod_row(i), 0, 0)),
            pl.BlockSpec((N_EXPERTS, D_MODEL), lambda i: (0, 0)),
            pl.BlockSpec((N_EXPERTS, 1), lambda i: (0, 0)),
            pl.BlockSpec((TM, TM), lambda i: (0, 0)),
        ],
        out_specs=[pl.BlockSpec((TM * SUBLANES, LANES), lambda i: (i, 0)),
                   pl.BlockSpec((TM, LANES), lambda i: (i, 0)),
                   pl.BlockSpec((1, SUBLANES, TM), lambda i: (i, 0, 0)),
                   pl.BlockSpec((1, SUBLANES, LANES), lambda i: (i, 0, 0))],
        out_shape=[jax.ShapeDtypeStruct((T_ALL * SUBLANES, LANES), F32),
                   jax.ShapeDtypeStruct((T_ALL, LANES), F32),
                   jax.ShapeDtypeStruct((N_TILES, SUBLANES, TM), jnp.int32),
                   jax.ShapeDtypeStruct((N_TILES, SUBLANES, LANES), jnp.int32)],
        compiler_params=_params(1),
        name="moe_route",
    )(x, gain, mods, wr_t, b_r, tri)


def _moe_positions(meta, counts):
    gid = meta[:, 0, :]
    rank = meta[:, 1, :]
    cnt = counts[:, :N_GROUPS, 0]
    shift = MOE_BLOCK.bit_length() - 1
    total = functools.reduce(lambda a, b: a + b, [cnt[i] for i in range(N_TILES)])
    padded = ((total + (MOE_BLOCK - 1)) >> shift) << shift
    starts = [jnp.zeros((), jnp.int32)]
    for g in range(N_GROUPS):
        starts.append(starts[-1] + padded[g])
    group_start = jnp.stack(starts[:N_GROUPS])
    rows = [group_start]
    for i in range(N_TILES - 1):
        rows.append(rows[-1] + cnt[i])
    base = jnp.stack(rows)
    pos = rank
    for g in range(N_GROUPS):
        pos = pos + jnp.where(gid == g, base[:, g:g + 1], 0)
    first_block = group_start >> shift
    blk = jnp.arange(MOE_BLOCKS, dtype=jnp.int32)
    block_gid = sum((blk >= first_block[g]).astype(jnp.int32) for g in range(1, N_GROUPS))
    block_used = (blk * MOE_BLOCK < starts[N_GROUPS]).astype(jnp.int32)
    later = jnp.where(block_gid[None, :] > block_gid[:, None], block_gid[None, :], N_GROUPS)
    next_group = jnp.min(later, axis=1)
    next_group = jnp.where(next_group == N_GROUPS, -1, next_group).astype(jnp.int32)
    return pos.reshape(T_ALL).astype(jnp.int32), block_gid, block_used, next_group


def _moe_permute_kernel(pos_ref, tiles_ref, gate_ref, sorted_ref, gsorted_ref):
    i = pl.program_id(0)

    @pl.when(i == 0)
    def _():
        def zero(b, carry):
            start = pl.multiple_of(b * TM, TM)
            sorted_ref[pl.ds(start, TM), :] = jnp.zeros((TM, LANES), F32)
            return carry
        lax.fori_loop(0, T_PAD * SUBLANES // TM, zero, 0)
        gsorted_ref[...] = jnp.zeros((T_PAD, LANES), F32)

    def move(t, carry):
        p = pos_ref[i * TM + t]
        dst = pl.multiple_of(p * SUBLANES, SUBLANES)
        src = pl.multiple_of(t * SUBLANES, SUBLANES)
        sorted_ref[pl.ds(dst, SUBLANES), :] = tiles_ref[pl.ds(src, SUBLANES), :]
        gsorted_ref[pl.ds(p, 1), :] = gate_ref[pl.ds(t, 1), :]
        return carry
    lax.fori_loop(0, TM, move, 0, unroll=8)


def moe_permute(pos, tiles, gates):
    return pl.pallas_call(
        _moe_permute_kernel,
        grid_spec=pltpu.PrefetchScalarGridSpec(
            num_scalar_prefetch=1,
            grid=(N_TILES,),
            in_specs=[pl.BlockSpec((TM * SUBLANES, LANES), lambda i, pos: (i, 0)),
                      pl.BlockSpec((TM, LANES), lambda i, pos: (i, 0))],
            out_specs=[pl.BlockSpec(memory_space=pltpu.VMEM), pl.BlockSpec(memory_space=pltpu.VMEM)],
        ),
        out_shape=[jax.ShapeDtypeStruct((T_PAD * SUBLANES, LANES), F32),
                   jax.ShapeDtypeStruct((T_PAD, LANES), F32)],
        compiler_params=_params(1),
        name="moe_permute",
    )(pos, tiles, gates)


def _group_changed(gid_ref, b):
    return (b == 0) | (gid_ref[b] != gid_ref[jnp.maximum(b - 1, 0)])


def _moe_expert_kernel(gid_ref, used_ref, next_ref, s_ref, gate_ref, w1_hbm, w3_hbm, w2_ref, y_ref,
                       w1f_ref, w3f_ref, w1b_ref, w3b_ref, w2b_ref, sem, *, layer):
    b = pl.program_id(0)

    def up_weight_copies(group):
        return (pltpu.make_async_copy(w1_hbm.at[layer, group], w1f_ref, sem.at[0]),
                pltpu.make_async_copy(w3_hbm.at[layer, group], w3f_ref, sem.at[1]))

    @pl.when(b == 0)
    def _():
        for copy in up_weight_copies(gid_ref[0]):
            copy.start()

    @pl.when(_group_changed(gid_ref, b))
    def _():
        for copy in up_weight_copies(gid_ref[b]):
            copy.wait()
        for j in range(EXP_PER_GROUP):
            cols = slice(j * D_FF, (j + 1) * D_FF)
            w1b_ref[:, cols] = w1f_ref[j].astype(BF16)
            w3b_ref[:, cols] = w3f_ref[j].astype(BF16)
        w2b_ref[...] = w2_ref[0, 0].astype(BF16)

        @pl.when(next_ref[b] >= 0)
        def _():
            for copy in up_weight_copies(next_ref[b]):
                copy.start()

    @pl.when(used_ref[b] == 1)
    def _():
        lhs = jnp.concatenate([s_ref[pl.ds(s, MOE_BLOCK, stride=SUBLANES), :].astype(BF16)
                               for s in range(SUBLANES)], axis=1)
        a = _dot(lhs, w1b_ref[...])
        g = _dot(lhs, w3b_ref[...])
        gates = gate_ref[...]
        lane = lax.broadcasted_iota(jnp.int32, (MOE_BLOCK, LANES), 1)
        first = gid_ref[b] * EXP_PER_GROUP
        parts = []
        for j in range(EXP_PER_GROUP):
            cols = slice(j * D_FF, (j + 1) * D_FF)
            gate = jnp.sum(jnp.where(lane == first + j, gates, 0.0), axis=1, keepdims=True)
            parts.append((_silu(a[:, cols]) * g[:, cols] * gate).astype(BF16))
        act = jnp.concatenate(parts, axis=1)
        y = _dot(act, w2b_ref[...])
        for s in range(SUBLANES):
            y_ref[pl.ds(s, MOE_BLOCK, stride=SUBLANES), :] = y[:, s * LANES:(s + 1) * LANES]

    @pl.when(used_ref[b] == 0)
    def _():
        y_ref[...] = jnp.zeros_like(y_ref)


def moe_experts(block_gid, block_used, next_group, sorted_rows, sorted_gates, w1, w3, w2, layer):
    group_ff = EXP_PER_GROUP * D_FF
    w1g = w1.reshape(DEPTH, N_GROUPS, EXP_PER_GROUP, D_MODEL, D_FF)
    w3g = w3.reshape(DEPTH, N_GROUPS, EXP_PER_GROUP, D_MODEL, D_FF)
    w2g = w2.reshape(DEPTH, N_GROUPS, group_ff, D_MODEL)
    block = lambda b, gid, used, nxt: (b, 0)
    return pl.pallas_call(
        functools.partial(_moe_expert_kernel, layer=layer),
        grid_spec=pltpu.PrefetchScalarGridSpec(
            num_scalar_prefetch=3,
            grid=(MOE_BLOCKS,),
            in_specs=[pl.BlockSpec((MOE_BLOCK * SUBLANES, LANES), block),
                      pl.BlockSpec((MOE_BLOCK, LANES), block),
                      pl.BlockSpec(memory_space=pl.ANY), pl.BlockSpec(memory_space=pl.ANY),
                      pl.BlockSpec((1, 1, group_ff, D_MODEL), lambda b, gid, used, nxt: (layer, gid[b], 0, 0))],
            out_specs=pl.BlockSpec((MOE_BLOCK * SUBLANES, LANES), block),
            scratch_shapes=[pltpu.VMEM((EXP_PER_GROUP, D_MODEL, D_FF), F32),
                            pltpu.VMEM((EXP_PER_GROUP, D_MODEL, D_FF), F32),
                            pltpu.VMEM((D_MODEL, group_ff), BF16), pltpu.VMEM((D_MODEL, group_ff), BF16),
                            pltpu.VMEM((group_ff, D_MODEL), BF16),
                            pltpu.SemaphoreType.DMA((2,))],
        ),
        out_shape=jax.ShapeDtypeStruct((T_PAD * SUBLANES, LANES), F32),
        compiler_params=_params(1),
        name="moe_experts",
    )(block_gid, block_used, next_group, sorted_rows, sorted_gates, w1g, w3g, w2g)


FINAL_TM = 512


def _moe_combine_kernel(pos_ref, ys_ref, x_ref, m_ref, *rest, tm, final):
    i = pl.program_id(0)
    buf_ref = rest[-2] if final else rest[-1]

    def move(t, carry):
        src = pl.multiple_of(pos_ref[i * tm + t] * SUBLANES, SUBLANES)
        dst = pl.multiple_of(t * SUBLANES, SUBLANES)
        buf_ref[pl.ds(dst, SUBLANES), :] = ys_ref[pl.ds(src, SUBLANES), :]
        return carry
    lax.fori_loop(0, tm, move, 0, unroll=8)

    x_new_ref = rest[-1] if final else rest[0]
    for s in range(SUBLANES):
        cols = slice(s * LANES, (s + 1) * LANES)
        y = buf_ref[pl.ds(s, tm, stride=SUBLANES), :]
        x_new_ref[:, cols] = x_ref[:, cols] + m_ref[0, 0, 5:6, cols] * y

    if final:
        gain_ref, ctx_ref, lat_ref = rest[:3]
        out = _rms(x_new_ref[...]) * gain_ref[...]

        @pl.when(i < T_CTX // tm)
        def _():
            ctx_ref[...] = out

        @pl.when(i >= T_CTX // tm)
        def _():
            lat_ref[...] = out


def moe_combine(pos, y_sorted, x, mods, layer, final_gain=None):
    final = final_gain is not None
    tm = FINAL_TM if final else TM
    ctx_tiles = T_CTX // tm
    tile = pl.BlockSpec((tm, D_MODEL), lambda i, pos: (i, 0))
    in_specs = [
        pl.BlockSpec(memory_space=pltpu.VMEM),
        tile,
        pl.BlockSpec((1, 1, 6, D_MODEL), lambda i, pos: (layer, _mod_row(i * tm // TM), 0, 0)),
    ]
    args = [pos, y_sorted, x, mods]
    scratch = [pltpu.VMEM((tm * SUBLANES, LANES), F32)]
    if final:
        in_specs.append(pl.BlockSpec((1, D_MODEL), lambda i, pos: (0, 0)))
        args.append(final_gain.reshape(1, D_MODEL))
        out_specs = [pl.BlockSpec((tm, D_MODEL), lambda i, pos: (jnp.minimum(i, ctx_tiles - 1), 0)),
                     pl.BlockSpec((tm, D_MODEL), lambda i, pos: (jnp.maximum(i - ctx_tiles, 0), 0))]
        out_shape = [jax.ShapeDtypeStruct((T_CTX, D_MODEL), F32), jax.ShapeDtypeStruct((T_LAT, D_MODEL), F32)]
        scratch.append(pltpu.VMEM((tm, D_MODEL), F32))
    else:
        out_specs = tile
        out_shape = jax.ShapeDtypeStruct((T_ALL, D_MODEL), F32)
    return pl.pallas_call(
        functools.partial(_moe_combine_kernel, tm=tm, final=final),
        grid_spec=pltpu.PrefetchScalarGridSpec(
            num_scalar_prefetch=1,
            grid=(T_ALL // tm,),
            in_specs=in_specs,
            out_specs=out_specs,
            scratch_shapes=scratch,
        ),
        out_shape=out_shape,
        compiler_params=_params(1),
        name="moe_combine",
    )(*args)


def moe(x, gain, mods, wr_t, b_r, tri, w1, w3, w2, layer, final_gain=None):
    tiles, gates, meta, counts = moe_route(x, gain, mods, wr_t, b_r, tri, layer)
    pos, block_gid, block_used, next_group = _moe_positions(meta, counts)
    sorted_rows, sorted_gates = moe_permute(pos, tiles, gates)
    y_sorted = moe_experts(block_gid, block_used, next_group, sorted_rows, sorted_gates, w1, w3, w2, layer)
    return moe_combine(pos, y_sorted, x, mods, layer, final_gain)


def kernel(x_prompt, x_sample, cache_attn_k, cache_attn_v, state_retention, c, c_ctx, w_ada, b_ada, norm1_g, norm2_g, final_g, w_in, attn_lambda, attn_subln_g, hy_conv_w, hy_conv_b, hy_f_w1, hy_f_b1, hy_f_w2, hy_f_b2, hy_f_w3, hy_skip, ret_decay_exp, w_branch_a, w_branch_b, w_branch_c, w_gate, b_gate, w_out, w_router, b_router, moe_w1, moe_w3, moe_w2):
    x = jnp.concatenate([x_prompt.reshape(T_CTX, D_MODEL), x_sample.reshape(T_LAT, D_MODEL)], axis=0)
    cond8 = jnp.concatenate([c_ctx[None, :], c, jnp.zeros((8 - 1 - DEC_BATCH, D_MODEL), F32)], axis=0)
    mods = ada_modulation(cond8, w_ada, b_ada).reshape(DEPTH, 8, 6, D_MODEL)

    cos, sin = _rope_tables()
    cos_c, sin_c = cos[:, :H_C * DK_C], sin[:, :H_C * DK_C]
    decay_rates = _filter_decay_rates()
    tables_ctx = _dft_tables(SEQ)
    tables_lat = _dft_tables(DEC_SEQ)
    filt_ctx = hyena_filters(SEQ, _filter_features(SEQ), decay_rates, tables_ctx[0], tables_ctx[1],
                             hy_f_w1, hy_f_b1, hy_f_w2, hy_f_b2, hy_f_w3)
    filt_lat = hyena_filters(DEC_SEQ, _filter_features(DEC_SEQ), decay_rates, tables_lat[0], tables_lat[1],
                             hy_f_w1, hy_f_b1, hy_f_w2, hy_f_b2, hy_f_w3)
    wr_t = w_router.T
    b_r = b_router.reshape(N_EXPERTS, 1)
    tri = jnp.asarray(np.triu(np.ones((TM, TM), np.float32), 1), dtype=BF16)

    threaded = ()
    for l in range(DEPTH):
        p, kv = in_projection(x, norm1_g, mods, w_in, w_gate, b_gate, l)

        ya, new_k, new_v, yb, yc, states = mixers_ctx(
            p, kv, attn_lambda, attn_subln_g, hy_conv_w, hy_conv_b, hy_skip, filt_ctx[0], filt_ctx[1],
            tables_ctx, ret_decay_exp, l, threaded)
        threaded = (new_k, new_v, states)
        ya = attention_lat(ya, p, kv, cache_attn_k, cache_attn_v, cos, sin, attn_lambda, attn_subln_g, l)
        yb = hyena_lat(yb, p, hy_conv_w, hy_conv_b, hy_skip, filt_lat[0], filt_lat[1], tables_lat, l)
        yc = retention_lat(yc, p, ret_decay_exp, cos_c, sin_c, state_retention, l)

        x = merge_branches(x, ya, yb, yc, p, mods, w_branch_a, w_branch_b, w_branch_c, w_out, l)
        x = moe(x, norm2_g, mods, wr_t, b_r, tri, moe_w1, moe_w3, moe_w2, l,
                final_g if l == DEPTH - 1 else None)

    y_prompt, y_sample = x
    return (y_prompt.reshape(BATCH, SEQ, D_MODEL), y_sample.reshape(DEC_BATCH, DEC_SEQ, D_MODEL),
            new_k, new_v, states)
```

```python
import functools
import math

import jax
import jax.numpy as jnp
import numpy as np
from jax import lax
from jax.experimental import pallas as pl
from jax.experimental.pallas import tpu as pltpu

F32 = jnp.float32
BF16 = jnp.bfloat16

D_MODEL = 1024
BATCH = 16
SEQ = 256
DEPTH = 4
DEC_BATCH = 2
DEC_SEQ = 1024
PAST_LEN = 256
GRID_W = 64
EPS = 1e-6
ROPE_BASE = 10000.0
H_A = 4
DH_A = 64
DV_A = 128
W_A = 512
C_B = 512
SHORT_K = 3
H_EMB = 33
H_FFN = 64
H_FAST_DECAY = 0.3
H_SLOW_DECAY = 1.5
H_TARGET = 1e-2
H_C = 4
DK_C = 64
DV_C = 128
W_C = 512
N_EXPERTS = 16
N_GROUPS = 4
EXP_PER_GROUP = 4
D_FF = 512
D_IN = 4608

T_CTX = BATCH * SEQ
T_LAT = DEC_BATCH * DEC_SEQ
T_ALL = T_CTX + T_LAT
TM = 1024
N_TILES = T_ALL // TM
CTX_TILES = T_CTX // TM
LANES = 128
SUBLANES = 8
MOE_BLOCK = 256
MOE_BLOCKS = T_ALL // MOE_BLOCK + N_GROUPS
T_PAD = MOE_BLOCKS * MOE_BLOCK
VMEM_LIMIT = 56 * 1024 * 1024


def _params(n_axes):
    return pltpu.CompilerParams(
        dimension_semantics=("arbitrary",) * n_axes, vmem_limit_bytes=VMEM_LIMIT)


def _mod_row(i):
    return jnp.maximum(i - (CTX_TILES - 1), 0)


def _rms(x):
    return x * lax.rsqrt(jnp.mean(x * x, axis=-1, keepdims=True) + EPS)


def _sigmoid(x):
    return 0.5 * jnp.tanh(0.5 * x) + 0.5


def _silu(x):
    return x * _sigmoid(x)


def _dot(a, b):
    return jnp.dot(a, b, preferred_element_type=F32)


def _dot_nt(a, b):
    return lax.dot_general(a, b, (((1,), (1,)), ((), ())), preferred_element_type=F32)


def _dot_tn(a, b):
    return lax.dot_general(a, b, (((0,), (0,)), ((), ())), preferred_element_type=F32)


def _split3(x):
    hi = x.astype(BF16)
    lo = (x - hi.astype(F32)).astype(BF16)
    return hi, lo


def _dot3(a, b):
    ah, al = _split3(a)
    bh, bl = _split3(b)
    return _dot(ah, bh) + (_dot(ah, bl) + _dot(al, bh))


def _dot3_nt(a, b):
    ah, al = _split3(a)
    bh, bl = _split3(b)
    return _dot_nt(ah, bh) + (_dot_nt(ah, bl) + _dot_nt(al, bh))


def _ada_kernel(c_ref, w_ref, b_ref, o_ref):
    s = _silu(c_ref[...])
    o_ref[0] = _dot(s.astype(BF16), w_ref[0].astype(BF16)) + b_ref[0]


def ada_modulation(cond8, w_ada, b_ada):
    tn = 1536
    n = 6 * D_MODEL
    return pl.pallas_call(
        _ada_kernel,
        grid=(DEPTH, n // tn),
        in_specs=[
            pl.BlockSpec((8, D_MODEL), lambda l, j: (0, 0)),
            pl.BlockSpec((1, D_MODEL, tn), lambda l, j: (l, 0, j)),
            pl.BlockSpec((1, 1, tn), lambda l, j: (l, 0, j)),
        ],
        out_specs=pl.BlockSpec((1, 8, tn), lambda l, j: (l, 0, j)),
        out_shape=jax.ShapeDtypeStruct((DEPTH, 8, n), F32),
        compiler_params=_params(2),
        name="ada_modulation",
    )(cond8, w_ada, b_ada.reshape(DEPTH, 1, n))


PROJ_TM = 3072
PROJ_TN = 512
GATE_BLOCKS = 3 * D_MODEL // PROJ_TN
IN_BLOCKS = D_IN // PROJ_TN
P_WIDTH = 3 * D_MODEL + D_IN
KV_FIRST = GATE_BLOCKS + 1
P_Q = 3 * D_MODEL // W_A
P_HYENA = (3 * D_MODEL + 3 * W_A) // (3 * C_B)
P_RET_Q = (3 * D_MODEL + 3 * W_A + 3 * C_B) // (H_C * DK_C)
P_RET_V = (3 * D_MODEL + 3 * W_A + 3 * C_B + 2 * H_C * DK_C) // W_C


def _in_proj_kernel(x_hbm, g_ref, m_ref, win_ref, wg_ref, bg_ref, p_ref, kv_ref, h_ref, x_ref, sem, *, layer):
    i = pl.program_id(0)
    j = pl.program_id(1)

    def x_copy(tile):
        start = pl.multiple_of(tile * PROJ_TM, PROJ_TM)
        return pltpu.make_async_copy(x_hbm.at[pl.ds(start, PROJ_TM), :], x_ref, sem.at[0])

    @pl.when((i == 0) & (j == 0))
    def _():
        x_copy(0).start()

    @pl.when(j == 0)
    def _():
        x_copy(i).wait()
        for s in range(PROJ_TM // TM):
            rows = slice(s * TM, (s + 1) * TM)
            mod = m_ref[layer, _mod_row(i * (PROJ_TM // TM) + s)]
            y = _rms(x_ref[rows, :]) * g_ref[layer:layer + 1, :]
            h_ref[rows, :] = (y * (1.0 + mod[1:2, :]) + mod[0:1, :]).astype(BF16)

        @pl.when(i + 1 < T_ALL // PROJ_TM)
        def _():
            x_copy(i + 1).start()

    @pl.when(j < GATE_BLOCKS)
    def _():
        acc = _dot(h_ref[...], wg_ref[0].astype(BF16)) + bg_ref[0]
        p_ref[...] = _sigmoid(acc).astype(BF16)

    @pl.when(j >= GATE_BLOCKS)
    def _():
        acc = _dot(h_ref[...], win_ref[0].astype(BF16))
        p_ref[...] = acc.astype(BF16)

        @pl.when((j == KV_FIRST) | (j == KV_FIRST + 1))
        def _():
            kv_ref[...] = acc


def in_projection(x, gain, mods, w_in, w_gate, b_gate, layer):
    return pl.pallas_call(
        functools.partial(_in_proj_kernel, layer=layer),
        grid=(T_ALL // PROJ_TM, GATE_BLOCKS + IN_BLOCKS),
        in_specs=[
            pl.BlockSpec(memory_space=pl.ANY),
            pl.BlockSpec((DEPTH, D_MODEL), lambda i, j: (0, 0)),
            pl.BlockSpec((DEPTH, 8, 6, D_MODEL), lambda i, j: (0, 0, 0, 0)),
            pl.BlockSpec((1, D_MODEL, PROJ_TN), lambda i, j: (layer, 0, jnp.maximum(j - GATE_BLOCKS, 0))),
            pl.BlockSpec((1, D_MODEL, PROJ_TN), lambda i, j: (layer, 0, jnp.minimum(j, GATE_BLOCKS - 1))),
            pl.BlockSpec((1, 1, PROJ_TN), lambda i, j: (layer, 0, jnp.minimum(j, GATE_BLOCKS - 1))),
        ],
        out_specs=[
            pl.BlockSpec((PROJ_TM, PROJ_TN), lambda i, j: (i, j)),
            pl.BlockSpec((PROJ_TM, PROJ_TN), lambda i, j: (i, jnp.clip(j - KV_FIRST, 0, 1))),
        ],
        out_shape=[jax.ShapeDtypeStruct((T_ALL, P_WIDTH), BF16),
                   jax.ShapeDtypeStruct((T_ALL, 2 * W_A), F32)],
        scratch_shapes=[pltpu.VMEM((PROJ_TM, D_MODEL), BF16), pltpu.VMEM((PROJ_TM, D_MODEL), F32),
                        pltpu.SemaphoreType.DMA((1,))],
        compiler_params=_params(2),
        name="in_projection",
    )(x, gain, mods, w_in, w_gate, b_gate.reshape(DEPTH, 1, 3 * D_MODEL))


def _rope_tables():
    n = DEC_SEQ
    rows = n // GRID_W
    row = jnp.repeat(jnp.arange(rows, dtype=F32), GRID_W)
    col = jnp.tile(jnp.arange(GRID_W, dtype=F32), rows)
    quarter = DH_A // 4
    inv = ROPE_BASE ** (-jnp.arange(quarter, dtype=F32) / quarter)
    ar = row[:, None] * inv
    ac = col[:, None] * inv
    ang = jnp.concatenate([ar, ar, ac, ac], axis=-1)
    cos = jnp.tile(jnp.cos(ang), (1, 512 // DH_A))
    sgn = jnp.tile(jnp.concatenate([-jnp.ones((quarter,), F32), jnp.ones((quarter,), F32)]), 512 // (2 * quarter))
    sin = jnp.tile(jnp.sin(ang), (1, 512 // DH_A)) * sgn
    return cos, sin


def _rope(x, cos, sin):
    w = x.shape[-1]
    lane = lax.broadcasted_iota(jnp.int32, x.shape, 1)
    first = (lane % 32) < 16
    rot = jnp.where(first, pltpu.roll(x, w - 16, 1), pltpu.roll(x, 16, 1))
    return x * cos + rot * sin


def _lambda(al_ref, layer):
    al = al_ref[layer]
    a = jnp.sum(al[0:1, :] * al[1:2, :], axis=-1, keepdims=True)
    b = jnp.sum(al[2:3, :] * al[3:4, :], axis=-1, keepdims=True)
    lam_init = 0.8 - 0.6 * math.exp(-0.3 * layer)
    return jnp.exp(a) - jnp.exp(b) + lam_init, lam_init


def _scores(q, k):
    return _dot_nt(q, k).astype(BF16)


def _softmax_numerators(s):
    return jnp.exp2(s - jnp.max(s, axis=-1, keepdims=True))


def _diff_attention(q, k, v, lam, lam_init, subln, o_ref):
    lane = lax.broadcasted_iota(jnp.int32, q.shape, 1)
    q = q * (DH_A ** -0.5 * math.log2(math.e))
    q0 = jnp.where((lane % LANES) < DH_A, q, 0.0).astype(BF16)
    q1 = jnp.where((lane % LANES) >= DH_A, q, 0.0).astype(BF16)
    ones = jnp.ones((k.shape[0], LANES), BF16)
    for h in range(H_A):
        sl = slice(h * LANES, (h + 1) * LANES)
        kh = k[:, sl]
        v_ones = jnp.concatenate([v[:, sl], ones], axis=1)
        o0 = _dot(_softmax_numerators(_scores(q0[:, sl], kh)), v_ones)
        o1 = _dot(_softmax_numerators(_scores(q1[:, sl], kh)), v_ones)
        r0 = 1.0 / o0[:, DV_A:DV_A + 1]
        r1 = lam / o1[:, DV_A:DV_A + 1]
        o = o0[:, :DV_A] * r0 - o1[:, :DV_A] * r1
        o = _rms(o) * subln * (1.0 - lam_init)
        o_ref[:, sl] = o.astype(o_ref.dtype)


def _attn_ctx_kernel(q_ref, k_ref, v_ref, al_ref, g_ref, o_ref, knew_ref, vnew_ref, *, layer):
    lam, lam_init = _lambda(al_ref, layer)
    subln = g_ref[layer:layer + 1, :]
    k = k_ref[...]
    v = v_ref[...]
    for h in range(H_A):
        knew_ref[0, 0, :, h, :] = k[:, h * LANES:(h + 1) * LANES]
        vnew_ref[0, 0, :, h, :] = v[:, h * LANES:(h + 1) * LANES]
    _diff_attention(q_ref[...].astype(F32), k.astype(BF16), v.astype(BF16), lam, lam_init, subln, o_ref)


QB = 256


def _attn_lat_kernel(ya_ref, q_ref, k_ref, v_ref, ck_ref, cv_ref, cosq_ref, sinq_ref, cos_ref, sin_ref,
                     al_ref, g_ref, o_ref, kall_ref, vall_ref, *, layer):
    del ya_ref

    @pl.when(pl.program_id(1) == 0)
    def _():
        kall_ref[0:PAST_LEN, :] = ck_ref[0, 0].astype(BF16)
        vall_ref[0:PAST_LEN, :] = cv_ref[0, 0].astype(BF16)
        kall_ref[PAST_LEN:, :] = _rope(k_ref[...], cos_ref[...], sin_ref[...]).astype(BF16)
        vall_ref[PAST_LEN:, :] = v_ref[...].astype(BF16)

    lam, lam_init = _lambda(al_ref, layer)
    subln = g_ref[layer:layer + 1, :]
    q = _rope(q_ref[...].astype(F32), cosq_ref[...], sinq_ref[...])
    _diff_attention(q, kall_ref[...], vall_ref[...], lam, lam_init, subln, o_ref)


def attention_lat(ya, q, kv, cache_k, cache_v, cos, sin, attn_lambda, subln_g, layer):
    nqb = DEC_SEQ // QB
    row0 = T_CTX // QB
    seq0 = T_CTX // DEC_SEQ
    return pl.pallas_call(
        functools.partial(_attn_lat_kernel, layer=layer),
        grid=(DEC_BATCH, nqb),
        in_specs=[
            pl.BlockSpec(memory_space=pl.ANY),
            pl.BlockSpec((QB, W_A), lambda b, i: (row0 + b * nqb + i, P_Q)),
            pl.BlockSpec((DEC_SEQ, W_A), lambda b, i: (seq0 + b, 0)),
            pl.BlockSpec((DEC_SEQ, W_A), lambda b, i: (seq0 + b, 1)),
            pl.BlockSpec((1, 1, PAST_LEN, W_A), lambda b, i: (b, layer, 0, 0)),
            pl.BlockSpec((1, 1, PAST_LEN, W_A), lambda b, i: (b, layer, 0, 0)),
            pl.BlockSpec((QB, W_A), lambda b, i: (i, 0)),
            pl.BlockSpec((QB, W_A), lambda b, i: (i, 0)),
            pl.BlockSpec((DEC_SEQ, W_A), lambda b, i: (0, 0)),
            pl.BlockSpec((DEC_SEQ, W_A), lambda b, i: (0, 0)),
            pl.BlockSpec((DEPTH, 4, DH_A), lambda b, i: (0, 0, 0)),
            pl.BlockSpec((DEPTH, DV_A), lambda b, i: (0, 0)),
        ],
        out_specs=pl.BlockSpec((QB, W_A), lambda b, i: (row0 + b * nqb + i, 0)),
        out_shape=jax.ShapeDtypeStruct((T_ALL, W_A), BF16),
        scratch_shapes=[pltpu.VMEM((PAST_LEN + DEC_SEQ, W_A), BF16),
                        pltpu.VMEM((PAST_LEN + DEC_SEQ, W_A), BF16)],
        input_output_aliases={0: 0},
        compiler_params=_params(2),
        name="attention_lat",
    )(ya, q, kv, kv, cache_k, cache_v, cos, sin, cos, sin, attn_lambda, subln_g)


def _dft_tables(n):
    k = np.arange(n, dtype=np.int64)
    prod = (2 * k[:, None] + 1) * k[None, :]
    ang = (prod % (4 * n)).astype(np.float64) * (math.pi / (2 * n))
    c = np.cos(ang).astype(np.float32)
    s = np.sin(ang).astype(np.float32)
    return tuple(jnp.asarray(t).astype(BF16) for t in (c, s, c.T, s.T))


def _filter_features(n):
    t = jnp.linspace(0.0, 1.0, n, dtype=F32)[:, None]
    bands = (H_EMB - 1) // 2
    w = 2.0 * math.pi * jnp.arange(n, dtype=F32)[:, None] / n
    fr = jnp.linspace(1e-4, bands - 1, bands, dtype=F32)
    z = jnp.concatenate([t, jnp.cos(w * fr), -jnp.sin(w * fr)], axis=-1)
    return jnp.pad(z, ((0, 0), (0, LANES - H_EMB)))


def _filter_decay_rates():
    max_decay = math.log(H_TARGET) / H_FAST_DECAY
    min_decay = math.log(H_TARGET) / H_SLOW_DECAY
    return jnp.abs(jnp.linspace(min_decay, max_decay, C_B, dtype=F32))[None, :]


def _hyena_filter_kernel(z_ref, w1_ref, b1_ref, w2_ref, b2_ref, w3_ref, dr_ref, c_ref, s_ref,
                         gre_ref, gim_ref):
    z = z_ref[...]
    hid = jnp.sin(_dot3(z, w1_ref[0]) + b1_ref[0])
    hid = jnp.sin(_dot3(hid, w2_ref[0]) + b2_ref[0])
    h = _dot(hid.astype(BF16), w3_ref[0].astype(BF16))
    window = jnp.exp(-z[:, 0:1] * dr_ref[...])
    hf = h[:, :C_B] * window
    hb = h[:, C_B:] * window
    total = jnp.sum(jnp.abs(hf) + jnp.abs(hb), axis=0, keepdims=True)
    hf = hf / total
    hb = hb / total
    row = lax.broadcasted_iota(jnp.int32, hb.shape, 0)
    hb = jnp.where(row == 0, 0.0, hb)
    gre_ref[0] = _dot(c_ref[...], (hf + hb).astype(BF16))
    gim_ref[0] = _dot(s_ref[...], (hb - hf).astype(BF16))


def hyena_filters(n, z, decay_rates, cf, sf, f_w1, f_b1, f_w2, f_b2, f_w3):
    w1 = jnp.pad(f_w1, ((0, 0), (0, LANES - H_EMB), (0, 0)))
    full = lambda shape: pl.BlockSpec(shape, lambda l: (0,) * len(shape))
    per_layer = lambda shape: pl.BlockSpec((1,) + shape, lambda l: (l,) + (0,) * len(shape))
    return pl.pallas_call(
        _hyena_filter_kernel,
        grid=(DEPTH,),
        in_specs=[
            full((n, LANES)),
            per_layer((LANES, H_FFN)), per_layer((1, H_FFN)),
            per_layer((H_FFN, H_FFN)), per_layer((1, H_FFN)),
            per_layer((H_FFN, 2 * C_B)),
            full((1, C_B)), full((n, n)), full((n, n)),
        ],
        out_specs=[per_layer((n, C_B)), per_layer((n, C_B))],
        out_shape=[jax.ShapeDtypeStruct((DEPTH, n, C_B), F32)] * 2,
        compiler_params=_params(1),
        name="hyena_filters",
    )(z, w1, f_b1.reshape(DEPTH, 1, H_FFN), f_w2, f_b2.reshape(DEPTH, 1, H_FFN), f_w3,
      decay_rates, cf, sf)


def _hyena_kernel(*refs, layer, n, aliased):
    if aliased:
        refs = refs[1:]
    u_ref, cw_ref, cb_ref, skip_ref, gre_ref, gim_ref, c_ref, s_ref, ct_ref, st_ref, o_ref = refs
    u = u_ref[...].astype(F32)
    row = lax.broadcasted_iota(jnp.int32, u.shape, 0)
    prev = jnp.where(row == 0, 0.0, pltpu.roll(u, 1, 0))
    nxt = jnp.where(row == n - 1, 0.0, pltpu.roll(u, n - 1, 0))
    cw = cw_ref[layer]
    uc = cb_ref[layer:layer + 1, :] + prev * cw[0:1, :] + u * cw[1:2, :] + nxt * cw[2:3, :]
    x0 = uc[:, :C_B]
    x1 = uc[:, C_B:2 * C_B]
    v = uc[:, 2 * C_B:]
    w = v * x1
    wb = w.astype(BF16)
    ure = _dot(c_ref[...], wb)
    uim = -_dot(s_ref[...], wb)
    gre = gre_ref[0]
    gim = gim_ref[0]
    yre = (ure * gre - uim * gim).astype(BF16)
    yim = (ure * gim + uim * gre).astype(BF16)
    y = (_dot(ct_ref[...], yre) - _dot(st_ref[...], yim)) * (1.0 / n)
    y = y + w * skip_ref[layer:layer + 1, :]
    o_ref[...] = (x0 * y).astype(o_ref.dtype)


def _full(shape):
    return pl.BlockSpec(shape, lambda *_: (0,) * len(shape))


def _hyena_specs(n, seq0, layer):
    return [
        pl.BlockSpec((n, 3 * C_B), lambda b, *_: (seq0 + b, P_HYENA)),
        _full((DEPTH, SHORT_K, 3 * C_B)), _full((DEPTH, 3 * C_B)), _full((DEPTH, C_B)),
        pl.BlockSpec((1, n, C_B), lambda *_: (layer, 0, 0)),
        pl.BlockSpec((1, n, C_B), lambda *_: (layer, 0, 0)),
        _full((n, n)), _full((n, n)), _full((n, n)), _full((n, n)),
    ]


def hyena_lat(yb, p, conv_w, conv_b, skip, gre, gim, tables, layer):
    seq0 = T_CTX // DEC_SEQ
    return pl.pallas_call(
        functools.partial(_hyena_kernel, layer=layer, n=DEC_SEQ, aliased=True),
        grid=(DEC_BATCH,),
        in_specs=[pl.BlockSpec(memory_space=pl.ANY)] + _hyena_specs(DEC_SEQ, seq0, layer),
        out_specs=pl.BlockSpec((DEC_SEQ, C_B), lambda b: (seq0 + b, 0)),
        out_shape=jax.ShapeDtypeStruct((T_ALL, C_B), BF16),
        input_output_aliases={0: 0},
        compiler_params=_params(1),
        name="hyena_lat",
    )(yb, p, conv_w, conv_b, skip, gre, gim, *tables)


RET_QB = 256


def _retention_kernel(*refs, layer, n, latent):
    if latent:
        _, q_ref, k_ref, v_ref, g_ref, de_ref, cos_ref, sin_ref, s0_ref, o_ref, w_ref, vec_ref = refs
        st_ref = None
    else:
        q_ref, k_ref, v_ref, g_ref, de_ref, o_ref, st_ref, w_ref, vec_ref = refs[-9:]
    log_gamma = jnp.log1p(-jnp.exp2(-de_ref[layer]))

    @pl.when(pl.program_id(0) == 0)
    def _():
        t = lax.broadcasted_iota(jnp.int32, (n, n), 0)
        s = lax.broadcasted_iota(jnp.int32, (n, n), 1)
        lag = (t - s).astype(F32)
        for h in range(H_C):
            rate = jnp.where(lag > 0, log_gamma[0:1, h:h + 1], log_gamma[1:2, h:h + 1])
            w_ref[h] = jnp.where(lag == 0, 2.0, jnp.exp(rate * jnp.abs(lag)))
            pos = lax.broadcasted_iota(jnp.int32, (n, LANES), 0).astype(F32)
            steps = (pos + 1.0, float(n) - pos) if latent else (float(n - 1) - pos, pos)
            for d in range(2):
                vec_ref[h, d] = jnp.exp(log_gamma[d:d + 1, h:h + 1] * steps[d])

    q = q_ref[...].astype(F32)
    k = k_ref[...].astype(F32) * (DK_C ** -0.5)
    if latent:
        q = _rope(q, cos_ref[...], sin_ref[...])
        k = _rope(k, cos_ref[...], sin_ref[...])
    v = v_ref[...].astype(BF16)
    gate = g_ref[...].astype(F32)
    lane = lax.broadcasted_iota(jnp.int32, (n, LANES), 1)
    zeros64 = jnp.zeros((DK_C, DV_C), F32)

    for h in range(H_C):
        pair = slice((h // 2) * LANES, (h // 2 + 1) * LANES)
        lo = (h % 2) * DK_C
        own = (lane >= lo) & (lane < lo + DK_C)
        vs = slice(h * DV_C, (h + 1) * DV_C)
        qh = jnp.where(own, q[:, pair], 0.0).astype(BF16)
        kh = jnp.where(own, k[:, pair], 0.0)
        kb = kh.astype(BF16)
        vh = v[:, vs]
        if latent:
            s0 = [jnp.concatenate([s0_ref[0, 0, d, h], zeros64] if lo == 0 else [zeros64, s0_ref[0, 0, d, h]],
                                  axis=0).astype(BF16) for d in range(2)]
            carry = _dot(qh, s0[0]) * vec_ref[h, 0] + _dot(qh, s0[1]) * vec_ref[h, 1]
        for r0 in range(0, n, RET_QB):
            rows = slice(r0, r0 + RET_QB)
            att = _dot_nt(qh[rows], kb) * w_ref[h, rows, :]
            out = _dot(att.astype(BF16), vh)
            if latent:
                out = out + carry[rows]
            o_ref[rows, vs] = (_rms(out) * _silu(gate[rows, vs])).astype(o_ref.dtype)
        if st_ref is not None:
            sf = _dot_tn((kh * vec_ref[h, 0]).astype(BF16), vh)
            sb = _dot_tn((kh * vec_ref[h, 1]).astype(BF16), vh)
            st_ref[0, 0, 0, h] = sf[lo:lo + DK_C, :]
            st_ref[0, 0, 1, h] = sb[lo:lo + DK_C, :]


def _retention_specs(n, seq0):
    return [
        pl.BlockSpec((n, H_C * DK_C), lambda b, *_: (seq0 + b, P_RET_Q)),
        pl.BlockSpec((n, H_C * DK_C), lambda b, *_: (seq0 + b, P_RET_Q + 1)),
        pl.BlockSpec((n, W_C), lambda b, *_: (seq0 + b, P_RET_V)),
        pl.BlockSpec((n, W_C), lambda b, *_: (seq0 + b, P_RET_V + 1)),
        _full((DEPTH, 2, H_C)),
    ]


def retention_lat(yc, p, decay_exp, cos, sin, state0, layer):
    n = DEC_SEQ
    seq0 = T_CTX // DEC_SEQ
    return pl.pallas_call(
        functools.partial(_retention_kernel, layer=layer, n=n, latent=True),
        grid=(DEC_BATCH,),
        in_specs=[pl.BlockSpec(memory_space=pl.ANY)] + _retention_specs(n, seq0) + [
            _full((n, H_C * DK_C)), _full((n, H_C * DK_C)),
            pl.BlockSpec((1, 1, 2, H_C, DK_C, DV_C), lambda b: (b, layer, 0, 0, 0, 0)),
        ],
        out_specs=pl.BlockSpec((n, W_C), lambda b: (seq0 + b, 0)),
        out_shape=jax.ShapeDtypeStruct((T_ALL, W_C), BF16),
        scratch_shapes=[pltpu.VMEM((H_C, n, n), F32), pltpu.VMEM((H_C, 2, n, LANES), F32)],
        input_output_aliases={0: 0},
        compiler_params=_params(1),
        name="retention_lat",
    )(yc, p, p, p, p, decay_exp, cos, sin, state0)


N_ATTN_IN = 5
N_HYENA_IN = 10
N_RET_IN = 5


def _mixers_ctx_kernel(*refs, layer, n_threaded):
    refs = refs[n_threaded:]
    attn_in = refs[:N_ATTN_IN]
    hyena_in = refs[N_ATTN_IN:N_ATTN_IN + N_HYENA_IN]
    ret_in = refs[N_ATTN_IN + N_HYENA_IN:N_ATTN_IN + N_HYENA_IN + N_RET_IN]
    ya_ref, knew_ref, vnew_ref, yb_ref, yc_ref, st_ref, w_ref, vec_ref = refs[N_ATTN_IN + N_HYENA_IN + N_RET_IN:]
    _attn_ctx_kernel(*attn_in, ya_ref, knew_ref, vnew_ref, layer=layer)
    _hyena_kernel(*hyena_in, yb_ref, layer=layer, n=SEQ, aliased=False)
    _retention_kernel(*ret_in, yc_ref, st_ref, w_ref, vec_ref, layer=layer, n=SEQ, latent=False)


def mixers_ctx(p, kv, attn_lambda, subln_g, conv_w, conv_b, skip, gre, gim, tables, decay_exp, layer, threaded):
    cache_shape = jax.ShapeDtypeStruct((BATCH, DEPTH, SEQ, H_A, DV_A), F32)
    cache_spec = pl.BlockSpec((1, 1, SEQ, H_A, DV_A), lambda b: (b, layer, 0, 0, 0))
    branch_shape = jax.ShapeDtypeStruct((T_ALL, W_A), BF16)
    branch_spec = pl.BlockSpec((SEQ, W_A), lambda b: (b, 0))
    n_threaded = len(threaded)
    attn_specs = [
        pl.BlockSpec((SEQ, W_A), lambda b: (b, P_Q)),
        pl.BlockSpec((SEQ, W_A), lambda b: (b, 0)),
        pl.BlockSpec((SEQ, W_A), lambda b: (b, 1)),
        _full((DEPTH, 4, DH_A)), _full((DEPTH, DV_A)),
    ]
    return pl.pallas_call(
        functools.partial(_mixers_ctx_kernel, layer=layer, n_threaded=n_threaded),
        grid=(BATCH,),
        in_specs=([pl.BlockSpec(memory_space=pl.ANY)] * n_threaded + attn_specs
                  + _hyena_specs(SEQ, 0, layer) + _retention_specs(SEQ, 0)),
        out_specs=[branch_spec, cache_spec, cache_spec, branch_spec, branch_spec,
                   pl.BlockSpec((1, 1, 2, H_C, DK_C, DV_C), lambda b: (b, layer, 0, 0, 0, 0))],
        out_shape=[branch_shape, cache_shape, cache_shape, branch_shape, branch_shape,
                   jax.ShapeDtypeStruct((BATCH, DEPTH, 2, H_C, DK_C, DV_C), F32)],
        scratch_shapes=[pltpu.VMEM((H_C, SEQ, SEQ), F32), pltpu.VMEM((H_C, 2, SEQ, LANES), F32)],
        input_output_aliases=dict(zip(range(n_threaded), (1, 2, 5))),
        compiler_params=_params(1),
        name="mixers_ctx",
    )(*threaded, p, kv, kv, attn_lambda, subln_g, p, conv_w, conv_b, skip, gre, gim, *tables,
      p, p, p, p, decay_exp)


def _merge_kernel(x_ref, ya_ref, yb_ref, yc_ref, g_ref, m_ref, wa_ref, wb_ref, wc_ref, wo_ref,
                  o_ref, wbf_ref, wobf_ref):
    @pl.when(pl.program_id(0) == 0)
    def _():
        wbf_ref[0] = wa_ref[0].astype(BF16)
        wbf_ref[1] = wb_ref[0].astype(BF16)
        wbf_ref[2] = wc_ref[0].astype(BF16)
        wobf_ref[...] = wo_ref[0].astype(BF16)

    g = g_ref[...]
    merged = (g[:, :D_MODEL] * _dot(ya_ref[...], wbf_ref[0])
              + g[:, D_MODEL:2 * D_MODEL] * _dot(yb_ref[...], wbf_ref[1])
              + g[:, 2 * D_MODEL:] * _dot(yc_ref[...], wbf_ref[2]))
    g1 = m_ref[0, 0, 2:3, :]
    o_ref[...] = x_ref[...] + g1 * _dot(merged.astype(BF16), wobf_ref[...])


def merge_branches(x, ya, yb, yc, gates, mods, w_a, w_b, w_c, w_out, layer):
    tile = lambda w: pl.BlockSpec((TM, w), lambda i: (i, 0))
    wspec = lambda k: pl.BlockSpec((1, k, D_MODEL), lambda i: (layer, 0, 0))
    return pl.pallas_call(
        _merge_kernel,
        grid=(N_TILES,),
        in_specs=[
            tile(D_MODEL), tile(W_A), tile(C_B), tile(W_C), tile(3 * D_MODEL),
            pl.BlockSpec((1, 1, 6, D_MODEL), lambda i: (layer, _mod_row(i), 0, 0)),
            wspec(W_A), wspec(C_B), wspec(W_C), wspec(D_MODEL),
        ],
        out_specs=tile(D_MODEL),
        out_shape=jax.ShapeDtypeStruct((T_ALL, D_MODEL), F32),
        scratch_shapes=[pltpu.VMEM((3, W_A, D_MODEL), BF16), pltpu.VMEM((D_MODEL, D_MODEL), BF16)],
        compiler_params=_params(1),
        name="merge_branches",
    )(x, ya, yb, yc, gates, mods, w_a, w_b, w_c, w_out)


def _route(h2, wr_t, b_r):
    logits = _dot3_nt(wr_t, h2)
    m = jnp.max(logits, axis=0, keepdims=True)
    e = jnp.exp(logits - m)
    scores = e / jnp.sum(e, axis=0, keepdims=True)
    sel = scores + b_r
    rows = [sel[i:i + 1, :] for i in range(N_EXPERTS)]
    in_group = []
    gscore = []
    for g in range(N_GROUPS):
        members = range(g * EXP_PER_GROUP, (g + 1) * EXP_PER_GROUP)
        total = None
        for i in members:
            rank = None
            for j in members:
                if j == i:
                    continue
                ahead = (rows[j] >= rows[i]) if j < i else (rows[j] > rows[i])
                ahead = ahead.astype(F32)
                rank = ahead if rank is None else rank + ahead
            chosen = rank < 2.0
            in_group.append(chosen)
            part = jnp.where(chosen, rows[i], 0.0)
            total = part if total is None else total + part
        gscore.append(total)
    gates = []
    group_hot = []
    for g in range(N_GROUPS):
        best = None
        for g2 in range(N_GROUPS):
            if g2 == g:
                continue
            wins = gscore[g] > gscore[g2] if g2 < g else gscore[g] >= gscore[g2]
            best = wins if best is None else best & wins
        group_hot.append(best.astype(F32))
        for i in range(g * EXP_PER_GROUP, (g + 1) * EXP_PER_GROUP):
            gates.append(jnp.where(best & in_group[i], scores[i:i + 1, :], 0.0))
    gates = jnp.concatenate(gates, axis=0)
    return gates / jnp.sum(gates, axis=0, keepdims=True), group_hot


def _moe_route_kernel(x_ref, g_ref, m_ref, wr_ref, br_ref, tri_ref, tiles_ref, gate_ref, meta_ref, count_ref,
                      *, layer):
    y = _rms(x_ref[...]) * g_ref[layer:layer + 1, :]
    h2 = y * (1.0 + m_ref[0, 0, 4:5, :]) + m_ref[0, 0, 3:4, :]
    gates, group_hot = _route(h2, wr_ref[...], br_ref[...])
    for s in range(SUBLANES):
        tiles_ref[pl.ds(s, TM, stride=SUBLANES), :] = h2[:, s * LANES:(s + 1) * LANES]
    padded = jnp.concatenate([gates, jnp.zeros((LANES - N_EXPERTS, TM), F32)], axis=0)
    gate_ref[...] = padded.T
    row = lax.broadcasted_iota(jnp.int32, (SUBLANES, TM), 0)
    hot = jnp.zeros((SUBLANES, TM), F32)
    for g in range(N_GROUPS):
        hot = jnp.where(row == g, group_hot[g], hot)
    before = _dot(hot.astype(BF16), tri_ref[...])
    rank = jnp.sum(hot * before, axis=0, keepdims=True)
    gid = group_hot[1] + 2.0 * group_hot[2] + 3.0 * group_hot[3]
    meta = jnp.where(row == 0, gid, jnp.where(row == 1, rank, 0.0))
    meta_ref[0] = meta.astype(jnp.int32)
    counts = jnp.sum(hot, axis=1, keepdims=True) + jnp.zeros((SUBLANES, LANES), F32)
    count_ref[0] = counts.astype(jnp.int32)


def moe_route(x, gain, mods, wr_t, b_r, tri, layer):
    return pl.pallas_call(
        functools.partial(_moe_route_kernel, layer=layer),
        grid=(N_TILES,),
        in_specs=[
            pl.BlockSpec((TM, D_MODEL), lambda i: (i, 0)),
            pl.BlockSpec((DEPTH, D_MODEL), lambda i: (0, 0)),
            pl.BlockSpec((1, 1, 6, D_MODEL), lambda i: (layer, _mod_row(i), 0, 0)),
            pl.BlockSpec((N_EXPERTS, D_MODEL), lambda i: (0, 0)),
            pl.BlockSpec((N_EXPERTS, 1), lambda i: (0, 0)),
            pl.BlockSpec((TM, TM), lambda i: (0, 0)),
        ],
        out_specs=[pl.BlockSpec((TM * SUBLANES, LANES), lambda i: (i, 0)),
                   pl.BlockSpec((TM, LANES), lambda i: (i, 0)),
                   pl.BlockSpec((1, SUBLANES, TM), lambda i: (i, 0, 0)),
                   pl.BlockSpec((1, SUBLANES, LANES), lambda i: (i, 0, 0))],
        out_shape=[jax.ShapeDtypeStruct((T_ALL * SUBLANES, LANES), F32),
                   jax.ShapeDtypeStruct((T_ALL, LANES), F32),
                   jax.ShapeDtypeStruct((N_TILES, SUBLANES, TM), jnp.int32),
                   jax.ShapeDtypeStruct((N_TILES, SUBLANES, LANES), jnp.int32)],
        compiler_params=_params(1),
        name="moe_route",
    )(x, gain, mods, wr_t, b_r, tri)


def _moe_positions(meta, counts):
    gid = meta[:, 0, :]
    rank = meta[:, 1, :]
    cnt = counts[:, :N_GROUPS, 0]
    total = jnp.sum(cnt, axis=0)
    padded = (total + (MOE_BLOCK - 1)) // MOE_BLOCK * MOE_BLOCK
    group_start = jnp.cumsum(padded) - padded
    base = group_start[None, :] + jnp.cumsum(cnt, axis=0) - cnt
    pos = rank
    for g in range(N_GROUPS):
        pos = pos + jnp.where(gid == g, base[:, g:g + 1], 0)
    first_block = group_start // MOE_BLOCK
    blk = jnp.arange(MOE_BLOCKS, dtype=jnp.int32)
    block_gid = sum((blk >= first_block[g]).astype(jnp.int32) for g in range(1, N_GROUPS))
    block_used = (blk * MOE_BLOCK < jnp.sum(padded)).astype(jnp.int32)
    later = jnp.where(block_gid[None, :] > block_gid[:, None], block_gid[None, :], N_GROUPS)
    next_group = jnp.min(later, axis=1)
    next_group = jnp.where(next_group == N_GROUPS, -1, next_group).astype(jnp.int32)
    return pos.reshape(T_ALL).astype(jnp.int32), block_gid, block_used, next_group


def _moe_permute_kernel(pos_ref, tiles_ref, gate_ref, sorted_ref, gsorted_ref):
    i = pl.program_id(0)

    @pl.when(i == 0)
    def _():
        def zero(b, carry):
            start = pl.multiple_of(b * TM, TM)
            sorted_ref[pl.ds(start, TM), :] = jnp.zeros((TM, LANES), F32)
            return carry
        lax.fori_loop(0, T_PAD * SUBLANES // TM, zero, 0)
        gsorted_ref[...] = jnp.zeros((T_PAD, LANES), F32)

    def move(t, carry):
        p = pos_ref[i * TM + t]
        dst = pl.multiple_of(p * SUBLANES, SUBLANES)
        src = pl.multiple_of(t * SUBLANES, SUBLANES)
        sorted_ref[pl.ds(dst, SUBLANES), :] = tiles_ref[pl.ds(src, SUBLANES), :]
        gsorted_ref[pl.ds(p, 1), :] = gate_ref[pl.ds(t, 1), :]
        return carry
    lax.fori_loop(0, TM, move, 0, unroll=8)


def moe_permute(pos, tiles, gates):
    return pl.pallas_call(
        _moe_permute_kernel,
        grid_spec=pltpu.PrefetchScalarGridSpec(
            num_scalar_prefetch=1,
            grid=(N_TILES,),
            in_specs=[pl.BlockSpec((TM * SUBLANES, LANES), lambda i, pos: (i, 0)),
                      pl.BlockSpec((TM, LANES), lambda i, pos: (i, 0))],
            out_specs=[pl.BlockSpec(memory_space=pltpu.VMEM), pl.BlockSpec(memory_space=pltpu.VMEM)],
        ),
        out_shape=[jax.ShapeDtypeStruct((T_PAD * SUBLANES, LANES), F32),
                   jax.ShapeDtypeStruct((T_PAD, LANES), F32)],
        compiler_params=_params(1),
        name="moe_permute",
    )(pos, tiles, gates)


def _group_changed(gid_ref, b):
    return (b == 0) | (gid_ref[b] != gid_ref[jnp.maximum(b - 1, 0)])


def _moe_expert_kernel(gid_ref, used_ref, next_ref, s_ref, gate_ref, w1_hbm, w3_hbm, w2_ref, y_ref,
                       w1f_ref, w3f_ref, w1b_ref, w3b_ref, w2b_ref, sem, *, layer):
    b = pl.program_id(0)

    def up_weight_copies(group):
        return (pltpu.make_async_copy(w1_hbm.at[layer, group], w1f_ref, sem.at[0]),
                pltpu.make_async_copy(w3_hbm.at[layer, group], w3f_ref, sem.at[1]))

    @pl.when(b == 0)
    def _():
        for copy in up_weight_copies(gid_ref[0]):
            copy.start()

    @pl.when(_group_changed(gid_ref, b))
    def _():
        for copy in up_weight_copies(gid_ref[b]):
            copy.wait()
        for j in range(EXP_PER_GROUP):
            cols = slice(j * D_FF, (j + 1) * D_FF)
            w1b_ref[:, cols] = w1f_ref[j].astype(BF16)
            w3b_ref[:, cols] = w3f_ref[j].astype(BF16)
        w2b_ref[...] = w2_ref[0, 0].astype(BF16)

        @pl.when(next_ref[b] >= 0)
        def _():
            for copy in up_weight_copies(next_ref[b]):
                copy.start()

    @pl.when(used_ref[b] == 1)
    def _():
        lhs = jnp.concatenate([s_ref[pl.ds(s, MOE_BLOCK, stride=SUBLANES), :].astype(BF16)
                               for s in range(SUBLANES)], axis=1)
        a = _dot(lhs, w1b_ref[...])
        g = _dot(lhs, w3b_ref[...])
        gates = gate_ref[...]
        lane = lax.broadcasted_iota(jnp.int32, (MOE_BLOCK, LANES), 1)
        first = gid_ref[b] * EXP_PER_GROUP
        parts = []
        for j in range(EXP_PER_GROUP):
            cols = slice(j * D_FF, (j + 1) * D_FF)
            gate = jnp.sum(jnp.where(lane == first + j, gates, 0.0), axis=1, keepdims=True)
            parts.append((_silu(a[:, cols]) * g[:, cols] * gate).astype(BF16))
        act = jnp.concatenate(parts, axis=1)
        y = _dot(act, w2b_ref[...])
        for s in range(SUBLANES):
            y_ref[pl.ds(s, MOE_BLOCK, stride=SUBLANES), :] = y[:, s * LANES:(s + 1) * LANES]

    @pl.when(used_ref[b] == 0)
    def _():
        y_ref[...] = jnp.zeros_like(y_ref)


def moe_experts(block_gid, block_used, next_group, sorted_rows, sorted_gates, w1, w3, w2, layer):
    group_ff = EXP_PER_GROUP * D_FF
    w1g = w1.reshape(DEPTH, N_GROUPS, EXP_PER_GROUP, D_MODEL, D_FF)
    w3g = w3.reshape(DEPTH, N_GROUPS, EXP_PER_GROUP, D_MODEL, D_FF)
    w2g = w2.reshape(DEPTH, N_GROUPS, group_ff, D_MODEL)
    block = lambda b, gid, used, nxt: (b, 0)
    return pl.pallas_call(
        functools.partial(_moe_expert_kernel, layer=layer),
        grid_spec=pltpu.PrefetchScalarGridSpec(
            num_scalar_prefetch=3,
            grid=(MOE_BLOCKS,),
            in_specs=[pl.BlockSpec((MOE_BLOCK * SUBLANES, LANES), block),
                      pl.BlockSpec((MOE_BLOCK, LANES), block),
                      pl.BlockSpec(memory_space=pl.ANY), pl.BlockSpec(memory_space=pl.ANY),
                      pl.BlockSpec((1, 1, group_ff, D_MODEL), lambda b, gid, used, nxt: (layer, gid[b], 0, 0))],
            out_specs=pl.BlockSpec((MOE_BLOCK * SUBLANES, LANES), block),
            scratch_shapes=[pltpu.VMEM((EXP_PER_GROUP, D_MODEL, D_FF), F32),
                            pltpu.VMEM((EXP_PER_GROUP, D_MODEL, D_FF), F32),
                            pltpu.VMEM((D_MODEL, group_ff), BF16), pltpu.VMEM((D_MODEL, group_ff), BF16),
                            pltpu.VMEM((group_ff, D_MODEL), BF16),
                            pltpu.SemaphoreType.DMA((2,))],
        ),
        out_shape=jax.ShapeDtypeStruct((T_PAD * SUBLANES, LANES), F32),
        compiler_params=_params(1),
        name="moe_experts",
    )(block_gid, block_used, next_group, sorted_rows, sorted_gates, w1g, w3g, w2g)


FINAL_TM = 512


def _moe_combine_kernel(pos_ref, ys_ref, x_ref, m_ref, *rest, tm, final):
    i = pl.program_id(0)
    buf_ref = rest[-2] if final else rest[-1]

    def move(t, carry):
        src = pl.multiple_of(pos_ref[i * tm + t] * SUBLANES, SUBLANES)
        dst = pl.multiple_of(t * SUBLANES, SUBLANES)
        buf_ref[pl.ds(dst, SUBLANES), :] = ys_ref[pl.ds(src, SUBLANES), :]
        return carry
    lax.fori_loop(0, tm, move, 0, unroll=8)

    x_new_ref = rest[-1] if final else rest[0]
    for s in range(SUBLANES):
        cols = slice(s * LANES, (s + 1) * LANES)
        y = buf_ref[pl.ds(s, tm, stride=SUBLANES), :]
        x_new_ref[:, cols] = x_ref[:, cols] + m_ref[0, 0, 5:6, cols] * y

    if final:
        gain_ref, ctx_ref, lat_ref = rest[:3]
        out = _rms(x_new_ref[...]) * gain_ref[...]

        @pl.when(i < T_CTX // tm)
        def _():
            ctx_ref[...] = out

        @pl.when(i >= T_CTX // tm)
        def _():
            lat_ref[...] = out


def moe_combine(pos, y_sorted, x, mods, layer, final_gain=None):
    final = final_gain is not None
    tm = FINAL_TM if final else TM
    ctx_tiles = T_CTX // tm
    tile = pl.BlockSpec((tm, D_MODEL), lambda i, pos: (i, 0))
    in_specs = [
        pl.BlockSpec(memory_space=pltpu.VMEM),
        tile,
        pl.BlockSpec((1, 1, 6, D_MODEL), lambda i, pos: (layer, _mod_row(i * tm // TM), 0, 0)),
    ]
    args = [pos, y_sorted, x, mods]
    scratch = [pltpu.VMEM((tm * SUBLANES, LANES), F32)]
    if final:
        in_specs.append(pl.BlockSpec((1, D_MODEL), lambda i, pos: (0, 0)))
        args.append(final_gain.reshape(1, D_MODEL))
        out_specs = [pl.BlockSpec((tm, D_MODEL), lambda i, pos: (jnp.minimum(i, ctx_tiles - 1), 0)),
                     pl.BlockSpec((tm, D_MODEL), lambda i, pos: (jnp.maximum(i - ctx_tiles, 0), 0))]
        out_shape = [jax.ShapeDtypeStruct((T_CTX, D_MODEL), F32), jax.ShapeDtypeStruct((T_LAT, D_MODEL), F32)]
        scratch.append(pltpu.VMEM((tm, D_MODEL), F32))
    else:
        out_specs = tile
        out_shape = jax.ShapeDtypeStruct((T_ALL, D_MODEL), F32)
    return pl.pallas_call(
        functools.partial(_moe_combine_kernel, tm=tm, final=final),
        grid_spec=pltpu.PrefetchScalarGridSpec(
            num_scalar_prefetch=1,
            grid=(T_ALL // tm,),
            in_specs=in_specs,
            out_specs=out_specs,
            scratch_shapes=scratch,
        ),
        out_shape=out_shape,
        compiler_params=_params(1),
        name="moe_combine",
    )(*args)


def moe(x, gain, mods, wr_t, b_r, tri, w1, w3, w2, layer, final_gain=None):
    tiles, gates, meta, counts = moe_route(x, gain, mods, wr_t, b_r, tri, layer)
    pos, block_gid, block_used, next_group = _moe_positions(meta, counts)
    sorted_rows, sorted_gates = moe_permute(pos, tiles, gates)
    y_sorted = moe_experts(block_gid, block_used, next_group, sorted_rows, sorted_gates, w1, w3, w2, layer)
    return moe_combine(pos, y_sorted, x, mods, layer, final_gain)


def kernel(x_prompt, x_sample, cache_attn_k, cache_attn_v, state_retention, c, c_ctx, w_ada, b_ada, norm1_g, norm2_g, final_g, w_in, attn_lambda, attn_subln_g, hy_conv_w, hy_conv_b, hy_f_w1, hy_f_b1, hy_f_w2, hy_f_b2, hy_f_w3, hy_skip, ret_decay_exp, w_branch_a, w_branch_b, w_branch_c, w_gate, b_gate, w_out, w_router, b_router, moe_w1, moe_w3, moe_w2):
    x = jnp.concatenate([x_prompt.reshape(T_CTX, D_MODEL), x_sample.reshape(T_LAT, D_MODEL)], axis=0)
    cond8 = jnp.concatenate([c_ctx[None, :], c, jnp.zeros((8 - 1 - DEC_BATCH, D_MODEL), F32)], axis=0)
    mods = ada_modulation(cond8, w_ada, b_ada).reshape(DEPTH, 8, 6, D_MODEL)

    cos, sin = _rope_tables()
    cos_c, sin_c = cos[:, :H_C * DK_C], sin[:, :H_C * DK_C]
    cache_k = cache_attn_k.reshape(DEC_BATCH, DEPTH, PAST_LEN, W_A)
    cache_v = cache_attn_v.reshape(DEC_BATCH, DEPTH, PAST_LEN, W_A)
    decay_rates = _filter_decay_rates()
    tables_ctx = _dft_tables(SEQ)
    tables_lat = _dft_tables(DEC_SEQ)
    filt_ctx = hyena_filters(SEQ, _filter_features(SEQ), decay_rates, tables_ctx[0], tables_ctx[1],
                             hy_f_w1, hy_f_b1, hy_f_w2, hy_f_b2, hy_f_w3)
    filt_lat = hyena_filters(DEC_SEQ, _filter_features(DEC_SEQ), decay_rates, tables_lat[0], tables_lat[1],
                             hy_f_w1, hy_f_b1, hy_f_w2, hy_f_b2, hy_f_w3)
    wr_t = w_router.T
    b_r = b_router.reshape(N_EXPERTS, 1)
    tri = jnp.asarray(np.triu(np.ones((TM, TM), np.float32), 1), dtype=BF16)

    threaded = ()
    for l in range(DEPTH):
        p, kv = in_projection(x, norm1_g, mods, w_in, w_gate, b_gate, l)

        ya, new_k, new_v, yb, yc, states = mixers_ctx(
            p, kv, attn_lambda, attn_subln_g, hy_conv_w, hy_conv_b, hy_skip, filt_ctx[0], filt_ctx[1],
            tables_ctx, ret_decay_exp, l, threaded)
        threaded = (new_k, new_v, states)
        ya = attention_lat(ya, p, kv, cache_k, cache_v, cos, sin, attn_lambda, attn_subln_g, l)
        yb = hyena_lat(yb, p, hy_conv_w, hy_conv_b, hy_skip, filt_lat[0], filt_lat[1], tables_lat, l)
        yc = retention_lat(yc, p, ret_decay_exp, cos_c, sin_c, state_retention, l)

        x = merge_branches(x, ya, yb, yc, p, mods, w_branch_a, w_branch_b, w_branch_c, w_out, l)
        x = moe(x, norm2_g, mods, wr_t, b_r, tri, moe_w1, moe_w3, moe_w2, l,
                final_g if l == DEPTH - 1 else None)

    y_prompt, y_sample = x
    return (y_prompt.reshape(BATCH, SEQ, D_MODEL), y_sample.reshape(DEC_BATCH, DEC_SEQ, D_MODEL),
            new_k, new_v, states)
```

```python
import functools
import math

import jax
import jax.numpy as jnp
import numpy as np
from jax import lax
from jax.experimental import pallas as pl
from jax.experimental.pallas import tpu as pltpu

F32 = jnp.float32
BF16 = jnp.bfloat16

D_MODEL = 1024
BATCH = 16
SEQ = 256
DEPTH = 4
DEC_BATCH = 2
DEC_SEQ = 1024
PAST_LEN = 256
GRID_W = 64
EPS = 1e-6
ROPE_BASE = 10000.0
H_A = 4
DH_A = 64
DV_A = 128
W_A = 512
C_B = 512
SHORT_K = 3
H_EMB = 33
H_FFN = 64
H_FAST_DECAY = 0.3
H_SLOW_DECAY = 1.5
H_TARGET = 1e-2
H_C = 4
DK_C = 64
DV_C = 128
W_C = 512
N_EXPERTS = 16
N_GROUPS = 4
EXP_PER_GROUP = 4
D_FF = 512
D_IN = 4608

T_CTX = BATCH * SEQ
T_LAT = DEC_BATCH * DEC_SEQ
T_ALL = T_CTX + T_LAT
TM = 1024
N_TILES = T_ALL // TM
CTX_TILES = T_CTX // TM
LANES = 128
SUBLANES = 8
MOE_BLOCK = 256
MOE_BLOCKS = T_ALL // MOE_BLOCK + N_GROUPS
T_PAD = MOE_BLOCKS * MOE_BLOCK
VMEM_LIMIT = 56 * 1024 * 1024


def _params(n_axes):
    return pltpu.CompilerParams(
        dimension_semantics=("arbitrary",) * n_axes, vmem_limit_bytes=VMEM_LIMIT)


def _mod_row(i):
    return jnp.maximum(i - (CTX_TILES - 1), 0)


def _rms(x):
    return x * lax.rsqrt(jnp.mean(x * x, axis=-1, keepdims=True) + EPS)


def _sigmoid(x):
    return 0.5 * jnp.tanh(0.5 * x) + 0.5


def _silu(x):
    return x * _sigmoid(x)


def _dot(a, b):
    return jnp.dot(a, b, preferred_element_type=F32)


def _dot_nt(a, b):
    return lax.dot_general(a, b, (((1,), (1,)), ((), ())), preferred_element_type=F32)


def _dot_tn(a, b):
    return lax.dot_general(a, b, (((0,), (0,)), ((), ())), preferred_element_type=F32)


def _split3(x):
    hi = x.astype(BF16)
    lo = (x - hi.astype(F32)).astype(BF16)
    return hi, lo


def _dot3(a, b):
    ah, al = _split3(a)
    bh, bl = _split3(b)
    return _dot(ah, bh) + (_dot(ah, bl) + _dot(al, bh))


def _dot3_nt(a, b):
    ah, al = _split3(a)
    bh, bl = _split3(b)
    return _dot_nt(ah, bh) + (_dot_nt(ah, bl) + _dot_nt(al, bh))


def _ada_kernel(c_ref, w_ref, b_ref, o_ref):
    s = _silu(c_ref[...])
    o_ref[0] = _dot(s.astype(BF16), w_ref[0].astype(BF16)) + b_ref[0]


def ada_modulation(cond8, w_ada, b_ada):
    tn = 1536
    n = 6 * D_MODEL
    return pl.pallas_call(
        _ada_kernel,
        grid=(DEPTH, n // tn),
        in_specs=[
            pl.BlockSpec((8, D_MODEL), lambda l, j: (0, 0)),
            pl.BlockSpec((1, D_MODEL, tn), lambda l, j: (l, 0, j)),
            pl.BlockSpec((1, 1, tn), lambda l, j: (l, 0, j)),
        ],
        out_specs=pl.BlockSpec((1, 8, tn), lambda l, j: (l, 0, j)),
        out_shape=jax.ShapeDtypeStruct((DEPTH, 8, n), F32),
        compiler_params=_params(2),
        name="ada_modulation",
    )(cond8, w_ada, b_ada.reshape(DEPTH, 1, n))


PROJ_TM = 3072
PROJ_TN = 512
GATE_BLOCKS = 3 * D_MODEL // PROJ_TN
IN_BLOCKS = D_IN // PROJ_TN
P_WIDTH = 3 * D_MODEL + D_IN
KV_FIRST = GATE_BLOCKS + 1
P_Q = 3 * D_MODEL // W_A
P_HYENA = (3 * D_MODEL + 3 * W_A) // (3 * C_B)
P_RET_Q = (3 * D_MODEL + 3 * W_A + 3 * C_B) // (H_C * DK_C)
P_RET_V = (3 * D_MODEL + 3 * W_A + 3 * C_B + 2 * H_C * DK_C) // W_C


def _in_proj_kernel(x_hbm, g_ref, m_ref, win_ref, wg_ref, bg_ref, p_ref, kv_ref, h_ref, x_ref, sem, *, layer):
    i = pl.program_id(0)
    j = pl.program_id(1)

    def x_copy(tile):
        start = pl.multiple_of(tile * PROJ_TM, PROJ_TM)
        return pltpu.make_async_copy(x_hbm.at[pl.ds(start, PROJ_TM), :], x_ref, sem.at[0])

    @pl.when((i == 0) & (j == 0))
    def _():
        x_copy(0).start()

    @pl.when(j == 0)
    def _():
        x_copy(i).wait()
        for s in range(PROJ_TM // TM):
            rows = slice(s * TM, (s + 1) * TM)
            mod = m_ref[layer, _mod_row(i * (PROJ_TM // TM) + s)]
            y = _rms(x_ref[rows, :]) * g_ref[layer:layer + 1, :]
            h_ref[rows, :] = (y * (1.0 + mod[1:2, :]) + mod[0:1, :]).astype(BF16)

        @pl.when(i + 1 < T_ALL // PROJ_TM)
        def _():
            x_copy(i + 1).start()

    @pl.when(j < GATE_BLOCKS)
    def _():
        acc = _dot(h_ref[...], wg_ref[0].astype(BF16)) + bg_ref[0]
        p_ref[...] = _sigmoid(acc).astype(BF16)

    @pl.when(j >= GATE_BLOCKS)
    def _():
        acc = _dot(h_ref[...], win_ref[0].astype(BF16))
        p_ref[...] = acc.astype(BF16)

        @pl.when((j == KV_FIRST) | (j == KV_FIRST + 1))
        def _():
            kv_ref[...] = acc


def in_projection(x, gain, mods, w_in, w_gate, b_gate, layer):
    return pl.pallas_call(
        functools.partial(_in_proj_kernel, layer=layer),
        grid=(T_ALL // PROJ_TM, GATE_BLOCKS + IN_BLOCKS),
        in_specs=[
            pl.BlockSpec(memory_space=pl.ANY),
            pl.BlockSpec((DEPTH, D_MODEL), lambda i, j: (0, 0)),
            pl.BlockSpec((DEPTH, 8, 6, D_MODEL), lambda i, j: (0, 0, 0, 0)),
            pl.BlockSpec((1, D_MODEL, PROJ_TN), lambda i, j: (layer, 0, jnp.maximum(j - GATE_BLOCKS, 0))),
            pl.BlockSpec((1, D_MODEL, PROJ_TN), lambda i, j: (layer, 0, jnp.minimum(j, GATE_BLOCKS - 1))),
            pl.BlockSpec((1, 1, PROJ_TN), lambda i, j: (layer, 0, jnp.minimum(j, GATE_BLOCKS - 1))),
        ],
        out_specs=[
            pl.BlockSpec((PROJ_TM, PROJ_TN), lambda i, j: (i, j)),
            pl.BlockSpec((PROJ_TM, PROJ_TN), lambda i, j: (i, jnp.clip(j - KV_FIRST, 0, 1))),
        ],
        out_shape=[jax.ShapeDtypeStruct((T_ALL, P_WIDTH), BF16),
                   jax.ShapeDtypeStruct((T_ALL, 2 * W_A), F32)],
        scratch_shapes=[pltpu.VMEM((PROJ_TM, D_MODEL), BF16), pltpu.VMEM((PROJ_TM, D_MODEL), F32),
                        pltpu.SemaphoreType.DMA((1,))],
        compiler_params=_params(2),
        name="in_projection",
    )(x, gain, mods, w_in, w_gate, b_gate.reshape(DEPTH, 1, 3 * D_MODEL))


def _rope_tables():
    n = DEC_SEQ
    rows = n // GRID_W
    row = jnp.repeat(jnp.arange(rows, dtype=F32), GRID_W)
    col = jnp.tile(jnp.arange(GRID_W, dtype=F32), rows)
    quarter = DH_A // 4
    inv = ROPE_BASE ** (-jnp.arange(quarter, dtype=F32) / quarter)
    ar = row[:, None] * inv
    ac = col[:, None] * inv
    ang = jnp.concatenate([ar, ar, ac, ac], axis=-1)
    cos = jnp.tile(jnp.cos(ang), (1, 512 // DH_A))
    sgn = jnp.tile(jnp.concatenate([-jnp.ones((quarter,), F32), jnp.ones((quarter,), F32)]), 512 // (2 * quarter))
    sin = jnp.tile(jnp.sin(ang), (1, 512 // DH_A)) * sgn
    return cos, sin


def _rope(x, cos, sin):
    w = x.shape[-1]
    lane = lax.broadcasted_iota(jnp.int32, x.shape, 1)
    first = (lane % 32) < 16
    rot = jnp.where(first, pltpu.roll(x, w - 16, 1), pltpu.roll(x, 16, 1))
    return x * cos + rot * sin


def _lambda(al_ref, layer):
    al = al_ref[layer]
    a = jnp.sum(al[0:1, :] * al[1:2, :], axis=-1, keepdims=True)
    b = jnp.sum(al[2:3, :] * al[3:4, :], axis=-1, keepdims=True)
    lam_init = 0.8 - 0.6 * math.exp(-0.3 * layer)
    return jnp.exp(a) - jnp.exp(b) + lam_init, lam_init


def _scores(q, k):
    return _dot_nt(q, k).astype(BF16)


def _softmax_numerators(s):
    return jnp.exp2(s - jnp.max(s, axis=-1, keepdims=True))


def _diff_attention(q, k, v, lam, lam_init, subln, o_ref):
    lane = lax.broadcasted_iota(jnp.int32, q.shape, 1)
    q = q * (DH_A ** -0.5 * math.log2(math.e))
    q0 = jnp.where((lane % LANES) < DH_A, q, 0.0).astype(BF16)
    q1 = jnp.where((lane % LANES) >= DH_A, q, 0.0).astype(BF16)
    ones = jnp.ones((k.shape[0], LANES), BF16)
    for h in range(H_A):
        sl = slice(h * LANES, (h + 1) * LANES)
        kh = k[:, sl]
        v_ones = jnp.concatenate([v[:, sl], ones], axis=1)
        o0 = _dot(_softmax_numerators(_scores(q0[:, sl], kh)), v_ones)
        o1 = _dot(_softmax_numerators(_scores(q1[:, sl], kh)), v_ones)
        r0 = 1.0 / o0[:, DV_A:DV_A + 1]
        r1 = lam / o1[:, DV_A:DV_A + 1]
        o = o0[:, :DV_A] * r0 - o1[:, :DV_A] * r1
        o = _rms(o) * subln * (1.0 - lam_init)
        o_ref[:, sl] = o.astype(o_ref.dtype)


def _attn_ctx_kernel(q_ref, k_ref, v_ref, al_ref, g_ref, o_ref, knew_ref, vnew_ref, *, layer):
    lam, lam_init = _lambda(al_ref, layer)
    subln = g_ref[layer:layer + 1, :]
    k = k_ref[...]
    v = v_ref[...]
    for h in range(H_A):
        knew_ref[0, 0, :, h, :] = k[:, h * LANES:(h + 1) * LANES]
        vnew_ref[0, 0, :, h, :] = v[:, h * LANES:(h + 1) * LANES]
    _diff_attention(q_ref[...].astype(F32), k.astype(BF16), v.astype(BF16), lam, lam_init, subln, o_ref)


QB = 256


def _attn_lat_kernel(ya_ref, q_ref, k_ref, v_ref, ck_ref, cv_ref, cosq_ref, sinq_ref, cos_ref, sin_ref,
                     al_ref, g_ref, o_ref, kall_ref, vall_ref, *, layer):
    del ya_ref

    @pl.when(pl.program_id(1) == 0)
    def _():
        kall_ref[0:PAST_LEN, :] = ck_ref[0, 0].astype(BF16)
        vall_ref[0:PAST_LEN, :] = cv_ref[0, 0].astype(BF16)
        kall_ref[PAST_LEN:, :] = _rope(k_ref[...], cos_ref[...], sin_ref[...]).astype(BF16)
        vall_ref[PAST_LEN:, :] = v_ref[...].astype(BF16)

    lam, lam_init = _lambda(al_ref, layer)
    subln = g_ref[layer:layer + 1, :]
    q = _rope(q_ref[...].astype(F32), cosq_ref[...], sinq_ref[...])
    _diff_attention(q, kall_ref[...], vall_ref[...], lam, lam_init, subln, o_ref)


def attention_lat(ya, q, kv, cache_k, cache_v, cos, sin, attn_lambda, subln_g, layer):
    nqb = DEC_SEQ // QB
    row0 = T_CTX // QB
    seq0 = T_CTX // DEC_SEQ
    return pl.pallas_call(
        functools.partial(_attn_lat_kernel, layer=layer),
        grid=(DEC_BATCH, nqb),
        in_specs=[
            pl.BlockSpec(memory_space=pl.ANY),
            pl.BlockSpec((QB, W_A), lambda b, i: (row0 + b * nqb + i, P_Q)),
            pl.BlockSpec((DEC_SEQ, W_A), lambda b, i: (seq0 + b, 0)),
            pl.BlockSpec((DEC_SEQ, W_A), lambda b, i: (seq0 + b, 1)),
            pl.BlockSpec((1, 1, PAST_LEN, W_A), lambda b, i: (b, layer, 0, 0)),
            pl.BlockSpec((1, 1, PAST_LEN, W_A), lambda b, i: (b, layer, 0, 0)),
            pl.BlockSpec((QB, W_A), lambda b, i: (i, 0)),
            pl.BlockSpec((QB, W_A), lambda b, i: (i, 0)),
            pl.BlockSpec((DEC_SEQ, W_A), lambda b, i: (0, 0)),
            pl.BlockSpec((DEC_SEQ, W_A), lambda b, i: (0, 0)),
            pl.BlockSpec((DEPTH, 4, DH_A), lambda b, i: (0, 0, 0)),
            pl.BlockSpec((DEPTH, DV_A), lambda b, i: (0, 0)),
        ],
        out_specs=pl.BlockSpec((QB, W_A), lambda b, i: (row0 + b * nqb + i, 0)),
        out_shape=jax.ShapeDtypeStruct((T_ALL, W_A), BF16),
        scratch_shapes=[pltpu.VMEM((PAST_LEN + DEC_SEQ, W_A), BF16),
                        pltpu.VMEM((PAST_LEN + DEC_SEQ, W_A), BF16)],
        input_output_aliases={0: 0},
        compiler_params=_params(2),
        name="attention_lat",
    )(ya, q, kv, kv, cache_k, cache_v, cos, sin, cos, sin, attn_lambda, subln_g)


def _dft_tables(n):
    k = np.arange(n, dtype=np.int64)
    prod = (2 * k[:, None] + 1) * k[None, :]
    ang = (prod % (4 * n)).astype(np.float64) * (math.pi / (2 * n))
    c = np.cos(ang).astype(np.float32)
    s = np.sin(ang).astype(np.float32)
    return tuple(jnp.asarray(t).astype(BF16) for t in (c, s, c.T, s.T))


def _filter_features(n):
    t = jnp.linspace(0.0, 1.0, n, dtype=F32)[:, None]
    bands = (H_EMB - 1) // 2
    w = 2.0 * math.pi * jnp.arange(n, dtype=F32)[:, None] / n
    fr = jnp.linspace(1e-4, bands - 1, bands, dtype=F32)
    z = jnp.concatenate([t, jnp.cos(w * fr), -jnp.sin(w * fr)], axis=-1)
    return jnp.pad(z, ((0, 0), (0, LANES - H_EMB)))


def _filter_decay_rates():
    max_decay = math.log(H_TARGET) / H_FAST_DECAY
    min_decay = math.log(H_TARGET) / H_SLOW_DECAY
    return jnp.abs(jnp.linspace(min_decay, max_decay, C_B, dtype=F32))[None, :]


def _hyena_filter_kernel(z_ref, w1_ref, b1_ref, w2_ref, b2_ref, w3_ref, dr_ref, c_ref, s_ref,
                         gre_ref, gim_ref):
    z = z_ref[...]
    hid = jnp.sin(_dot3(z, w1_ref[0]) + b1_ref[0])
    hid = jnp.sin(_dot3(hid, w2_ref[0]) + b2_ref[0])
    h = _dot(hid.astype(BF16), w3_ref[0].astype(BF16))
    window = jnp.exp(-z[:, 0:1] * dr_ref[...])
    hf = h[:, :C_B] * window
    hb = h[:, C_B:] * window
    total = jnp.sum(jnp.abs(hf) + jnp.abs(hb), axis=0, keepdims=True)
    hf = hf / total
    hb = hb / total
    row = lax.broadcasted_iota(jnp.int32, hb.shape, 0)
    hb = jnp.where(row == 0, 0.0, hb)
    gre_ref[0] = _dot(c_ref[...], (hf + hb).astype(BF16))
    gim_ref[0] = _dot(s_ref[...], (hb - hf).astype(BF16))


def hyena_filters(n, z, decay_rates, cf, sf, f_w1, f_b1, f_w2, f_b2, f_w3):
    w1 = jnp.pad(f_w1, ((0, 0), (0, LANES - H_EMB), (0, 0)))
    full = lambda shape: pl.BlockSpec(shape, lambda l: (0,) * len(shape))
    per_layer = lambda shape: pl.BlockSpec((1,) + shape, lambda l: (l,) + (0,) * len(shape))
    return pl.pallas_call(
        _hyena_filter_kernel,
        grid=(DEPTH,),
        in_specs=[
            full((n, LANES)),
            per_layer((LANES, H_FFN)), per_layer((1, H_FFN)),
            per_layer((H_FFN, H_FFN)), per_layer((1, H_FFN)),
            per_layer((H_FFN, 2 * C_B)),
            full((1, C_B)), full((n, n)), full((n, n)),
        ],
        out_specs=[per_layer((n, C_B)), per_layer((n, C_B))],
        out_shape=[jax.ShapeDtypeStruct((DEPTH, n, C_B), F32)] * 2,
        compiler_params=_params(1),
        name="hyena_filters",
    )(z, w1, f_b1.reshape(DEPTH, 1, H_FFN), f_w2, f_b2.reshape(DEPTH, 1, H_FFN), f_w3,
      decay_rates, cf, sf)


def _hyena_kernel(*refs, layer, n, aliased):
    if aliased:
        refs = refs[1:]
    u_ref, cw_ref, cb_ref, skip_ref, gre_ref, gim_ref, c_ref, s_ref, ct_ref, st_ref, o_ref = refs
    u = u_ref[...].astype(F32)
    row = lax.broadcasted_iota(jnp.int32, u.shape, 0)
    prev = jnp.where(row == 0, 0.0, pltpu.roll(u, 1, 0))
    nxt = jnp.where(row == n - 1, 0.0, pltpu.roll(u, n - 1, 0))
    cw = cw_ref[layer]
    uc = cb_ref[layer:layer + 1, :] + prev * cw[0:1, :] + u * cw[1:2, :] + nxt * cw[2:3, :]
    x0 = uc[:, :C_B]
    x1 = uc[:, C_B:2 * C_B]
    v = uc[:, 2 * C_B:]
    w = v * x1
    wb = w.astype(BF16)
    ure = _dot(c_ref[...], wb)
    uim = -_dot(s_ref[...], wb)
    gre = gre_ref[0]
    gim = gim_ref[0]
    yre = (ure * gre - uim * gim).astype(BF16)
    yim = (ure * gim + uim * gre).astype(BF16)
    y = (_dot(ct_ref[...], yre) - _dot(st_ref[...], yim)) * (1.0 / n)
    y = y + w * skip_ref[layer:layer + 1, :]
    o_ref[...] = (x0 * y).astype(o_ref.dtype)


def _full(shape):
    return pl.BlockSpec(shape, lambda *_: (0,) * len(shape))


def _hyena_specs(n, seq0, layer):
    return [
        pl.BlockSpec((n, 3 * C_B), lambda b, *_: (seq0 + b, P_HYENA)),
        _full((DEPTH, SHORT_K, 3 * C_B)), _full((DEPTH, 3 * C_B)), _full((DEPTH, C_B)),
        pl.BlockSpec((1, n, C_B), lambda *_: (layer, 0, 0)),
        pl.BlockSpec((1, n, C_B), lambda *_: (layer, 0, 0)),
        _full((n, n)), _full((n, n)), _full((n, n)), _full((n, n)),
    ]


def hyena_lat(yb, p, conv_w, conv_b, skip, gre, gim, tables, layer):
    seq0 = T_CTX // DEC_SEQ
    return pl.pallas_call(
        functools.partial(_hyena_kernel, layer=layer, n=DEC_SEQ, aliased=True),
        grid=(DEC_BATCH,),
        in_specs=[pl.BlockSpec(memory_space=pl.ANY)] + _hyena_specs(DEC_SEQ, seq0, layer),
        out_specs=pl.BlockSpec((DEC_SEQ, C_B), lambda b: (seq0 + b, 0)),
        out_shape=jax.ShapeDtypeStruct((T_ALL, C_B), BF16),
        input_output_aliases={0: 0},
        compiler_params=_params(1),
        name="hyena_lat",
    )(yb, p, conv_w, conv_b, skip, gre, gim, *tables)


RET_QB = 256


def _retention_kernel(*refs, layer, n, latent):
    if latent:
        _, q_ref, k_ref, v_ref, g_ref, de_ref, cos_ref, sin_ref, s0_ref, o_ref, w_ref, vec_ref = refs
        st_ref = None
    else:
        q_ref, k_ref, v_ref, g_ref, de_ref, o_ref, st_ref, w_ref, vec_ref = refs[-9:]
    log_gamma = jnp.log1p(-jnp.exp2(-de_ref[layer]))

    @pl.when(pl.program_id(0) == 0)
    def _():
        t = lax.broadcasted_iota(jnp.int32, (n, n), 0)
        s = lax.broadcasted_iota(jnp.int32, (n, n), 1)
        lag = (t - s).astype(F32)
        for h in range(H_C):
            rate = jnp.where(lag > 0, log_gamma[0:1, h:h + 1], log_gamma[1:2, h:h + 1])
            w_ref[h] = jnp.where(lag == 0, 2.0, jnp.exp(rate * jnp.abs(lag))).astype(BF16)
            pos = lax.broadcasted_iota(jnp.int32, (n, LANES), 0).astype(F32)
            steps = (pos + 1.0, float(n) - pos) if latent else (float(n - 1) - pos, pos)
            for d in range(2):
                vec_ref[h, d] = jnp.exp(log_gamma[d:d + 1, h:h + 1] * steps[d])

    q = q_ref[...].astype(F32)
    k = k_ref[...].astype(F32) * (DK_C ** -0.5)
    if latent:
        q = _rope(q, cos_ref[...], sin_ref[...])
        k = _rope(k, cos_ref[...], sin_ref[...])
    v = v_ref[...].astype(BF16)
    gate = g_ref[...].astype(F32)
    lane = lax.broadcasted_iota(jnp.int32, (n, LANES), 1)
    zeros64 = jnp.zeros((DK_C, DV_C), F32)

    for h in range(H_C):
        pair = slice((h // 2) * LANES, (h // 2 + 1) * LANES)
        lo = (h % 2) * DK_C
        own = (lane >= lo) & (lane < lo + DK_C)
        vs = slice(h * DV_C, (h + 1) * DV_C)
        qh = jnp.where(own, q[:, pair], 0.0).astype(BF16)
        kh = jnp.where(own, k[:, pair], 0.0)
        kb = kh.astype(BF16)
        vh = v[:, vs]
        if latent:
            s0 = [jnp.concatenate([s0_ref[0, 0, d, h], zeros64] if lo == 0 else [zeros64, s0_ref[0, 0, d, h]],
                                  axis=0).astype(BF16) for d in range(2)]
            carry = _dot(qh, s0[0]) * vec_ref[h, 0] + _dot(qh, s0[1]) * vec_ref[h, 1]
        for r0 in range(0, n, RET_QB):
            rows = slice(r0, r0 + RET_QB)
            att = _dot_nt(qh[rows], kb).astype(BF16) * w_ref[h, rows, :]
            out = _dot(att, vh)
            if latent:
                out = out + carry[rows]
            o_ref[rows, vs] = (_rms(out) * _silu(gate[rows, vs])).astype(o_ref.dtype)
        if st_ref is not None:
            sf = _dot_tn((kh * vec_ref[h, 0]).astype(BF16), vh)
            sb = _dot_tn((kh * vec_ref[h, 1]).astype(BF16), vh)
            st_ref[0, 0, 0, h] = sf[lo:lo + DK_C, :]
            st_ref[0, 0, 1, h] = sb[lo:lo + DK_C, :]


def _retention_specs(n, seq0):
    return [
        pl.BlockSpec((n, H_C * DK_C), lambda b, *_: (seq0 + b, P_RET_Q)),
        pl.BlockSpec((n, H_C * DK_C), lambda b, *_: (seq0 + b, P_RET_Q + 1)),
        pl.BlockSpec((n, W_C), lambda b, *_: (seq0 + b, P_RET_V)),
        pl.BlockSpec((n, W_C), lambda b, *_: (seq0 + b, P_RET_V + 1)),
        _full((DEPTH, 2, H_C)),
    ]


def retention_lat(yc, p, decay_exp, cos, sin, state0, layer):
    n = DEC_SEQ
    seq0 = T_CTX // DEC_SEQ
    return pl.pallas_call(
        functools.partial(_retention_kernel, layer=layer, n=n, latent=True),
        grid=(DEC_BATCH,),
        in_specs=[pl.BlockSpec(memory_space=pl.ANY)] + _retention_specs(n, seq0) + [
            _full((n, H_C * DK_C)), _full((n, H_C * DK_C)),
            pl.BlockSpec((1, 1, 2, H_C, DK_C, DV_C), lambda b: (b, layer, 0, 0, 0, 0)),
        ],
        out_specs=pl.BlockSpec((n, W_C), lambda b: (seq0 + b, 0)),
        out_shape=jax.ShapeDtypeStruct((T_ALL, W_C), BF16),
        scratch_shapes=[pltpu.VMEM((H_C, n, n), BF16), pltpu.VMEM((H_C, 2, n, LANES), F32)],
        input_output_aliases={0: 0},
        compiler_params=_params(1),
        name="retention_lat",
    )(yc, p, p, p, p, decay_exp, cos, sin, state0)


N_ATTN_IN = 5
N_HYENA_IN = 10
N_RET_IN = 5


def _mixers_ctx_kernel(*refs, layer, n_threaded):
    refs = refs[n_threaded:]
    attn_in = refs[:N_ATTN_IN]
    hyena_in = refs[N_ATTN_IN:N_ATTN_IN + N_HYENA_IN]
    ret_in = refs[N_ATTN_IN + N_HYENA_IN:N_ATTN_IN + N_HYENA_IN + N_RET_IN]
    ya_ref, knew_ref, vnew_ref, yb_ref, yc_ref, st_ref, w_ref, vec_ref = refs[N_ATTN_IN + N_HYENA_IN + N_RET_IN:]
    _attn_ctx_kernel(*attn_in, ya_ref, knew_ref, vnew_ref, layer=layer)
    _hyena_kernel(*hyena_in, yb_ref, layer=layer, n=SEQ, aliased=False)
    _retention_kernel(*ret_in, yc_ref, st_ref, w_ref, vec_ref, layer=layer, n=SEQ, latent=False)


def mixers_ctx(p, kv, attn_lambda, subln_g, conv_w, conv_b, skip, gre, gim, tables, decay_exp, layer, threaded):
    cache_shape = jax.ShapeDtypeStruct((BATCH, DEPTH, SEQ, H_A, DV_A), F32)
    cache_spec = pl.BlockSpec((1, 1, SEQ, H_A, DV_A), lambda b: (b, layer, 0, 0, 0))
    branch_shape = jax.ShapeDtypeStruct((T_ALL, W_A), BF16)
    branch_spec = pl.BlockSpec((SEQ, W_A), lambda b: (b, 0))
    n_threaded = len(threaded)
    attn_specs = [
        pl.BlockSpec((SEQ, W_A), lambda b: (b, P_Q)),
        pl.BlockSpec((SEQ, W_A), lambda b: (b, 0)),
        pl.BlockSpec((SEQ, W_A), lambda b: (b, 1)),
        _full((DEPTH, 4, DH_A)), _full((DEPTH, DV_A)),
    ]
    return pl.pallas_call(
        functools.partial(_mixers_ctx_kernel, layer=layer, n_threaded=n_threaded),
        grid=(BATCH,),
        in_specs=([pl.BlockSpec(memory_space=pl.ANY)] * n_threaded + attn_specs
                  + _hyena_specs(SEQ, 0, layer) + _retention_specs(SEQ, 0)),
        out_specs=[branch_spec, cache_spec, cache_spec, branch_spec, branch_spec,
                   pl.BlockSpec((1, 1, 2, H_C, DK_C, DV_C), lambda b: (b, layer, 0, 0, 0, 0))],
        out_shape=[branch_shape, cache_shape, cache_shape, branch_shape, branch_shape,
                   jax.ShapeDtypeStruct((BATCH, DEPTH, 2, H_C, DK_C, DV_C), F32)],
        scratch_shapes=[pltpu.VMEM((H_C, SEQ, SEQ), BF16), pltpu.VMEM((H_C, 2, SEQ, LANES), F32)],
        input_output_aliases=dict(zip(range(n_threaded), (1, 2, 5))),
        compiler_params=_params(1),
        name="mixers_ctx",
    )(*threaded, p, kv, kv, attn_lambda, subln_g, p, conv_w, conv_b, skip, gre, gim, *tables,
      p, p, p, p, decay_exp)


def _merge_kernel(x_ref, ya_ref, yb_ref, yc_ref, g_ref, m_ref, wa_ref, wb_ref, wc_ref, wo_ref,
                  o_ref, wbf_ref, wobf_ref):
    @pl.when(pl.program_id(0) == 0)
    def _():
        wbf_ref[0] = wa_ref[0].astype(BF16)
        wbf_ref[1] = wb_ref[0].astype(BF16)
        wbf_ref[2] = wc_ref[0].astype(BF16)
        wobf_ref[...] = wo_ref[0].astype(BF16)

    g = g_ref[...]
    merged = (g[:, :D_MODEL] * _dot(ya_ref[...], wbf_ref[0])
              + g[:, D_MODEL:2 * D_MODEL] * _dot(yb_ref[...], wbf_ref[1])
              + g[:, 2 * D_MODEL:] * _dot(yc_ref[...], wbf_ref[2]))
    g1 = m_ref[0, 0, 2:3, :]
    o_ref[...] = x_ref[...] + g1 * _dot(merged.astype(BF16), wobf_ref[...])


def merge_branches(x, ya, yb, yc, gates, mods, w_a, w_b, w_c, w_out, layer):
    tile = lambda w: pl.BlockSpec((TM, w), lambda i: (i, 0))
    wspec = lambda k: pl.BlockSpec((1, k, D_MODEL), lambda i: (layer, 0, 0))
    return pl.pallas_call(
        _merge_kernel,
        grid=(N_TILES,),
        in_specs=[
            tile(D_MODEL), tile(W_A), tile(C_B), tile(W_C), tile(3 * D_MODEL),
            pl.BlockSpec((1, 1, 6, D_MODEL), lambda i: (layer, _mod_row(i), 0, 0)),
            wspec(W_A), wspec(C_B), wspec(W_C), wspec(D_MODEL),
        ],
        out_specs=tile(D_MODEL),
        out_shape=jax.ShapeDtypeStruct((T_ALL, D_MODEL), F32),
        scratch_shapes=[pltpu.VMEM((3, W_A, D_MODEL), BF16), pltpu.VMEM((D_MODEL, D_MODEL), BF16)],
        compiler_params=_params(1),
        name="merge_branches",
    )(x, ya, yb, yc, gates, mods, w_a, w_b, w_c, w_out)


def _route(h2, wr_t, b_r):
    logits = _dot3_nt(wr_t, h2)
    m = jnp.max(logits, axis=0, keepdims=True)
    e = jnp.exp(logits - m)
    scores = e / jnp.sum(e, axis=0, keepdims=True)
    sel = scores + b_r
    rows = [sel[i:i + 1, :] for i in range(N_EXPERTS)]
    in_group = []
    gscore = []
    for g in range(N_GROUPS):
        members = range(g * EXP_PER_GROUP, (g + 1) * EXP_PER_GROUP)
        total = None
        for i in members:
            rank = None
            for j in members:
                if j == i:
                    continue
                ahead = (rows[j] >= rows[i]) if j < i else (rows[j] > rows[i])
                ahead = ahead.astype(F32)
                rank = ahead if rank is None else rank + ahead
            chosen = rank < 2.0
            in_group.append(chosen)
            part = jnp.where(chosen, rows[i], 0.0)
            total = part if total is None else total + part
        gscore.append(total)
    gates = []
    group_hot = []
    for g in range(N_GROUPS):
        best = None
        for g2 in range(N_GROUPS):
            if g2 == g:
                continue
            wins = gscore[g] > gscore[g2] if g2 < g else gscore[g] >= gscore[g2]
            best = wins if best is None else best & wins
        group_hot.append(best.astype(F32))
        for i in range(g * EXP_PER_GROUP, (g + 1) * EXP_PER_GROUP):
            gates.append(jnp.where(best & in_group[i], scores[i:i + 1, :], 0.0))
    gates = jnp.concatenate(gates, axis=0)
    return gates / jnp.sum(gates, axis=0, keepdims=True), group_hot


def _moe_route_kernel(x_ref, g_ref, m_ref, wr_ref, br_ref, tri_ref, tiles_ref, gate_ref, meta_ref, count_ref,
                      *, layer):
    y = _rms(x_ref[...]) * g_ref[layer:layer + 1, :]
    h2 = y * (1.0 + m_ref[0, 0, 4:5, :]) + m_ref[0, 0, 3:4, :]
    gates, group_hot = _route(h2, wr_ref[...], br_ref[...])
    for s in range(SUBLANES):
        tiles_ref[pl.ds(s, TM, stride=SUBLANES), :] = h2[:, s * LANES:(s + 1) * LANES]
    padded = jnp.concatenate([gates, jnp.zeros((LANES - N_EXPERTS, TM), F32)], axis=0)
    gate_ref[...] = padded.T
    row = lax.broadcasted_iota(jnp.int32, (SUBLANES, TM), 0)
    hot = jnp.zeros((SUBLANES, TM), F32)
    for g in range(N_GROUPS):
        hot = jnp.where(row == g, group_hot[g], hot)
    before = _dot(hot.astype(BF16), tri_ref[...])
    rank = jnp.sum(hot * before, axis=0, keepdims=True)
    gid = group_hot[1] + 2.0 * group_hot[2] + 3.0 * group_hot[3]
    meta = jnp.where(row == 0, gid, jnp.where(row == 1, rank, 0.0))
    meta_ref[0] = meta.astype(jnp.int32)
    counts = jnp.sum(hot, axis=1, keepdims=True) + jnp.zeros((SUBLANES, LANES), F32)
    count_ref[0] = counts.astype(jnp.int32)


def moe_route(x, gain, mods, wr_t, b_r, tri, layer):
    return pl.pallas_call(
        functools.partial(_moe_route_kernel, layer=layer),
        grid=(N_TILES,),
        in_specs=[
            pl.BlockSpec((TM, D_MODEL), lambda i: (i, 0)),
            pl.BlockSpec((DEPTH, D_MODEL), lambda i: (0, 0)),
            pl.BlockSpec((1, 1, 6, D_MODEL), lambda i: (layer, _mod_row(i), 0, 0)),
            pl.BlockSpec((N_EXPERTS, D_MODEL), lambda i: (0, 0)),
            pl.BlockSpec((N_EXPERTS, 1), lambda i: (0, 0)),
            pl.BlockSpec((TM, TM), lambda i: (0, 0)),
        ],
        out_specs=[pl.BlockSpec((TM * SUBLANES, LANES), lambda i: (i, 0)),
                   pl.BlockSpec((TM, LANES), lambda i: (i, 0)),
                   pl.BlockSpec((1, SUBLANES, TM), lambda i: (i, 0, 0)),
                   pl.BlockSpec((1, SUBLANES, LANES), lambda i: (i, 0, 0))],
        out_shape=[jax.ShapeDtypeStruct((T_ALL * SUBLANES, LANES), F32),
                   jax.ShapeDtypeStruct((T_ALL, LANES), F32),
                   jax.ShapeDtypeStruct((N_TILES, SUBLANES, TM), jnp.int32),
                   jax.ShapeDtypeStruct((N_TILES, SUBLANES, LANES), jnp.int32)],
        compiler_params=_params(1),
        name="moe_route",
    )(x, gain, mods, wr_t, b_r, tri)


def _moe_positions(meta, counts):
    gid = meta[:, 0, :]
    rank = meta[:, 1, :]
    cnt = counts[:, :N_GROUPS, 0]
    total = jnp.sum(cnt, axis=0)
    padded = (total + (MOE_BLOCK - 1)) // MOE_BLOCK * MOE_BLOCK
    group_start = jnp.cumsum(padded) - padded
    base = group_start[None, :] + jnp.cumsum(cnt, axis=0) - cnt
    pos = rank
    for g in range(N_GROUPS):
        pos = pos + jnp.where(gid == g, base[:, g:g + 1], 0)
    first_block = group_start // MOE_BLOCK
    blk = jnp.arange(MOE_BLOCKS, dtype=jnp.int32)
    block_gid = sum((blk >= first_block[g]).astype(jnp.int32) for g in range(1, N_GROUPS))
    block_used = (blk * MOE_BLOCK < jnp.sum(padded)).astype(jnp.int32)
    later = jnp.where(block_gid[None, :] > block_gid[:, None], block_gid[None, :], N_GROUPS)
    next_group = jnp.min(later, axis=1)
    next_group = jnp.where(next_group == N_GROUPS, -1, next_group).astype(jnp.int32)
    return pos.reshape(T_ALL).astype(jnp.int32), block_gid, block_used, next_group


def _moe_permute_kernel(pos_ref, tiles_ref, gate_ref, sorted_ref, gsorted_ref):
    i = pl.program_id(0)

    @pl.when(i == 0)
    def _():
        def zero(b, carry):
            start = pl.multiple_of(b * TM, TM)
            sorted_ref[pl.ds(start, TM), :] = jnp.zeros((TM, LANES), F32)
            return carry
        lax.fori_loop(0, T_PAD * SUBLANES // TM, zero, 0)
        gsorted_ref[...] = jnp.zeros((T_PAD, LANES), F32)

    def move(t, carry):
        p = pos_ref[i * TM + t]
        dst = pl.multiple_of(p * SUBLANES, SUBLANES)
        src = pl.multiple_of(t * SUBLANES, SUBLANES)
        sorted_ref[pl.ds(dst, SUBLANES), :] = tiles_ref[pl.ds(src, SUBLANES), :]
        gsorted_ref[pl.ds(p, 1), :] = gate_ref[pl.ds(t, 1), :]
        return carry
    lax.fori_loop(0, TM, move, 0, unroll=8)


def moe_permute(pos, tiles, gates):
    return pl.pallas_call(
        _moe_permute_kernel,
        grid_spec=pltpu.PrefetchScalarGridSpec(
            num_scalar_prefetch=1,
            grid=(N_TILES,),
            in_specs=[pl.BlockSpec((TM * SUBLANES, LANES), lambda i, pos: (i, 0)),
                      pl.BlockSpec((TM, LANES), lambda i, pos: (i, 0))],
            out_specs=[pl.BlockSpec(memory_space=pltpu.VMEM), pl.BlockSpec(memory_space=pltpu.VMEM)],
        ),
        out_shape=[jax.ShapeDtypeStruct((T_PAD * SUBLANES, LANES), F32),
                   jax.ShapeDtypeStruct((T_PAD, LANES), F32)],
        compiler_params=_params(1),
        name="moe_permute",
    )(pos, tiles, gates)


def _group_changed(gid_ref, b):
    return (b == 0) | (gid_ref[b] != gid_ref[jnp.maximum(b - 1, 0)])


def _moe_expert_kernel(gid_ref, used_ref, next_ref, s_ref, gate_ref, w1_hbm, w3_hbm, w2_ref, y_ref,
                       w1f_ref, w3f_ref, w1b_ref, w3b_ref, w2b_ref, sem, *, layer):
    b = pl.program_id(0)

    def up_weight_copies(group):
        return (pltpu.make_async_copy(w1_hbm.at[layer, group], w1f_ref, sem.at[0]),
                pltpu.make_async_copy(w3_hbm.at[layer, group], w3f_ref, sem.at[1]))

    @pl.when(b == 0)
    def _():
        for copy in up_weight_copies(gid_ref[0]):
            copy.start()

    @pl.when(_group_changed(gid_ref, b))
    def _():
        for copy in up_weight_copies(gid_ref[b]):
            copy.wait()
        for j in range(EXP_PER_GROUP):
            cols = slice(j * D_FF, (j + 1) * D_FF)
            w1b_ref[:, cols] = w1f_ref[j].astype(BF16)
            w3b_ref[:, cols] = w3f_ref[j].astype(BF16)
        w2b_ref[...] = w2_ref[0, 0].astype(BF16)

        @pl.when(next_ref[b] >= 0)
        def _():
            for copy in up_weight_copies(next_ref[b]):
                copy.start()

    @pl.when(used_ref[b] == 1)
    def _():
        lhs = jnp.concatenate([s_ref[pl.ds(s, MOE_BLOCK, stride=SUBLANES), :].astype(BF16)
                               for s in range(SUBLANES)], axis=1)
        a = _dot(lhs, w1b_ref[...])
        g = _dot(lhs, w3b_ref[...])
        gates = gate_ref[...]
        lane = lax.broadcasted_iota(jnp.int32, (MOE_BLOCK, LANES), 1)
        first = gid_ref[b] * EXP_PER_GROUP
        parts = []
        for j in range(EXP_PER_GROUP):
            cols = slice(j * D_FF, (j + 1) * D_FF)
            gate = jnp.sum(jnp.where(lane == first + j, gates, 0.0), axis=1, keepdims=True)
            parts.append((_silu(a[:, cols]) * g[:, cols] * gate).astype(BF16))
        act = jnp.concatenate(parts, axis=1)
        y = _dot(act, w2b_ref[...])
        for s in range(SUBLANES):
            y_ref[pl.ds(s, MOE_BLOCK, stride=SUBLANES), :] = y[:, s * LANES:(s + 1) * LANES]

    @pl.when(used_ref[b] == 0)
    def _():
        y_ref[...] = jnp.zeros_like(y_ref)


def moe_experts(block_gid, block_used, next_group, sorted_rows, sorted_gates, w1, w3, w2, layer):
    group_ff = EXP_PER_GROUP * D_FF
    w1g = w1.reshape(DEPTH, N_GROUPS, EXP_PER_GROUP, D_MODEL, D_FF)
    w3g = w3.reshape(DEPTH, N_GROUPS, EXP_PER_GROUP, D_MODEL, D_FF)
    w2g = w2.reshape(DEPTH, N_GROUPS, group_ff, D_MODEL)
    block = lambda b, gid, used, nxt: (b, 0)
    return pl.pallas_call(
        functools.partial(_moe_expert_kernel, layer=layer),
        grid_spec=pltpu.PrefetchScalarGridSpec(
            num_scalar_prefetch=3,
            grid=(MOE_BLOCKS,),
            in_specs=[pl.BlockSpec((MOE_BLOCK * SUBLANES, LANES), block),
                      pl.BlockSpec((MOE_BLOCK, LANES), block),
                      pl.BlockSpec(memory_space=pl.ANY), pl.BlockSpec(memory_space=pl.ANY),
                      pl.BlockSpec((1, 1, group_ff, D_MODEL), lambda b, gid, used, nxt: (layer, gid[b], 0, 0))],
            out_specs=pl.BlockSpec((MOE_BLOCK * SUBLANES, LANES), block),
            scratch_shapes=[pltpu.VMEM((EXP_PER_GROUP, D_MODEL, D_FF), F32),
                            pltpu.VMEM((EXP_PER_GROUP, D_MODEL, D_FF), F32),
                            pltpu.VMEM((D_MODEL, group_ff), BF16), pltpu.VMEM((D_MODEL, group_ff), BF16),
                            pltpu.VMEM((group_ff, D_MODEL), BF16),
                            pltpu.SemaphoreType.DMA((2,))],
        ),
        out_shape=jax.ShapeDtypeStruct((T_PAD * SUBLANES, LANES), F32),
        compiler_params=_params(1),
        name="moe_experts",
    )(block_gid, block_used, next_group, sorted_rows, sorted_gates, w1g, w3g, w2g)


FINAL_TM = 512


def _moe_combine_kernel(pos_ref, ys_ref, x_ref, m_ref, *rest, tm, final):
    i = pl.program_id(0)
    buf_ref = rest[-2] if final else rest[-1]

    def move(t, carry):
        src = pl.multiple_of(pos_ref[i * tm + t] * SUBLANES, SUBLANES)
        dst = pl.multiple_of(t * SUBLANES, SUBLANES)
        buf_ref[pl.ds(dst, SUBLANES), :] = ys_ref[pl.ds(src, SUBLANES), :]
        return carry
    lax.fori_loop(0, tm, move, 0, unroll=8)

    x_new_ref = rest[-1] if final else rest[0]
    for s in range(SUBLANES):
        cols = slice(s * LANES, (s + 1) * LANES)
        y = buf_ref[pl.ds(s, tm, stride=SUBLANES), :]
        x_new_ref[:, cols] = x_ref[:, cols] + m_ref[0, 0, 5:6, cols] * y

    if final:
        gain_ref, ctx_ref, lat_ref = rest[:3]
        out = _rms(x_new_ref[...]) * gain_ref[...]

        @pl.when(i < T_CTX // tm)
        def _():
            ctx_ref[...] = out

        @pl.when(i >= T_CTX // tm)
        def _():
            lat_ref[...] = out


def moe_combine(pos, y_sorted, x, mods, layer, final_gain=None):
    final = final_gain is not None
    tm = FINAL_TM if final else TM
    ctx_tiles = T_CTX // tm
    tile = pl.BlockSpec((tm, D_MODEL), lambda i, pos: (i, 0))
    in_specs = [
        pl.BlockSpec(memory_space=pltpu.VMEM),
        tile,
        pl.BlockSpec((1, 1, 6, D_MODEL), lambda i, pos: (layer, _mod_row(i * tm // TM), 0, 0)),
    ]
    args = [pos, y_sorted, x, mods]
    scratch = [pltpu.VMEM((tm * SUBLANES, LANES), F32)]
    if final:
        in_specs.append(pl.BlockSpec((1, D_MODEL), lambda i, pos: (0, 0)))
        args.append(final_gain.reshape(1, D_MODEL))
        out_specs = [pl.BlockSpec((tm, D_MODEL), lambda i, pos: (jnp.minimum(i, ctx_tiles - 1), 0)),
                     pl.BlockSpec((tm, D_MODEL), lambda i, pos: (jnp.maximum(i - ctx_tiles, 0), 0))]
        out_shape = [jax.ShapeDtypeStruct((T_CTX, D_MODEL), F32), jax.ShapeDtypeStruct((T_LAT, D_MODEL), F32)]
        scratch.append(pltpu.VMEM((tm, D_MODEL), F32))
    else:
        out_specs = tile
        out_shape = jax.ShapeDtypeStruct((T_ALL, D_MODEL), F32)
    return pl.pallas_call(
        functools.partial(_moe_combine_kernel, tm=tm, final=final),
        grid_spec=pltpu.PrefetchScalarGridSpec(
            num_scalar_prefetch=1,
            grid=(T_ALL // tm,),
            in_specs=in_specs,
            out_specs=out_specs,
            scratch_shapes=scratch,
        ),
        out_shape=out_shape,
        compiler_params=_params(1),
        name="moe_combine",
    )(*args)


def moe(x, gain, mods, wr_t, b_r, tri, w1, w3, w2, layer, final_gain=None):
    tiles, gates, meta, counts = moe_route(x, gain, mods, wr_t, b_r, tri, layer)
    pos, block_gid, block_used, next_group = _moe_positions(meta, counts)
    sorted_rows, sorted_gates = moe_permute(pos, tiles, gates)
    y_sorted = moe_experts(block_gid, block_used, next_group, sorted_rows, sorted_gates, w1, w3, w2, layer)
    return moe_combine(pos, y_sorted, x, mods, layer, final_gain)


def kernel(x_prompt, x_sample, cache_attn_k, cache_attn_v, state_retention, c, c_ctx, w_ada, b_ada, norm1_g, norm2_g, final_g, w_in, attn_lambda, attn_subln_g, hy_conv_w, hy_conv_b, hy_f_w1, hy_f_b1, hy_f_w2, hy_f_b2, hy_f_w3, hy_skip, ret_decay_exp, w_branch_a, w_branch_b, w_branch_c, w_gate, b_gate, w_out, w_router, b_router, moe_w1, moe_w3, moe_w2):
    x = jnp.concatenate([x_prompt.reshape(T_CTX, D_MODEL), x_sample.reshape(T_LAT, D_MODEL)], axis=0)
    cond8 = jnp.concatenate([c_ctx[None, :], c, jnp.zeros((8 - 1 - DEC_BATCH, D_MODEL), F32)], axis=0)
    mods = ada_modulation(cond8, w_ada, b_ada).reshape(DEPTH, 8, 6, D_MODEL)

    cos, sin = _rope_tables()
    cos_c, sin_c = cos[:, :H_C * DK_C], sin[:, :H_C * DK_C]
    cache_k = cache_attn_k.reshape(DEC_BATCH, DEPTH, PAST_LEN, W_A)
    cache_v = cache_attn_v.reshape(DEC_BATCH, DEPTH, PAST_LEN, W_A)
    decay_rates = _filter_decay_rates()
    tables_ctx = _dft_tables(SEQ)
    tables_lat = _dft_tables(DEC_SEQ)
    filt_ctx = hyena_filters(SEQ, _filter_features(SEQ), decay_rates, tables_ctx[0], tables_ctx[1],
                             hy_f_w1, hy_f_b1, hy_f_w2, hy_f_b2, hy_f_w3)
    filt_lat = hyena_filters(DEC_SEQ, _filter_features(DEC_SEQ), decay_rates, tables_lat[0], tables_lat[1],
                             hy_f_w1, hy_f_b1, hy_f_w2, hy_f_b2, hy_f_w3)
    wr_t = w_router.T
    b_r = b_router.reshape(N_EXPERTS, 1)
    tri = jnp.asarray(np.triu(np.ones((TM, TM), np.float32), 1), dtype=BF16)

    threaded = ()
    for l in range(DEPTH):
        p, kv = in_projection(x, norm1_g, mods, w_in, w_gate, b_gate, l)

        ya, new_k, new_v, yb, yc, states = mixers_ctx(
            p, kv, attn_lambda, attn_subln_g, hy_conv_w, hy_conv_b, hy_skip, filt_ctx[0], filt_ctx[1],
            tables_ctx, ret_decay_exp, l, threaded)
        threaded = (new_k, new_v, states)
        ya = attention_lat(ya, p, kv, cache_k, cache_v, cos, sin, attn_lambda, attn_subln_g, l)
        yb = hyena_lat(yb, p, hy_conv_w, hy_conv_b, hy_skip, filt_lat[0], filt_lat[1], tables_lat, l)
        yc = retention_lat(yc, p, ret_decay_exp, cos_c, sin_c, state_retention, l)

        x = merge_branches(x, ya, yb, yc, p, mods, w_branch_a, w_branch_b, w_branch_c, w_out, l)
        x = moe(x, norm2_g, mods, wr_t, b_r, tri, moe_w1, moe_w3, moe_w2, l,
                final_g if l == DEPTH - 1 else None)

    y_prompt, y_sample = x
    return (y_prompt.reshape(BATCH, SEQ, D_MODEL), y_sample.reshape(DEC_BATCH, DEC_SEQ, D_MODEL),
            new_k, new_v, states)
```

```python
import functools
import math

import jax
import jax.numpy as jnp
import numpy as np
from jax import lax
from jax.experimental import pallas as pl
from jax.experimental.pallas import tpu as pltpu

F32 = jnp.float32
BF16 = jnp.bfloat16

D_MODEL = 1024
BATCH = 16
SEQ = 256
DEPTH = 4
DEC_BATCH = 2
DEC_SEQ = 1024
PAST_LEN = 256
GRID_W = 64
EPS = 1e-6
ROPE_BASE = 10000.0
H_A = 4
DH_A = 64
DV_A = 128
W_A = 512
C_B = 512
SHORT_K = 3
H_EMB = 33
H_FFN = 64
H_FAST_DECAY = 0.3
H_SLOW_DECAY = 1.5
H_TARGET = 1e-2
H_C = 4
DK_C = 64
DV_C = 128
W_C = 512
N_EXPERTS = 16
N_GROUPS = 4
EXP_PER_GROUP = 4
D_FF = 512
D_IN = 4608

T_CTX = BATCH * SEQ
T_LAT = DEC_BATCH * DEC_SEQ
T_ALL = T_CTX + T_LAT
TM = 1024
N_TILES = T_ALL // TM
CTX_TILES = T_CTX // TM
LANES = 128
SUBLANES = 8
MOE_BLOCK = 256
MOE_BLOCKS = T_ALL // MOE_BLOCK + N_GROUPS
T_PAD = MOE_BLOCKS * MOE_BLOCK
VMEM_LIMIT = 56 * 1024 * 1024


def _params(n_axes):
    return pltpu.CompilerParams(
        dimension_semantics=("arbitrary",) * n_axes, vmem_limit_bytes=VMEM_LIMIT)


def _mod_row(i):
    return jnp.maximum(i - (CTX_TILES - 1), 0)


def _rms(x):
    return x * lax.rsqrt(jnp.mean(x * x, axis=-1, keepdims=True) + EPS)


def _sigmoid(x):
    return 0.5 * jnp.tanh(0.5 * x) + 0.5


def _silu(x):
    return x * _sigmoid(x)


def _dot(a, b):
    return jnp.dot(a, b, preferred_element_type=F32)


def _dot_nt(a, b):
    return lax.dot_general(a, b, (((1,), (1,)), ((), ())), preferred_element_type=F32)


def _dot_tn(a, b):
    return lax.dot_general(a, b, (((0,), (0,)), ((), ())), preferred_element_type=F32)


def _split3(x):
    hi = x.astype(BF16)
    lo = (x - hi.astype(F32)).astype(BF16)
    return hi, lo


def _dot3(a, b):
    ah, al = _split3(a)
    bh, bl = _split3(b)
    return _dot(ah, bh) + (_dot(ah, bl) + _dot(al, bh))


def _dot3_nt(a, b):
    ah, al = _split3(a)
    bh, bl = _split3(b)
    return _dot_nt(ah, bh) + (_dot_nt(ah, bl) + _dot_nt(al, bh))


def _ada_kernel(c_ref, w_ref, b_ref, o_ref):
    s = _silu(c_ref[...])
    o_ref[0] = _dot(s.astype(BF16), w_ref[0].astype(BF16)) + b_ref[0]


def ada_modulation(cond8, w_ada, b_ada):
    tn = 1536
    n = 6 * D_MODEL
    return pl.pallas_call(
        _ada_kernel,
        grid=(DEPTH, n // tn),
        in_specs=[
            pl.BlockSpec((8, D_MODEL), lambda l, j: (0, 0)),
            pl.BlockSpec((1, D_MODEL, tn), lambda l, j: (l, 0, j)),
            pl.BlockSpec((1, 1, tn), lambda l, j: (l, 0, j)),
        ],
        out_specs=pl.BlockSpec((1, 8, tn), lambda l, j: (l, 0, j)),
        out_shape=jax.ShapeDtypeStruct((DEPTH, 8, n), F32),
        compiler_params=_params(2),
        name="ada_modulation",
    )(cond8, w_ada, b_ada.reshape(DEPTH, 1, n))


PROJ_TM = 3072
PROJ_TN = 512
GATE_BLOCKS = 3 * D_MODEL // PROJ_TN
IN_BLOCKS = D_IN // PROJ_TN
P_WIDTH = 3 * D_MODEL + D_IN
KV_FIRST = GATE_BLOCKS + 1
P_Q = 3 * D_MODEL // W_A
P_HYENA = (3 * D_MODEL + 3 * W_A) // (3 * C_B)
P_RET_Q = (3 * D_MODEL + 3 * W_A + 3 * C_B) // (H_C * DK_C)
P_RET_V = (3 * D_MODEL + 3 * W_A + 3 * C_B + 2 * H_C * DK_C) // W_C


def _in_proj_kernel(x_hbm, g_ref, m_ref, win_ref, wg_ref, bg_ref, p_ref, kv_ref, h_ref, x_ref, sem, *, layer):
    i = pl.program_id(0)
    j = pl.program_id(1)

    def x_copy(tile):
        start = pl.multiple_of(tile * PROJ_TM, PROJ_TM)
        return pltpu.make_async_copy(x_hbm.at[pl.ds(start, PROJ_TM), :], x_ref, sem.at[0])

    @pl.when((i == 0) & (j == 0))
    def _():
        x_copy(0).start()

    @pl.when(j == 0)
    def _():
        x_copy(i).wait()
        for s in range(PROJ_TM // TM):
            rows = slice(s * TM, (s + 1) * TM)
            mod = m_ref[layer, _mod_row(i * (PROJ_TM // TM) + s)]
            y = _rms(x_ref[rows, :]) * g_ref[layer:layer + 1, :]
            h_ref[rows, :] = (y * (1.0 + mod[1:2, :]) + mod[0:1, :]).astype(BF16)

        @pl.when(i + 1 < T_ALL // PROJ_TM)
        def _():
            x_copy(i + 1).start()

    @pl.when(j < GATE_BLOCKS)
    def _():
        acc = _dot(h_ref[...], wg_ref[0].astype(BF16)) + bg_ref[0]
        p_ref[...] = _sigmoid(acc).astype(BF16)

    @pl.when(j >= GATE_BLOCKS)
    def _():
        acc = _dot(h_ref[...], win_ref[0].astype(BF16))
        p_ref[...] = acc.astype(BF16)

        @pl.when((j == KV_FIRST) | (j == KV_FIRST + 1))
        def _():
            kv_ref[...] = acc


def in_projection(x, gain, mods, w_in, w_gate, b_gate, layer):
    return pl.pallas_call(
        functools.partial(_in_proj_kernel, layer=layer),
        grid=(T_ALL // PROJ_TM, GATE_BLOCKS + IN_BLOCKS),
        in_specs=[
            pl.BlockSpec(memory_space=pl.ANY),
            pl.BlockSpec((DEPTH, D_MODEL), lambda i, j: (0, 0)),
            pl.BlockSpec((DEPTH, 8, 6, D_MODEL), lambda i, j: (0, 0, 0, 0)),
            pl.BlockSpec((1, D_MODEL, PROJ_TN), lambda i, j: (layer, 0, jnp.maximum(j - GATE_BLOCKS, 0))),
            pl.BlockSpec((1, D_MODEL, PROJ_TN), lambda i, j: (layer, 0, jnp.minimum(j, GATE_BLOCKS - 1))),
            pl.BlockSpec((1, 1, PROJ_TN), lambda i, j: (layer, 0, jnp.minimum(j, GATE_BLOCKS - 1))),
        ],
        out_specs=[
            pl.BlockSpec((PROJ_TM, PROJ_TN), lambda i, j: (i, j)),
            pl.BlockSpec((PROJ_TM, PROJ_TN), lambda i, j: (i, jnp.clip(j - KV_FIRST, 0, 1))),
        ],
        out_shape=[jax.ShapeDtypeStruct((T_ALL, P_WIDTH), BF16),
                   jax.ShapeDtypeStruct((T_ALL, 2 * W_A), F32)],
        scratch_shapes=[pltpu.VMEM((PROJ_TM, D_MODEL), BF16), pltpu.VMEM((PROJ_TM, D_MODEL), F32),
                        pltpu.SemaphoreType.DMA((1,))],
        compiler_params=_params(2),
        name="in_projection",
    )(x, gain, mods, w_in, w_gate, b_gate.reshape(DEPTH, 1, 3 * D_MODEL))


def _rope_tables():
    n = DEC_SEQ
    rows = n // GRID_W
    row = jnp.repeat(jnp.arange(rows, dtype=F32), GRID_W)
    col = jnp.tile(jnp.arange(GRID_W, dtype=F32), rows)
    quarter = DH_A // 4
    inv = ROPE_BASE ** (-jnp.arange(quarter, dtype=F32) / quarter)
    ar = row[:, None] * inv
    ac = col[:, None] * inv
    ang = jnp.concatenate([ar, ar, ac, ac], axis=-1)
    cos = jnp.tile(jnp.cos(ang), (1, 512 // DH_A))
    sgn = jnp.tile(jnp.concatenate([-jnp.ones((quarter,), F32), jnp.ones((quarter,), F32)]), 512 // (2 * quarter))
    sin = jnp.tile(jnp.sin(ang), (1, 512 // DH_A)) * sgn
    return cos, sin


def _rope(x, cos, sin):
    w = x.shape[-1]
    lane = lax.broadcasted_iota(jnp.int32, x.shape, 1)
    first = (lane % 32) < 16
    rot = jnp.where(first, pltpu.roll(x, w - 16, 1), pltpu.roll(x, 16, 1))
    return x * cos + rot * sin


def _lambda(al_ref, layer):
    al = al_ref[layer]
    a = jnp.sum(al[0:1, :] * al[1:2, :], axis=-1, keepdims=True)
    b = jnp.sum(al[2:3, :] * al[3:4, :], axis=-1, keepdims=True)
    lam_init = 0.8 - 0.6 * math.exp(-0.3 * layer)
    return jnp.exp(a) - jnp.exp(b) + lam_init, lam_init


def _scores(q, k):
    return _dot_nt(q, k).astype(BF16)


def _softmax_numerators(s):
    return jnp.exp2(s - jnp.max(s, axis=-1, keepdims=True))


def _with_ones(v):
    ones = jnp.ones((v.shape[0], LANES), BF16)
    return jnp.concatenate([part for h in range(H_A) for part in (v[:, h * LANES:(h + 1) * LANES], ones)], axis=1)


def _diff_attention(q, k, v, lam, lam_init, subln, o_ref):
    lane = lax.broadcasted_iota(jnp.int32, q.shape, 1)
    q = q * (DH_A ** -0.5 * math.log2(math.e))
    q0 = jnp.where((lane % LANES) < DH_A, q, 0.0).astype(BF16)
    q1 = jnp.where((lane % LANES) >= DH_A, q, 0.0).astype(BF16)
    for h in range(H_A):
        sl = slice(h * LANES, (h + 1) * LANES)
        kh = k[:, sl]
        v_ones = v[:, 2 * h * LANES:2 * (h + 1) * LANES]
        o0 = _dot(_softmax_numerators(_scores(q0[:, sl], kh)), v_ones)
        o1 = _dot(_softmax_numerators(_scores(q1[:, sl], kh)), v_ones)
        r0 = 1.0 / o0[:, DV_A:DV_A + 1]
        r1 = lam / o1[:, DV_A:DV_A + 1]
        o = o0[:, :DV_A] * r0 - o1[:, :DV_A] * r1
        o = _rms(o) * subln * (1.0 - lam_init)
        o_ref[:, sl] = o.astype(o_ref.dtype)


def _attn_ctx_kernel(q_ref, k_ref, v_ref, al_ref, g_ref, o_ref, knew_ref, vnew_ref, *, layer):
    lam, lam_init = _lambda(al_ref, layer)
    subln = g_ref[layer:layer + 1, :]
    k = k_ref[...]
    v = v_ref[...]
    for h in range(H_A):
        knew_ref[0, 0, :, h, :] = k[:, h * LANES:(h + 1) * LANES]
        vnew_ref[0, 0, :, h, :] = v[:, h * LANES:(h + 1) * LANES]
    _diff_attention(q_ref[...].astype(F32), k.astype(BF16), _with_ones(v.astype(BF16)), lam, lam_init, subln,
                    o_ref)


QB = 256


def _attn_lat_kernel(ya_ref, q_ref, k_ref, v_ref, ck_ref, cv_ref, cosq_ref, sinq_ref, cos_ref, sin_ref,
                     al_ref, g_ref, o_ref, kall_ref, vall_ref, *, layer):
    del ya_ref

    @pl.when(pl.program_id(1) == 0)
    def _():
        kall_ref[0:PAST_LEN, :] = ck_ref[0, 0].astype(BF16)
        vall_ref[0:PAST_LEN, :] = _with_ones(cv_ref[0, 0].astype(BF16))
        kall_ref[PAST_LEN:, :] = _rope(k_ref[...], cos_ref[...], sin_ref[...]).astype(BF16)
        vall_ref[PAST_LEN:, :] = _with_ones(v_ref[...].astype(BF16))

    lam, lam_init = _lambda(al_ref, layer)
    subln = g_ref[layer:layer + 1, :]
    q = _rope(q_ref[...].astype(F32), cosq_ref[...], sinq_ref[...])
    _diff_attention(q, kall_ref[...], vall_ref[...], lam, lam_init, subln, o_ref)


def attention_lat(ya, q, kv, cache_k, cache_v, cos, sin, attn_lambda, subln_g, layer):
    nqb = DEC_SEQ // QB
    row0 = T_CTX // QB
    seq0 = T_CTX // DEC_SEQ
    return pl.pallas_call(
        functools.partial(_attn_lat_kernel, layer=layer),
        grid=(DEC_BATCH, nqb),
        in_specs=[
            pl.BlockSpec(memory_space=pl.ANY),
            pl.BlockSpec((QB, W_A), lambda b, i: (row0 + b * nqb + i, P_Q)),
            pl.BlockSpec((DEC_SEQ, W_A), lambda b, i: (seq0 + b, 0)),
            pl.BlockSpec((DEC_SEQ, W_A), lambda b, i: (seq0 + b, 1)),
            pl.BlockSpec((1, 1, PAST_LEN, W_A), lambda b, i: (b, layer, 0, 0)),
            pl.BlockSpec((1, 1, PAST_LEN, W_A), lambda b, i: (b, layer, 0, 0)),
            pl.BlockSpec((QB, W_A), lambda b, i: (i, 0)),
            pl.BlockSpec((QB, W_A), lambda b, i: (i, 0)),
            pl.BlockSpec((DEC_SEQ, W_A), lambda b, i: (0, 0)),
            pl.BlockSpec((DEC_SEQ, W_A), lambda b, i: (0, 0)),
            pl.BlockSpec((DEPTH, 4, DH_A), lambda b, i: (0, 0, 0)),
            pl.BlockSpec((DEPTH, DV_A), lambda b, i: (0, 0)),
        ],
        out_specs=pl.BlockSpec((QB, W_A), lambda b, i: (row0 + b * nqb + i, 0)),
        out_shape=jax.ShapeDtypeStruct((T_ALL, W_A), BF16),
        scratch_shapes=[pltpu.VMEM((PAST_LEN + DEC_SEQ, W_A), BF16),
                        pltpu.VMEM((PAST_LEN + DEC_SEQ, 2 * W_A), BF16)],
        input_output_aliases={0: 0},
        compiler_params=_params(2),
        name="attention_lat",
    )(ya, q, kv, kv, cache_k, cache_v, cos, sin, cos, sin, attn_lambda, subln_g)


def _dft_tables(n):
    k = np.arange(n, dtype=np.int64)
    prod = (2 * k[:, None] + 1) * k[None, :]
    ang = (prod % (4 * n)).astype(np.float64) * (math.pi / (2 * n))
    c = np.cos(ang).astype(np.float32)
    s = np.sin(ang).astype(np.float32)
    return tuple(jnp.asarray(t).astype(BF16) for t in (c, s, c.T, s.T))


def _filter_features(n):
    t = jnp.linspace(0.0, 1.0, n, dtype=F32)[:, None]
    bands = (H_EMB - 1) // 2
    w = 2.0 * math.pi * jnp.arange(n, dtype=F32)[:, None] / n
    fr = jnp.linspace(1e-4, bands - 1, bands, dtype=F32)
    z = jnp.concatenate([t, jnp.cos(w * fr), -jnp.sin(w * fr)], axis=-1)
    return jnp.pad(z, ((0, 0), (0, LANES - H_EMB)))


def _filter_decay_rates():
    max_decay = math.log(H_TARGET) / H_FAST_DECAY
    min_decay = math.log(H_TARGET) / H_SLOW_DECAY
    return jnp.abs(jnp.linspace(min_decay, max_decay, C_B, dtype=F32))[None, :]


def _hyena_filter_kernel(z_ref, w1_ref, b1_ref, w2_ref, b2_ref, w3_ref, dr_ref, c_ref, s_ref,
                         gre_ref, gim_ref):
    z = z_ref[...]
    hid = jnp.sin(_dot3(z, w1_ref[0]) + b1_ref[0])
    hid = jnp.sin(_dot3(hid, w2_ref[0]) + b2_ref[0])
    h = _dot(hid.astype(BF16), w3_ref[0].astype(BF16))
    window = jnp.exp(-z[:, 0:1] * dr_ref[...])
    hf = h[:, :C_B] * window
    hb = h[:, C_B:] * window
    total = jnp.sum(jnp.abs(hf) + jnp.abs(hb), axis=0, keepdims=True)
    hf = hf / total
    hb = hb / total
    row = lax.broadcasted_iota(jnp.int32, hb.shape, 0)
    hb = jnp.where(row == 0, 0.0, hb)
    gre_ref[0] = _dot(c_ref[...], (hf + hb).astype(BF16))
    gim_ref[0] = _dot(s_ref[...], (hb - hf).astype(BF16))


def hyena_filters(n, z, decay_rates, cf, sf, f_w1, f_b1, f_w2, f_b2, f_w3):
    w1 = jnp.pad(f_w1, ((0, 0), (0, LANES - H_EMB), (0, 0)))
    full = lambda shape: pl.BlockSpec(shape, lambda l: (0,) * len(shape))
    per_layer = lambda shape: pl.BlockSpec((1,) + shape, lambda l: (l,) + (0,) * len(shape))
    return pl.pallas_call(
        _hyena_filter_kernel,
        grid=(DEPTH,),
        in_specs=[
            full((n, LANES)),
            per_layer((LANES, H_FFN)), per_layer((1, H_FFN)),
            per_layer((H_FFN, H_FFN)), per_layer((1, H_FFN)),
            per_layer((H_FFN, 2 * C_B)),
            full((1, C_B)), full((n, n)), full((n, n)),
        ],
        out_specs=[per_layer((n, C_B)), per_layer((n, C_B))],
        out_shape=[jax.ShapeDtypeStruct((DEPTH, n, C_B), F32)] * 2,
        compiler_params=_params(1),
        name="hyena_filters",
    )(z, w1, f_b1.reshape(DEPTH, 1, H_FFN), f_w2, f_b2.reshape(DEPTH, 1, H_FFN), f_w3,
      decay_rates, cf, sf)


def _hyena_kernel(*refs, layer, n, aliased):
    if aliased:
        refs = refs[1:]
    u_ref, cw_ref, cb_ref, skip_ref, gre_ref, gim_ref, c_ref, s_ref, ct_ref, st_ref, o_ref = refs
    u = u_ref[...].astype(F32)
    row = lax.broadcasted_iota(jnp.int32, u.shape, 0)
    prev = jnp.where(row == 0, 0.0, pltpu.roll(u, 1, 0))
    nxt = jnp.where(row == n - 1, 0.0, pltpu.roll(u, n - 1, 0))
    cw = cw_ref[layer]
    uc = cb_ref[layer:layer + 1, :] + prev * cw[0:1, :] + u * cw[1:2, :] + nxt * cw[2:3, :]
    x0 = uc[:, :C_B]
    x1 = uc[:, C_B:2 * C_B]
    v = uc[:, 2 * C_B:]
    w = v * x1
    wb = w.astype(BF16)
    ure = _dot(c_ref[...], wb)
    uim = -_dot(s_ref[...], wb)
    gre = gre_ref[0]
    gim = gim_ref[0]
    yre = (ure * gre - uim * gim).astype(BF16)
    yim = (ure * gim + uim * gre).astype(BF16)
    y = (_dot(ct_ref[...], yre) - _dot(st_ref[...], yim)) * (1.0 / n)
    y = y + w * skip_ref[layer:layer + 1, :]
    o_ref[...] = (x0 * y).astype(o_ref.dtype)


def _full(shape):
    return pl.BlockSpec(shape, lambda *_: (0,) * len(shape))


def _hyena_specs(n, seq0, layer):
    return [
        pl.BlockSpec((n, 3 * C_B), lambda b, *_: (seq0 + b, P_HYENA)),
        _full((DEPTH, SHORT_K, 3 * C_B)), _full((DEPTH, 3 * C_B)), _full((DEPTH, C_B)),
        pl.BlockSpec((1, n, C_B), lambda *_: (layer, 0, 0)),
        pl.BlockSpec((1, n, C_B), lambda *_: (layer, 0, 0)),
        _full((n, n)), _full((n, n)), _full((n, n)), _full((n, n)),
    ]


def hyena_lat(yb, p, conv_w, conv_b, skip, gre, gim, tables, layer):
    seq0 = T_CTX // DEC_SEQ
    return pl.pallas_call(
        functools.partial(_hyena_kernel, layer=layer, n=DEC_SEQ, aliased=True),
        grid=(DEC_BATCH,),
        in_specs=[pl.BlockSpec(memory_space=pl.ANY)] + _hyena_specs(DEC_SEQ, seq0, layer),
        out_specs=pl.BlockSpec((DEC_SEQ, C_B), lambda b: (seq0 + b, 0)),
        out_shape=jax.ShapeDtypeStruct((T_ALL, C_B), BF16),
        input_output_aliases={0: 0},
        compiler_params=_params(1),
        name="hyena_lat",
    )(yb, p, conv_w, conv_b, skip, gre, gim, *tables)


RET_QB = 256


def _retention_kernel(*refs, layer, n, latent):
    if latent:
        _, q_ref, k_ref, v_ref, g_ref, de_ref, cos_ref, sin_ref, s0_ref, o_ref, w_ref, vec_ref = refs
        st_ref = None
    else:
        q_ref, k_ref, v_ref, g_ref, de_ref, o_ref, st_ref, w_ref, vec_ref = refs[-9:]
    log_gamma = jnp.log1p(-jnp.exp2(-de_ref[layer]))

    @pl.when(pl.program_id(0) == 0)
    def _():
        t = lax.broadcasted_iota(jnp.int32, (n, n), 0)
        s = lax.broadcasted_iota(jnp.int32, (n, n), 1)
        lag = (t - s).astype(F32)
        for h in range(H_C):
            rate = jnp.where(lag > 0, log_gamma[0:1, h:h + 1], log_gamma[1:2, h:h + 1])
            w_ref[h] = jnp.where(lag == 0, 2.0, jnp.exp(rate * jnp.abs(lag)))
            pos = lax.broadcasted_iota(jnp.int32, (n, LANES), 0).astype(F32)
            steps = (pos + 1.0, float(n) - pos) if latent else (float(n - 1) - pos, pos)
            for d in range(2):
                vec_ref[h, d] = jnp.exp(log_gamma[d:d + 1, h:h + 1] * steps[d])

    q = q_ref[...].astype(F32)
    k = k_ref[...].astype(F32) * (DK_C ** -0.5)
    if latent:
        q = _rope(q, cos_ref[...], sin_ref[...])
        k = _rope(k, cos_ref[...], sin_ref[...])
    v = v_ref[...].astype(BF16)
    gate = g_ref[...].astype(F32)
    lane = lax.broadcasted_iota(jnp.int32, (n, LANES), 1)
    zeros64 = jnp.zeros((DK_C, DV_C), F32)

    for h in range(H_C):
        pair = slice((h // 2) * LANES, (h // 2 + 1) * LANES)
        lo = (h % 2) * DK_C
        own = (lane >= lo) & (lane < lo + DK_C)
        vs = slice(h * DV_C, (h + 1) * DV_C)
        qh = jnp.where(own, q[:, pair], 0.0).astype(BF16)
        kh = jnp.where(own, k[:, pair], 0.0)
        kb = kh.astype(BF16)
        vh = v[:, vs]
        if latent:
            s0 = [jnp.concatenate([s0_ref[0, 0, d, h], zeros64] if lo == 0 else [zeros64, s0_ref[0, 0, d, h]],
                                  axis=0).astype(BF16) for d in range(2)]
            carry = _dot(qh, s0[0]) * vec_ref[h, 0] + _dot(qh, s0[1]) * vec_ref[h, 1]
        for r0 in range(0, n, RET_QB):
            rows = slice(r0, r0 + RET_QB)
            att = _dot_nt(qh[rows], kb) * w_ref[h, rows, :]
            out = _dot(att.astype(BF16), vh)
            if latent:
                out = out + carry[rows]
            o_ref[rows, vs] = (_rms(out) * _silu(gate[rows, vs])).astype(o_ref.dtype)
        if st_ref is not None:
            sf = _dot_tn((kh * vec_ref[h, 0]).astype(BF16), vh)
            sb = _dot_tn((kh * vec_ref[h, 1]).astype(BF16), vh)
            st_ref[0, 0, 0, h] = sf[lo:lo + DK_C, :]
            st_ref[0, 0, 1, h] = sb[lo:lo + DK_C, :]


def _retention_specs(n, seq0):
    return [
        pl.BlockSpec((n, H_C * DK_C), lambda b, *_: (seq0 + b, P_RET_Q)),
        pl.BlockSpec((n, H_C * DK_C), lambda b, *_: (seq0 + b, P_RET_Q + 1)),
        pl.BlockSpec((n, W_C), lambda b, *_: (seq0 + b, P_RET_V)),
        pl.BlockSpec((n, W_C), lambda b, *_: (seq0 + b, P_RET_V + 1)),
        _full((DEPTH, 2, H_C)),
    ]


def retention_lat(yc, p, decay_exp, cos, sin, state0, layer):
    n = DEC_SEQ
    seq0 = T_CTX // DEC_SEQ
    return pl.pallas_call(
        functools.partial(_retention_kernel, layer=layer, n=n, latent=True),
        grid=(DEC_BATCH,),
        in_specs=[pl.BlockSpec(memory_space=pl.ANY)] + _retention_specs(n, seq0) + [
            _full((n, H_C * DK_C)), _full((n, H_C * DK_C)),
            pl.BlockSpec((1, 1, 2, H_C, DK_C, DV_C), lambda b: (b, layer, 0, 0, 0, 0)),
        ],
        out_specs=pl.BlockSpec((n, W_C), lambda b: (seq0 + b, 0)),
        out_shape=jax.ShapeDtypeStruct((T_ALL, W_C), BF16),
        scratch_shapes=[pltpu.VMEM((H_C, n, n), F32), pltpu.VMEM((H_C, 2, n, LANES), F32)],
        input_output_aliases={0: 0},
        compiler_params=_params(1),
        name="retention_lat",
    )(yc, p, p, p, p, decay_exp, cos, sin, state0)


N_ATTN_IN = 5
N_HYENA_IN = 10
N_RET_IN = 5


def _mixers_ctx_kernel(*refs, layer, n_threaded):
    refs = refs[n_threaded:]
    attn_in = refs[:N_ATTN_IN]
    hyena_in = refs[N_ATTN_IN:N_ATTN_IN + N_HYENA_IN]
    ret_in = refs[N_ATTN_IN + N_HYENA_IN:N_ATTN_IN + N_HYENA_IN + N_RET_IN]
    ya_ref, knew_ref, vnew_ref, yb_ref, yc_ref, st_ref, w_ref, vec_ref = refs[N_ATTN_IN + N_HYENA_IN + N_RET_IN:]
    _attn_ctx_kernel(*attn_in, ya_ref, knew_ref, vnew_ref, layer=layer)
    _hyena_kernel(*hyena_in, yb_ref, layer=layer, n=SEQ, aliased=False)
    _retention_kernel(*ret_in, yc_ref, st_ref, w_ref, vec_ref, layer=layer, n=SEQ, latent=False)


def mixers_ctx(p, kv, attn_lambda, subln_g, conv_w, conv_b, skip, gre, gim, tables, decay_exp, layer, threaded):
    cache_shape = jax.ShapeDtypeStruct((BATCH, DEPTH, SEQ, H_A, DV_A), F32)
    cache_spec = pl.BlockSpec((1, 1, SEQ, H_A, DV_A), lambda b: (b, layer, 0, 0, 0))
    branch_shape = jax.ShapeDtypeStruct((T_ALL, W_A), BF16)
    branch_spec = pl.BlockSpec((SEQ, W_A), lambda b: (b, 0))
    n_threaded = len(threaded)
    attn_specs = [
        pl.BlockSpec((SEQ, W_A), lambda b: (b, P_Q)),
        pl.BlockSpec((SEQ, W_A), lambda b: (b, 0)),
        pl.BlockSpec((SEQ, W_A), lambda b: (b, 1)),
        _full((DEPTH, 4, DH_A)), _full((DEPTH, DV_A)),
    ]
    return pl.pallas_call(
        functools.partial(_mixers_ctx_kernel, layer=layer, n_threaded=n_threaded),
        grid=(BATCH,),
        in_specs=([pl.BlockSpec(memory_space=pl.ANY)] * n_threaded + attn_specs
                  + _hyena_specs(SEQ, 0, layer) + _retention_specs(SEQ, 0)),
        out_specs=[branch_spec, cache_spec, cache_spec, branch_spec, branch_spec,
                   pl.BlockSpec((1, 1, 2, H_C, DK_C, DV_C), lambda b: (b, layer, 0, 0, 0, 0))],
        out_shape=[branch_shape, cache_shape, cache_shape, branch_shape, branch_shape,
                   jax.ShapeDtypeStruct((BATCH, DEPTH, 2, H_C, DK_C, DV_C), F32)],
        scratch_shapes=[pltpu.VMEM((H_C, SEQ, SEQ), F32), pltpu.VMEM((H_C, 2, SEQ, LANES), F32)],
        input_output_aliases=dict(zip(range(n_threaded), (1, 2, 5))),
        compiler_params=_params(1),
        name="mixers_ctx",
    )(*threaded, p, kv, kv, attn_lambda, subln_g, p, conv_w, conv_b, skip, gre, gim, *tables,
      p, p, p, p, decay_exp)


def _merge_kernel(x_ref, ya_ref, yb_ref, yc_ref, g_ref, m_ref, wa_ref, wb_ref, wc_ref, wo_ref,
                  o_ref, wbf_ref, wobf_ref):
    @pl.when(pl.program_id(0) == 0)
    def _():
        wbf_ref[0] = wa_ref[0].astype(BF16)
        wbf_ref[1] = wb_ref[0].astype(BF16)
        wbf_ref[2] = wc_ref[0].astype(BF16)
        wobf_ref[...] = wo_ref[0].astype(BF16)

    g = g_ref[...]
    merged = (g[:, :D_MODEL] * _dot(ya_ref[...], wbf_ref[0])
              + g[:, D_MODEL:2 * D_MODEL] * _dot(yb_ref[...], wbf_ref[1])
              + g[:, 2 * D_MODEL:] * _dot(yc_ref[...], wbf_ref[2]))
    g1 = m_ref[0, 0, 2:3, :]
    o_ref[...] = x_ref[...] + g1 * _dot(merged.astype(BF16), wobf_ref[...])


def merge_branches(x, ya, yb, yc, gates, mods, w_a, w_b, w_c, w_out, layer):
    tile = lambda w: pl.BlockSpec((TM, w), lambda i: (i, 0))
    wspec = lambda k: pl.BlockSpec((1, k, D_MODEL), lambda i: (layer, 0, 0))
    return pl.pallas_call(
        _merge_kernel,
        grid=(N_TILES,),
        in_specs=[
            tile(D_MODEL), tile(W_A), tile(C_B), tile(W_C), tile(3 * D_MODEL),
            pl.BlockSpec((1, 1, 6, D_MODEL), lambda i: (layer, _mod_row(i), 0, 0)),
            wspec(W_A), wspec(C_B), wspec(W_C), wspec(D_MODEL),
        ],
        out_specs=tile(D_MODEL),
        out_shape=jax.ShapeDtypeStruct((T_ALL, D_MODEL), F32),
        scratch_shapes=[pltpu.VMEM((3, W_A, D_MODEL), BF16), pltpu.VMEM((D_MODEL, D_MODEL), BF16)],
        compiler_params=_params(1),
        name="merge_branches",
    )(x, ya, yb, yc, gates, mods, w_a, w_b, w_c, w_out)


def _route(h2, wr_t, b_r):
    logits = _dot3_nt(wr_t, h2)
    m = jnp.max(logits, axis=0, keepdims=True)
    e = jnp.exp(logits - m)
    scores = e / jnp.sum(e, axis=0, keepdims=True)
    sel = scores + b_r
    rows = [sel[i:i + 1, :] for i in range(N_EXPERTS)]
    in_group = []
    gscore = []
    for g in range(N_GROUPS):
        members = range(g * EXP_PER_GROUP, (g + 1) * EXP_PER_GROUP)
        total = None
        for i in members:
            rank = None
            for j in members:
                if j == i:
                    continue
                ahead = (rows[j] >= rows[i]) if j < i else (rows[j] > rows[i])
                ahead = ahead.astype(F32)
                rank = ahead if rank is None else rank + ahead
            chosen = rank < 2.0
            in_group.append(chosen)
            part = jnp.where(chosen, rows[i], 0.0)
            total = part if total is None else total + part
        gscore.append(total)
    gates = []
    group_hot = []
    for g in range(N_GROUPS):
        best = None
        for g2 in range(N_GROUPS):
            if g2 == g:
                continue
            wins = gscore[g] > gscore[g2] if g2 < g else gscore[g] >= gscore[g2]
            best = wins if best is None else best & wins
        group_hot.append(best.astype(F32))
        for i in range(g * EXP_PER_GROUP, (g + 1) * EXP_PER_GROUP):
            gates.append(jnp.where(best & in_group[i], scores[i:i + 1, :], 0.0))
    gates = jnp.concatenate(gates, axis=0)
    return gates / jnp.sum(gates, axis=0, keepdims=True), group_hot


def _moe_route_kernel(x_ref, g_ref, m_ref, wr_ref, br_ref, tri_ref, tiles_ref, gate_ref, meta_ref, count_ref,
                      *, layer):
    y = _rms(x_ref[...]) * g_ref[layer:layer + 1, :]
    h2 = y * (1.0 + m_ref[0, 0, 4:5, :]) + m_ref[0, 0, 3:4, :]
    gates, group_hot = _route(h2, wr_ref[...], br_ref[...])
    for s in range(SUBLANES):
        tiles_ref[pl.ds(s, TM, stride=SUBLANES), :] = h2[:, s * LANES:(s + 1) * LANES]
    padded = jnp.concatenate([gates, jnp.zeros((LANES - N_EXPERTS, TM), F32)], axis=0)
    gate_ref[...] = padded.T
    row = lax.broadcasted_iota(jnp.int32, (SUBLANES, TM), 0)
    hot = jnp.zeros((SUBLANES, TM), F32)
    for g in range(N_GROUPS):
        hot = jnp.where(row == g, group_hot[g], hot)
    before = _dot(hot.astype(BF16), tri_ref[...])
    rank = jnp.sum(hot * before, axis=0, keepdims=True)
    gid = group_hot[1] + 2.0 * group_hot[2] + 3.0 * group_hot[3]
    meta = jnp.where(row == 0, gid, jnp.where(row == 1, rank, 0.0))
    meta_ref[0] = meta.astype(jnp.int32)
    counts = jnp.sum(hot, axis=1, keepdims=True) + jnp.zeros((SUBLANES, LANES), F32)
    count_ref[0] = counts.astype(jnp.int32)


def moe_route(x, gain, mods, wr_t, b_r, tri, layer):
    return pl.pallas_call(
        functools.partial(_moe_route_kernel, layer=layer),
        grid=(N_TILES,),
        in_specs=[
            pl.BlockSpec((TM, D_MODEL), lambda i: (i, 0)),
            pl.BlockSpec((DEPTH, D_MODEL), lambda i: (0, 0)),
            pl.BlockSpec((1, 1, 6, D_MODEL), lambda i: (layer, _mod_row(i), 0, 0)),
            pl.BlockSpec((N_EXPERTS, D_MODEL), lambda i: (0, 0)),
            pl.BlockSpec((N_EXPERTS, 1), lambda i: (0, 0)),
            pl.BlockSpec((TM, TM), lambda i: (0, 0)),
        ],
        out_specs=[pl.BlockSpec((TM * SUBLANES, LANES), lambda i: (i, 0)),
                   pl.BlockSpec((TM, LANES), lambda i: (i, 0)),
                   pl.BlockSpec((1, SUBLANES, TM), lambda i: (i, 0, 0)),
                   pl.BlockSpec((1, SUBLANES, LANES), lambda i: (i, 0, 0))],
        out_shape=[jax.ShapeDtypeStruct((T_ALL * SUBLANES, LANES), F32),
                   jax.ShapeDtypeStruct((T_ALL, LANES), F32),
                   jax.ShapeDtypeStruct((N_TILES, SUBLANES, TM), jnp.int32),
                   jax.ShapeDtypeStruct((N_TILES, SUBLANES, LANES), jnp.int32)],
        compiler_params=_params(1),
        name="moe_route",
    )(x, gain, mods, wr_t, b_r, tri)


def _moe_positions(meta, counts):
    gid = meta[:, 0, :]
    rank = meta[:, 1, :]
    cnt = counts[:, :N_GROUPS, 0]
    total = jnp.sum(cnt, axis=0)
    padded = (total + (MOE_BLOCK - 1)) // MOE_BLOCK * MOE_BLOCK
    group_start = jnp.cumsum(padded) - padded
    base = group_start[None, :] + jnp.cumsum(cnt, axis=0) - cnt
    pos = rank
    for g in range(N_GROUPS):
        pos = pos + jnp.where(gid == g, base[:, g:g + 1], 0)
    first_block = group_start // MOE_BLOCK
    blk = jnp.arange(MOE_BLOCKS, dtype=jnp.int32)
    block_gid = sum((blk >= first_block[g]).astype(jnp.int32) for g in range(1, N_GROUPS))
    block_used = (blk * MOE_BLOCK < jnp.sum(padded)).astype(jnp.int32)
    later = jnp.where(block_gid[None, :] > block_gid[:, None], block_gid[None, :], N_GROUPS)
    next_group = jnp.min(later, axis=1)
    next_group = jnp.where(next_group == N_GROUPS, -1, next_group).astype(jnp.int32)
    return pos.reshape(T_ALL).astype(jnp.int32), block_gid, block_used, next_group


def _moe_permute_kernel(pos_ref, tiles_ref, gate_ref, sorted_ref, gsorted_ref):
    i = pl.program_id(0)

    @pl.when(i == 0)
    def _():
        def zero(b, carry):
            start = pl.multiple_of(b * TM, TM)
            sorted_ref[pl.ds(start, TM), :] = jnp.zeros((TM, LANES), F32)
            return carry
        lax.fori_loop(0, T_PAD * SUBLANES // TM, zero, 0)
        gsorted_ref[...] = jnp.zeros((T_PAD, LANES), F32)

    def move(t, carry):
        p = pos_ref[i * TM + t]
        dst = pl.multiple_of(p * SUBLANES, SUBLANES)
        src = pl.multiple_of(t * SUBLANES, SUBLANES)
        sorted_ref[pl.ds(dst, SUBLANES), :] = tiles_ref[pl.ds(src, SUBLANES), :]
        gsorted_ref[pl.ds(p, 1), :] = gate_ref[pl.ds(t, 1), :]
        return carry
    lax.fori_loop(0, TM, move, 0, unroll=8)


def moe_permute(pos, tiles, gates):
    return pl.pallas_call(
        _moe_permute_kernel,
        grid_spec=pltpu.PrefetchScalarGridSpec(
            num_scalar_prefetch=1,
            grid=(N_TILES,),
            in_specs=[pl.BlockSpec((TM * SUBLANES, LANES), lambda i, pos: (i, 0)),
                      pl.BlockSpec((TM, LANES), lambda i, pos: (i, 0))],
            out_specs=[pl.BlockSpec(memory_space=pltpu.VMEM), pl.BlockSpec(memory_space=pltpu.VMEM)],
        ),
        out_shape=[jax.ShapeDtypeStruct((T_PAD * SUBLANES, LANES), F32),
                   jax.ShapeDtypeStruct((T_PAD, LANES), F32)],
        compiler_params=_params(1),
        name="moe_permute",
    )(pos, tiles, gates)


def _group_changed(gid_ref, b):
    return (b == 0) | (gid_ref[b] != gid_ref[jnp.maximum(b - 1, 0)])


def _moe_expert_kernel(gid_ref, used_ref, next_ref, s_ref, gate_ref, w1_hbm, w3_hbm, w2_ref, y_ref,
                       w1f_ref, w3f_ref, w1b_ref, w3b_ref, w2b_ref, sem, *, layer):
    b = pl.program_id(0)

    def up_weight_copies(group):
        return (pltpu.make_async_copy(w1_hbm.at[layer, group], w1f_ref, sem.at[0]),
                pltpu.make_async_copy(w3_hbm.at[layer, group], w3f_ref, sem.at[1]))

    @pl.when(b == 0)
    def _():
        for copy in up_weight_copies(gid_ref[0]):
            copy.start()

    @pl.when(_group_changed(gid_ref, b))
    def _():
        for copy in up_weight_copies(gid_ref[b]):
            copy.wait()
        for j in range(EXP_PER_GROUP):
            cols = slice(j * D_FF, (j + 1) * D_FF)
            w1b_ref[:, cols] = w1f_ref[j].astype(BF16)
            w3b_ref[:, cols] = w3f_ref[j].astype(BF16)
        w2b_ref[...] = w2_ref[0, 0].astype(BF16)

        @pl.when(next_ref[b] >= 0)
        def _():
            for copy in up_weight_copies(next_ref[b]):
                copy.start()

    @pl.when(used_ref[b] == 1)
    def _():
        lhs = jnp.concatenate([s_ref[pl.ds(s, MOE_BLOCK, stride=SUBLANES), :].astype(BF16)
                               for s in range(SUBLANES)], axis=1)
        a = _dot(lhs, w1b_ref[...])
        g = _dot(lhs, w3b_ref[...])
        gates = gate_ref[...]
        lane = lax.broadcasted_iota(jnp.int32, (MOE_BLOCK, LANES), 1)
        first = gid_ref[b] * EXP_PER_GROUP
        parts = []
        for j in range(EXP_PER_GROUP):
            cols = slice(j * D_FF, (j + 1) * D_FF)
            gate = jnp.sum(jnp.where(lane == first + j, gates, 0.0), axis=1, keepdims=True)
            parts.append((_silu(a[:, cols]) * g[:, cols] * gate).astype(BF16))
        act = jnp.concatenate(parts, axis=1)
        y = _dot(act, w2b_ref[...])
        for s in range(SUBLANES):
            y_ref[pl.ds(s, MOE_BLOCK, stride=SUBLANES), :] = y[:, s * LANES:(s + 1) * LANES]

    @pl.when(used_ref[b] == 0)
    def _():
        y_ref[...] = jnp.zeros_like(y_ref)


def moe_experts(block_gid, block_used, next_group, sorted_rows, sorted_gates, w1, w3, w2, layer):
    group_ff = EXP_PER_GROUP * D_FF
    w1g = w1.reshape(DEPTH, N_GROUPS, EXP_PER_GROUP, D_MODEL, D_FF)
    w3g = w3.reshape(DEPTH, N_GROUPS, EXP_PER_GROUP, D_MODEL, D_FF)
    w2g = w2.reshape(DEPTH, N_GROUPS, group_ff, D_MODEL)
    block = lambda b, gid, used, nxt: (b, 0)
    return pl.pallas_call(
        functools.partial(_moe_expert_kernel, layer=layer),
        grid_spec=pltpu.PrefetchScalarGridSpec(
            num_scalar_prefetch=3,
            grid=(MOE_BLOCKS,),
            in_specs=[pl.BlockSpec((MOE_BLOCK * SUBLANES, LANES), block),
                      pl.BlockSpec((MOE_BLOCK, LANES), block),
                      pl.BlockSpec(memory_space=pl.ANY), pl.BlockSpec(memory_space=pl.ANY),
                      pl.BlockSpec((1, 1, group_ff, D_MODEL), lambda b, gid, used, nxt: (layer, gid[b], 0, 0))],
            out_specs=pl.BlockSpec((MOE_BLOCK * SUBLANES, LANES), block),
            scratch_shapes=[pltpu.VMEM((EXP_PER_GROUP, D_MODEL, D_FF), F32),
                            pltpu.VMEM((EXP_PER_GROUP, D_MODEL, D_FF), F32),
                            pltpu.VMEM((D_MODEL, group_ff), BF16), pltpu.VMEM((D_MODEL, group_ff), BF16),
                            pltpu.VMEM((group_ff, D_MODEL), BF16),
                            pltpu.SemaphoreType.DMA((2,))],
        ),
        out_shape=jax.ShapeDtypeStruct((T_PAD * SUBLANES, LANES), F32),
        compiler_params=_params(1),
        name="moe_experts",
    )(block_gid, block_used, next_group, sorted_rows, sorted_gates, w1g, w3g, w2g)


FINAL_TM = 512


def _moe_combine_kernel(pos_ref, ys_ref, x_ref, m_ref, *rest, tm, final):
    i = pl.program_id(0)
    buf_ref = rest[-2] if final else rest[-1]

    def move(t, carry):
        src = pl.multiple_of(pos_ref[i * tm + t] * SUBLANES, SUBLANES)
        dst = pl.multiple_of(t * SUBLANES, SUBLANES)
        buf_ref[pl.ds(dst, SUBLANES), :] = ys_ref[pl.ds(src, SUBLANES), :]
        return carry
    lax.fori_loop(0, tm, move, 0, unroll=8)

    x_new_ref = rest[-1] if final else rest[0]
    for s in range(SUBLANES):
        cols = slice(s * LANES, (s + 1) * LANES)
        y = buf_ref[pl.ds(s, tm, stride=SUBLANES), :]
        x_new_ref[:, cols] = x_ref[:, cols] + m_ref[0, 0, 5:6, cols] * y

    if final:
        gain_ref, ctx_ref, lat_ref = rest[:3]
        out = _rms(x_new_ref[...]) * gain_ref[...]

        @pl.when(i < T_CTX // tm)
        def _():
            ctx_ref[...] = out

        @pl.when(i >= T_CTX // tm)
        def _():
            lat_ref[...] = out


def moe_combine(pos, y_sorted, x, mods, layer, final_gain=None):
    final = final_gain is not None
    tm = FINAL_TM if final else TM
    ctx_tiles = T_CTX // tm
    tile = pl.BlockSpec((tm, D_MODEL), lambda i, pos: (i, 0))
    in_specs = [
        pl.BlockSpec(memory_space=pltpu.VMEM),
        tile,
        pl.BlockSpec((1, 1, 6, D_MODEL), lambda i, pos: (layer, _mod_row(i * tm // TM), 0, 0)),
    ]
    args = [pos, y_sorted, x, mods]
    scratch = [pltpu.VMEM((tm * SUBLANES, LANES), F32)]
    if final:
        in_specs.append(pl.BlockSpec((1, D_MODEL), lambda i, pos: (0, 0)))
        args.append(final_gain.reshape(1, D_MODEL))
        out_specs = [pl.BlockSpec((tm, D_MODEL), lambda i, pos: (jnp.minimum(i, ctx_tiles - 1), 0)),
                     pl.BlockSpec((tm, D_MODEL), lambda i, pos: (jnp.maximum(i - ctx_tiles, 0), 0))]
        out_shape = [jax.ShapeDtypeStruct((T_CTX, D_MODEL), F32), jax.ShapeDtypeStruct((T_LAT, D_MODEL), F32)]
        scratch.append(pltpu.VMEM((tm, D_MODEL), F32))
    else:
        out_specs = tile
        out_shape = jax.ShapeDtypeStruct((T_ALL, D_MODEL), F32)
    return pl.pallas_call(
        functools.partial(_moe_combine_kernel, tm=tm, final=final),
        grid_spec=pltpu.PrefetchScalarGridSpec(
            num_scalar_prefetch=1,
            grid=(T_ALL // tm,),
            in_specs=in_specs,
            out_specs=out_specs,
            scratch_shapes=scratch,
        ),
        out_shape=out_shape,
        compiler_params=_params(1),
        name="moe_combine",
    )(*args)


def moe(x, gain, mods, wr_t, b_r, tri, w1, w3, w2, layer, final_gain=None):
    tiles, gates, meta, counts = moe_route(x, gain, mods, wr_t, b_r, tri, layer)
    pos, block_gid, block_used, next_group = _moe_positions(meta, counts)
    sorted_rows, sorted_gates = moe_permute(pos, tiles, gates)
    y_sorted = moe_experts(block_gid, block_used, next_group, sorted_rows, sorted_gates, w1, w3, w2, layer)
    return moe_combine(pos, y_sorted, x, mods, layer, final_gain)


def kernel(x_prompt, x_sample, cache_attn_k, cache_attn_v, state_retention, c, c_ctx, w_ada, b_ada, norm1_g, norm2_g, final_g, w_in, attn_lambda, attn_subln_g, hy_conv_w, hy_conv_b, hy_f_w1, hy_f_b1, hy_f_w2, hy_f_b2, hy_f_w3, hy_skip, ret_decay_exp, w_branch_a, w_branch_b, w_branch_c, w_gate, b_gate, w_out, w_router, b_router, moe_w1, moe_w3, moe_w2):
    x = jnp.concatenate([x_prompt.reshape(T_CTX, D_MODEL), x_sample.reshape(T_LAT, D_MODEL)], axis=0)
    cond8 = jnp.concatenate([c_ctx[None, :], c, jnp.zeros((8 - 1 - DEC_BATCH, D_MODEL), F32)], axis=0)
    mods = ada_modulation(cond8, w_ada, b_ada).reshape(DEPTH, 8, 6, D_MODEL)

    cos, sin = _rope_tables()
    cos_c, sin_c = cos[:, :H_C * DK_C], sin[:, :H_C * DK_C]
    cache_k = cache_attn_k.reshape(DEC_BATCH, DEPTH, PAST_LEN, W_A)
    cache_v = cache_attn_v.reshape(DEC_BATCH, DEPTH, PAST_LEN, W_A)
    decay_rates = _filter_decay_rates()
    tables_ctx = _dft_tables(SEQ)
    tables_lat = _dft_tables(DEC_SEQ)
    filt_ctx = hyena_filters(SEQ, _filter_features(SEQ), decay_rates, tables_ctx[0], tables_ctx[1],
                             hy_f_w1, hy_f_b1, hy_f_w2, hy_f_b2, hy_f_w3)
    filt_lat = hyena_filters(DEC_SEQ, _filter_features(DEC_SEQ), decay_rates, tables_lat[0], tables_lat[1],
                             hy_f_w1, hy_f_b1, hy_f_w2, hy_f_b2, hy_f_w3)
    wr_t = w_router.T
    b_r = b_router.reshape(N_EXPERTS, 1)
    tri = jnp.asarray(np.triu(np.ones((TM, TM), np.float32), 1), dtype=BF16)

    threaded = ()
    for l in range(DEPTH):
        p, kv = in_projection(x, norm1_g, mods, w_in, w_gate, b_gate, l)

        ya, new_k, new_v, yb, yc, states = mixers_ctx(
            p, kv, attn_lambda, attn_subln_g, hy_conv_w, hy_conv_b, hy_skip, filt_ctx[0], filt_ctx[1],
            tables_ctx, ret_decay_exp, l, threaded)
        threaded = (new_k, new_v, states)
        ya = attention_lat(ya, p, kv, cache_k, cache_v, cos, sin, attn_lambda, attn_subln_g, l)
        yb = hyena_lat(yb, p, hy_conv_w, hy_conv_b, hy_skip, filt_lat[0], filt_lat[1], tables_lat, l)
        yc = retention_lat(yc, p, ret_decay_exp, cos_c, sin_c, state_retention, l)

        x = merge_branches(x, ya, yb, yc, p, mods, w_branch_a, w_branch_b, w_branch_c, w_out, l)
        x = moe(x, norm2_g, mods, wr_t, b_r, tri, moe_w1, moe_w3, moe_w2, l,
                final_g if l == DEPTH - 1 else None)

    y_prompt, y_sample = x
    return (y_prompt.reshape(BATCH, SEQ, D_MODEL), y_sample.reshape(DEC_BATCH, DEC_SEQ, D_MODEL),
            new_k, new_v, states)
```

```python
import functools
import math

import jax
import jax.numpy as jnp
import numpy as np
from jax import lax
from jax.experimental import pallas as pl
from jax.experimental.pallas import tpu as pltpu

F32 = jnp.float32
BF16 = jnp.bfloat16

D_MODEL = 1024
BATCH = 16
SEQ = 256
DEPTH = 4
DEC_BATCH = 2
DEC_SEQ = 1024
PAST_LEN = 256
GRID_W = 64
EPS = 1e-6
ROPE_BASE = 10000.0
H_A = 4
DH_A = 64
DV_A = 128
W_A = 512
C_B = 512
SHORT_K = 3
H_EMB = 33
H_FFN = 64
H_FAST_DECAY = 0.3
H_SLOW_DECAY = 1.5
H_TARGET = 1e-2
H_C = 4
DK_C = 64
DV_C = 128
W_C = 512
N_EXPERTS = 16
N_GROUPS = 4
EXP_PER_GROUP = 4
D_FF = 512
D_IN = 4608

T_CTX = BATCH * SEQ
T_LAT = DEC_BATCH * DEC_SEQ
T_ALL = T_CTX + T_LAT
TM = 1024
N_TILES = T_ALL // TM
CTX_TILES = T_CTX // TM
LANES = 128
SUBLANES = 8
MOE_BLOCK = 256
MOE_BLOCKS = T_ALL // MOE_BLOCK + N_GROUPS
T_PAD = MOE_BLOCKS * MOE_BLOCK
VMEM_LIMIT = 56 * 1024 * 1024


def _params(n_axes):
    return pltpu.CompilerParams(
        dimension_semantics=("arbitrary",) * n_axes, vmem_limit_bytes=VMEM_LIMIT)


def _mod_row(i):
    return jnp.maximum(i - (CTX_TILES - 1), 0)


def _rms(x):
    return x * lax.rsqrt(jnp.mean(x * x, axis=-1, keepdims=True) + EPS)


def _sigmoid(x):
    return 0.5 * jnp.tanh(0.5 * x) + 0.5


def _silu(x):
    return x * _sigmoid(x)


def _dot(a, b):
    return jnp.dot(a, b, preferred_element_type=F32)


def _dot_nt(a, b):
    return lax.dot_general(a, b, (((1,), (1,)), ((), ())), preferred_element_type=F32)


def _dot_tn(a, b):
    return lax.dot_general(a, b, (((0,), (0,)), ((), ())), preferred_element_type=F32)


def _split3(x):
    hi = x.astype(BF16)
    lo = (x - hi.astype(F32)).astype(BF16)
    return hi, lo


def _dot3(a, b):
    ah, al = _split3(a)
    bh, bl = _split3(b)
    return _dot(ah, bh) + (_dot(ah, bl) + _dot(al, bh))


def _dot3_nt(a, b):
    ah, al = _split3(a)
    bh, bl = _split3(b)
    return _dot_nt(ah, bh) + (_dot_nt(ah, bl) + _dot_nt(al, bh))


def _ada_kernel(c_ref, w_ref, b_ref, o_ref):
    s = _silu(c_ref[...])
    o_ref[0] = _dot(s.astype(BF16), w_ref[0].astype(BF16)) + b_ref[0]


def ada_modulation(cond8, w_ada, b_ada):
    tn = 3072
    n = 6 * D_MODEL
    return pl.pallas_call(
        _ada_kernel,
        grid=(DEPTH, n // tn),
        in_specs=[
            pl.BlockSpec((8, D_MODEL), lambda l, j: (0, 0)),
            pl.BlockSpec((1, D_MODEL, tn), lambda l, j: (l, 0, j)),
            pl.BlockSpec((1, 1, tn), lambda l, j: (l, 0, j)),
        ],
        out_specs=pl.BlockSpec((1, 8, tn), lambda l, j: (l, 0, j)),
        out_shape=jax.ShapeDtypeStruct((DEPTH, 8, n), F32),
        compiler_params=_params(2),
        name="ada_modulation",
    )(cond8, w_ada, b_ada.reshape(DEPTH, 1, n))


PROJ_TM = 3072
PROJ_TN = 512
GATE_BLOCKS = 3 * D_MODEL // PROJ_TN
IN_BLOCKS = D_IN // PROJ_TN
P_WIDTH = 3 * D_MODEL + D_IN
KV_FIRST = GATE_BLOCKS + 1
P_Q = 3 * D_MODEL // W_A
P_HYENA = (3 * D_MODEL + 3 * W_A) // (3 * C_B)
P_RET_Q = (3 * D_MODEL + 3 * W_A + 3 * C_B) // (H_C * DK_C)
P_RET_V = (3 * D_MODEL + 3 * W_A + 3 * C_B + 2 * H_C * DK_C) // W_C


def _in_proj_kernel(x_hbm, g_ref, m_ref, win_ref, wg_ref, bg_ref, p_ref, kv_ref, h_ref, x_ref, sem, *, layer):
    i = pl.program_id(0)
    j = pl.program_id(1)

    def x_copy(tile):
        start = pl.multiple_of(tile * PROJ_TM, PROJ_TM)
        return pltpu.make_async_copy(x_hbm.at[pl.ds(start, PROJ_TM), :], x_ref, sem.at[0])

    @pl.when((i == 0) & (j == 0))
    def _():
        x_copy(0).start()

    @pl.when(j == 0)
    def _():
        x_copy(i).wait()
        for s in range(PROJ_TM // TM):
            rows = slice(s * TM, (s + 1) * TM)
            mod = m_ref[layer, _mod_row(i * (PROJ_TM // TM) + s)]
            y = _rms(x_ref[rows, :]) * g_ref[layer:layer + 1, :]
            h_ref[rows, :] = (y * (1.0 + mod[1:2, :]) + mod[0:1, :]).astype(BF16)

        @pl.when(i + 1 < T_ALL // PROJ_TM)
        def _():
            x_copy(i + 1).start()

    @pl.when(j < GATE_BLOCKS)
    def _():
        acc = _dot(h_ref[...], wg_ref[0].astype(BF16)) + bg_ref[0]
        p_ref[...] = _sigmoid(acc).astype(BF16)

    @pl.when(j >= GATE_BLOCKS)
    def _():
        acc = _dot(h_ref[...], win_ref[0].astype(BF16))
        p_ref[...] = acc.astype(BF16)

        @pl.when((j == KV_FIRST) | (j == KV_FIRST + 1))
        def _():
            kv_ref[...] = acc


def in_projection(x, gain, mods, w_in, w_gate, b_gate, layer):
    return pl.pallas_call(
        functools.partial(_in_proj_kernel, layer=layer),
        grid=(T_ALL // PROJ_TM, GATE_BLOCKS + IN_BLOCKS),
        in_specs=[
            pl.BlockSpec(memory_space=pl.ANY),
            pl.BlockSpec((DEPTH, D_MODEL), lambda i, j: (0, 0)),
            pl.BlockSpec((DEPTH, 8, 6, D_MODEL), lambda i, j: (0, 0, 0, 0)),
            pl.BlockSpec((1, D_MODEL, PROJ_TN), lambda i, j: (layer, 0, jnp.maximum(j - GATE_BLOCKS, 0))),
            pl.BlockSpec((1, D_MODEL, PROJ_TN), lambda i, j: (layer, 0, jnp.minimum(j, GATE_BLOCKS - 1))),
            pl.BlockSpec((1, 1, PROJ_TN), lambda i, j: (layer, 0, jnp.minimum(j, GATE_BLOCKS - 1))),
        ],
        out_specs=[
            pl.BlockSpec((PROJ_TM, PROJ_TN), lambda i, j: (i, j)),
            pl.BlockSpec((PROJ_TM, PROJ_TN), lambda i, j: (i, jnp.clip(j - KV_FIRST, 0, 1))),
        ],
        out_shape=[jax.ShapeDtypeStruct((T_ALL, P_WIDTH), BF16),
                   jax.ShapeDtypeStruct((T_ALL, 2 * W_A), F32)],
        scratch_shapes=[pltpu.VMEM((PROJ_TM, D_MODEL), BF16), pltpu.VMEM((PROJ_TM, D_MODEL), F32),
                        pltpu.SemaphoreType.DMA((1,))],
        compiler_params=_params(2),
        name="in_projection",
    )(x, gain, mods, w_in, w_gate, b_gate.reshape(DEPTH, 1, 3 * D_MODEL))


def _rope_tables():
    n = DEC_SEQ
    rows = n // GRID_W
    row = jnp.repeat(jnp.arange(rows, dtype=F32), GRID_W)
    col = jnp.tile(jnp.arange(GRID_W, dtype=F32), rows)
    quarter = DH_A // 4
    inv = ROPE_BASE ** (-jnp.arange(quarter, dtype=F32) / quarter)
    ar = row[:, None] * inv
    ac = col[:, None] * inv
    ang = jnp.concatenate([ar, ar, ac, ac], axis=-1)
    cos = jnp.tile(jnp.cos(ang), (1, 512 // DH_A))
    sgn = jnp.tile(jnp.concatenate([-jnp.ones((quarter,), F32), jnp.ones((quarter,), F32)]), 512 // (2 * quarter))
    sin = jnp.tile(jnp.sin(ang), (1, 512 // DH_A)) * sgn
    return cos, sin


def _rope(x, cos, sin):
    w = x.shape[-1]
    lane = lax.broadcasted_iota(jnp.int32, x.shape, 1)
    first = (lane % 32) < 16
    rot = jnp.where(first, pltpu.roll(x, w - 16, 1), pltpu.roll(x, 16, 1))
    return x * cos + rot * sin


def _lambda(al_ref, layer):
    al = al_ref[layer]
    a = jnp.sum(al[0:1, :] * al[1:2, :], axis=-1, keepdims=True)
    b = jnp.sum(al[2:3, :] * al[3:4, :], axis=-1, keepdims=True)
    lam_init = 0.8 - 0.6 * math.exp(-0.3 * layer)
    return jnp.exp(a) - jnp.exp(b) + lam_init, lam_init


def _scores(q, k):
    return _dot_nt(q, k).astype(BF16)


def _softmax_numerators(s):
    return jnp.exp2(s - jnp.max(s, axis=-1, keepdims=True))


def _diff_attention(q, k, v, lam, lam_init, subln, o_ref):
    lane = lax.broadcasted_iota(jnp.int32, q.shape, 1)
    q = q * (DH_A ** -0.5 * math.log2(math.e))
    q0 = jnp.where((lane % LANES) < DH_A, q, 0.0).astype(BF16)
    q1 = jnp.where((lane % LANES) >= DH_A, q, 0.0).astype(BF16)
    ones = jnp.ones((k.shape[0], LANES), BF16)
    for h in range(H_A):
        sl = slice(h * LANES, (h + 1) * LANES)
        kh = k[:, sl]
        v_ones = jnp.concatenate([v[:, sl], ones], axis=1)
        o0 = _dot(_softmax_numerators(_scores(q0[:, sl], kh)), v_ones)
        o1 = _dot(_softmax_numerators(_scores(q1[:, sl], kh)), v_ones)
        r0 = 1.0 / o0[:, DV_A:DV_A + 1]
        r1 = lam / o1[:, DV_A:DV_A + 1]
        o = o0[:, :DV_A] * r0 - o1[:, :DV_A] * r1
        o = _rms(o) * subln * (1.0 - lam_init)
        o_ref[:, sl] = o.astype(o_ref.dtype)


def _attn_ctx_kernel(q_ref, k_ref, v_ref, al_ref, g_ref, o_ref, knew_ref, vnew_ref, *, layer):
    lam, lam_init = _lambda(al_ref, layer)
    subln = g_ref[layer:layer + 1, :]
    k = k_ref[...]
    v = v_ref[...]
    for h in range(H_A):
        knew_ref[0, 0, :, h, :] = k[:, h * LANES:(h + 1) * LANES]
        vnew_ref[0, 0, :, h, :] = v[:, h * LANES:(h + 1) * LANES]
    _diff_attention(q_ref[...].astype(F32), k.astype(BF16), v.astype(BF16), lam, lam_init, subln, o_ref)


QB = 256


def _attn_lat_kernel(ya_ref, q_ref, k_ref, v_ref, ck_ref, cv_ref, cosq_ref, sinq_ref, cos_ref, sin_ref,
                     al_ref, g_ref, o_ref, kall_ref, vall_ref, *, layer):
    del ya_ref

    @pl.when(pl.program_id(1) == 0)
    def _():
        kall_ref[0:PAST_LEN, :] = ck_ref[0, 0].astype(BF16)
        vall_ref[0:PAST_LEN, :] = cv_ref[0, 0].astype(BF16)
        kall_ref[PAST_LEN:, :] = _rope(k_ref[...], cos_ref[...], sin_ref[...]).astype(BF16)
        vall_ref[PAST_LEN:, :] = v_ref[...].astype(BF16)

    lam, lam_init = _lambda(al_ref, layer)
    subln = g_ref[layer:layer + 1, :]
    q = _rope(q_ref[...].astype(F32), cosq_ref[...], sinq_ref[...])
    _diff_attention(q, kall_ref[...], vall_ref[...], lam, lam_init, subln, o_ref)


def attention_lat(ya, q, kv, cache_k, cache_v, cos, sin, attn_lambda, subln_g, layer):
    nqb = DEC_SEQ // QB
    row0 = T_CTX // QB
    seq0 = T_CTX // DEC_SEQ
    return pl.pallas_call(
        functools.partial(_attn_lat_kernel, layer=layer),
        grid=(DEC_BATCH, nqb),
        in_specs=[
            pl.BlockSpec(memory_space=pl.ANY),
            pl.BlockSpec((QB, W_A), lambda b, i: (row0 + b * nqb + i, P_Q)),
            pl.BlockSpec((DEC_SEQ, W_A), lambda b, i: (seq0 + b, 0)),
            pl.BlockSpec((DEC_SEQ, W_A), lambda b, i: (seq0 + b, 1)),
            pl.BlockSpec((1, 1, PAST_LEN, W_A), lambda b, i: (b, layer, 0, 0)),
            pl.BlockSpec((1, 1, PAST_LEN, W_A), lambda b, i: (b, layer, 0, 0)),
            pl.BlockSpec((QB, W_A), lambda b, i: (i, 0)),
            pl.BlockSpec((QB, W_A), lambda b, i: (i, 0)),
            pl.BlockSpec((DEC_SEQ, W_A), lambda b, i: (0, 0)),
            pl.BlockSpec((DEC_SEQ, W_A), lambda b, i: (0, 0)),
            pl.BlockSpec((DEPTH, 4, DH_A), lambda b, i: (0, 0, 0)),
            pl.BlockSpec((DEPTH, DV_A), lambda b, i: (0, 0)),
        ],
        out_specs=pl.BlockSpec((QB, W_A), lambda b, i: (row0 + b * nqb + i, 0)),
        out_shape=jax.ShapeDtypeStruct((T_ALL, W_A), BF16),
        scratch_shapes=[pltpu.VMEM((PAST_LEN + DEC_SEQ, W_A), BF16),
                        pltpu.VMEM((PAST_LEN + DEC_SEQ, W_A), BF16)],
        input_output_aliases={0: 0},
        compiler_params=_params(2),
        name="attention_lat",
    )(ya, q, kv, kv, cache_k, cache_v, cos, sin, cos, sin, attn_lambda, subln_g)


def _dft_tables(n):
    k = np.arange(n, dtype=np.int64)
    prod = (2 * k[:, None] + 1) * k[None, :]
    ang = (prod % (4 * n)).astype(np.float64) * (math.pi / (2 * n))
    c = np.cos(ang).astype(np.float32)
    s = np.sin(ang).astype(np.float32)
    return tuple(jnp.asarray(t).astype(BF16) for t in (c, s, c.T, s.T))


def _filter_features(n):
    t = jnp.linspace(0.0, 1.0, n, dtype=F32)[:, None]
    bands = (H_EMB - 1) // 2
    w = 2.0 * math.pi * jnp.arange(n, dtype=F32)[:, None] / n
    fr = jnp.linspace(1e-4, bands - 1, bands, dtype=F32)
    z = jnp.concatenate([t, jnp.cos(w * fr), -jnp.sin(w * fr)], axis=-1)
    return jnp.pad(z, ((0, 0), (0, LANES - H_EMB)))


def _filter_decay_rates():
    max_decay = math.log(H_TARGET) / H_FAST_DECAY
    min_decay = math.log(H_TARGET) / H_SLOW_DECAY
    return jnp.abs(jnp.linspace(min_decay, max_decay, C_B, dtype=F32))[None, :]


def _hyena_filter_kernel(z_ref, w1_ref, b1_ref, w2_ref, b2_ref, w3_ref, dr_ref, c_ref, s_ref,
                         gre_ref, gim_ref):
    z = z_ref[...]
    hid = jnp.sin(_dot3(z, w1_ref[0]) + b1_ref[0])
    hid = jnp.sin(_dot3(hid, w2_ref[0]) + b2_ref[0])
    h = _dot(hid.astype(BF16), w3_ref[0].astype(BF16))
    window = jnp.exp(-z[:, 0:1] * dr_ref[...])
    hf = h[:, :C_B] * window
    hb = h[:, C_B:] * window
    total = jnp.sum(jnp.abs(hf) + jnp.abs(hb), axis=0, keepdims=True)
    hf = hf / total
    hb = hb / total
    row = lax.broadcasted_iota(jnp.int32, hb.shape, 0)
    hb = jnp.where(row == 0, 0.0, hb)
    gre_ref[0] = _dot(c_ref[...], (hf + hb).astype(BF16))
    gim_ref[0] = _dot(s_ref[...], (hb - hf).astype(BF16))


def hyena_filters(n, z, decay_rates, cf, sf, f_w1, f_b1, f_w2, f_b2, f_w3):
    w1 = jnp.pad(f_w1, ((0, 0), (0, LANES - H_EMB), (0, 0)))
    full = lambda shape: pl.BlockSpec(shape, lambda l: (0,) * len(shape))
    per_layer = lambda shape: pl.BlockSpec((1,) + shape, lambda l: (l,) + (0,) * len(shape))
    return pl.pallas_call(
        _hyena_filter_kernel,
        grid=(DEPTH,),
        in_specs=[
            full((n, LANES)),
            per_layer((LANES, H_FFN)), per_layer((1, H_FFN)),
            per_layer((H_FFN, H_FFN)), per_layer((1, H_FFN)),
            per_layer((H_FFN, 2 * C_B)),
            full((1, C_B)), full((n, n)), full((n, n)),
        ],
        out_specs=[per_layer((n, C_B)), per_layer((n, C_B))],
        out_shape=[jax.ShapeDtypeStruct((DEPTH, n, C_B), F32)] * 2,
        compiler_params=_params(1),
        name="hyena_filters",
    )(z, w1, f_b1.reshape(DEPTH, 1, H_FFN), f_w2, f_b2.reshape(DEPTH, 1, H_FFN), f_w3,
      decay_rates, cf, sf)


def _hyena_kernel(*refs, layer, n, aliased):
    if aliased:
        refs = refs[1:]
    u_ref, cw_ref, cb_ref, skip_ref, gre_ref, gim_ref, c_ref, s_ref, ct_ref, st_ref, o_ref = refs
    u = u_ref[...].astype(F32)
    row = lax.broadcasted_iota(jnp.int32, u.shape, 0)
    prev = jnp.where(row == 0, 0.0, pltpu.roll(u, 1, 0))
    nxt = jnp.where(row == n - 1, 0.0, pltpu.roll(u, n - 1, 0))
    cw = cw_ref[layer]
    uc = cb_ref[layer:layer + 1, :] + prev * cw[0:1, :] + u * cw[1:2, :] + nxt * cw[2:3, :]
    x0 = uc[:, :C_B]
    x1 = uc[:, C_B:2 * C_B]
    v = uc[:, 2 * C_B:]
    w = v * x1
    wb = w.astype(BF16)
    ure = _dot(c_ref[...], wb)
    uim = -_dot(s_ref[...], wb)
    gre = gre_ref[0]
    gim = gim_ref[0]
    yre = (ure * gre - uim * gim).astype(BF16)
    yim = (ure * gim + uim * gre).astype(BF16)
    y = (_dot(ct_ref[...], yre) - _dot(st_ref[...], yim)) * (1.0 / n)
    y = y + w * skip_ref[layer:layer + 1, :]
    o_ref[...] = (x0 * y).astype(o_ref.dtype)


def _full(shape):
    return pl.BlockSpec(shape, lambda *_: (0,) * len(shape))


def _hyena_specs(n, seq0, layer):
    return [
        pl.BlockSpec((n, 3 * C_B), lambda b, *_: (seq0 + b, P_HYENA)),
        _full((DEPTH, SHORT_K, 3 * C_B)), _full((DEPTH, 3 * C_B)), _full((DEPTH, C_B)),
        pl.BlockSpec((1, n, C_B), lambda *_: (layer, 0, 0)),
        pl.BlockSpec((1, n, C_B), lambda *_: (layer, 0, 0)),
        _full((n, n)), _full((n, n)), _full((n, n)), _full((n, n)),
    ]


def hyena_lat(yb, p, conv_w, conv_b, skip, gre, gim, tables, layer):
    seq0 = T_CTX // DEC_SEQ
    return pl.pallas_call(
        functools.partial(_hyena_kernel, layer=layer, n=DEC_SEQ, aliased=True),
        grid=(DEC_BATCH,),
        in_specs=[pl.BlockSpec(memory_space=pl.ANY)] + _hyena_specs(DEC_SEQ, seq0, layer),
        out_specs=pl.BlockSpec((DEC_SEQ, C_B), lambda b: (seq0 + b, 0)),
        out_shape=jax.ShapeDtypeStruct((T_ALL, C_B), BF16),
        input_output_aliases={0: 0},
        compiler_params=_params(1),
        name="hyena_lat",
    )(yb, p, conv_w, conv_b, skip, gre, gim, *tables)


RET_QB = 256


def _retention_kernel(*refs, layer, n, latent):
    if latent:
        _, q_ref, k_ref, v_ref, g_ref, de_ref, cos_ref, sin_ref, s0_ref, o_ref, w_ref, vec_ref = refs
        st_ref = None
    else:
        q_ref, k_ref, v_ref, g_ref, de_ref, o_ref, st_ref, w_ref, vec_ref = refs[-9:]
    log_gamma = jnp.log1p(-jnp.exp2(-de_ref[layer]))

    @pl.when(pl.program_id(0) == 0)
    def _():
        t = lax.broadcasted_iota(jnp.int32, (n, n), 0)
        s = lax.broadcasted_iota(jnp.int32, (n, n), 1)
        lag = (t - s).astype(F32)
        for h in range(H_C):
            rate = jnp.where(lag > 0, log_gamma[0:1, h:h + 1], log_gamma[1:2, h:h + 1])
            w_ref[h] = jnp.where(lag == 0, 2.0, jnp.exp(rate * jnp.abs(lag)))
            pos = lax.broadcasted_iota(jnp.int32, (n, LANES), 0).astype(F32)
            steps = (pos + 1.0, float(n) - pos) if latent else (float(n - 1) - pos, pos)
            for d in range(2):
                vec_ref[h, d] = jnp.exp(log_gamma[d:d + 1, h:h + 1] * steps[d])

    q = q_ref[...].astype(F32)
    k = k_ref[...].astype(F32) * (DK_C ** -0.5)
    if latent:
        q = _rope(q, cos_ref[...], sin_ref[...])
        k = _rope(k, cos_ref[...], sin_ref[...])
    v = v_ref[...].astype(BF16)
    gate = g_ref[...].astype(F32)
    lane = lax.broadcasted_iota(jnp.int32, (n, LANES), 1)
    zeros64 = jnp.zeros((DK_C, DV_C), F32)

    for h in range(H_C):
        pair = slice((h // 2) * LANES, (h // 2 + 1) * LANES)
        lo = (h % 2) * DK_C
        own = (lane >= lo) & (lane < lo + DK_C)
        vs = slice(h * DV_C, (h + 1) * DV_C)
        qh = jnp.where(own, q[:, pair], 0.0).astype(BF16)
        kh = jnp.where(own, k[:, pair], 0.0)
        kb = kh.astype(BF16)
        vh = v[:, vs]
        if latent:
            s0 = [jnp.concatenate([s0_ref[0, 0, d, h], zeros64] if lo == 0 else [zeros64, s0_ref[0, 0, d, h]],
                                  axis=0).astype(BF16) for d in range(2)]
            carry = _dot(qh, s0[0]) * vec_ref[h, 0] + _dot(qh, s0[1]) * vec_ref[h, 1]
        for r0 in range(0, n, RET_QB):
            rows = slice(r0, r0 + RET_QB)
            att = _dot_nt(qh[rows], kb) * w_ref[h, rows, :]
            out = _dot(att.astype(BF16), vh)
            if latent:
                out = out + carry[rows]
            o_ref[rows, vs] = (_rms(out) * _silu(gate[rows, vs])).astype(o_ref.dtype)
        if st_ref is not None:
            sf = _dot_tn((kh * vec_ref[h, 0]).astype(BF16), vh)
            sb = _dot_tn((kh * vec_ref[h, 1]).astype(BF16), vh)
            st_ref[0, 0, 0, h] = sf[lo:lo + DK_C, :]
            st_ref[0, 0, 1, h] = sb[lo:lo + DK_C, :]


def _retention_specs(n, seq0):
    return [
        pl.BlockSpec((n, H_C * DK_C), lambda b, *_: (seq0 + b, P_RET_Q)),
        pl.BlockSpec((n, H_C * DK_C), lambda b, *_: (seq0 + b, P_RET_Q + 1)),
        pl.BlockSpec((n, W_C), lambda b, *_: (seq0 + b, P_RET_V)),
        pl.BlockSpec((n, W_C), lambda b, *_: (seq0 + b, P_RET_V + 1)),
        _full((DEPTH, 2, H_C)),
    ]


def retention_lat(yc, p, decay_exp, cos, sin, state0, layer):
    n = DEC_SEQ
    seq0 = T_CTX // DEC_SEQ
    return pl.pallas_call(
        functools.partial(_retention_kernel, layer=layer, n=n, latent=True),
        grid=(DEC_BATCH,),
        in_specs=[pl.BlockSpec(memory_space=pl.ANY)] + _retention_specs(n, seq0) + [
            _full((n, H_C * DK_C)), _full((n, H_C * DK_C)),
            pl.BlockSpec((1, 1, 2, H_C, DK_C, DV_C), lambda b: (b, layer, 0, 0, 0, 0)),
        ],
        out_specs=pl.BlockSpec((n, W_C), lambda b: (seq0 + b, 0)),
        out_shape=jax.ShapeDtypeStruct((T_ALL, W_C), BF16),
        scratch_shapes=[pltpu.VMEM((H_C, n, n), F32), pltpu.VMEM((H_C, 2, n, LANES), F32)],
        input_output_aliases={0: 0},
        compiler_params=_params(1),
        name="retention_lat",
    )(yc, p, p, p, p, decay_exp, cos, sin, state0)


N_ATTN_IN = 5
N_HYENA_IN = 10
N_RET_IN = 5


def _mixers_ctx_kernel(*refs, layer, n_threaded):
    refs = refs[n_threaded:]
    attn_in = refs[:N_ATTN_IN]
    hyena_in = refs[N_ATTN_IN:N_ATTN_IN + N_HYENA_IN]
    ret_in = refs[N_ATTN_IN + N_HYENA_IN:N_ATTN_IN + N_HYENA_IN + N_RET_IN]
    ya_ref, knew_ref, vnew_ref, yb_ref, yc_ref, st_ref, w_ref, vec_ref = refs[N_ATTN_IN + N_HYENA_IN + N_RET_IN:]
    _attn_ctx_kernel(*attn_in, ya_ref, knew_ref, vnew_ref, layer=layer)
    _hyena_kernel(*hyena_in, yb_ref, layer=layer, n=SEQ, aliased=False)
    _retention_kernel(*ret_in, yc_ref, st_ref, w_ref, vec_ref, layer=layer, n=SEQ, latent=False)


def mixers_ctx(p, kv, attn_lambda, subln_g, conv_w, conv_b, skip, gre, gim, tables, decay_exp, layer, threaded):
    cache_shape = jax.ShapeDtypeStruct((BATCH, DEPTH, SEQ, H_A, DV_A), F32)
    cache_spec = pl.BlockSpec((1, 1, SEQ, H_A, DV_A), lambda b: (b, layer, 0, 0, 0))
    branch_shape = jax.ShapeDtypeStruct((T_ALL, W_A), BF16)
    branch_spec = pl.BlockSpec((SEQ, W_A), lambda b: (b, 0))
    n_threaded = len(threaded)
    attn_specs = [
        pl.BlockSpec((SEQ, W_A), lambda b: (b, P_Q)),
        pl.BlockSpec((SEQ, W_A), lambda b: (b, 0)),
        pl.BlockSpec((SEQ, W_A), lambda b: (b, 1)),
        _full((DEPTH, 4, DH_A)), _full((DEPTH, DV_A)),
    ]
    return pl.pallas_call(
        functools.partial(_mixers_ctx_kernel, layer=layer, n_threaded=n_threaded),
        grid=(BATCH,),
        in_specs=([pl.BlockSpec(memory_space=pl.ANY)] * n_threaded + attn_specs
                  + _hyena_specs(SEQ, 0, layer) + _retention_specs(SEQ, 0)),
        out_specs=[branch_spec, cache_spec, cache_spec, branch_spec, branch_spec,
                   pl.BlockSpec((1, 1, 2, H_C, DK_C, DV_C), lambda b: (b, layer, 0, 0, 0, 0))],
        out_shape=[branch_shape, cache_shape, cache_shape, branch_shape, branch_shape,
                   jax.ShapeDtypeStruct((BATCH, DEPTH, 2, H_C, DK_C, DV_C), F32)],
        scratch_shapes=[pltpu.VMEM((H_C, SEQ, SEQ), F32), pltpu.VMEM((H_C, 2, SEQ, LANES), F32)],
        input_output_aliases=dict(zip(range(n_threaded), (1, 2, 5))),
        compiler_params=_params(1),
        name="mixers_ctx",
    )(*threaded, p, kv, kv, attn_lambda, subln_g, p, conv_w, conv_b, skip, gre, gim, *tables,
      p, p, p, p, decay_exp)


def _merge_kernel(x_ref, ya_ref, yb_ref, yc_ref, g_ref, m_ref, wa_ref, wb_ref, wc_ref, wo_ref,
                  o_ref, wbf_ref, wobf_ref):
    @pl.when(pl.program_id(0) == 0)
    def _():
        wbf_ref[0] = wa_ref[0].astype(BF16)
        wbf_ref[1] = wb_ref[0].astype(BF16)
        wbf_ref[2] = wc_ref[0].astype(BF16)
        wobf_ref[...] = wo_ref[0].astype(BF16)

    g = g_ref[...]
    merged = (g[:, :D_MODEL] * _dot(ya_ref[...], wbf_ref[0])
              + g[:, D_MODEL:2 * D_MODEL] * _dot(yb_ref[...], wbf_ref[1])
              + g[:, 2 * D_MODEL:] * _dot(yc_ref[...], wbf_ref[2]))
    g1 = m_ref[0, 0, 2:3, :]
    o_ref[...] = x_ref[...] + g1 * _dot(merged.astype(BF16), wobf_ref[...])


def merge_branches(x, ya, yb, yc, gates, mods, w_a, w_b, w_c, w_out, layer):
    tile = lambda w: pl.BlockSpec((TM, w), lambda i: (i, 0))
    wspec = lambda k: pl.BlockSpec((1, k, D_MODEL), lambda i: (layer, 0, 0))
    return pl.pallas_call(
        _merge_kernel,
        grid=(N_TILES,),
        in_specs=[
            tile(D_MODEL), tile(W_A), tile(C_B), tile(W_C), tile(3 * D_MODEL),
            pl.BlockSpec((1, 1, 6, D_MODEL), lambda i: (layer, _mod_row(i), 0, 0)),
            wspec(W_A), wspec(C_B), wspec(W_C), wspec(D_MODEL),
        ],
        out_specs=tile(D_MODEL),
        out_shape=jax.ShapeDtypeStruct((T_ALL, D_MODEL), F32),
        scratch_shapes=[pltpu.VMEM((3, W_A, D_MODEL), BF16), pltpu.VMEM((D_MODEL, D_MODEL), BF16)],
        compiler_params=_params(1),
        name="merge_branches",
    )(x, ya, yb, yc, gates, mods, w_a, w_b, w_c, w_out)


def _route(h2, wr_t, b_r):
    logits = _dot3_nt(wr_t, h2)
    m = jnp.max(logits, axis=0, keepdims=True)
    e = jnp.exp(logits - m)
    scores = e / jnp.sum(e, axis=0, keepdims=True)
    sel = scores + b_r
    rows = [sel[i:i + 1, :] for i in range(N_EXPERTS)]
    in_group = []
    gscore = []
    for g in range(N_GROUPS):
        members = range(g * EXP_PER_GROUP, (g + 1) * EXP_PER_GROUP)
        total = None
        for i in members:
            rank = None
            for j in members:
                if j == i:
                    continue
                ahead = (rows[j] >= rows[i]) if j < i else (rows[j] > rows[i])
                ahead = ahead.astype(F32)
                rank = ahead if rank is None else rank + ahead
            chosen = rank < 2.0
            in_group.append(chosen)
            part = jnp.where(chosen, rows[i], 0.0)
            total = part if total is None else total + part
        gscore.append(total)
    gates = []
    group_hot = []
    for g in range(N_GROUPS):
        best = None
        for g2 in range(N_GROUPS):
            if g2 == g:
                continue
            wins = gscore[g] > gscore[g2] if g2 < g else gscore[g] >= gscore[g2]
            best = wins if best is None else best & wins
        group_hot.append(best.astype(F32))
        for i in range(g * EXP_PER_GROUP, (g + 1) * EXP_PER_GROUP):
            gates.append(jnp.where(best & in_group[i], scores[i:i + 1, :], 0.0))
    gates = jnp.concatenate(gates, axis=0)
    return gates / jnp.sum(gates, axis=0, keepdims=True), group_hot


def _moe_route_kernel(x_ref, g_ref, m_ref, wr_ref, br_ref, tri_ref, tiles_ref, gate_ref, meta_ref, count_ref,
                      *, layer):
    y = _rms(x_ref[...]) * g_ref[layer:layer + 1, :]
    h2 = y * (1.0 + m_ref[0, 0, 4:5, :]) + m_ref[0, 0, 3:4, :]
    gates, group_hot = _route(h2, wr_ref[...], br_ref[...])
    for s in range(SUBLANES):
        tiles_ref[pl.ds(s, TM, stride=SUBLANES), :] = h2[:, s * LANES:(s + 1) * LANES]
    padded = jnp.concatenate([gates, jnp.zeros((LANES - N_EXPERTS, TM), F32)], axis=0)
    gate_ref[...] = padded.T
    row = lax.broadcasted_iota(jnp.int32, (SUBLANES, TM), 0)
    hot = jnp.zeros((SUBLANES, TM), F32)
    for g in range(N_GROUPS):
        hot = jnp.where(row == g, group_hot[g], hot)
    before = _dot(hot.astype(BF16), tri_ref[...])
    rank = jnp.sum(hot * before, axis=0, keepdims=True)
    gid = group_hot[1] + 2.0 * group_hot[2] + 3.0 * group_hot[3]
    meta = jnp.where(row == 0, gid, jnp.where(row == 1, rank, 0.0))
    meta_ref[0] = meta.astype(jnp.int32)
    counts = jnp.sum(hot, axis=1, keepdims=True) + jnp.zeros((SUBLANES, LANES), F32)
    count_ref[0] = counts.astype(jnp.int32)


def moe_route(x, gain, mods, wr_t, b_r, tri, layer):
    return pl.pallas_call(
        functools.partial(_moe_route_kernel, layer=layer),
        grid=(N_TILES,),
        in_specs=[
            pl.BlockSpec((TM, D_MODEL), lambda i: (i, 0)),
            pl.BlockSpec((DEPTH, D_MODEL), lambda i: (0, 0)),
            pl.BlockSpec((1, 1, 6, D_MODEL), lambda i: (layer, _mod_row(i), 0, 0)),
            pl.BlockSpec((N_EXPERTS, D_MODEL), lambda i: (0, 0)),
            pl.BlockSpec((N_EXPERTS, 1), lambda i: (0, 0)),
            pl.BlockSpec((TM, TM), lambda i: (0, 0)),
        ],
        out_specs=[pl.BlockSpec((TM * SUBLANES, LANES), lambda i: (i, 0)),
                   pl.BlockSpec((TM, LANES), lambda i: (i, 0)),
                   pl.BlockSpec((1, SUBLANES, TM), lambda i: (i, 0, 0)),
                   pl.BlockSpec((1, SUBLANES, LANES), lambda i: (i, 0, 0))],
        out_shape=[jax.ShapeDtypeStruct((T_ALL * SUBLANES, LANES), F32),
                   jax.ShapeDtypeStruct((T_ALL, LANES), F32),
                   jax.ShapeDtypeStruct((N_TILES, SUBLANES, TM), jnp.int32),
                   jax.ShapeDtypeStruct((N_TILES, SUBLANES, LANES), jnp.int32)],
        compiler_params=_params(1),
        name="moe_route",
    )(x, gain, mods, wr_t, b_r, tri)


def _moe_positions(meta, counts):
    gid = meta[:, 0, :]
    rank = meta[:, 1, :]
    cnt = counts[:, :N_GROUPS, 0]
    total = jnp.sum(cnt, axis=0)
    padded = (total + (MOE_BLOCK - 1)) // MOE_BLOCK * MOE_BLOCK
    group_start = jnp.cumsum(padded) - padded
    base = group_start[None, :] + jnp.cumsum(cnt, axis=0) - cnt
    pos = rank
    for g in range(N_GROUPS):
        pos = pos + jnp.where(gid == g, base[:, g:g + 1], 0)
    first_block = group_start // MOE_BLOCK
    blk = jnp.arange(MOE_BLOCKS, dtype=jnp.int32)
    block_gid = sum((blk >= first_block[g]).astype(jnp.int32) for g in range(1, N_GROUPS))
    block_used = (blk * MOE_BLOCK < jnp.sum(padded)).astype(jnp.int32)
    later = jnp.where(block_gid[None, :] > block_gid[:, None], block_gid[None, :], N_GROUPS)
    next_group = jnp.min(later, axis=1)
    next_group = jnp.where(next_group == N_GROUPS, -1, next_group).astype(jnp.int32)
    return pos.reshape(T_ALL).astype(jnp.int32), block_gid, block_used, next_group


def _moe_permute_kernel(pos_ref, tiles_ref, gate_ref, sorted_ref, gsorted_ref):
    i = pl.program_id(0)

    @pl.when(i == 0)
    def _():
        def zero(b, carry):
            start = pl.multiple_of(b * TM, TM)
            sorted_ref[pl.ds(start, TM), :] = jnp.zeros((TM, LANES), F32)
            return carry
        lax.fori_loop(0, T_PAD * SUBLANES // TM, zero, 0)
        gsorted_ref[...] = jnp.zeros((T_PAD, LANES), F32)

    def move(t, carry):
        p = pos_ref[i * TM + t]
        dst = pl.multiple_of(p * SUBLANES, SUBLANES)
        src = pl.multiple_of(t * SUBLANES, SUBLANES)
        sorted_ref[pl.ds(dst, SUBLANES), :] = tiles_ref[pl.ds(src, SUBLANES), :]
        gsorted_ref[pl.ds(p, 1), :] = gate_ref[pl.ds(t, 1), :]
        return carry
    lax.fori_loop(0, TM, move, 0, unroll=16)


def moe_permute(pos, tiles, gates):
    return pl.pallas_call(
        _moe_permute_kernel,
        grid_spec=pltpu.PrefetchScalarGridSpec(
            num_scalar_prefetch=1,
            grid=(N_TILES,),
            in_specs=[pl.BlockSpec((TM * SUBLANES, LANES), lambda i, pos: (i, 0)),
                      pl.BlockSpec((TM, LANES), lambda i, pos: (i, 0))],
            out_specs=[pl.BlockSpec(memory_space=pltpu.VMEM), pl.BlockSpec(memory_space=pltpu.VMEM)],
        ),
        out_shape=[jax.ShapeDtypeStruct((T_PAD * SUBLANES, LANES), F32),
                   jax.ShapeDtypeStruct((T_PAD, LANES), F32)],
        compiler_params=_params(1),
        name="moe_permute",
    )(pos, tiles, gates)


def _group_changed(gid_ref, b):
    return (b == 0) | (gid_ref[b] != gid_ref[jnp.maximum(b - 1, 0)])


def _moe_expert_kernel(gid_ref, used_ref, next_ref, s_ref, gate_ref, w1_hbm, w3_hbm, w2_ref, y_ref,
                       w1f_ref, w3f_ref, w1b_ref, w3b_ref, w2b_ref, sem, *, layer):
    b = pl.program_id(0)

    def up_weight_copies(group):
        return (pltpu.make_async_copy(w1_hbm.at[layer, group], w1f_ref, sem.at[0]),
                pltpu.make_async_copy(w3_hbm.at[layer, group], w3f_ref, sem.at[1]))

    @pl.when(b == 0)
    def _():
        for copy in up_weight_copies(gid_ref[0]):
            copy.start()

    @pl.when(_group_changed(gid_ref, b))
    def _():
        for copy in up_weight_copies(gid_ref[b]):
            copy.wait()
        for j in range(EXP_PER_GROUP):
            cols = slice(j * D_FF, (j + 1) * D_FF)
            w1b_ref[:, cols] = w1f_ref[j].astype(BF16)
            w3b_ref[:, cols] = w3f_ref[j].astype(BF16)
        w2b_ref[...] = w2_ref[0, 0].astype(BF16)

        @pl.when(next_ref[b] >= 0)
        def _():
            for copy in up_weight_copies(next_ref[b]):
                copy.start()

    @pl.when(used_ref[b] == 1)
    def _():
        lhs = jnp.concatenate([s_ref[pl.ds(s, MOE_BLOCK, stride=SUBLANES), :].astype(BF16)
                               for s in range(SUBLANES)], axis=1)
        a = _dot(lhs, w1b_ref[...])
        g = _dot(lhs, w3b_ref[...])
        gates = gate_ref[...]
        lane = lax.broadcasted_iota(jnp.int32, (MOE_BLOCK, LANES), 1)
        first = gid_ref[b] * EXP_PER_GROUP
        parts = []
        for j in range(EXP_PER_GROUP):
            cols = slice(j * D_FF, (j + 1) * D_FF)
            gate = jnp.sum(jnp.where(lane == first + j, gates, 0.0), axis=1, keepdims=True)
            parts.append((_silu(a[:, cols]) * g[:, cols] * gate).astype(BF16))
        act = jnp.concatenate(parts, axis=1)
        y = _dot(act, w2b_ref[...])
        for s in range(SUBLANES):
            y_ref[pl.ds(s, MOE_BLOCK, stride=SUBLANES), :] = y[:, s * LANES:(s + 1) * LANES]

    @pl.when(used_ref[b] == 0)
    def _():
        y_ref[...] = jnp.zeros_like(y_ref)


def moe_experts(block_gid, block_used, next_group, sorted_rows, sorted_gates, w1, w3, w2, layer):
    group_ff = EXP_PER_GROUP * D_FF
    w1g = w1.reshape(DEPTH, N_GROUPS, EXP_PER_GROUP, D_MODEL, D_FF)
    w3g = w3.reshape(DEPTH, N_GROUPS, EXP_PER_GROUP, D_MODEL, D_FF)
    w2g = w2.reshape(DEPTH, N_GROUPS, group_ff, D_MODEL)
    block = lambda b, gid, used, nxt: (b, 0)
    return pl.pallas_call(
        functools.partial(_moe_expert_kernel, layer=layer),
        grid_spec=pltpu.PrefetchScalarGridSpec(
            num_scalar_prefetch=3,
            grid=(MOE_BLOCKS,),
            in_specs=[pl.BlockSpec((MOE_BLOCK * SUBLANES, LANES), block),
                      pl.BlockSpec((MOE_BLOCK, LANES), block),
                      pl.BlockSpec(memory_space=pl.ANY), pl.BlockSpec(memory_space=pl.ANY),
                      pl.BlockSpec((1, 1, group_ff, D_MODEL), lambda b, gid, used, nxt: (layer, gid[b], 0, 0))],
            out_specs=pl.BlockSpec((MOE_BLOCK * SUBLANES, LANES), block),
            scratch_shapes=[pltpu.VMEM((EXP_PER_GROUP, D_MODEL, D_FF), F32),
                            pltpu.VMEM((EXP_PER_GROUP, D_MODEL, D_FF), F32),
                            pltpu.VMEM((D_MODEL, group_ff), BF16), pltpu.VMEM((D_MODEL, group_ff), BF16),
                            pltpu.VMEM((group_ff, D_MODEL), BF16),
                            pltpu.SemaphoreType.DMA((2,))],
        ),
        out_shape=jax.ShapeDtypeStruct((T_PAD * SUBLANES, LANES), F32),
        compiler_params=_params(1),
        name="moe_experts",
    )(block_gid, block_used, next_group, sorted_rows, sorted_gates, w1g, w3g, w2g)


FINAL_TM = 512


def _moe_combine_kernel(pos_ref, ys_ref, x_ref, m_ref, *rest, tm, final):
    i = pl.program_id(0)
    buf_ref = rest[-2] if final else rest[-1]

    def move(t, carry):
        src = pl.multiple_of(pos_ref[i * tm + t] * SUBLANES, SUBLANES)
        dst = pl.multiple_of(t * SUBLANES, SUBLANES)
        buf_ref[pl.ds(dst, SUBLANES), :] = ys_ref[pl.ds(src, SUBLANES), :]
        return carry
    lax.fori_loop(0, tm, move, 0, unroll=16)

    x_new_ref = rest[-1] if final else rest[0]
    for s in range(SUBLANES):
        cols = slice(s * LANES, (s + 1) * LANES)
        y = buf_ref[pl.ds(s, tm, stride=SUBLANES), :]
        x_new_ref[:, cols] = x_ref[:, cols] + m_ref[0, 0, 5:6, cols] * y

    if final:
        gain_ref, ctx_ref, lat_ref = rest[:3]
        out = _rms(x_new_ref[...]) * gain_ref[...]

        @pl.when(i < T_CTX // tm)
        def _():
            ctx_ref[...] = out

        @pl.when(i >= T_CTX // tm)
        def _():
            lat_ref[...] = out


def moe_combine(pos, y_sorted, x, mods, layer, final_gain=None):
    final = final_gain is not None
    tm = FINAL_TM if final else TM
    ctx_tiles = T_CTX // tm
    tile = pl.BlockSpec((tm, D_MODEL), lambda i, pos: (i, 0))
    in_specs = [
        pl.BlockSpec(memory_space=pltpu.VMEM),
        tile,
        pl.BlockSpec((1, 1, 6, D_MODEL), lambda i, pos: (layer, _mod_row(i * tm // TM), 0, 0)),
    ]
    args = [pos, y_sorted, x, mods]
    scratch = [pltpu.VMEM((tm * SUBLANES, LANES), F32)]
    if final:
        in_specs.append(pl.BlockSpec((1, D_MODEL), lambda i, pos: (0, 0)))
        args.append(final_gain.reshape(1, D_MODEL))
        out_specs = [pl.BlockSpec((tm, D_MODEL), lambda i, pos: (jnp.minimum(i, ctx_tiles - 1), 0)),
                     pl.BlockSpec((tm, D_MODEL), lambda i, pos: (jnp.maximum(i - ctx_tiles, 0), 0))]
        out_shape = [jax.ShapeDtypeStruct((T_CTX, D_MODEL), F32), jax.ShapeDtypeStruct((T_LAT, D_MODEL), F32)]
        scratch.append(pltpu.VMEM((tm, D_MODEL), F32))
    else:
        out_specs = tile
        out_shape = jax.ShapeDtypeStruct((T_ALL, D_MODEL), F32)
    return pl.pallas_call(
        functools.partial(_moe_combine_kernel, tm=tm, final=final),
        grid_spec=pltpu.PrefetchScalarGridSpec(
            num_scalar_prefetch=1,
            grid=(T_ALL // tm,),
            in_specs=in_specs,
            out_specs=out_specs,
            scratch_shapes=scratch,
        ),
        out_shape=out_shape,
        compiler_params=_params(1),
        name="moe_combine",
    )(*args)


def moe(x, gain, mods, wr_t, b_r, tri, w1, w3, w2, layer, final_gain=None):
    tiles, gates, meta, counts = moe_route(x, gain, mods, wr_t, b_r, tri, layer)
    pos, block_gid, block_used, next_group = _moe_positions(meta, counts)
    sorted_rows, sorted_gates = moe_permute(pos, tiles, gates)
    y_sorted = moe_experts(block_gid, block_used, next_group, sorted_rows, sorted_gates, w1, w3, w2, layer)
    return moe_combine(pos, y_sorted, x, mods, layer, final_gain)


def kernel(x_prompt, x_sample, cache_attn_k, cache_attn_v, state_retention, c, c_ctx, w_ada, b_ada, norm1_g, norm2_g, final_g, w_in, attn_lambda, attn_subln_g, hy_conv_w, hy_conv_b, hy_f_w1, hy_f_b1, hy_f_w2, hy_f_b2, hy_f_w3, hy_skip, ret_decay_exp, w_branch_a, w_branch_b, w_branch_c, w_gate, b_gate, w_out, w_router, b_router, moe_w1, moe_w3, moe_w2):
    x = jnp.concatenate([x_prompt.reshape(T_CTX, D_MODEL), x_sample.reshape(T_LAT, D_MODEL)], axis=0)
    cond8 = jnp.concatenate([c_ctx[None, :], c, jnp.zeros((8 - 1 - DEC_BATCH, D_MODEL), F32)], axis=0)
    mods = ada_modulation(cond8, w_ada, b_ada).reshape(DEPTH, 8, 6, D_MODEL)

    cos, sin = _rope_tables()
    cos_c, sin_c = cos[:, :H_C * DK_C], sin[:, :H_C * DK_C]
    cache_k = cache_attn_k.reshape(DEC_BATCH, DEPTH, PAST_LEN, W_A)
    cache_v = cache_attn_v.reshape(DEC_BATCH, DEPTH, PAST_LEN, W_A)
    decay_rates = _filter_decay_rates()
    tables_ctx = _dft_tables(SEQ)
    tables_lat = _dft_tables(DEC_SEQ)
    filt_ctx = hyena_filters(SEQ, _filter_features(SEQ), decay_rates, tables_ctx[0], tables_ctx[1],
                             hy_f_w1, hy_f_b1, hy_f_w2, hy_f_b2, hy_f_w3)
    filt_lat = hyena_filters(DEC_SEQ, _filter_features(DEC_SEQ), decay_rates, tables_lat[0], tables_lat[1],
                             hy_f_w1, hy_f_b1, hy_f_w2, hy_f_b2, hy_f_w3)
    wr_t = w_router.T
    b_r = b_router.reshape(N_EXPERTS, 1)
    tri = jnp.asarray(np.triu(np.ones((TM, TM), np.float32), 1), dtype=BF16)

    threaded = ()
    for l in range(DEPTH):
        p, kv = in_projection(x, norm1_g, mods, w_in, w_gate, b_gate, l)

        ya, new_k, new_v, yb, yc, states = mixers_ctx(
            p, kv, attn_lambda, attn_subln_g, hy_conv_w, hy_conv_b, hy_skip, filt_ctx[0], filt_ctx[1],
            tables_ctx, ret_decay_exp, l, threaded)
        threaded = (new_k, new_v, states)
        ya = attention_lat(ya, p, kv, cache_k, cache_v, cos, sin, attn_lambda, attn_subln_g, l)
        yb = hyena_lat(yb, p, hy_conv_w, hy_conv_b, hy_skip, filt_lat[0], filt_lat[1], tables_lat, l)
        yc = retention_lat(yc, p, ret_decay_exp, cos_c, sin_c, state_retention, l)

        x = merge_branches(x, ya, yb, yc, p, mods, w_branch_a, w_branch_b, w_branch_c, w_out, l)
        x = moe(x, norm2_g, mods, wr_t, b_r, tri, moe_w1, moe_w3, moe_w2, l,
                final_g if l == DEPTH - 1 else None)

    y_prompt, y_sample = x
    return (y_prompt.reshape(BATCH, SEQ, D_MODEL), y_sample.reshape(DEC_BATCH, DEC_SEQ, D_MODEL),
            new_k, new_v, states)
```
